```python
import jax, jax.numpy as jnp
from jax import lax
import numpy as np

D_MODEL = 2048
BATCH = 16
SEQ = 2048
DEPTH = 1
DEC_BATCH = 16
DEC_SEQ = 32
PAST_LEN = 4096

CHUNK = 64
N_MEM = 256
EPS = 1e-6

RWKV_HEADS = 16
RWKV_HEAD_DIM = 64
RWKV_DIM = RWKV_HEADS * RWKV_HEAD_DIM
DECAY_LORA = 96
ICLR_LORA = 96
GATE_LORA = 256
GN_EPS = 64e-5
RWKV_COLS = 3 * RWKV_DIM + DECAY_LORA + ICLR_LORA + GATE_LORA

MLA_HEADS = 16
QK_NOPE = 128
QK_ROPE = 64
V_HEAD = 128
Q_LORA = 768
KV_LORA = 512
ROPE_THETA = 10000.0
MLA_DIM = MLA_HEADS * V_HEAD
MLA_SCALE = (QK_NOPE + QK_ROPE) ** -0.5
Q_BLOCK = 128

MEM_HEADS = 4
MEM_HEAD_DIM = D_MODEL // MEM_HEADS

N_GROUPS = 8
EXPERTS_PER_GROUP = 8
N_EXPERTS = N_GROUPS * EXPERTS_PER_GROUP
TOP_K = 2
EXPERT_FFN = 512
MOE_BLOCK = 128

IN_COLS = Q_LORA + KV_LORA + QK_ROPE + RWKV_COLS + 2 * D_MODEL

kernel_name = 'hybrid_rwkv7_mla_hiermoe_stream_step'


def rmsnorm(x, g):
    xf = x.astype(jnp.float32)
    y = xf * lax.rsqrt(jnp.mean(xf * xf, axis=-1, keepdims=True) + EPS)
    return (y * g.astype(jnp.float32)).astype(x.dtype)


def rope(x, pos):
    half = QK_ROPE // 2
    inv = ROPE_THETA ** (-jnp.arange(half, dtype=jnp.float32) / half)
    ang = pos.astype(jnp.float32)[:, None] * inv[None, :]
    shp = (ang.shape[0],) + (1,) * (x.ndim - 3) + (half,)
    cos, sin = jnp.cos(ang).reshape(shp), jnp.sin(ang).reshape(shp)
    xf = x.astype(jnp.float32)
    x1, x2 = xf[..., :half], xf[..., half:]
    return jnp.concatenate([x1 * cos - x2 * sin, x1 * sin + x2 * cos], axis=-1).astype(x.dtype)


def in_project(u, lp):
    p = jnp.einsum('bsd,dc->bsc', u, lp['w_in'])
    i0 = Q_LORA
    i1 = i0 + KV_LORA
    i2 = i1 + QK_ROPE
    i3 = i2 + RWKV_COLS
    return p[..., :i0], p[..., i0:i1], p[..., i1:i2], p[..., i2:i3], p[..., i3:]


def mla_queries(q_raw, pos, lp):
    q = jnp.einsum('bsc,chd->bshd', rmsnorm(q_raw, lp['g_q']), lp['w_uq'])
    return q[..., :QK_NOPE], rope(q[..., QK_NOPE:], pos)


def mla_latents(c_raw, kr_raw, pos, lp):
    return rmsnorm(c_raw, lp['g_kv']), rope(kr_raw, pos)


def mla_prompt(q_nope, q_rope, c, kr, w_uk, w_uv):
    B, S = c.shape[:2]
    k_nope = jnp.einsum('bsc,chd->bshd', c, w_uk)
    v = jnp.einsum('bsc,chd->bshd', c, w_uv)
    k_chunk = jnp.arange(S) // CHUNK
    n_blk = S // Q_BLOCK

    def to_blocks(t):
        return jnp.moveaxis(t.reshape((B, n_blk, Q_BLOCK) + t.shape[2:]), 1, 0)

    def attend_block(args):
        qn, qr, q_start = args
        s = jnp.einsum('bqhd,bkhd->bhqk', qn, k_nope) + jnp.einsum('bqhr,bkr->bhqk', qr, kr)
        s = s.astype(jnp.float32) * MLA_SCALE
        q_chunk = (q_start + jnp.arange(Q_BLOCK)) // CHUNK
        mask = k_chunk[None, :] <= q_chunk[:, None]
        p = jax.nn.softmax(jnp.where(mask, s, -jnp.inf), axis=-1)
        return jnp.einsum('bhqk,bkhd->bqhd', p.astype(v.dtype), v)

    o = lax.map(attend_block, (to_blocks(q_nope), to_blocks(q_rope), jnp.arange(n_blk) * Q_BLOCK))
    return jnp.moveaxis(o, 0, 1).reshape(B, S, MLA_DIM)


def mla_sample(q_nope, q_rope, c_all, kr_all, w_uk, w_uv):
    B, Sq = q_nope.shape[:2]
    q_abs = jnp.einsum('bqhd,chd->bqhc', q_nope, w_uk)
    s = jnp.einsum('bqhc,bkc->bhqk', q_abs, c_all) + jnp.einsum('bqhr,bkr->bhqk', q_rope, kr_all)
    p = jax.nn.softmax(s.astype(jnp.float32) * MLA_SCALE, axis=-1)
    o_lat = jnp.einsum('bhqk,bkc->bqhc', p.astype(c_all.dtype), c_all)
    o = jnp.einsum('bqhc,chd->bqhd', o_lat, w_uv)
    return o.reshape(B, Sq, MLA_DIM)


def rwkv_branch(p_rw, shift_prev, wkv0, lp):
    B, S, _ = p_rw.shape

    def heads(t):
        return t.reshape(B, S, RWKV_HEADS, RWKV_HEAD_DIM)

    prev = jnp.concatenate([shift_prev, p_rw[:, :-1]], axis=1)
    xs = p_rw + lp['rw_mu'] * (prev - p_rw)
    c0, c1, c2 = RWKV_DIM, 2 * RWKV_DIM, 3 * RWKV_DIM
    c3 = c2 + DECAY_LORA
    c4 = c3 + ICLR_LORA
    r, k, v = xs[..., :c0], xs[..., c0:c1], xs[..., c1:c2]
    dw, da, dg = xs[..., c2:c3], xs[..., c3:c4], xs[..., c4:]
    w_log = -jax.nn.softplus(-(lp['rw_w0'] + jnp.tanh(dw) @ lp['rw_wb'])) - 0.5
    decay = jnp.exp(-jnp.exp(w_log.astype(jnp.float32)))
    a = jax.nn.sigmoid(lp['rw_a0'] + da @ lp['rw_ab'])
    g = jax.nn.sigmoid(dg) @ lp['rw_gb']
    kk = heads(k * lp['rw_kk']).astype(jnp.float32)
    kk = kk / jnp.maximum(jnp.sqrt(jnp.sum(kk * kk, axis=-1, keepdims=True)), 1e-12)
    k = k * (1.0 + (a - 1.0) * lp['rw_ka'])
    r_h, k_h, v_h = heads(r), heads(k), heads(v)

    def step(st, inp):
        r_t, w_t, k_t, v_t, kk_t, a_t = inp
        sa = jnp.einsum('bhvk,bhk->bhv', st, kk_t)
        st = st * w_t[:, :, None, :] - sa[..., None] * (kk_t * a_t)[:, :, None, :] + v_t[..., None] * k_t[:, :, None, :]
        return st, jnp.einsum('bhvk,bhk->bhv', st, r_t)

    seq = tuple(jnp.moveaxis(t.astype(jnp.float32), 1, 0) for t in (r_h, heads(decay), k_h, v_h, kk, heads(a)))
    wkv, o = lax.scan(step, wkv0.astype(jnp.float32), seq)
    o = jnp.moveaxis(o, 0, 1)
    mu = jnp.mean(o, axis=-1, keepdims=True)
    var = jnp.mean(jnp.square(o - mu), axis=-1, keepdims=True)
    o = ((o - mu) * lax.rsqrt(var + GN_EPS)).reshape(B, S, RWKV_DIM) * lp['rw_ln_g'] + lp['rw_ln_b']
    bonus = jnp.sum(r_h * k_h * lp['rw_rk'], axis=-1, keepdims=True) * v_h
    y = (o + bonus.reshape(B, S, RWKV_DIM)) * g
    return y.astype(p_rw.dtype), wkv.astype(wkv0.dtype), p_rw[:, -1:]


def mix_out(y_rw, y_mla, gate_raw, lp):
    g_rw, g_mla = gate_raw[..., :D_MODEL], gate_raw[..., D_MODEL:]
    merged = jax.nn.sigmoid(g_rw) * (y_rw @ lp['w_br_rwkv']) + jax.nn.sigmoid(g_mla) * (y_mla @ lp['w_br_mla'])
    return merged @ lp['w_out']


def mem_kv(mem, lp):
    mn = rmsnorm(mem, lp['g_mem'])
    return (jnp.einsum('bmd,dhe->bmhe', mn, lp['w_mk']), jnp.einsum('bmd,dhe->bmhe', mn, lp['w_mv']))


def cross_attend(hn, mk, mv, lp):
    q = jnp.einsum('bsd,dhe->bshe', hn, lp['w_mq'])
    s = jnp.einsum('bshe,bmhe->bhsm', q, mk).astype(jnp.float32) * (MEM_HEAD_DIM ** -0.5)
    p = jax.nn.softmax(s, axis=-1)
    o = jnp.einsum('bhsm,bmhe->bshe', p.astype(mv.dtype), mv)
    return jnp.einsum('bshe,hed->bsd', o, lp['w_mo'])


def grouped_experts(xt, e_idx, wts, w_eg, w_eu, w_ed):
    T, D = xt.shape
    M = T * TOP_K
    n_blk = -(-(M + N_EXPERTS * (MOE_BLOCK - 1)) // MOE_BLOCK)
    flat_e = e_idx.reshape(M)
    order = jnp.argsort(flat_e)
    sorted_e = flat_e[order]
    counts = jnp.bincount(flat_e, length=N_EXPERTS)
    padded = (counts + MOE_BLOCK - 1) // MOE_BLOCK * MOE_BLOCK
    pad_end = jnp.cumsum(padded)
    pad_start = pad_end - padded
    start = jnp.cumsum(counts) - counts
    dest = pad_start[sorted_e] + jnp.arange(M) - start[sorted_e]
    tok_sorted = (order // TOP_K).astype(jnp.int32)
    slot_tok = jnp.full((n_blk * MOE_BLOCK,), T, jnp.int32).at[dest].set(tok_sorted)
    blk_e = jnp.minimum(jnp.searchsorted(pad_end, jnp.arange(n_blk) * MOE_BLOCK, side='right'), N_EXPERTS - 1)
    x_pad = jnp.concatenate([xt, jnp.zeros((1, D), xt.dtype)], axis=0)

    def run_block(args):
        tok, e = args
        xb = x_pad[tok]
        hb = jax.nn.silu(xb @ w_eg[e]) * (xb @ w_eu[e])
        return hb @ w_ed[e]

    y_slot = lax.map(run_block, (slot_tok.reshape(n_blk, MOE_BLOCK), blk_e)).reshape(-1, D)
    y_asg = y_slot[dest] * wts.reshape(M)[order][:, None].astype(xt.dtype)
    return jax.ops.segment_sum(y_asg, tok_sorted, num_segments=T)


def hier_moe(hn, lp):
    B, S, D = hn.shape
    xt = hn.reshape(B * S, D)
    T = xt.shape[0]
    lg = (xt @ lp['w_rg']).astype(jnp.float32) + lp['b_rg']
    g_sel = jnp.argmax(lg, axis=-1)
    p_grp = jnp.take_along_axis(jax.nn.softmax(lg, axis=-1), g_sel[:, None], axis=1)
    le = ((xt @ lp['w_re']).astype(jnp.float32) + lp['b_re']).reshape(T, N_GROUPS, EXPERTS_PER_GROUP)
    le = jnp.take_along_axis(le, g_sel[:, None, None], axis=1)[:, 0]
    top_l, top_i = lax.top_k(le, TOP_K)
    wts = p_grp * jax.nn.softmax(top_l, axis=-1)
    e_idx = g_sel[:, None] * EXPERTS_PER_GROUP + top_i
    y = grouped_experts(xt, e_idx, wts, lp['w_eg'], lp['w_eu'], lp['w_ed'])
    return y.reshape(B, S, D)


def layer_prompt(x, mem, lp):
    B, S, _ = x.shape
    pos = jnp.arange(S)
    u = rmsnorm(x, lp['g_mix'])
    q_raw, c_raw, kr_raw, p_rw, gate_raw = in_project(u, lp)
    q_nope, q_rope = mla_queries(q_raw, pos, lp)
    c, kr = mla_latents(c_raw, kr_raw, pos, lp)
    y_mla = mla_prompt(q_nope, q_rope, c, kr, lp['w_uk'], lp['w_uv'])
    shift0 = jnp.zeros((B, 1, RWKV_COLS), x.dtype)
    wkv0 = jnp.zeros((B, RWKV_HEADS, RWKV_HEAD_DIM, RWKV_HEAD_DIM), x.dtype)
    y_rw, wkv, shift = rwkv_branch(p_rw, shift0, wkv0, lp)
    h = x + mix_out(y_rw, y_mla, gate_raw, lp)
    mk, mv = mem_kv(mem, lp)
    h = h + cross_attend(rmsnorm(h, lp['g_cross']), mk, mv, lp)
    h = h + hier_moe(rmsnorm(h, lp['g_moe']), lp)
    return h, c, kr, mk, mv, wkv, shift


def layer_sample(x, c_past, kr_past, mk, mv, wkv0, shift0, lp):
    B, S, _ = x.shape
    pos = c_past.shape[1] + jnp.arange(S)
    u = rmsnorm(x, lp['g_mix'])
    q_raw, c_raw, kr_raw, p_rw, gate_raw = in_project(u, lp)
    q_nope, q_rope = mla_queries(q_raw, pos, lp)
    c, kr = mla_latents(c_raw, kr_raw, pos, lp)
    y_mla = mla_sample(q_nope, q_rope, jnp.concatenate([c_past, c], axis=1),
                       jnp.concatenate([kr_past, kr], axis=1), lp['w_uk'], lp['w_uv'])
    y_rw, wkv, shift = rwkv_branch(p_rw, shift0, wkv0, lp)
    h = x + mix_out(y_rw, y_mla, gate_raw, lp)
    h = h + cross_attend(rmsnorm(h, lp['g_cross']), mk, mv, lp)
    h = h + hier_moe(rmsnorm(h, lp['g_moe']), lp)
    return h, c, kr, wkv, shift


def setup_inputs(seed: int = 0) -> dict:
    key = jax.random.key(seed)
    ks = iter(jax.random.split(key, 64))
    L = DEPTH

    def nrm(shape, scale=1.0):
        return jax.random.normal(next(ks), shape, jnp.float32) * scale

    def gain(shape):
        return 1.0 + nrm(shape, 0.02)

    return {
        'x_prompt': nrm((BATCH, SEQ, D_MODEL)),
        'x_sample': nrm((DEC_BATCH, DEC_SEQ, D_MODEL)),
        'mem_prompt': nrm((BATCH, N_MEM, D_MODEL)),
        'cache_kv_latent': nrm((L, DEC_BATCH, PAST_LEN, KV_LORA)),
        'cache_k_rope': nrm((L, DEC_BATCH, PAST_LEN, QK_ROPE)),
        'cache_mem_k': nrm((L, DEC_BATCH, N_MEM, MEM_HEADS, MEM_HEAD_DIM)),
        'cache_mem_v': nrm((L, DEC_BATCH, N_MEM, MEM_HEADS, MEM_HEAD_DIM)),
        'state_wkv': nrm((L, DEC_BATCH, RWKV_HEADS, RWKV_HEAD_DIM, RWKV_HEAD_DIM)),
        'state_shift': nrm((L, DEC_BATCH, 1, RWKV_COLS)),
        'g_mix': gain((L, D_MODEL)),
        'w_in': nrm((L, D_MODEL, IN_COLS), D_MODEL ** -0.5),
        'g_q': gain((L, Q_LORA)),
        'w_uq': nrm((L, Q_LORA, MLA_HEADS, QK_NOPE + QK_ROPE), Q_LORA ** -0.5),
        'g_kv': gain((L, KV_LORA)),
        'w_uk': nrm((L, KV_LORA, MLA_HEADS, QK_NOPE), KV_LORA ** -0.5),
        'w_uv': nrm((L, KV_LORA, MLA_HEADS, V_HEAD), KV_LORA ** -0.5),
        'rw_mu': jax.random.uniform(next(ks), (L, RWKV_COLS), jnp.float32),
        'rw_w0': jax.random.uniform(next(ks), (L, RWKV_DIM), jnp.float32, -2.0, 2.0),
        'rw_wb': nrm((L, DECAY_LORA, RWKV_DIM), 0.5 * DECAY_LORA ** -0.5),
        'rw_a0': nrm((L, RWKV_DIM), 0.5),
        'rw_ab': nrm((L, ICLR_LORA, RWKV_DIM), 0.5 * ICLR_LORA ** -0.5),
        'rw_gb': nrm((L, GATE_LORA, RWKV_DIM), GATE_LORA ** -0.5),
        'rw_kk': 0.85 + nrm((L, RWKV_DIM), 0.02),
        'rw_ka': 1.0 + nrm((L, RWKV_DIM), 0.02),
        'rw_rk': nrm((L, RWKV_HEADS, RWKV_HEAD_DIM), 0.1),
        'rw_ln_g': gain((L, RWKV_DIM)),
        'rw_ln_b': nrm((L, RWKV_DIM), 0.02),
        'w_br_rwkv': nrm((L, RWKV_DIM, D_MODEL), RWKV_DIM ** -0.5),
        'w_br_mla': nrm((L, MLA_DIM, D_MODEL), MLA_DIM ** -0.5),
        'w_out': nrm((L, D_MODEL, D_MODEL), D_MODEL ** -0.5),
        'g_cross': gain((L, D_MODEL)),
        'g_mem': gain((L, D_MODEL)),
        'w_mq': nrm((L, D_MODEL, MEM_HEADS, MEM_HEAD_DIM), D_MODEL ** -0.5),
        'w_mk': nrm((L, D_MODEL, MEM_HEADS, MEM_HEAD_DIM), D_MODEL ** -0.5),
        'w_mv': nrm((L, D_MODEL, MEM_HEADS, MEM_HEAD_DIM), D_MODEL ** -0.5),
        'w_mo': nrm((L, MEM_HEADS, MEM_HEAD_DIM, D_MODEL), D_MODEL ** -0.5),
        'g_moe': gain((L, D_MODEL)),
        'w_rg': nrm((L, D_MODEL, N_GROUPS), D_MODEL ** -0.5),
        'b_rg': nrm((L, N_GROUPS), 0.01),
        'w_re': nrm((L, D_MODEL, N_EXPERTS), D_MODEL ** -0.5),
        'b_re': nrm((L, N_EXPERTS), 0.01),
        'w_eg': nrm((L, N_EXPERTS, D_MODEL, EXPERT_FFN), D_MODEL ** -0.5),
        'w_eu': nrm((L, N_EXPERTS, D_MODEL, EXPERT_FFN), D_MODEL ** -0.5),
        'w_ed': nrm((L, N_EXPERTS, EXPERT_FFN, D_MODEL), EXPERT_FFN ** -0.5),
        'g_final': gain((D_MODEL,)),
    }


def reference(x_prompt, x_sample, mem_prompt, cache_kv_latent, cache_k_rope, cache_mem_k, cache_mem_v,
              state_wkv, state_shift, g_mix, w_in, g_q, w_uq, g_kv, w_uk, w_uv, rw_mu, rw_w0, rw_wb,
              rw_a0, rw_ab, rw_gb, rw_kk, rw_ka, rw_rk, rw_ln_g, rw_ln_b, w_br_rwkv, w_br_mla, w_out,
              g_cross, g_mem, w_mq, w_mk, w_mv, w_mo, g_moe, w_rg, b_rg, w_re, b_re, w_eg, w_eu, w_ed,
              g_final):
    h_p, h_s = x_prompt, x_sample
    p_c, p_kr, p_mk, p_mv, p_wkv, p_sh = [], [], [], [], [], []
    s_c, s_kr, s_wkv, s_sh = [], [], [], []
    for l in range(DEPTH):
        lp = {
            'g_mix': g_mix[l], 'w_in': w_in[l], 'g_q': g_q[l], 'w_uq': w_uq[l], 'g_kv': g_kv[l],
            'w_uk': w_uk[l], 'w_uv': w_uv[l], 'rw_mu': rw_mu[l], 'rw_w0': rw_w0[l], 'rw_wb': rw_wb[l],
            'rw_a0': rw_a0[l], 'rw_ab': rw_ab[l], 'rw_gb': rw_gb[l], 'rw_kk': rw_kk[l], 'rw_ka': rw_ka[l],
            'rw_rk': rw_rk[l], 'rw_ln_g': rw_ln_g[l], 'rw_ln_b': rw_ln_b[l], 'w_br_rwkv': w_br_rwkv[l],
            'w_br_mla': w_br_mla[l], 'w_out': w_out[l], 'g_cross': g_cross[l], 'g_mem': g_mem[l],
            'w_mq': w_mq[l], 'w_mk': w_mk[l], 'w_mv': w_mv[l], 'w_mo': w_mo[l], 'g_moe': g_moe[l],
            'w_rg': w_rg[l], 'b_rg': b_rg[l], 'w_re': w_re[l], 'b_re': b_re[l],
            'w_eg': w_eg[l], 'w_eu': w_eu[l], 'w_ed': w_ed[l],
        }
        h_p, c, kr, mk, mv, wkv, sh = layer_prompt(h_p, mem_prompt, lp)
        p_c.append(c); p_kr.append(kr); p_mk.append(mk); p_mv.append(mv); p_wkv.append(wkv); p_sh.append(sh)
        h_s, c, kr, wkv, sh = layer_sample(h_s, cache_kv_latent[l], cache_k_rope[l], cache_mem_k[l],
                                           cache_mem_v[l], state_wkv[l], state_shift[l], lp)
        s_c.append(c); s_kr.append(kr); s_wkv.append(wkv); s_sh.append(sh)
    y_prompt = rmsnorm(h_p, g_final)
    y_sample = rmsnorm(h_s, g_final)
    new_kv_latent_prompt = jnp.stack(p_c)
    new_k_rope_prompt = jnp.stack(p_kr)
    new_mem_k_prompt = jnp.stack(p_mk)
    new_mem_v_prompt = jnp.stack(p_mv)
    new_wkv_prompt = jnp.stack(p_wkv)
    new_shift_prompt = jnp.stack(p_sh)
    new_kv_latent_sample = jnp.stack(s_c)
    new_k_rope_sample = jnp.stack(s_kr)
    new_wkv_sample = jnp.stack(s_wkv)
    new_shift_sample = jnp.stack(s_sh)
    return (y_prompt, y_sample, new_kv_latent_prompt, new_k_rope_prompt, new_mem_k_prompt, new_mem_v_prompt,
            new_wkv_prompt, new_shift_prompt, new_kv_latent_sample, new_k_rope_sample, new_wkv_sample,
            new_shift_sample)
```

```python
import functools

import jax
import jax.numpy as jnp
from jax import lax
from jax.experimental import pallas as pl
from jax.experimental.pallas import tpu as pltpu

F32 = jnp.float32
BF16 = jnp.bfloat16

EPS = 1e-6
GN_EPS = 64e-5
ROPE_THETA = 10000.0
CHUNK = 64
TOP_K = 2
NEG = -1e30

LANE = 128
V7X_VMEM_LIMIT = 56 << 20
ROPE = 64
MOE_ROWS = 256


def _cp(*sem):
    return pltpu.CompilerParams(dimension_semantics=sem, vmem_limit_bytes=V7X_VMEM_LIMIT)


def _tile(n, pref):
    if n <= pref:
        return n
    for t in range(pref, 7, -1):
        if n % t == 0 and t % 8 == 0:
            return t
    return n


def _resident(shape):
    nd = len(shape)
    return pl.BlockSpec(shape, lambda *_: (0,) * nd, pipeline_mode=pl.Buffered(1))


def _rms(x, g):
    return x * lax.rsqrt(jnp.mean(x * x, axis=-1, keepdims=True) + EPS) * g


def _sigmoid(x):
    return 1.0 / (1.0 + jnp.exp(-x))


def _dot(a, b):
    return jnp.dot(a, b, preferred_element_type=F32)


def _dot_t(a, b):
    return lax.dot_general(a, b, (((1,), (1,)), ((), ())), preferred_element_type=F32)


def _seg_sum(x, e, et):
    hi = x.astype(BF16)
    lo = (x - hi.astype(F32)).astype(BF16)
    s = _dot(hi, e) + _dot(lo, e)
    shi = s.astype(BF16)
    slo = (s - shi.astype(F32)).astype(BF16)
    return _dot(shi, et) + _dot(slo, et)


def _mm_norm_kernel(x_ref, g_ref, w_ref, o_ref, xn_ref):
    @pl.when(pl.program_id(1) == 0)
    def _():
        xn_ref[...] = _rms(x_ref[...], g_ref[...]).astype(BF16)

    o_ref[...] = _dot(xn_ref[...], w_ref[...]).astype(o_ref.dtype)


def _mm_norm(x, g, w, tm=1024, tn=512, out_dtype=F32):
    m, k = x.shape
    n = w.shape[1]
    tm, tn = _tile(m, tm), _tile(n, tn)
    return pl.pallas_call(
        _mm_norm_kernel,
        grid=(m // tm, n // tn),
        in_specs=[pl.BlockSpec((tm, k), lambda i, j: (i, 0)),
                  pl.BlockSpec((1, k), lambda i, j: (0, 0)),
                  pl.BlockSpec((k, tn), lambda i, j: (0, j))],
        out_specs=pl.BlockSpec((tm, tn), lambda i, j: (i, j)),
        out_shape=jax.ShapeDtypeStruct((m, n), out_dtype),
        scratch_shapes=[pltpu.VMEM((tm, k), BF16)],
        compiler_params=_cp("parallel", "arbitrary"),
        name="mm_norm",
    )(x, g, w)


def _mla_proj_kernel(x_ref, gmix_ref, wm_ref, gq_ref, wq_ref, gkv_ref, wuk_ref, wuv_ref, cs_ref,
                     q_ref, k_ref, v_ref, c_ref, kr_ref, *, heads, q_lora, scale):
    u = _rms(x_ref[0], gmix_ref[...]).astype(BF16)
    p = _dot(u, wm_ref[...])
    cs = cs_ref[...]
    lane = lax.broadcasted_iota(jnp.int32, cs.shape, 1)

    def rope_tile(t2):
        t = t2 * cs
        return jnp.where(lane < ROPE, t + pltpu.roll(t, ROPE, axis=1), 0.0)

    krt = rope_tile(p[:, q_lora:q_lora + LANE])
    kr_ref[0] = krt[:, :ROPE]
    krb = krt.astype(BF16)
    c = _rms(p[:, q_lora + LANE:], gkv_ref[...])
    c_ref[0] = c
    cb = c.astype(BF16)
    kn = _dot(cb, wuk_ref[...])
    vv = _dot(cb, wuv_ref[...])
    qn = _rms(p[:, :q_lora], gq_ref[...]).astype(BF16)
    qf = _dot(qn, wq_ref[...])
    for h in range(heads):
        q_ref[0, h, :, :LANE] = (qf[:, h * 256:h * 256 + LANE] * scale).astype(BF16)
        q_ref[0, h, :, LANE:] = (rope_tile(qf[:, h * 256 + LANE:(h + 1) * 256]) * scale).astype(BF16)
        k_ref[0, h, :, :LANE] = kn[:, h * LANE:(h + 1) * LANE].astype(BF16)
        k_ref[0, h, :, LANE:] = krb
        v_ref[0, h] = vv[:, h * LANE:(h + 1) * LANE].astype(BF16)


def _mla_proj(x, gmix, wm, gq, wq, gkv, wuk, wuv, cs, heads, scale, tm=256):
    b, s, d = x.shape
    q_lora = gq.shape[1]
    kv_lora = gkv.shape[1]
    tm = _tile(s, tm)
    kern = functools.partial(_mla_proj_kernel, heads=heads, q_lora=q_lora, scale=scale)
    return pl.pallas_call(
        kern,
        grid=(b, s // tm),
        in_specs=[pl.BlockSpec((1, tm, d), lambda i, j: (i, j, 0)),
                  _resident(gmix.shape), _resident(wm.shape), _resident(gq.shape), _resident(wq.shape),
                  _resident(gkv.shape), _resident(wuk.shape), _resident(wuv.shape),
                  pl.BlockSpec((tm, LANE), lambda i, j: (j, 0))],
        out_specs=[pl.BlockSpec((1, heads, tm, 256), lambda i, j: (i, 0, j, 0)),
                   pl.BlockSpec((1, heads, tm, 256), lambda i, j: (i, 0, j, 0)),
                   pl.BlockSpec((1, heads, tm, LANE), lambda i, j: (i, 0, j, 0)),
                   pl.BlockSpec((1, tm, kv_lora), lambda i, j: (i, j, 0)),
                   pl.BlockSpec((1, tm, ROPE), lambda i, j: (i, j, 0))],
        out_shape=[jax.ShapeDtypeStruct((b, heads, s, 256), BF16),
                   jax.ShapeDtypeStruct((b, heads, s, 256), BF16),
                   jax.ShapeDtypeStruct((b, heads, s, LANE), BF16),
                   jax.ShapeDtypeStruct((b, s, kv_lora), F32),
                   jax.ShapeDtypeStruct((b, s, ROPE), F32)],
        compiler_params=_cp("parallel", "parallel"),
        name="mla_proj",
    )(x, gmix, wm, gq, wq, gkv, wuk, wuv, cs)


def _flash_kernel(q_ref, k_ref, v_ref, o_ref, *, tq):
    qi = pl.program_id(2)
    q = q_ref[0, 0]

    def tile(j, carry, diag):
        m, l, acc = carry
        start = pl.multiple_of(j * tq, tq)
        ks = k_ref[0, 0, pl.ds(start, tq), :]
        vs = v_ref[0, 0, pl.ds(start, tq), :]
        s = _dot_t(q, ks)
        if diag:
            qc = lax.broadcasted_iota(jnp.int32, s.shape, 0) // CHUNK
            kc = lax.broadcasted_iota(jnp.int32, s.shape, 1) // CHUNK
            s = jnp.where(kc <= qc, s, NEG)
        m_new = jnp.maximum(m, jnp.max(s, axis=-1, keepdims=True))
        alpha = jnp.exp(m - m_new)
        p = jnp.exp(s - m_new)
        l = l * alpha + jnp.sum(p, axis=-1, keepdims=True)
        acc = acc * alpha + _dot(p.astype(BF16), vs)
        return m_new, l, acc

    init = (jnp.full((tq, 1), NEG, F32), jnp.zeros((tq, 1), F32), jnp.zeros((tq, LANE), F32))
    carry = lax.fori_loop(0, qi, lambda j, c: tile(j, c, False), init)
    _, l, acc = tile(qi, carry, True)
    o_ref[0] = (acc / l).astype(o_ref.dtype)


def _mla_prompt_attn(q, k, v, tq=512):
    b, heads, s, _ = q.shape
    tq = _tile(s, tq)
    assert tq % CHUNK == 0
    return pl.pallas_call(
        functools.partial(_flash_kernel, tq=tq),
        grid=(b, heads, s // tq),
        in_specs=[pl.BlockSpec((1, 1, tq, 256), lambda i, h, j: (i, h, j, 0)),
                  pl.BlockSpec((1, 1, s, 256), lambda i, h, j: (i, h, 0, 0)),
                  pl.BlockSpec((1, 1, s, LANE), lambda i, h, j: (i, h, 0, 0))],
        out_specs=pl.BlockSpec((1, tq, LANE), lambda i, h, j: (i, j, h)),
        out_shape=jax.ShapeDtypeStruct((b, s, heads * LANE), BF16),
        compiler_params=_cp("parallel", "parallel", "arbitrary"),
        name="mla_prompt_attn",
    )(q, k, v)


def _mla_sample_kernel(q_ref, cp_ref, krp_ref, cn_ref, krn_ref, wuk_ref, wuv_ref, o_ref,
                       qa_ref, qr_ref, m_ref, l_ref, acc_ref, *, heads, sq):
    kb = pl.program_id(1)

    @pl.when(kb == 0)
    def _():
        for h in range(heads):
            qh = q_ref[0, h]
            qa_ref[h * sq:(h + 1) * sq, :] = _dot_t(qh[:, :LANE], wuk_ref[:, h * LANE:(h + 1) * LANE]).astype(BF16)
            qr_ref[h * sq:(h + 1) * sq, :] = qh[:, LANE:]
        m_ref[...] = jnp.full(m_ref.shape, NEG, F32)
        l_ref[...] = jnp.zeros(l_ref.shape, F32)
        acc_ref[...] = jnp.zeros(acc_ref.shape, F32)

    def update(cb, krb):
        s = _dot_t(qa_ref[...], cb) + _dot_t(qr_ref[...], krb)
        m_new = jnp.maximum(m_ref[...], jnp.max(s, axis=-1, keepdims=True))
        alpha = jnp.exp(m_ref[...] - m_new)
        p = jnp.exp(s - m_new)
        l_ref[...] = l_ref[...] * alpha + jnp.sum(p, axis=-1, keepdims=True)
        acc_ref[...] = acc_ref[...] * alpha + _dot(p.astype(BF16), cb)
        m_ref[...] = m_new

    update(cp_ref[0].astype(BF16), krp_ref[0])

    @pl.when(kb == pl.num_programs(1) - 1)
    def _():
        update(cn_ref[0].astype(BF16), krn_ref[0, 0])
        o_lat = (acc_ref[...] / l_ref[...]).astype(BF16)
        for h in range(heads):
            o_ref[0, :, h * LANE:(h + 1) * LANE] = _dot(
                o_lat[h * sq:(h + 1) * sq], wuv_ref[:, h * LANE:(h + 1) * LANE]).astype(o_ref.dtype)


def _mla_sample_attn(q, c_past, kr_past, c_new, k_new, wuk, wuv, tk=1024):
    b, heads, sq, _ = q.shape
    past, kv_lora = c_past.shape[1:]
    tk = _tile(past, tk)
    kern = functools.partial(_mla_sample_kernel, heads=heads, sq=sq)
    return pl.pallas_call(
        kern,
        grid=(b, past // tk),
        in_specs=[pl.BlockSpec((1, heads, sq, 256), lambda i, j: (i, 0, 0, 0)),
                  pl.BlockSpec((1, tk, kv_lora), lambda i, j: (i, j, 0)),
                  pl.BlockSpec((1, tk, LANE), lambda i, j: (i, j, 0)),
                  pl.BlockSpec((1, sq, kv_lora), lambda i, j: (i, 0, 0)),
                  pl.BlockSpec((1, 1, sq, LANE), lambda i, j: (i, 0, 0, 1)),
                  _resident(wuk.shape), _resident(wuv.shape)],
        out_specs=pl.BlockSpec((1, sq, heads * LANE), lambda i, j: (i, 0, 0)),
        out_shape=jax.ShapeDtypeStruct((b, sq, heads * LANE), BF16),
        scratch_shapes=[pltpu.VMEM((heads * sq, kv_lora), BF16), pltpu.VMEM((heads * sq, LANE), BF16),
                        pltpu.VMEM((heads * sq, 1), F32), pltpu.VMEM((heads * sq, 1), F32),
                        pltpu.VMEM((heads * sq, kv_lora), F32)],
        compiler_params=_cp("parallel", "arbitrary"),
        name="mla_sample_attn",
    )(q, c_past, kr_past, c_new, k_new, wuk, wuv)


def _rwkv_pre_kernel(x_ref, gmix_ref, w_ref, sh0_ref, mu_ref, w0_ref, wb_ref, a0_ref, ab_ref, gb_ref,
                     kkp_ref, ka_ref, rk_ref, e_ref, et_ref,
                     r_ref, dec_ref, k_ref, v_ref, kk_ref, b_ref, g_ref, bonus_ref, sh_ref, last_ref, *, dim):
    si = pl.program_id(1)
    u = _rms(x_ref[0], gmix_ref[...]).astype(BF16)
    p = _dot(u, w_ref[...])
    tm = p.shape[0]

    @pl.when(si == 0)
    def _():
        last_ref[...] = sh0_ref[0]

    row = lax.broadcasted_iota(jnp.int32, p.shape, 0)
    prev = jnp.where(row == 0, last_ref[...], pltpu.roll(p, 1, axis=0))
    last_ref[...] = p[tm - 1:tm, :]
    sh_ref[0] = p[tm - 1:tm, :]
    xs = p + mu_ref[...] * (prev - p)
    r = xs[:, :dim]
    k = xs[:, dim:2 * dim]
    v = xs[:, 2 * dim:3 * dim]
    dw = xs[:, 3 * dim:3 * dim + LANE]
    da = xs[:, 3 * dim + LANE:3 * dim + 2 * LANE]
    dg = xs[:, 3 * dim + 2 * LANE:]
    z = -(w0_ref[...] + _dot(jnp.tanh(dw).astype(BF16), wb_ref[...]))
    softplus = jnp.maximum(z, 0.0) + jnp.log(1.0 + jnp.exp(-jnp.abs(z)))
    dec_ref[0] = jnp.exp(-jnp.exp(-softplus - 0.5))
    a = _sigmoid(a0_ref[...] + _dot(da.astype(BF16), ab_ref[...]))
    g_ref[0] = _dot(_sigmoid(dg).astype(BF16), gb_ref[...])
    kk = k * kkp_ref[...]
    nrm = jnp.sqrt(_seg_sum(kk * kk, e_ref[...], et_ref[...]))
    kk = kk / jnp.maximum(nrm, 1e-12)
    k2 = k * (1.0 + (a - 1.0) * ka_ref[...])
    r_ref[0] = r
    k_ref[0] = k2
    v_ref[0] = v
    kk_ref[0] = kk
    b_ref[0] = kk * a
    bonus_ref[0] = _seg_sum(r * k2 * rk_ref[...], e_ref[...], et_ref[...]) * v


def _rwkv_pre(x, gmix, w, sh0, mu, w0, wb, a0, ab, gb, kkp, ka, rk, e, et, tm=256):
    b, s, d = x.shape
    dim = w0.shape[1]
    width = w.shape[1]
    tm = _tile(s, tm)
    row = lambda i, j: (i, j, 0)
    act = pl.BlockSpec((1, tm, dim), row)
    act_shape = jax.ShapeDtypeStruct((b, s, dim), F32)
    consts = (gmix, w, mu, w0, wb, a0, ab, gb, kkp, ka, rk, e, et)
    return pl.pallas_call(
        functools.partial(_rwkv_pre_kernel, dim=dim),
        grid=(b, s // tm),
        in_specs=[pl.BlockSpec((1, tm, d), row), _resident(gmix.shape), _resident(w.shape),
                  pl.BlockSpec((1, 1, width), lambda i, j: (i, 0, 0))] + [_resident(c.shape) for c in consts[2:]],
        out_specs=[act] * 8 + [pl.BlockSpec((1, 1, width), lambda i, j: (i, 0, 0))],
        out_shape=[act_shape] * 8 + [jax.ShapeDtypeStruct((b, 1, width), F32)],
        scratch_shapes=[pltpu.VMEM((1, width), F32)],
        compiler_params=_cp("parallel", "arbitrary"),
        name="rwkv_pre",
    )(x, gmix, w, sh0, *consts[2:])


def _rwkv_scan_kernel(r_ref, dec_ref, k_ref, v_ref, kk_ref, b_ref, s0_ref, o_ref, sT_ref, st_ref, *, ts, n):
    tb = pl.program_id(1)

    @pl.when(tb == 0)
    def _():
        st_ref[...] = s0_ref[...]

    def step(i, carry):
        vv = v_ref[i]

        def p1(kx, sa):
            return sa + st_ref[kx] * kk_ref[i, pl.ds(kx, 1), :]

        sa = lax.fori_loop(0, n, p1, jnp.zeros_like(vv), unroll=8)

        def p2(kx, o):
            s_new = (st_ref[kx] * dec_ref[i, pl.ds(kx, 1), :] - sa * b_ref[i, pl.ds(kx, 1), :]
                     + vv * k_ref[i, pl.ds(kx, 1), :])
            st_ref[kx] = s_new
            return o + s_new * r_ref[i, pl.ds(kx, 1), :]

        o_ref[i] = lax.fori_loop(0, n, p2, jnp.zeros_like(vv), unroll=8)
        return carry

    lax.fori_loop(0, ts, step, 0)

    @pl.when(tb == pl.num_programs(1) - 1)
    def _():
        sT_ref[...] = st_ref[...]


def _rwkv_scan(r, dec, k, v, kk, bb, s0, ts=32):
    s, n, streams = r.shape
    assert streams % LANE == 0
    ts = _tile(s, ts)
    seq = pl.BlockSpec((ts, n, LANE), lambda g, t: (t, 0, g))
    state = pl.BlockSpec((n, n, LANE), lambda g, t: (0, 0, g))
    return pl.pallas_call(
        functools.partial(_rwkv_scan_kernel, ts=ts, n=n),
        grid=(streams // LANE, s // ts),
        in_specs=[seq] * 6 + [state],
        out_specs=[seq, state],
        out_shape=[jax.ShapeDtypeStruct((s, n, streams), F32), jax.ShapeDtypeStruct((n, n, streams), F32)],
        scratch_shapes=[pltpu.VMEM((n, n, LANE), F32)],
        compiler_params=_cp("parallel", "arbitrary"),
        name="rwkv_scan",
    )(r, dec, k, v, kk, bb, s0)


def _mix_kernel(o_ref, bonus_ref, g_ref, ymla_ref, grw_ref, gmla_ref, lng_ref, lnb_ref, e_ref, et_ref,
                wr_ref, wm_ref, out_ref, *, inv_n):
    o = o_ref[...]
    e, et = e_ref[...], et_ref[...]
    d = o - _seg_sum(o, e, et) * inv_n
    var = _seg_sum(d * d, e, et) * inv_n
    on = d * lax.rsqrt(var + GN_EPS) * lng_ref[...] + lnb_ref[...]
    y = ((on + bonus_ref[...]) * g_ref[...]).astype(BF16)
    merged = _sigmoid(grw_ref[...]) * _dot(y, wr_ref[...]) + _sigmoid(gmla_ref[...]) * _dot(ymla_ref[...], wm_ref[...])
    out_ref[...] = merged.astype(out_ref.dtype)


def _mix(o, bonus, g, ymla, gates, lng, lnb, e, et, wr, wm, head_dim, tm=256):
    t, dim = o.shape
    d = wr.shape[1]
    md = ymla.shape[1]
    tm = _tile(t, tm)
    row = lambda i: (i, 0)
    return pl.pallas_call(
        functools.partial(_mix_kernel, inv_n=1.0 / head_dim),
        grid=(t // tm,),
        in_specs=[pl.BlockSpec((tm, dim), row)] * 3 + [pl.BlockSpec((tm, md), row),
                  pl.BlockSpec((tm, d), lambda i: (i, 0)), pl.BlockSpec((tm, d), lambda i: (i, 1))]
                 + [_resident(c.shape) for c in (lng, lnb, e, et, wr, wm)],
        out_specs=pl.BlockSpec((tm, d), row),
        out_shape=jax.ShapeDtypeStruct((t, d), BF16),
        compiler_params=_cp("parallel"),
        name="rwkv_post_mix",
    )(o, bonus, g, ymla, gates, gates, lng, lnb, e, et, wr, wm)


def _outproj_kernel(x_ref, m_ref, wo_ref, gc_ref, wq_ref, h_ref, q_ref):
    h = x_ref[...] + _dot(m_ref[...], wo_ref[...])
    h_ref[...] = h
    q_ref[...] = _dot(_rms(h, gc_ref[...]).astype(BF16), wq_ref[...]).astype(q_ref.dtype)


def _outproj(x, merged, wo, gc, wq, tm=256):
    t, d = x.shape
    tm = _tile(t, tm)
    row = lambda i: (i, 0)
    return pl.pallas_call(
        _outproj_kernel,
        grid=(t // tm,),
        in_specs=[pl.BlockSpec((tm, d), row), pl.BlockSpec((tm, d), row),
                  _resident(wo.shape), _resident(gc.shape), _resident(wq.shape)],
        out_specs=[pl.BlockSpec((tm, d), row), pl.BlockSpec((tm, d), row)],
        out_shape=[jax.ShapeDtypeStruct((t, d), F32), jax.ShapeDtypeStruct((t, d), BF16)],
        compiler_params=_cp("parallel"),
        name="outproj_crossq",
    )(x, merged, wo, gc, wq)


def _cross_kernel(q_ref, mk_ref, mv_ref, h_ref, wo_ref, gm_ref, wr_ref, br_ref, h2_ref, hn_ref, lg_ref,
                  *, heads, scale):
    q = q_ref[0]
    mk = mk_ref[0]
    mv = mv_ref[0]
    hd = q.shape[1] // heads
    outs = []
    for h in range(heads):
        sl = slice(h * hd, (h + 1) * hd)
        s = _dot_t(q[:, sl], mk[:, sl]) * scale
        p = jnp.exp(s - jnp.max(s, axis=-1, keepdims=True))
        p = p / jnp.sum(p, axis=-1, keepdims=True)
        outs.append(_dot(p.astype(BF16), mv[:, sl]).astype(BF16))
    o = jnp.concatenate(outs, axis=1)
    h2 = h_ref[0] + _dot(o, wo_ref[...])
    h2_ref[0] = h2
    hn = _rms(h2, gm_ref[...]).astype(BF16)
    hn_ref[0] = hn
    lg_ref[0] = _dot(hn, wr_ref[...]) + br_ref[...]


def _cross(q, mk, mv, h, wo, gm, wr, br, heads, tm=256):
    b, s, d = h.shape
    n_mem = mk.shape[1]
    tm = _tile(s, tm)
    row = lambda i, j: (i, j, 0)
    mem = pl.BlockSpec((1, n_mem, d), lambda i, j: (i, 0, 0))
    kern = functools.partial(_cross_kernel, heads=heads, scale=float(d // heads) ** -0.5)
    return pl.pallas_call(
        kern,
        grid=(b, s // tm),
        in_specs=[pl.BlockSpec((1, tm, d), row), mem, mem, pl.BlockSpec((1, tm, d), row),
                  _resident(wo.shape), _resident(gm.shape), _resident(wr.shape), _resident(br.shape)],
        out_specs=[pl.BlockSpec((1, tm, d), row), pl.BlockSpec((1, tm, d), row), pl.BlockSpec((1, tm, LANE), row)],
        out_shape=[jax.ShapeDtypeStruct((b, s, d), F32), jax.ShapeDtypeStruct((b, s, d), BF16),
                   jax.ShapeDtypeStruct((b, s, LANE), F32)],
        compiler_params=_cp("parallel", "parallel"),
        name="cross_attn_router",
    )(q, mk, mv, h, wo, gm, wr, br)


def _moe_kernel(be_ref, on_ref, x_ref, sw_ref, wg_ref, wu_ref, wd_ref, y_ref):
    i = pl.program_id(0)

    @pl.when(on_ref[i] != 0)
    def _():
        xb = x_ref[...]
        g = _dot(xb, wg_ref[0])
        u = _dot(xb, wu_ref[0])
        hb = (g * _sigmoid(g) * u).astype(BF16)
        y_ref[...] = _dot(hb, wd_ref[0]) * sw_ref[...]

    @pl.when(on_ref[i] == 0)
    def _():
        y_ref[...] = jnp.zeros(y_ref.shape, y_ref.dtype)


def _moe_experts(blk_e, blk_on, x_slot, slot_w, wg, wu, wd):
    ns, d = x_slot.shape
    f = wg.shape[2]
    n_blk = ns // MOE_ROWS
    wspec = lambda shp: pl.BlockSpec((1,) + shp, lambda i, be, on: (be[i], 0, 0))
    grid_spec = pltpu.PrefetchScalarGridSpec(
        num_scalar_prefetch=2,
        grid=(n_blk,),
        in_specs=[pl.BlockSpec((MOE_ROWS, d), lambda i, be, on: (i, 0)),
                  pl.BlockSpec((MOE_ROWS, 1), lambda i, be, on: (i, 0)),
                  wspec((d, f)), wspec((d, f)), wspec((f, d))],
        out_specs=pl.BlockSpec((MOE_ROWS, d), lambda i, be, on: (i, 0)),
    )
    return pl.pallas_call(
        _moe_kernel,
        grid_spec=grid_spec,
        out_shape=jax.ShapeDtypeStruct((ns, d), F32),
        compiler_params=_cp("arbitrary"),
        name="moe_experts",
    )(blk_e, blk_on, x_slot, slot_w, wg, wu, wd)


def _final_kernel(h_ref, ya_ref, yb_ref, g_ref, o_ref):
    o_ref[...] = _rms(h_ref[...] + (ya_ref[...] + yb_ref[...]), g_ref[...])


def _final(h, y2, g, row_off, tm=512):
    t, d = h.shape
    tm = _tile(t, tm)
    assert row_off % tm == 0
    off = row_off // tm
    return pl.pallas_call(
        _final_kernel,
        grid=(t // tm,),
        in_specs=[pl.BlockSpec((tm, d), lambda i: (i, 0)),
                  pl.BlockSpec((tm, d), lambda i: (i + off, 0)), pl.BlockSpec((tm, d), lambda i: (i + off, 1)),
                  pl.BlockSpec((1, d), lambda i: (0, 0))],
        out_specs=pl.BlockSpec((tm, d), lambda i: (i, 0)),
        out_shape=jax.ShapeDtypeStruct((t, d), F32),
        compiler_params=_cp("parallel"),
        name="moe_combine_norm",
    )(h, y2, y2, g)


def _pad_cols(w, n):
    return jnp.pad(w, ((0, 0),) * (w.ndim - 1) + ((0, n - w.shape[-1]),))


def _pad_rows(w, n):
    return jnp.pad(w, ((0, n - w.shape[0]), (0, 0)))


def _rot_half_cols(w):
    half = w.shape[-1] // 2
    return jnp.concatenate([-w[..., half:], w[..., :half]], axis=-1)


def _rope_table(pos):
    half = ROPE // 2
    inv = ROPE_THETA ** (-jnp.arange(half, dtype=F32) / half)
    ang = pos.astype(F32)[:, None] * inv[None, :]
    cos, sin = jnp.cos(ang), jnp.sin(ang)
    return jnp.concatenate([cos, cos, sin, sin], axis=1)


def _prep_layer(lp, dims):
    ql, kl, rope, dim, dl, il, gl, d = (dims[k] for k in ("q_lora", "kv_lora", "rope", "rw_dim", "decay_lora",
                                                          "iclr_lora", "gate_lora", "d_model"))
    w_in = lp["w_in"]
    i0, i1, i2 = ql, ql + kl, ql + kl + rope
    i3 = i2 + 3 * dim + dl + il + gl
    w_kr = w_in[:, i1:i2]
    out = {}
    out["w_mla"] = jnp.concatenate(
        [w_in[:, :i0], w_kr, _rot_half_cols(w_kr), w_in[:, i0:i1]], axis=1).astype(BF16)
    rw = w_in[:, i2:i3]
    c3 = 3 * dim
    glp = -(-gl // LANE) * LANE

    def regroup(t):
        return jnp.concatenate([t[..., :c3], _pad_cols(t[..., c3:c3 + dl], LANE),
                                _pad_cols(t[..., c3 + dl:c3 + dl + il], LANE),
                                _pad_cols(t[..., c3 + dl + il:], glp)], axis=-1)

    out["regroup"] = regroup
    out["w_rw"] = regroup(rw).astype(BF16)
    out["mu"] = regroup(lp["rw_mu"][None, :])
    out["w_gates"] = w_in[:, i3:].astype(BF16)
    heads = lp["w_uq"].shape[1]
    nope = lp["w_uq"].shape[2] - rope
    wq = lp["w_uq"]
    out["w_q"] = jnp.concatenate([wq[..., :nope], wq[..., nope:], _rot_half_cols(wq[..., nope:])],
                                 axis=-1).reshape(ql, heads * 256).astype(BF16)
    out["w_uk"] = lp["w_uk"].reshape(kl, -1).astype(BF16)
    out["w_uv"] = lp["w_uv"].reshape(kl, -1).astype(BF16)
    out["wb"] = _pad_rows(lp["rw_wb"], LANE).astype(BF16)
    out["ab"] = _pad_rows(lp["rw_ab"], LANE).astype(BF16)
    out["gb"] = _pad_rows(lp["rw_gb"], glp).astype(BF16)
    n_heads = dims["rw_heads"]
    hd = dim // n_heads
    e = (jnp.arange(dim)[:, None] // hd == jnp.arange(LANE)[None, :]).astype(BF16)
    out["e"], out["et"] = e, e.T
    out["w_br_rwkv"] = lp["w_br_rwkv"].astype(BF16)
    out["w_br_mla"] = lp["w_br_mla"].astype(BF16)
    out["w_out"] = lp["w_out"].astype(BF16)
    out["w_mq"] = lp["w_mq"].reshape(d, d).astype(BF16)
    out["w_mkv"] = jnp.concatenate([lp["w_mk"].reshape(d, d), lp["w_mv"].reshape(d, d)], axis=1).astype(BF16)
    out["w_mo"] = lp["w_mo"].reshape(d, d).astype(BF16)
    n_g, n_e = lp["w_rg"].shape[1], lp["w_re"].shape[1]
    out["w_r"] = _pad_cols(jnp.concatenate([lp["w_rg"], lp["w_re"]], axis=1), LANE).astype(BF16)
    out["b_r"] = _pad_cols(jnp.concatenate([lp["b_rg"], lp["b_re"]])[None, :], LANE)
    out["w_eg"] = lp["w_eg"].astype(BF16)
    out["w_eu"] = lp["w_eu"].astype(BF16)
    out["w_ed"] = lp["w_ed"].astype(BF16)
    return out


def _row(v):
    return v.reshape(1, -1)


def _to_streams(t, heads):
    b, s, c = t.shape
    return t.reshape(b, s, heads, c // heads).transpose(1, 3, 0, 2).reshape(s, c // heads, b * heads)


def _from_streams(t, b):
    s, n, bh = t.shape
    return t.reshape(s, n, b, bh // b).transpose(2, 0, 3, 1).reshape(b * s, (bh // b) * n)


def _mixer(x, lp, wp, dims, pos, shift0, wkv0, attend):
    b, s, d = x.shape
    heads, rw_heads = dims["mla_heads"], dims["rw_heads"]
    gmix = _row(lp["g_mix"])
    cs = _rope_table(pos)
    scale = float(dims["nope"] + dims["rope"]) ** -0.5
    q, k, v, c, kr = _mla_proj(x, gmix, wp["w_mla"], _row(lp["g_q"]), wp["w_q"], _row(lp["g_kv"]),
                               wp["w_uk"], wp["w_uv"], cs, heads, scale)
    y_mla = attend(q, k, v, c)
    sh0 = wp["regroup"](shift0)
    r, dec, k2, vr, kk, bb, g, bonus, sh = _rwkv_pre(
        x, gmix, wp["w_rw"], sh0, wp["mu"], _row(lp["rw_w0"]), wp["wb"], _row(lp["rw_a0"]), wp["ab"], wp["gb"],
        _row(lp["rw_kk"]), _row(lp["rw_ka"]), _row(lp["rw_rk"]), wp["e"], wp["et"])
    n = dims["rw_dim"] // rw_heads
    s0 = wkv0.transpose(3, 2, 0, 1).reshape(n, n, b * rw_heads)
    o_t, s_t = _rwkv_scan(*(_to_streams(t, rw_heads) for t in (r, dec, k2, vr, kk, bb)), s0)
    wkv = s_t.reshape(n, n, b, rw_heads).transpose(2, 3, 1, 0)
    o = _from_streams(o_t, b)
    gates = _mm_norm(x.reshape(b * s, d), gmix, wp["w_gates"])
    dim = dims["rw_dim"]
    merged = _mix(o, bonus.reshape(b * s, dim), g.reshape(b * s, dim), y_mla.reshape(b * s, -1), gates,
                  _row(lp["rw_ln_g"]), _row(lp["rw_ln_b"]), wp["e"], wp["et"], wp["w_br_rwkv"], wp["w_br_mla"], n)
    h, qc = _outproj(x.reshape(b * s, d), merged, wp["w_out"], _row(lp["g_cross"]), wp["w_mq"])
    c3, dl, il = 3 * dim, dims["decay_lora"], dims["iclr_lora"]
    shift = jnp.concatenate([sh[..., :c3], sh[..., c3:c3 + dl], sh[..., c3 + LANE:c3 + LANE + il],
                             sh[..., c3 + 2 * LANE:c3 + 2 * LANE + dims["gate_lora"]]], axis=-1)
    return h.reshape(b, s, d), qc.reshape(b, s, d), c, kr, k, wkv, shift


def _route(logits, n_groups, per_group):
    lg = logits[:, :n_groups]
    g_sel = jnp.argmax(lg, axis=-1)
    p_grp = jnp.take_along_axis(jax.nn.softmax(lg, axis=-1), g_sel[:, None], axis=1)
    le = logits[:, n_groups:n_groups + n_groups * per_group].reshape(-1, n_groups, per_group)
    le = jnp.take_along_axis(le, g_sel[:, None, None], axis=1)[:, 0]
    top_l, top_i = lax.top_k(le, TOP_K)
    wts = p_grp * jax.nn.softmax(top_l, axis=-1)
    return g_sel[:, None] * per_group + top_i, wts


def _moe(hn, logits, wp, n_groups, per_group):
    t, d = hn.shape
    n_exp = n_groups * per_group
    e_idx, wts = _route(logits, n_groups, per_group)
    m = t * TOP_K
    n_blk = -(-(m + n_exp * (MOE_ROWS - 1)) // MOE_ROWS)
    flat_e = e_idx.reshape(m).astype(jnp.int32)
    order = jnp.argsort(flat_e)
    sorted_e = flat_e[order]
    counts = jnp.bincount(flat_e, length=n_exp)
    padded = (counts + MOE_ROWS - 1) // MOE_ROWS * MOE_ROWS
    pad_end = jnp.cumsum(padded)
    pad_start = pad_end - padded
    start = jnp.cumsum(counts) - counts
    dest = (pad_start[sorted_e] + jnp.arange(m) - start[sorted_e]).astype(jnp.int32)
    ns = n_blk * MOE_ROWS
    slot_tok = jnp.full((ns,), t, jnp.int32).at[dest].set((order // TOP_K).astype(jnp.int32))
    slot_w = jnp.zeros((ns,), F32).at[dest].set(wts.reshape(m)[order])
    blk_start = jnp.arange(n_blk) * MOE_ROWS
    blk_e = jnp.minimum(jnp.searchsorted(pad_end, blk_start, side="right"), n_exp - 1).astype(jnp.int32)
    blk_on = (blk_start < pad_end[-1]).astype(jnp.int32)
    x_pad = jnp.concatenate([hn, jnp.zeros((1, d), hn.dtype)], axis=0)
    x_slot = x_pad[slot_tok]
    y_slot = _moe_experts(blk_e, blk_on, x_slot, slot_w[:, None], wp["w_eg"], wp["w_eu"], wp["w_ed"])
    slot_of = jnp.zeros((m,), jnp.int32).at[order].set(dest)
    return y_slot[slot_of].reshape(t, TOP_K * d)


def kernel(x_prompt, x_sample, mem_prompt, cache_kv_latent, cache_k_rope, cache_mem_k, cache_mem_v, state_wkv, state_shift, g_mix, w_in, g_q, w_uq, g_kv, w_uk, w_uv, rw_mu, rw_w0, rw_wb, rw_a0, rw_ab, rw_gb, rw_kk, rw_ka, rw_rk, rw_ln_g, rw_ln_b, w_br_rwkv, w_br_mla, w_out, g_cross, g_mem, w_mq, w_mk, w_mv, w_mo, g_moe, w_rg, b_rg, w_re, b_re, w_eg, w_eu, w_ed, g_final):
    depth = g_mix.shape[0]
    b, s, d = x_prompt.shape
    bs, ss, _ = x_sample.shape
    past = cache_kv_latent.shape[2]
    n_mem = mem_prompt.shape[1]
    mem_heads = w_mq.shape[2]
    rw_heads, rw_hd = rw_rk.shape[1:]
    dims = dict(d_model=d, q_lora=g_q.shape[1], kv_lora=g_kv.shape[1], mla_heads=w_uq.shape[2],
                nope=w_uk.shape[3], rope=cache_k_rope.shape[3], rw_heads=rw_heads, rw_dim=rw_heads * rw_hd,
                decay_lora=rw_wb.shape[1], iclr_lora=rw_ab.shape[1], gate_lora=rw_gb.shape[1])
    assert dims["rope"] == ROPE and dims["nope"] == LANE and w_uv.shape[3] == LANE
    assert dims["decay_lora"] <= LANE and dims["iclr_lora"] <= LANE and rw_heads <= LANE
    n_groups, n_exp = w_rg.shape[2], w_re.shape[2]
    per_group = n_exp // n_groups
    stacked = dict(g_mix=g_mix, w_in=w_in, g_q=g_q, w_uq=w_uq, g_kv=g_kv, w_uk=w_uk, w_uv=w_uv, rw_mu=rw_mu,
                   rw_w0=rw_w0, rw_wb=rw_wb, rw_a0=rw_a0, rw_ab=rw_ab, rw_gb=rw_gb, rw_kk=rw_kk, rw_ka=rw_ka,
                   rw_rk=rw_rk, rw_ln_g=rw_ln_g, rw_ln_b=rw_ln_b, w_br_rwkv=w_br_rwkv, w_br_mla=w_br_mla,
                   w_out=w_out, g_cross=g_cross, g_mem=g_mem, w_mq=w_mq, w_mk=w_mk, w_mv=w_mv, w_mo=w_mo,
                   g_moe=g_moe, w_rg=w_rg, b_rg=b_rg, w_re=w_re, b_re=b_re, w_eg=w_eg, w_eu=w_eu, w_ed=w_ed)
    h_p, h_s = x_prompt, x_sample
    outs = [[] for _ in range(10)]
    for l in range(depth):
        lp = {name: val[l] for name, val in stacked.items()}
        wp = _prep_layer(lp, dims)

        h, qc, c, kr, _, wkv, shift = _mixer(
            h_p, lp, wp, dims, jnp.arange(s), jnp.zeros((b, 1, state_shift.shape[3]), F32),
            jnp.zeros((b, rw_heads, rw_hd, rw_hd), F32), lambda q, k, v, c_new: _mla_prompt_attn(q, k, v))
        mkv = _mm_norm(mem_prompt.reshape(b * n_mem, d), _row(lp["g_mem"]), wp["w_mkv"])
        mk, mv = mkv[:, :d].reshape(b, n_mem, d), mkv[:, d:].reshape(b, n_mem, d)
        gm, wr, br = _row(lp["g_moe"]), wp["w_r"], wp["b_r"]
        h2_p, hn_p, lg_p = _cross(qc, mk.astype(BF16), mv.astype(BF16), h, wp["w_mo"], gm, wr, br, mem_heads)
        for lst, val in zip(outs[:6], (c, kr, mk.reshape(b, n_mem, mem_heads, -1),
                                       mv.reshape(b, n_mem, mem_heads, -1), wkv, shift)):
            lst.append(val)

        kr_past = jnp.pad(cache_k_rope[l], ((0, 0), (0, 0), (0, LANE - ROPE))).astype(BF16)

        def attend_sample(q, k, v, c_new, l=l, kr_past=kr_past, wp=wp):
            return _mla_sample_attn(q, cache_kv_latent[l], kr_past, c_new, k, wp["w_uk"], wp["w_uv"])

        h, qc, c, kr, _, wkv, shift = _mixer(h_s, lp, wp, dims, past + jnp.arange(ss), state_shift[l],
                                             state_wkv[l], attend_sample)
        h2_s, hn_s, lg_s = _cross(qc, cache_mem_k[l].reshape(bs, n_mem, d).astype(BF16),
                                  cache_mem_v[l].reshape(bs, n_mem, d).astype(BF16), h, wp["w_mo"], gm, wr, br,
                                  mem_heads)
        for lst, val in zip(outs[6:], (c, kr, wkv, shift)):
            lst.append(val)

        tp, tsm = b * s, bs * ss
        hn = jnp.concatenate([hn_p.reshape(tp, d), hn_s.reshape(tsm, d)], axis=0)
        lg = jnp.concatenate([lg_p.reshape(tp, LANE), lg_s.reshape(tsm, LANE)], axis=0)
        y2 = _moe(hn, lg, wp, n_groups, per_group)
        last = l == depth - 1
        g_fin = _row(g_final) if last else None
        if last:
            h_p = _final(h2_p.reshape(tp, d), y2, g_fin, 0).reshape(b, s, d)
            h_s = _final(h2_s.reshape(tsm, d), y2, g_fin, tp).reshape(bs, ss, d)
        else:
            h_p = (h2_p.reshape(tp, d) + y2[:tp, :d] + y2[:tp, d:]).reshape(b, s, d)
            h_s = (h2_s.reshape(tsm, d) + y2[tp:, :d] + y2[tp:, d:]).reshape(bs, ss, d)
    stacks = [jnp.stack(o) for o in outs]
    return (h_p, h_s, *stacks)
```

```python
import functools

import jax
import jax.numpy as jnp
from jax import lax
from jax.experimental import pallas as pl
from jax.experimental.pallas import tpu as pltpu

F32 = jnp.float32
BF16 = jnp.bfloat16

EPS = 1e-6
GN_EPS = 64e-5
ROPE_THETA = 10000.0
CHUNK = 64
TOP_K = 2
NEG = -1e30

LANE = 128
V7X_VMEM_LIMIT = 56 << 20
ROPE = 64
MOE_ROWS = 256
SCAN_TS = 16
SCAN_UNROLL = 16


def _cp(*sem):
    return pltpu.CompilerParams(dimension_semantics=sem, vmem_limit_bytes=V7X_VMEM_LIMIT)


def _tile(n, pref):
    if n <= pref:
        return n
    for t in range(pref, 7, -1):
        if n % t == 0 and t % 8 == 0:
            return t
    return n


def _resident(shape):
    nd = len(shape)
    return pl.BlockSpec(shape, lambda *_: (0,) * nd, pipeline_mode=pl.Buffered(1))


def _rms(x, g):
    return x * lax.rsqrt(jnp.mean(x * x, axis=-1, keepdims=True) + EPS) * g


def _sigmoid(x):
    return 1.0 / (1.0 + jnp.exp(-x))


def _dot(a, b):
    return jnp.dot(a, b, preferred_element_type=F32)


def _dot_t(a, b):
    return lax.dot_general(a, b, (((1,), (1,)), ((), ())), preferred_element_type=F32)


def _seg_sum(x, e, et):
    hi = x.astype(BF16)
    lo = (x - hi.astype(F32)).astype(BF16)
    s = _dot(hi, e) + _dot(lo, e)
    shi = s.astype(BF16)
    slo = (s - shi.astype(F32)).astype(BF16)
    return _dot(shi, et) + _dot(slo, et)


def _mm_norm_kernel(x_ref, g_ref, w_ref, o_ref, xn_ref):
    @pl.when(pl.program_id(1) == 0)
    def _():
        xn_ref[...] = _rms(x_ref[...], g_ref[...]).astype(BF16)

    o_ref[...] = _dot(xn_ref[...], w_ref[...]).astype(o_ref.dtype)


def _mm_norm(x, g, w, tm=1024, tn=512, out_dtype=F32):
    m, k = x.shape
    n = w.shape[1]
    tm, tn = _tile(m, tm), _tile(n, tn)
    return pl.pallas_call(
        _mm_norm_kernel,
        grid=(m // tm, n // tn),
        in_specs=[pl.BlockSpec((tm, k), lambda i, j: (i, 0)),
                  pl.BlockSpec((1, k), lambda i, j: (0, 0)),
                  pl.BlockSpec((k, tn), lambda i, j: (0, j))],
        out_specs=pl.BlockSpec((tm, tn), lambda i, j: (i, j)),
        out_shape=jax.ShapeDtypeStruct((m, n), out_dtype),
        scratch_shapes=[pltpu.VMEM((tm, k), BF16)],
        compiler_params=_cp("parallel", "arbitrary"),
        name="mm_norm",
    )(x, g, w)


def _mla_proj_kernel(x_ref, gmix_ref, wm_ref, gq_ref, wq_ref, gkv_ref, wuk_ref, wuv_ref, cs_ref,
                     q_ref, k_ref, v_ref, c_ref, kr_ref, *, heads, q_lora, scale):
    u = _rms(x_ref[0], gmix_ref[...]).astype(BF16)
    p = _dot(u, wm_ref[...])
    cs = cs_ref[...]
    lane = lax.broadcasted_iota(jnp.int32, cs.shape, 1)

    def rope_tile(t2):
        t = t2 * cs
        return jnp.where(lane < ROPE, t + pltpu.roll(t, ROPE, axis=1), 0.0)

    krt = rope_tile(p[:, q_lora:q_lora + LANE])
    kr_ref[0] = krt[:, :ROPE]
    krb = krt.astype(BF16)
    c = _rms(p[:, q_lora + LANE:], gkv_ref[...])
    c_ref[0] = c
    cb = c.astype(BF16)
    kn = _dot(cb, wuk_ref[...])
    vv = _dot(cb, wuv_ref[...])
    qn = _rms(p[:, :q_lora], gq_ref[...]).astype(BF16)
    qf = _dot(qn, wq_ref[...])
    for h in range(heads):
        q_ref[0, h, :, :LANE] = (qf[:, h * 256:h * 256 + LANE] * scale).astype(BF16)
        q_ref[0, h, :, LANE:] = (rope_tile(qf[:, h * 256 + LANE:(h + 1) * 256]) * scale).astype(BF16)
        k_ref[0, h, :, :LANE] = kn[:, h * LANE:(h + 1) * LANE].astype(BF16)
        k_ref[0, h, :, LANE:] = krb
        v_ref[0, h] = vv[:, h * LANE:(h + 1) * LANE].astype(BF16)


def _mla_proj(x, gmix, wm, gq, wq, gkv, wuk, wuv, cs, heads, scale, tm=256):
    b, s, d = x.shape
    q_lora = gq.shape[1]
    kv_lora = gkv.shape[1]
    tm = _tile(s, tm)
    kern = functools.partial(_mla_proj_kernel, heads=heads, q_lora=q_lora, scale=scale)
    return pl.pallas_call(
        kern,
        grid=(b, s // tm),
        in_specs=[pl.BlockSpec((1, tm, d), lambda i, j: (i, j, 0)),
                  _resident(gmix.shape), _resident(wm.shape), _resident(gq.shape), _resident(wq.shape),
                  _resident(gkv.shape), _resident(wuk.shape), _resident(wuv.shape),
                  pl.BlockSpec((tm, LANE), lambda i, j: (j, 0))],
        out_specs=[pl.BlockSpec((1, heads, tm, 256), lambda i, j: (i, 0, j, 0)),
                   pl.BlockSpec((1, heads, tm, 256), lambda i, j: (i, 0, j, 0)),
                   pl.BlockSpec((1, heads, tm, LANE), lambda i, j: (i, 0, j, 0)),
                   pl.BlockSpec((1, tm, kv_lora), lambda i, j: (i, j, 0)),
                   pl.BlockSpec((1, tm, ROPE), lambda i, j: (i, j, 0))],
        out_shape=[jax.ShapeDtypeStruct((b, heads, s, 256), BF16),
                   jax.ShapeDtypeStruct((b, heads, s, 256), BF16),
                   jax.ShapeDtypeStruct((b, heads, s, LANE), BF16),
                   jax.ShapeDtypeStruct((b, s, kv_lora), F32),
                   jax.ShapeDtypeStruct((b, s, ROPE), F32)],
        compiler_params=_cp("parallel", "parallel"),
        name="mla_proj",
    )(x, gmix, wm, gq, wq, gkv, wuk, wuv, cs)


def _flash_kernel(q_ref, k_ref, v_ref, o_ref, *, tq):
    qi = pl.program_id(2)
    q = q_ref[0, 0]

    def tile(j, carry, diag):
        m, l, acc = carry
        start = pl.multiple_of(j * tq, tq)
        ks = k_ref[0, 0, pl.ds(start, tq), :]
        vs = v_ref[0, 0, pl.ds(start, tq), :]
        s = _dot_t(q, ks)
        if diag:
            qc = lax.broadcasted_iota(jnp.int32, s.shape, 0) // CHUNK
            kc = lax.broadcasted_iota(jnp.int32, s.shape, 1) // CHUNK
            s = jnp.where(kc <= qc, s, NEG)
        m_new = jnp.maximum(m, jnp.max(s, axis=-1, keepdims=True))
        alpha = jnp.exp(m - m_new)
        p = jnp.exp(s - m_new)
        l = l * alpha + jnp.sum(p, axis=-1, keepdims=True)
        acc = acc * alpha + _dot(p.astype(BF16), vs)
        return m_new, l, acc

    init = (jnp.full((tq, 1), NEG, F32), jnp.zeros((tq, 1), F32), jnp.zeros((tq, LANE), F32))
    carry = lax.fori_loop(0, qi, lambda j, c: tile(j, c, False), init)
    _, l, acc = tile(qi, carry, True)
    o_ref[0] = (acc / l).astype(o_ref.dtype)


def _mla_prompt_attn(q, k, v, tq=512):
    b, heads, s, _ = q.shape
    tq = _tile(s, tq)
    assert tq % CHUNK == 0
    return pl.pallas_call(
        functools.partial(_flash_kernel, tq=tq),
        grid=(b, heads, s // tq),
        in_specs=[pl.BlockSpec((1, 1, tq, 256), lambda i, h, j: (i, h, j, 0)),
                  pl.BlockSpec((1, 1, s, 256), lambda i, h, j: (i, h, 0, 0)),
                  pl.BlockSpec((1, 1, s, LANE), lambda i, h, j: (i, h, 0, 0))],
        out_specs=pl.BlockSpec((1, tq, LANE), lambda i, h, j: (i, j, h)),
        out_shape=jax.ShapeDtypeStruct((b, s, heads * LANE), BF16),
        compiler_params=_cp("parallel", "parallel", "arbitrary"),
        name="mla_prompt_attn",
    )(q, k, v)


def _mla_sample_kernel(q_ref, cp_ref, krp_ref, cn_ref, krn_ref, wuk_ref, wuv_ref, o_ref,
                       qa_ref, qr_ref, m_ref, l_ref, acc_ref, *, heads, sq):
    kb = pl.program_id(1)

    @pl.when(kb == 0)
    def _():
        for h in range(heads):
            qh = q_ref[0, h]
            qa_ref[h * sq:(h + 1) * sq, :] = _dot_t(qh[:, :LANE], wuk_ref[:, h * LANE:(h + 1) * LANE]).astype(BF16)
            qr_ref[h * sq:(h + 1) * sq, :] = qh[:, LANE:]
        m_ref[...] = jnp.full(m_ref.shape, NEG, F32)
        l_ref[...] = jnp.zeros(l_ref.shape, F32)
        acc_ref[...] = jnp.zeros(acc_ref.shape, F32)

    def update(cb, krb):
        s = _dot_t(qa_ref[...], cb) + _dot_t(qr_ref[...], krb)
        m_new = jnp.maximum(m_ref[...], jnp.max(s, axis=-1, keepdims=True))
        alpha = jnp.exp(m_ref[...] - m_new)
        p = jnp.exp(s - m_new)
        l_ref[...] = l_ref[...] * alpha + jnp.sum(p, axis=-1, keepdims=True)
        acc_ref[...] = acc_ref[...] * alpha + _dot(p.astype(BF16), cb)
        m_ref[...] = m_new

    update(cp_ref[0].astype(BF16), krp_ref[0])

    @pl.when(kb == pl.num_programs(1) - 1)
    def _():
        update(cn_ref[0].astype(BF16), krn_ref[0, 0])
        o_lat = (acc_ref[...] / l_ref[...]).astype(BF16)
        for h in range(heads):
            o_ref[0, :, h * LANE:(h + 1) * LANE] = _dot(
                o_lat[h * sq:(h + 1) * sq], wuv_ref[:, h * LANE:(h + 1) * LANE]).astype(o_ref.dtype)


def _mla_sample_attn(q, c_past, kr_past, c_new, k_new, wuk, wuv, tk=1024):
    b, heads, sq, _ = q.shape
    past, kv_lora = c_past.shape[1:]
    tk = _tile(past, tk)
    kern = functools.partial(_mla_sample_kernel, heads=heads, sq=sq)
    return pl.pallas_call(
        kern,
        grid=(b, past // tk),
        in_specs=[pl.BlockSpec((1, heads, sq, 256), lambda i, j: (i, 0, 0, 0)),
                  pl.BlockSpec((1, tk, kv_lora), lambda i, j: (i, j, 0)),
                  pl.BlockSpec((1, tk, LANE), lambda i, j: (i, j, 0)),
                  pl.BlockSpec((1, sq, kv_lora), lambda i, j: (i, 0, 0)),
                  pl.BlockSpec((1, 1, sq, LANE), lambda i, j: (i, 0, 0, 1)),
                  _resident(wuk.shape), _resident(wuv.shape)],
        out_specs=pl.BlockSpec((1, sq, heads * LANE), lambda i, j: (i, 0, 0)),
        out_shape=jax.ShapeDtypeStruct((b, sq, heads * LANE), BF16),
        scratch_shapes=[pltpu.VMEM((heads * sq, kv_lora), BF16), pltpu.VMEM((heads * sq, LANE), BF16),
                        pltpu.VMEM((heads * sq, 1), F32), pltpu.VMEM((heads * sq, 1), F32),
                        pltpu.VMEM((heads * sq, kv_lora), F32)],
        compiler_params=_cp("parallel", "arbitrary"),
        name="mla_sample_attn",
    )(q, c_past, kr_past, c_new, k_new, wuk, wuv)


def _rwkv_pre_kernel(x_ref, gmix_ref, w_ref, sh0_ref, mu_ref, w0_ref, wb_ref, a0_ref, ab_ref, gb_ref,
                     kkp_ref, ka_ref, rk_ref, e_ref, et_ref,
                     r_ref, dec_ref, k_ref, v_ref, kk_ref, b_ref, g_ref, bonus_ref, sh_ref, last_ref, *, dim):
    si = pl.program_id(1)

    def put(ref, val):
        nb, _, tiles, ts, _ = ref.shape
        for tl in range(tiles):
            ref[:, 0, tl] = val[:, tl * LANE:(tl + 1) * LANE].reshape(nb, ts, LANE)

    u = _rms(x_ref[0], gmix_ref[...]).astype(BF16)
    p = _dot(u, w_ref[...])
    tm = p.shape[0]

    @pl.when(si == 0)
    def _():
        last_ref[...] = sh0_ref[0]

    row = lax.broadcasted_iota(jnp.int32, p.shape, 0)
    prev = jnp.where(row == 0, last_ref[...], pltpu.roll(p, 1, axis=0))
    last_ref[...] = p[tm - 1:tm, :]
    sh_ref[0] = p[tm - 1:tm, :]
    xs = p + mu_ref[...] * (prev - p)
    r = xs[:, :dim]
    k = xs[:, dim:2 * dim]
    v = xs[:, 2 * dim:3 * dim]
    dw = xs[:, 3 * dim:3 * dim + LANE]
    da = xs[:, 3 * dim + LANE:3 * dim + 2 * LANE]
    dg = xs[:, 3 * dim + 2 * LANE:]
    z = -(w0_ref[...] + _dot(jnp.tanh(dw).astype(BF16), wb_ref[...]))
    softplus = jnp.maximum(z, 0.0) + jnp.log(1.0 + jnp.exp(-jnp.abs(z)))
    put(dec_ref, jnp.exp(-jnp.exp(-softplus - 0.5)))
    a = _sigmoid(a0_ref[...] + _dot(da.astype(BF16), ab_ref[...]))
    g_ref[0] = _dot(_sigmoid(dg).astype(BF16), gb_ref[...])
    kk = k * kkp_ref[...]
    nrm = jnp.sqrt(_seg_sum(kk * kk, e_ref[...], et_ref[...]))
    kk = kk / jnp.maximum(nrm, 1e-12)
    k2 = k * (1.0 + (a - 1.0) * ka_ref[...])
    put(r_ref, r)
    put(k_ref, k2)
    put(v_ref, v)
    put(kk_ref, kk)
    put(b_ref, kk * a)
    bonus_ref[0] = _seg_sum(r * k2 * rk_ref[...], e_ref[...], et_ref[...]) * v


def _rwkv_pre(x, gmix, w, sh0, mu, w0, wb, a0, ab, gb, kkp, ka, rk, e, et, tm=256):
    b, s, d = x.shape
    dim = w0.shape[1]
    width = w.shape[1]
    tm = _tile(s, tm)
    ts = min(SCAN_TS, s)
    assert tm % ts == 0 and dim % LANE == 0
    row = lambda i, j: (i, j, 0)
    act = pl.BlockSpec((1, tm, dim), row)
    act_shape = jax.ShapeDtypeStruct((b, s, dim), F32)
    seq = pl.BlockSpec((tm // ts, 1, dim // LANE, ts, LANE), lambda i, j: (j, i, 0, 0, 0))
    seq_shape = jax.ShapeDtypeStruct((s // ts, b, dim // LANE, ts, LANE), F32)
    consts = (gmix, w, mu, w0, wb, a0, ab, gb, kkp, ka, rk, e, et)
    return pl.pallas_call(
        functools.partial(_rwkv_pre_kernel, dim=dim),
        grid=(b, s // tm),
        in_specs=[pl.BlockSpec((1, tm, d), row), _resident(gmix.shape), _resident(w.shape),
                  pl.BlockSpec((1, 1, width), lambda i, j: (i, 0, 0))] + [_resident(c.shape) for c in consts[2:]],
        out_specs=[seq] * 6 + [act] * 2 + [pl.BlockSpec((1, 1, width), lambda i, j: (i, 0, 0))],
        out_shape=[seq_shape] * 6 + [act_shape] * 2 + [jax.ShapeDtypeStruct((b, 1, width), F32)],
        scratch_shapes=[pltpu.VMEM((1, width), F32)],
        compiler_params=_cp("parallel", "arbitrary"),
        name="rwkv_pre",
    )(x, gmix, w, sh0, *consts[2:])


def _rwkv_scan_kernel(r_ref, dec_ref, k_ref, v_ref, kk_ref, b_ref, s0_ref, o_ref, st_ref, xt_ref, ot_ref,
                      *, ts, n, unroll):
    tb = pl.program_id(1)
    hpl = LANE // n

    @pl.when(tb == 0)
    def _():
        st_ref[...] = s0_ref[...]

    srcs = (r_ref, dec_ref, k_ref, v_ref, kk_ref, b_ref)
    for i in range(ts):
        for a, ref in enumerate(srcs):
            xt_ref[a, i] = ref[pl.ds(i, LANE, stride=ts), :].T

    def step(i, c):
        for h2 in range(hpl):
            base = h2 * n
            vv = xt_ref[3, i, base:base + n, :]

            def p1(kx, sa):
                return sa + st_ref[h2, kx] * xt_ref[4, i, pl.ds(base + kx, 1), :]

            sa = lax.fori_loop(0, n, p1, jnp.zeros_like(vv), unroll=unroll)

            def p2(kx, o):
                s_new = (st_ref[h2, kx] * xt_ref[1, i, pl.ds(base + kx, 1), :]
                         - sa * xt_ref[5, i, pl.ds(base + kx, 1), :] + vv * xt_ref[2, i, pl.ds(base + kx, 1), :])
                st_ref[h2, kx] = s_new
                return o + s_new * xt_ref[0, i, pl.ds(base + kx, 1), :]

            ot_ref[i, base:base + n, :] = lax.fori_loop(0, n, p2, jnp.zeros_like(vv), unroll=unroll)
        return c

    lax.fori_loop(0, ts, step, 0)
    for i in range(ts):
        o_ref[pl.ds(i, LANE, stride=ts), :] = ot_ref[i].T


def _rwkv_scan(r, dec, k, v, kk, bb, s0):
    nt, b, tiles, ts, _ = r.shape
    n = s0.shape[2]
    flat = lambda t: t.reshape(nt, b * tiles * ts, LANE)
    bg = LANE // tiles
    assert b % bg == 0 and LANE % tiles == 0 and bg % 8 == 0
    groups = b // bg
    hpl = LANE // n
    seq = pl.BlockSpec((None, bg * tiles * ts, LANE), lambda g, t: (t, g, 0))
    state = pl.BlockSpec((None, hpl, n, n, LANE), lambda g, t: (g, 0, 0, 0, 0))
    o, st = pl.pallas_call(
        functools.partial(_rwkv_scan_kernel, ts=ts, n=n, unroll=SCAN_UNROLL),
        grid=(groups, nt),
        in_specs=[seq] * 6 + [state],
        out_specs=[seq, state],
        out_shape=[jax.ShapeDtypeStruct((nt, b * tiles * ts, LANE), F32), jax.ShapeDtypeStruct(s0.shape, F32)],
        scratch_shapes=[pltpu.VMEM((6, ts, LANE, LANE), F32), pltpu.VMEM((ts, LANE, LANE), F32)],
        compiler_params=_cp("parallel", "arbitrary"),
        name="rwkv_scan",
    )(*(flat(t) for t in (r, dec, k, v, kk, bb)), s0)
    return o.reshape(r.shape), st


def _state_to_streams(wkv):
    b, h, n = wkv.shape[:3]
    hpl = LANE // n
    tiles = h // hpl
    bg = LANE // tiles
    g = b // bg
    t = wkv.reshape(g, bg, tiles, hpl, n, n)
    return t.transpose(0, 3, 5, 4, 1, 2).reshape(g, hpl, n, n, LANE)


def _state_from_streams(st, b, h):
    g, hpl, n = st.shape[:3]
    tiles = h // hpl
    bg = LANE // tiles
    t = st.reshape(g, hpl, n, n, bg, tiles)
    return t.transpose(0, 4, 5, 1, 3, 2).reshape(b, h, n, n)


def _mix_kernel(o_ref, bonus_ref, g_ref, ymla_ref, grw_ref, gmla_ref, lng_ref, lnb_ref, e_ref, et_ref,
                wr_ref, wm_ref, out_ref, *, inv_n):
    nb, _, tiles, ts, _ = o_ref.shape
    o = jnp.concatenate([o_ref[:, 0, tl].reshape(nb * ts, LANE) for tl in range(tiles)], axis=1)
    e, et = e_ref[...], et_ref[...]
    d = o - _seg_sum(o, e, et) * inv_n
    var = _seg_sum(d * d, e, et) * inv_n
    on = d * lax.rsqrt(var + GN_EPS) * lng_ref[...] + lnb_ref[...]
    y = ((on + bonus_ref[0]) * g_ref[0]).astype(BF16)
    merged = _sigmoid(grw_ref[0]) * _dot(y, wr_ref[...]) + _sigmoid(gmla_ref[0]) * _dot(ymla_ref[0], wm_ref[...])
    out_ref[0] = merged.astype(out_ref.dtype)


def _mix(o, bonus, g, ymla, gates, lng, lnb, e, et, wr, wm, head_dim, tm=256):
    nt, b, tiles, ts, _ = o.shape
    s = nt * ts
    dim = tiles * LANE
    d = wr.shape[1]
    md = ymla.shape[2]
    tm = _tile(s, tm)
    assert tm % ts == 0
    row = lambda i, j: (i, j, 0)
    return pl.pallas_call(
        functools.partial(_mix_kernel, inv_n=1.0 / head_dim),
        grid=(b, s // tm),
        in_specs=[pl.BlockSpec((tm // ts, 1, tiles, ts, LANE), lambda i, j: (j, i, 0, 0, 0)),
                  pl.BlockSpec((1, tm, dim), row), pl.BlockSpec((1, tm, dim), row), pl.BlockSpec((1, tm, md), row),
                  pl.BlockSpec((1, tm, d), lambda i, j: (i, j, 0)), pl.BlockSpec((1, tm, d), lambda i, j: (i, j, 1))]
                 + [_resident(c.shape) for c in (lng, lnb, e, et, wr, wm)],
        out_specs=pl.BlockSpec((1, tm, d), row),
        out_shape=jax.ShapeDtypeStruct((b, s, d), BF16),
        compiler_params=_cp("parallel", "parallel"),
        name="rwkv_post_mix",
    )(o, bonus, g, ymla, gates, gates, lng, lnb, e, et, wr, wm)


def _outproj_kernel(x_ref, m_ref, wo_ref, gc_ref, wq_ref, h_ref, q_ref):
    h = x_ref[...] + _dot(m_ref[...], wo_ref[...])
    h_ref[...] = h
    q_ref[...] = _dot(_rms(h, gc_ref[...]).astype(BF16), wq_ref[...]).astype(q_ref.dtype)


def _outproj(x, merged, wo, gc, wq, tm=256):
    t, d = x.shape
    tm = _tile(t, tm)
    row = lambda i: (i, 0)
    return pl.pallas_call(
        _outproj_kernel,
        grid=(t // tm,),
        in_specs=[pl.BlockSpec((tm, d), row), pl.BlockSpec((tm, d), row),
                  _resident(wo.shape), _resident(gc.shape), _resident(wq.shape)],
        out_specs=[pl.BlockSpec((tm, d), row), pl.BlockSpec((tm, d), row)],
        out_shape=[jax.ShapeDtypeStruct((t, d), F32), jax.ShapeDtypeStruct((t, d), BF16)],
        compiler_params=_cp("parallel"),
        name="outproj_crossq",
    )(x, merged, wo, gc, wq)


def _cross_kernel(q_ref, mk_ref, mv_ref, h_ref, wo_ref, gm_ref, wr_ref, br_ref, h2_ref, hn_ref, lg_ref,
                  *, heads, scale):
    q = q_ref[0]
    mk = mk_ref[0]
    mv = mv_ref[0]
    hd = q.shape[1] // heads
    outs = []
    for h in range(heads):
        sl = slice(h * hd, (h + 1) * hd)
        s = _dot_t(q[:, sl], mk[:, sl]) * scale
        p = jnp.exp(s - jnp.max(s, axis=-1, keepdims=True))
        p = p / jnp.sum(p, axis=-1, keepdims=True)
        outs.append(_dot(p.astype(BF16), mv[:, sl]).astype(BF16))
    o = jnp.concatenate(outs, axis=1)
    h2 = h_ref[0] + _dot(o, wo_ref[...])
    h2_ref[0] = h2
    hn = _rms(h2, gm_ref[...]).astype(BF16)
    hn_ref[0] = hn
    lg_ref[0] = _dot(hn, wr_ref[...]) + br_ref[...]


def _cross(q, mk, mv, h, wo, gm, wr, br, heads, tm=256):
    b, s, d = h.shape
    n_mem = mk.shape[1]
    tm = _tile(s, tm)
    row = lambda i, j: (i, j, 0)
    mem = pl.BlockSpec((1, n_mem, d), lambda i, j: (i, 0, 0))
    kern = functools.partial(_cross_kernel, heads=heads, scale=float(d // heads) ** -0.5)
    return pl.pallas_call(
        kern,
        grid=(b, s // tm),
        in_specs=[pl.BlockSpec((1, tm, d), row), mem, mem, pl.BlockSpec((1, tm, d), row),
                  _resident(wo.shape), _resident(gm.shape), _resident(wr.shape), _resident(br.shape)],
        out_specs=[pl.BlockSpec((1, tm, d), row), pl.BlockSpec((1, tm, d), row), pl.BlockSpec((1, tm, LANE), row)],
        out_shape=[jax.ShapeDtypeStruct((b, s, d), F32), jax.ShapeDtypeStruct((b, s, d), BF16),
                   jax.ShapeDtypeStruct((b, s, LANE), F32)],
        compiler_params=_cp("parallel", "parallel"),
        name="cross_attn_router",
    )(q, mk, mv, h, wo, gm, wr, br)


def _moe_kernel(be_ref, on_ref, x_ref, sw_ref, wg_ref, wu_ref, wd_ref, y_ref):
    i = pl.program_id(0)

    @pl.when(on_ref[i] != 0)
    def _():
        xb = x_ref[...]
        g = _dot(xb, wg_ref[0])
        u = _dot(xb, wu_ref[0])
        hb = (g * _sigmoid(g) * u).astype(BF16)
        y_ref[...] = _dot(hb, wd_ref[0]) * sw_ref[...]

    @pl.when(on_ref[i] == 0)
    def _():
        y_ref[...] = jnp.zeros(y_ref.shape, y_ref.dtype)


def _moe_experts(blk_e, blk_on, x_slot, slot_w, wg, wu, wd):
    ns, d = x_slot.shape
    f = wg.shape[2]
    n_blk = ns // MOE_ROWS
    wspec = lambda shp: pl.BlockSpec((1,) + shp, lambda i, be, on: (be[i], 0, 0))
    grid_spec = pltpu.PrefetchScalarGridSpec(
        num_scalar_prefetch=2,
        grid=(n_blk,),
        in_specs=[pl.BlockSpec((MOE_ROWS, d), lambda i, be, on: (i, 0)),
                  pl.BlockSpec((MOE_ROWS, 1), lambda i, be, on: (i, 0)),
                  wspec((d, f)), wspec((d, f)), wspec((f, d))],
        out_specs=pl.BlockSpec((MOE_ROWS, d), lambda i, be, on: (i, 0)),
    )
    return pl.pallas_call(
        _moe_kernel,
        grid_spec=grid_spec,
        out_shape=jax.ShapeDtypeStruct((ns, d), F32),
        compiler_params=_cp("arbitrary"),
        name="moe_experts",
    )(blk_e, blk_on, x_slot, slot_w, wg, wu, wd)


def _final_kernel(h_ref, ya_ref, yb_ref, g_ref, o_ref):
    o_ref[...] = _rms(h_ref[...] + (ya_ref[...] + yb_ref[...]), g_ref[...])


def _final(h, ya, yb, g, row_off, tm=512):
    t, d = h.shape
    tm = _tile(t, tm)
    assert row_off % tm == 0
    off = row_off // tm
    return pl.pallas_call(
        _final_kernel,
        grid=(t // tm,),
        in_specs=[pl.BlockSpec((tm, d), lambda i: (i, 0)),
                  pl.BlockSpec((tm, d), lambda i: (i + off, 0)), pl.BlockSpec((tm, d), lambda i: (i + off, 0)),
                  pl.BlockSpec((1, d), lambda i: (0, 0))],
        out_specs=pl.BlockSpec((tm, d), lambda i: (i, 0)),
        out_shape=jax.ShapeDtypeStruct((t, d), F32),
        compiler_params=_cp("parallel"),
        name="moe_combine_norm",
    )(h, ya, yb, g)


def _pad_cols(w, n):
    return jnp.pad(w, ((0, 0),) * (w.ndim - 1) + ((0, n - w.shape[-1]),))


def _pad_rows(w, n):
    return jnp.pad(w, ((0, n - w.shape[0]), (0, 0)))


def _rot_half_cols(w):
    half = w.shape[-1] // 2
    return jnp.concatenate([-w[..., half:], w[..., :half]], axis=-1)


def _rope_table(pos):
    half = ROPE // 2
    inv = ROPE_THETA ** (-jnp.arange(half, dtype=F32) / half)
    ang = pos.astype(F32)[:, None] * inv[None, :]
    cos, sin = jnp.cos(ang), jnp.sin(ang)
    return jnp.concatenate([cos, cos, sin, sin], axis=1)


def _prep_layer(lp, dims):
    ql, kl, rope, dim, dl, il, gl, d = (dims[k] for k in ("q_lora", "kv_lora", "rope", "rw_dim", "decay_lora",
                                                          "iclr_lora", "gate_lora", "d_model"))
    w_in = lp["w_in"]
    i0, i1, i2 = ql, ql + kl, ql + kl + rope
    i3 = i2 + 3 * dim + dl + il + gl
    w_kr = w_in[:, i1:i2]
    out = {}
    out["w_mla"] = jnp.concatenate(
        [w_in[:, :i0], w_kr, _rot_half_cols(w_kr), w_in[:, i0:i1]], axis=1).astype(BF16)
    rw = w_in[:, i2:i3]
    c3 = 3 * dim
    glp = -(-gl // LANE) * LANE

    def regroup(t):
        return jnp.concatenate([t[..., :c3], _pad_cols(t[..., c3:c3 + dl], LANE),
                                _pad_cols(t[..., c3 + dl:c3 + dl + il], LANE),
                                _pad_cols(t[..., c3 + dl + il:], glp)], axis=-1)

    out["regroup"] = regroup
    out["w_rw"] = regroup(rw).astype(BF16)
    out["mu"] = regroup(lp["rw_mu"][None, :])
    out["w_gates"] = w_in[:, i3:].astype(BF16)
    heads = lp["w_uq"].shape[1]
    nope = lp["w_uq"].shape[2] - rope
    wq = lp["w_uq"]
    out["w_q"] = jnp.concatenate([wq[..., :nope], wq[..., nope:], _rot_half_cols(wq[..., nope:])],
                                 axis=-1).reshape(ql, heads * 256).astype(BF16)
    out["w_uk"] = lp["w_uk"].reshape(kl, -1).astype(BF16)
    out["w_uv"] = lp["w_uv"].reshape(kl, -1).astype(BF16)
    out["wb"] = _pad_rows(lp["rw_wb"], LANE).astype(BF16)
    out["ab"] = _pad_rows(lp["rw_ab"], LANE).astype(BF16)
    out["gb"] = _pad_rows(lp["rw_gb"], glp).astype(BF16)
    n_heads = dims["rw_heads"]
    hd = dim // n_heads
    e = (jnp.arange(dim)[:, None] // hd == jnp.arange(LANE)[None, :]).astype(BF16)
    out["e"], out["et"] = e, e.T
    out["w_br_rwkv"] = lp["w_br_rwkv"].astype(BF16)
    out["w_br_mla"] = lp["w_br_mla"].astype(BF16)
    out["w_out"] = lp["w_out"].astype(BF16)
    out["w_mq"] = lp["w_mq"].reshape(d, d).astype(BF16)
    out["w_mkv"] = jnp.concatenate([lp["w_mk"].reshape(d, d), lp["w_mv"].reshape(d, d)], axis=1).astype(BF16)
    out["w_mo"] = lp["w_mo"].reshape(d, d).astype(BF16)
    n_g, n_e = lp["w_rg"].shape[1], lp["w_re"].shape[1]
    out["w_r"] = _pad_cols(jnp.concatenate([lp["w_rg"], lp["w_re"]], axis=1), LANE).astype(BF16)
    out["b_r"] = _pad_cols(jnp.concatenate([lp["b_rg"], lp["b_re"]])[None, :], LANE)
    out["w_eg"] = lp["w_eg"].astype(BF16)
    out["w_eu"] = lp["w_eu"].astype(BF16)
    out["w_ed"] = lp["w_ed"].astype(BF16)
    return out


def _row(v):
    return v.reshape(1, -1)


def _mixer(x, lp, wp, dims, pos, shift0, wkv0, attend):
    b, s, d = x.shape
    heads, rw_heads = dims["mla_heads"], dims["rw_heads"]
    gmix = _row(lp["g_mix"])
    cs = _rope_table(pos)
    scale = float(dims["nope"] + dims["rope"]) ** -0.5
    q, k, v, c, kr = _mla_proj(x, gmix, wp["w_mla"], _row(lp["g_q"]), wp["w_q"], _row(lp["g_kv"]),
                               wp["w_uk"], wp["w_uv"], cs, heads, scale)
    y_mla = attend(q, k, v, c)
    sh0 = wp["regroup"](shift0)
    r, dec, k2, vr, kk, bb, g, bonus, sh = _rwkv_pre(
        x, gmix, wp["w_rw"], sh0, wp["mu"], _row(lp["rw_w0"]), wp["wb"], _row(lp["rw_a0"]), wp["ab"], wp["gb"],
        _row(lp["rw_kk"]), _row(lp["rw_ka"]), _row(lp["rw_rk"]), wp["e"], wp["et"])
    n = dims["rw_dim"] // rw_heads
    o, s_t = _rwkv_scan(r, dec, k2, vr, kk, bb, _state_to_streams(wkv0))
    wkv = _state_from_streams(s_t, b, rw_heads)
    gates = _mm_norm(x.reshape(b * s, d), gmix, wp["w_gates"]).reshape(b, s, 2 * d)
    dim = dims["rw_dim"]
    merged = _mix(o, bonus, g, y_mla, gates, _row(lp["rw_ln_g"]), _row(lp["rw_ln_b"]), wp["e"], wp["et"],
                  wp["w_br_rwkv"], wp["w_br_mla"], n).reshape(b * s, d)
    h, qc = _outproj(x.reshape(b * s, d), merged, wp["w_out"], _row(lp["g_cross"]), wp["w_mq"])
    c3, dl, il = 3 * dim, dims["decay_lora"], dims["iclr_lora"]
    shift = jnp.concatenate([sh[..., :c3], sh[..., c3:c3 + dl], sh[..., c3 + LANE:c3 + LANE + il],
                             sh[..., c3 + 2 * LANE:c3 + 2 * LANE + dims["gate_lora"]]], axis=-1)
    return h.reshape(b, s, d), qc.reshape(b, s, d), c, kr, k, wkv, shift


def _route(logits, n_groups, per_group):
    lg = logits[:, :n_groups]
    g_sel = jnp.argmax(lg, axis=-1)
    p_grp = jnp.take_along_axis(jax.nn.softmax(lg, axis=-1), g_sel[:, None], axis=1)
    le = logits[:, n_groups:n_groups + n_groups * per_group].reshape(-1, n_groups, per_group)
    le = jnp.take_along_axis(le, g_sel[:, None, None], axis=1)[:, 0]
    top_l, top_i = lax.top_k(le, TOP_K)
    wts = p_grp * jax.nn.softmax(top_l, axis=-1)
    return g_sel[:, None] * per_group + top_i, wts


def _moe(hn, logits, wp, n_groups, per_group):
    t, d = hn.shape
    n_exp = n_groups * per_group
    e_idx, wts = _route(logits, n_groups, per_group)
    m = t * TOP_K
    n_blk = -(-(m + n_exp * (MOE_ROWS - 1)) // MOE_ROWS)
    ns = n_blk * MOE_ROWS
    flat_e = e_idx.reshape(m).astype(jnp.int32)
    order = jnp.argsort(flat_e).astype(jnp.int32)
    counts = jnp.bincount(flat_e, length=n_exp).astype(jnp.int32)
    padded = (counts + MOE_ROWS - 1) // MOE_ROWS * MOE_ROWS
    pad_end = jnp.cumsum(padded)
    pad_start = pad_end - padded
    start = jnp.cumsum(counts) - counts
    blk_start = jnp.arange(n_blk, dtype=jnp.int32) * MOE_ROWS
    blk_e = jnp.minimum(jnp.sum(pad_end[None, :] <= blk_start[:, None], axis=1), n_exp - 1).astype(jnp.int32)
    blk_on = (blk_start < pad_end[-1]).astype(jnp.int32)
    slot_e = jnp.repeat(blk_e, MOE_ROWS)
    off = jnp.arange(ns, dtype=jnp.int32) - pad_start[slot_e]
    valid = (off < counts[slot_e]) & (jnp.repeat(blk_on, MOE_ROWS) != 0)
    asg = order[jnp.clip(start[slot_e] + off, 0, m - 1)]
    slot_tok = jnp.where(valid, asg // TOP_K, t)
    slot_w = jnp.where(valid, wts.reshape(m)[asg], 0.0)
    x_pad = jnp.concatenate([hn, jnp.zeros((1, d), hn.dtype)], axis=0)
    x_slot = x_pad[slot_tok]
    y_slot = _moe_experts(blk_e, blk_on, x_slot, slot_w[:, None], wp["w_eg"], wp["w_eu"], wp["w_ed"])
    rank = jnp.argsort(order).astype(jnp.int32)
    slot_of = (pad_start[flat_e] + rank - start[flat_e]).reshape(t, TOP_K)
    return [y_slot[slot_of[:, c]] for c in range(TOP_K)]


def kernel(x_prompt, x_sample, mem_prompt, cache_kv_latent, cache_k_rope, cache_mem_k, cache_mem_v, state_wkv, state_shift, g_mix, w_in, g_q, w_uq, g_kv, w_uk, w_uv, rw_mu, rw_w0, rw_wb, rw_a0, rw_ab, rw_gb, rw_kk, rw_ka, rw_rk, rw_ln_g, rw_ln_b, w_br_rwkv, w_br_mla, w_out, g_cross, g_mem, w_mq, w_mk, w_mv, w_mo, g_moe, w_rg, b_rg, w_re, b_re, w_eg, w_eu, w_ed, g_final):
    depth = g_mix.shape[0]
    b, s, d = x_prompt.shape
    bs, ss, _ = x_sample.shape
    past = cache_kv_latent.shape[2]
    n_mem = mem_prompt.shape[1]
    mem_heads = w_mq.shape[2]
    rw_heads, rw_hd = rw_rk.shape[1:]
    dims = dict(d_model=d, q_lora=g_q.shape[1], kv_lora=g_kv.shape[1], mla_heads=w_uq.shape[2],
                nope=w_uk.shape[3], rope=cache_k_rope.shape[3], rw_heads=rw_heads, rw_dim=rw_heads * rw_hd,
                decay_lora=rw_wb.shape[1], iclr_lora=rw_ab.shape[1], gate_lora=rw_gb.shape[1])
    assert dims["rope"] == ROPE and dims["nope"] == LANE and w_uv.shape[3] == LANE
    assert dims["decay_lora"] <= LANE and dims["iclr_lora"] <= LANE and rw_heads <= LANE
    n_groups, n_exp = w_rg.shape[2], w_re.shape[2]
    per_group = n_exp // n_groups
    stacked = dict(g_mix=g_mix, w_in=w_in, g_q=g_q, w_uq=w_uq, g_kv=g_kv, w_uk=w_uk, w_uv=w_uv, rw_mu=rw_mu,
                   rw_w0=rw_w0, rw_wb=rw_wb, rw_a0=rw_a0, rw_ab=rw_ab, rw_gb=rw_gb, rw_kk=rw_kk, rw_ka=rw_ka,
                   rw_rk=rw_rk, rw_ln_g=rw_ln_g, rw_ln_b=rw_ln_b, w_br_rwkv=w_br_rwkv, w_br_mla=w_br_mla,
                   w_out=w_out, g_cross=g_cross, g_mem=g_mem, w_mq=w_mq, w_mk=w_mk, w_mv=w_mv, w_mo=w_mo,
                   g_moe=g_moe, w_rg=w_rg, b_rg=b_rg, w_re=w_re, b_re=b_re, w_eg=w_eg, w_eu=w_eu, w_ed=w_ed)
    h_p, h_s = x_prompt, x_sample
    outs = [[] for _ in range(10)]
    for l in range(depth):
        lp = {name: val[l] for name, val in stacked.items()}
        wp = _prep_layer(lp, dims)

        h, qc, c, kr, _, wkv, shift = _mixer(
            h_p, lp, wp, dims, jnp.arange(s), jnp.zeros((b, 1, state_shift.shape[3]), F32),
            jnp.zeros((b, rw_heads, rw_hd, rw_hd), F32), lambda q, k, v, c_new: _mla_prompt_attn(q, k, v))
        mkv = _mm_norm(mem_prompt.reshape(b * n_mem, d), _row(lp["g_mem"]), wp["w_mkv"])
        mk, mv = mkv[:, :d].reshape(b, n_mem, d), mkv[:, d:].reshape(b, n_mem, d)
        gm, wr, br = _row(lp["g_moe"]), wp["w_r"], wp["b_r"]
        h2_p, hn_p, lg_p = _cross(qc, mk.astype(BF16), mv.astype(BF16), h, wp["w_mo"], gm, wr, br, mem_heads)
        for lst, val in zip(outs[:6], (c, kr, mk.reshape(b, n_mem, mem_heads, -1),
                                       mv.reshape(b, n_mem, mem_heads, -1), wkv, shift)):
            lst.append(val)

        kr_past = jnp.pad(cache_k_rope[l], ((0, 0), (0, 0), (0, LANE - ROPE))).astype(BF16)

        def attend_sample(q, k, v, c_new, l=l, kr_past=kr_past, wp=wp):
            return _mla_sample_attn(q, cache_kv_latent[l], kr_past, c_new, k, wp["w_uk"], wp["w_uv"])

        h, qc, c, kr, _, wkv, shift = _mixer(h_s, lp, wp, dims, past + jnp.arange(ss), state_shift[l],
                                             state_wkv[l], attend_sample)
        h2_s, hn_s, lg_s = _cross(qc, cache_mem_k[l].reshape(bs, n_mem, d).astype(BF16),
                                  cache_mem_v[l].reshape(bs, n_mem, d).astype(BF16), h, wp["w_mo"], gm, wr, br,
                                  mem_heads)
        for lst, val in zip(outs[6:], (c, kr, wkv, shift)):
            lst.append(val)

        tp, tsm = b * s, bs * ss
        hn = jnp.concatenate([hn_p.reshape(tp, d), hn_s.reshape(tsm, d)], axis=0)
        lg = jnp.concatenate([lg_p.reshape(tp, LANE), lg_s.reshape(tsm, LANE)], axis=0)
        ya, yb = _moe(hn, lg, wp, n_groups, per_group)
        last = l == depth - 1
        g_fin = _row(g_final) if last else None
        if last:
            h_p = _final(h2_p.reshape(tp, d), ya, yb, g_fin, 0).reshape(b, s, d)
            h_s = _final(h2_s.reshape(tsm, d), ya, yb, g_fin, tp).reshape(bs, ss, d)
        else:
            h_p = (h2_p.reshape(tp, d) + ya[:tp] + yb[:tp]).reshape(b, s, d)
            h_s = (h2_s.reshape(tsm, d) + ya[tp:] + yb[tp:]).reshape(bs, ss, d)
    stacks = [jnp.stack(o) for o in outs]
    return (h_p, h_s, *stacks)
```

```python
import functools

import jax
import jax.numpy as jnp
from jax import lax
from jax.experimental import pallas as pl
from jax.experimental.pallas import tpu as pltpu

F32 = jnp.float32
BF16 = jnp.bfloat16

EPS = 1e-6
GN_EPS = 64e-5
ROPE_THETA = 10000.0
CHUNK = 64
TOP_K = 2
NEG = -1e30

LANE = 128
V7X_VMEM_LIMIT = 56 << 20
ROPE = 64
MOE_ROWS = 256
SCAN_TS = 16
SCAN_UNROLL = 16


def _cp(*sem):
    return pltpu.CompilerParams(dimension_semantics=sem, vmem_limit_bytes=V7X_VMEM_LIMIT)


def _tile(n, pref):
    if n <= pref:
        return n
    for t in range(pref, 7, -1):
        if n % t == 0 and t % 8 == 0:
            return t
    return n


def _resident(shape):
    nd = len(shape)
    return pl.BlockSpec(shape, lambda *_: (0,) * nd, pipeline_mode=pl.Buffered(1))


def _rms(x, g):
    return x * lax.rsqrt(jnp.mean(x * x, axis=-1, keepdims=True) + EPS) * g


def _sigmoid(x):
    return 1.0 / (1.0 + jnp.exp(-x))


def _dot(a, b):
    return jnp.dot(a, b, preferred_element_type=F32)


def _dot_t(a, b):
    return lax.dot_general(a, b, (((1,), (1,)), ((), ())), preferred_element_type=F32)


def _seg_sum(x, e, et):
    hi = x.astype(BF16)
    lo = (x - hi.astype(F32)).astype(BF16)
    s = _dot(hi, e) + _dot(lo, e)
    shi = s.astype(BF16)
    slo = (s - shi.astype(F32)).astype(BF16)
    return _dot(shi, et) + _dot(slo, et)


def _mm_norm_kernel(x_ref, g_ref, w_ref, o_ref, xn_ref):
    @pl.when(pl.program_id(1) == 0)
    def _():
        xn_ref[...] = _rms(x_ref[...], g_ref[...]).astype(BF16)

    o_ref[...] = _dot(xn_ref[...], w_ref[...]).astype(o_ref.dtype)


def _mm_norm(x, g, w, tm=1024, tn=512, out_dtype=F32):
    m, k = x.shape
    n = w.shape[1]
    tm, tn = _tile(m, tm), _tile(n, tn)
    return pl.pallas_call(
        _mm_norm_kernel,
        grid=(m // tm, n // tn),
        in_specs=[pl.BlockSpec((tm, k), lambda i, j: (i, 0)),
                  pl.BlockSpec((1, k), lambda i, j: (0, 0)),
                  pl.BlockSpec((k, tn), lambda i, j: (0, j))],
        out_specs=pl.BlockSpec((tm, tn), lambda i, j: (i, j)),
        out_shape=jax.ShapeDtypeStruct((m, n), out_dtype),
        scratch_shapes=[pltpu.VMEM((tm, k), BF16)],
        compiler_params=_cp("parallel", "arbitrary"),
        name="mm_norm",
    )(x, g, w)


def _mla_proj_kernel(x_ref, gmix_ref, wm_ref, gq_ref, wq_ref, gkv_ref, wuk_ref, wuv_ref, cs_ref,
                     q_ref, k_ref, v_ref, c_ref, kr_ref, *, heads, q_lora, scale):
    u = _rms(x_ref[0], gmix_ref[...]).astype(BF16)
    p = _dot(u, wm_ref[...])
    cs = cs_ref[...]
    lane = lax.broadcasted_iota(jnp.int32, cs.shape, 1)

    def rope_tile(t2):
        t = t2 * cs
        return jnp.where(lane < ROPE, t + pltpu.roll(t, ROPE, axis=1), 0.0)

    krt = rope_tile(p[:, q_lora:q_lora + LANE])
    kr_ref[0] = krt[:, :ROPE]
    krb = krt.astype(BF16)
    c = _rms(p[:, q_lora + LANE:], gkv_ref[...])
    c_ref[0] = c
    cb = c.astype(BF16)
    kn = _dot(cb, wuk_ref[...])
    vt = _dot_t(wuv_ref[...], cb)
    qn = _rms(p[:, :q_lora], gq_ref[...]).astype(BF16)
    qf = _dot(qn, wq_ref[...])
    for h in range(heads):
        q_ref[0, h, :, :LANE] = (qf[:, h * 256:h * 256 + LANE] * scale).astype(BF16)
        q_ref[0, h, :, LANE:] = (rope_tile(qf[:, h * 256 + LANE:(h + 1) * 256]) * scale).astype(BF16)
        k_ref[0, h, :, :LANE] = kn[:, h * LANE:(h + 1) * LANE].astype(BF16)
        k_ref[0, h, :, LANE:] = krb
        v_ref[0, h] = vt[h * LANE:(h + 1) * LANE, :].astype(BF16)


def _mla_proj(x, gmix, wm, gq, wq, gkv, wuk, wuv, cs, heads, scale, tm=256):
    b, s, d = x.shape
    q_lora = gq.shape[1]
    kv_lora = gkv.shape[1]
    tm = _tile(s, tm)
    kern = functools.partial(_mla_proj_kernel, heads=heads, q_lora=q_lora, scale=scale)
    return pl.pallas_call(
        kern,
        grid=(b, s // tm),
        in_specs=[pl.BlockSpec((1, tm, d), lambda i, j: (i, j, 0)),
                  _resident(gmix.shape), _resident(wm.shape), _resident(gq.shape), _resident(wq.shape),
                  _resident(gkv.shape), _resident(wuk.shape), _resident(wuv.shape),
                  pl.BlockSpec((tm, LANE), lambda i, j: (j, 0))],
        out_specs=[pl.BlockSpec((1, heads, tm, 256), lambda i, j: (i, 0, j, 0)),
                   pl.BlockSpec((1, heads, tm, 256), lambda i, j: (i, 0, j, 0)),
                   pl.BlockSpec((1, heads, LANE, tm), lambda i, j: (i, 0, 0, j)),
                   pl.BlockSpec((1, tm, kv_lora), lambda i, j: (i, j, 0)),
                   pl.BlockSpec((1, tm, ROPE), lambda i, j: (i, j, 0))],
        out_shape=[jax.ShapeDtypeStruct((b, heads, s, 256), BF16),
                   jax.ShapeDtypeStruct((b, heads, s, 256), BF16),
                   jax.ShapeDtypeStruct((b, heads, LANE, s), BF16),
                   jax.ShapeDtypeStruct((b, s, kv_lora), F32),
                   jax.ShapeDtypeStruct((b, s, ROPE), F32)],
        compiler_params=_cp("parallel", "parallel"),
        name="mla_proj",
    )(x, gmix, wm, gq, wq, gkv, wuk, wuv, cs)


def _flash_kernel(q_ref, k_ref, vt_ref, o_ref, m_ref, l_ref, acc_ref, *, tq, hps):
    qi = pl.program_id(2)
    m_ref[...] = jnp.full(m_ref.shape, NEG, F32)
    l_ref[...] = jnp.zeros(l_ref.shape, F32)
    acc_ref[...] = jnp.zeros(acc_ref.shape, F32)

    def tile(j, diag):
        start = pl.multiple_of(j * tq, tq)
        scores = [_dot_t(k_ref[0, h, pl.ds(start, tq), :], q_ref[0, h]) for h in range(hps)]
        for h in range(hps):
            s = scores[h]
            if diag:
                kc = lax.broadcasted_iota(jnp.int32, s.shape, 0) // CHUNK
                qc = lax.broadcasted_iota(jnp.int32, s.shape, 1) // CHUNK
                s = jnp.where(kc <= qc, s, NEG)
            m = m_ref[h]
            m_new = jnp.maximum(m, jnp.max(s, axis=0, keepdims=True))
            alpha = jnp.exp(m - m_new)
            p = jnp.exp(s - m_new)
            l_ref[h] = l_ref[h] * alpha + jnp.sum(p, axis=0, keepdims=True)
            acc_ref[h] = acc_ref[h] * alpha + _dot(vt_ref[0, h, :, pl.ds(start, tq)], p.astype(BF16))
            m_ref[h] = m_new

    def body(j, c):
        tile(j, False)
        return c

    lax.fori_loop(0, qi, body, 0)
    tile(qi, True)
    for h in range(hps):
        o_ref[0, :, h * LANE:(h + 1) * LANE] = (acc_ref[h] * (1.0 / l_ref[h])).T.astype(o_ref.dtype)


def _mla_prompt_attn(q, k, vt, tq=512, hps=4):
    b, heads, s, _ = q.shape
    tq = _tile(s, tq)
    hps = min(hps, heads)
    assert tq % CHUNK == 0 and heads % hps == 0
    return pl.pallas_call(
        functools.partial(_flash_kernel, tq=tq, hps=hps),
        grid=(b, heads // hps, s // tq),
        in_specs=[pl.BlockSpec((1, hps, tq, 256), lambda i, h, j: (i, h, j, 0)),
                  pl.BlockSpec((1, hps, s, 256), lambda i, h, j: (i, h, 0, 0)),
                  pl.BlockSpec((1, hps, LANE, s), lambda i, h, j: (i, h, 0, 0))],
        out_specs=pl.BlockSpec((1, tq, hps * LANE), lambda i, h, j: (i, j, h)),
        out_shape=jax.ShapeDtypeStruct((b, s, heads * LANE), BF16),
        scratch_shapes=[pltpu.VMEM((hps, 1, tq), F32), pltpu.VMEM((hps, 1, tq), F32),
                        pltpu.VMEM((hps, LANE, tq), F32)],
        compiler_params=_cp("parallel", "parallel", "arbitrary"),
        name="mla_prompt_attn",
    )(q, k, vt)


def _mla_sample_kernel(q_ref, cp_ref, krp_ref, cn_ref, krn_ref, wuk_ref, wuv_ref, o_ref,
                       qa_ref, qr_ref, m_ref, l_ref, acc_ref, *, heads, sq):
    kb = pl.program_id(1)

    @pl.when(kb == 0)
    def _():
        for h in range(heads):
            qh = q_ref[0, h]
            qa_ref[h * sq:(h + 1) * sq, :] = _dot_t(qh[:, :LANE], wuk_ref[:, h * LANE:(h + 1) * LANE]).astype(BF16)
            qr_ref[h * sq:(h + 1) * sq, :] = qh[:, LANE:]
        m_ref[...] = jnp.full(m_ref.shape, NEG, F32)
        l_ref[...] = jnp.zeros(l_ref.shape, F32)
        acc_ref[...] = jnp.zeros(acc_ref.shape, F32)

    def update(cb, krb):
        s = _dot_t(qa_ref[...], cb) + _dot_t(qr_ref[...], krb)
        m_new = jnp.maximum(m_ref[...], jnp.max(s, axis=-1, keepdims=True))
        alpha = jnp.exp(m_ref[...] - m_new)
        p = jnp.exp(s - m_new)
        l_ref[...] = l_ref[...] * alpha + jnp.sum(p, axis=-1, keepdims=True)
        acc_ref[...] = acc_ref[...] * alpha + _dot(p.astype(BF16), cb)
        m_ref[...] = m_new

    update(cp_ref[0].astype(BF16), krp_ref[0])

    @pl.when(kb == pl.num_programs(1) - 1)
    def _():
        update(cn_ref[0].astype(BF16), krn_ref[0, 0])
        o_lat = (acc_ref[...] / l_ref[...]).astype(BF16)
        for h in range(heads):
            o_ref[0, :, h * LANE:(h + 1) * LANE] = _dot(
                o_lat[h * sq:(h + 1) * sq], wuv_ref[:, h * LANE:(h + 1) * LANE]).astype(o_ref.dtype)


def _mla_sample_attn(q, c_past, kr_past, c_new, k_new, wuk, wuv, tk=1024):
    b, heads, sq, _ = q.shape
    past, kv_lora = c_past.shape[1:]
    tk = _tile(past, tk)
    kern = functools.partial(_mla_sample_kernel, heads=heads, sq=sq)
    return pl.pallas_call(
        kern,
        grid=(b, past // tk),
        in_specs=[pl.BlockSpec((1, heads, sq, 256), lambda i, j: (i, 0, 0, 0)),
                  pl.BlockSpec((1, tk, kv_lora), lambda i, j: (i, j, 0)),
                  pl.BlockSpec((1, tk, LANE), lambda i, j: (i, j, 0)),
                  pl.BlockSpec((1, sq, kv_lora), lambda i, j: (i, 0, 0)),
                  pl.BlockSpec((1, 1, sq, LANE), lambda i, j: (i, 0, 0, 1)),
                  _resident(wuk.shape), _resident(wuv.shape)],
        out_specs=pl.BlockSpec((1, sq, heads * LANE), lambda i, j: (i, 0, 0)),
        out_shape=jax.ShapeDtypeStruct((b, sq, heads * LANE), BF16),
        scratch_shapes=[pltpu.VMEM((heads * sq, kv_lora), BF16), pltpu.VMEM((heads * sq, LANE), BF16),
                        pltpu.VMEM((heads * sq, 1), F32), pltpu.VMEM((heads * sq, 1), F32),
                        pltpu.VMEM((heads * sq, kv_lora), F32)],
        compiler_params=_cp("parallel", "arbitrary"),
        name="mla_sample_attn",
    )(q, c_past, kr_past, c_new, k_new, wuk, wuv)


def _rwkv_pre_kernel(x_ref, gmix_ref, w_ref, sh0_ref, mu_ref, w0_ref, wb_ref, a0_ref, ab_ref, gb_ref,
                     kkp_ref, ka_ref, rk_ref, e_ref, et_ref,
                     r_ref, dec_ref, k_ref, v_ref, kk_ref, b_ref, g_ref, bonus_ref, sh_ref, last_ref, *, dim):
    si = pl.program_id(1)

    def put(ref, val):
        nb, _, tiles, ts, _ = ref.shape
        for tl in range(tiles):
            ref[:, 0, tl] = val[:, tl * LANE:(tl + 1) * LANE].reshape(nb, ts, LANE)

    u = _rms(x_ref[0], gmix_ref[...]).astype(BF16)
    p = _dot(u, w_ref[...])
    tm = p.shape[0]

    @pl.when(si == 0)
    def _():
        last_ref[...] = sh0_ref[0]

    row = lax.broadcasted_iota(jnp.int32, p.shape, 0)
    prev = jnp.where(row == 0, last_ref[...], pltpu.roll(p, 1, axis=0))
    last_ref[...] = p[tm - 1:tm, :]
    sh_ref[0] = p[tm - 1:tm, :]
    xs = p + mu_ref[...] * (prev - p)
    r = xs[:, :dim]
    k = xs[:, dim:2 * dim]
    v = xs[:, 2 * dim:3 * dim]
    dw = xs[:, 3 * dim:3 * dim + LANE]
    da = xs[:, 3 * dim + LANE:3 * dim + 2 * LANE]
    dg = xs[:, 3 * dim + 2 * LANE:]
    z = -(w0_ref[...] + _dot(jnp.tanh(dw).astype(BF16), wb_ref[...]))
    softplus = jnp.maximum(z, 0.0) + jnp.log(1.0 + jnp.exp(-jnp.abs(z)))
    put(dec_ref, jnp.exp(-jnp.exp(-softplus - 0.5)))
    a = _sigmoid(a0_ref[...] + _dot(da.astype(BF16), ab_ref[...]))
    g_ref[0] = _dot(_sigmoid(dg).astype(BF16), gb_ref[...])
    kk = k * kkp_ref[...]
    nrm = jnp.sqrt(_seg_sum(kk * kk, e_ref[...], et_ref[...]))
    kk = kk / jnp.maximum(nrm, 1e-12)
    k2 = k * (1.0 + (a - 1.0) * ka_ref[...])
    put(r_ref, r)
    put(k_ref, k2)
    put(v_ref, v)
    put(kk_ref, kk)
    put(b_ref, kk * a)
    bonus_ref[0] = _seg_sum(r * k2 * rk_ref[...], e_ref[...], et_ref[...]) * v


def _rwkv_pre(x, gmix, w, sh0, mu, w0, wb, a0, ab, gb, kkp, ka, rk, e, et, tm=256):
    b, s, d = x.shape
    dim = w0.shape[1]
    width = w.shape[1]
    tm = _tile(s, tm)
    ts = min(SCAN_TS, s)
    assert tm % ts == 0 and dim % LANE == 0
    row = lambda i, j: (i, j, 0)
    act = pl.BlockSpec((1, tm, dim), row)
    act_shape = jax.ShapeDtypeStruct((b, s, dim), F32)
    seq = pl.BlockSpec((tm // ts, 1, dim // LANE, ts, LANE), lambda i, j: (j, i, 0, 0, 0))
    seq_shape = jax.ShapeDtypeStruct((s // ts, b, dim // LANE, ts, LANE), F32)
    consts = (gmix, w, mu, w0, wb, a0, ab, gb, kkp, ka, rk, e, et)
    return pl.pallas_call(
        functools.partial(_rwkv_pre_kernel, dim=dim),
        grid=(b, s // tm),
        in_specs=[pl.BlockSpec((1, tm, d), row), _resident(gmix.shape), _resident(w.shape),
                  pl.BlockSpec((1, 1, width), lambda i, j: (i, 0, 0))] + [_resident(c.shape) for c in consts[2:]],
        out_specs=[seq] * 6 + [act] * 2 + [pl.BlockSpec((1, 1, width), lambda i, j: (i, 0, 0))],
        out_shape=[seq_shape] * 6 + [act_shape] * 2 + [jax.ShapeDtypeStruct((b, 1, width), F32)],
        scratch_shapes=[pltpu.VMEM((1, width), F32)],
        compiler_params=_cp("parallel", "arbitrary"),
        name="rwkv_pre",
    )(x, gmix, w, sh0, *consts[2:])


def _rwkv_scan_kernel(r_ref, dec_ref, k_ref, v_ref, kk_ref, b_ref, s0_ref, o_ref, st_ref, xt_ref, ot_ref,
                      *, ts, n, unroll):
    tb = pl.program_id(1)
    hpl = LANE // n

    @pl.when(tb == 0)
    def _():
        st_ref[...] = s0_ref[...]

    srcs = (r_ref, dec_ref, k_ref, v_ref, kk_ref, b_ref)
    for i in range(ts):
        for a, ref in enumerate(srcs):
            xt_ref[a, i] = ref[pl.ds(i, LANE, stride=ts), :].T

    def step(i, c):
        for h2 in range(hpl):
            base = h2 * n
            vv = xt_ref[3, i, base:base + n, :]

            def p1(kx, sa):
                return sa + st_ref[h2, kx] * xt_ref[4, i, pl.ds(base + kx, 1), :]

            sa = lax.fori_loop(0, n, p1, jnp.zeros_like(vv), unroll=unroll)

            def p2(kx, o):
                s_new = (st_ref[h2, kx] * xt_ref[1, i, pl.ds(base + kx, 1), :]
                         - sa * xt_ref[5, i, pl.ds(base + kx, 1), :] + vv * xt_ref[2, i, pl.ds(base + kx, 1), :])
                st_ref[h2, kx] = s_new
                return o + s_new * xt_ref[0, i, pl.ds(base + kx, 1), :]

            ot_ref[i, base:base + n, :] = lax.fori_loop(0, n, p2, jnp.zeros_like(vv), unroll=unroll)
        return c

    lax.fori_loop(0, ts, step, 0)
    for i in range(ts):
        o_ref[pl.ds(i, LANE, stride=ts), :] = ot_ref[i].T


def _rwkv_scan(r, dec, k, v, kk, bb, s0):
    nt, b, tiles, ts, _ = r.shape
    n = s0.shape[2]
    flat = lambda t: t.reshape(nt, b * tiles * ts, LANE)
    bg = LANE // tiles
    assert b % bg == 0 and LANE % tiles == 0 and bg % 8 == 0
    groups = b // bg
    hpl = LANE // n
    seq = pl.BlockSpec((None, bg * tiles * ts, LANE), lambda g, t: (t, g, 0))
    state = pl.BlockSpec((None, hpl, n, n, LANE), lambda g, t: (g, 0, 0, 0, 0))
    o, st = pl.pallas_call(
        functools.partial(_rwkv_scan_kernel, ts=ts, n=n, unroll=SCAN_UNROLL),
        grid=(groups, nt),
        in_specs=[seq] * 6 + [state],
        out_specs=[seq, state],
        out_shape=[jax.ShapeDtypeStruct((nt, b * tiles * ts, LANE), F32), jax.ShapeDtypeStruct(s0.shape, F32)],
        scratch_shapes=[pltpu.VMEM((6, ts, LANE, LANE), F32), pltpu.VMEM((ts, LANE, LANE), F32)],
        compiler_params=_cp("parallel", "arbitrary"),
        name="rwkv_scan",
    )(*(flat(t) for t in (r, dec, k, v, kk, bb)), s0)
    return o.reshape(r.shape), st


def _state_to_streams(wkv):
    b, h, n = wkv.shape[:3]
    hpl = LANE // n
    tiles = h // hpl
    bg = LANE // tiles
    g = b // bg
    t = wkv.reshape(g, bg, tiles, hpl, n, n)
    return t.transpose(0, 3, 5, 4, 1, 2).reshape(g, hpl, n, n, LANE)


def _state_from_streams(st, b, h):
    g, hpl, n = st.shape[:3]
    tiles = h // hpl
    bg = LANE // tiles
    t = st.reshape(g, hpl, n, n, bg, tiles)
    return t.transpose(0, 4, 5, 1, 3, 2).reshape(b, h, n, n)


def _mix_kernel(o_ref, bonus_ref, g_ref, ymla_ref, grw_ref, gmla_ref, lng_ref, lnb_ref, e_ref, et_ref,
                wr_ref, wm_ref, out_ref, *, inv_n):
    nb, _, tiles, ts, _ = o_ref.shape
    o = jnp.concatenate([o_ref[:, 0, tl].reshape(nb * ts, LANE) for tl in range(tiles)], axis=1)
    e, et = e_ref[...], et_ref[...]
    d = o - _seg_sum(o, e, et) * inv_n
    var = _seg_sum(d * d, e, et) * inv_n
    on = d * lax.rsqrt(var + GN_EPS) * lng_ref[...] + lnb_ref[...]
    y = ((on + bonus_ref[0]) * g_ref[0]).astype(BF16)
    merged = _sigmoid(grw_ref[0]) * _dot(y, wr_ref[...]) + _sigmoid(gmla_ref[0]) * _dot(ymla_ref[0], wm_ref[...])
    out_ref[0] = merged.astype(out_ref.dtype)


def _mix(o, bonus, g, ymla, gates, lng, lnb, e, et, wr, wm, head_dim, tm=256):
    nt, b, tiles, ts, _ = o.shape
    s = nt * ts
    dim = tiles * LANE
    d = wr.shape[1]
    md = ymla.shape[2]
    tm = _tile(s, tm)
    assert tm % ts == 0
    row = lambda i, j: (i, j, 0)
    return pl.pallas_call(
        functools.partial(_mix_kernel, inv_n=1.0 / head_dim),
        grid=(b, s // tm),
        in_specs=[pl.BlockSpec((tm // ts, 1, tiles, ts, LANE), lambda i, j: (j, i, 0, 0, 0)),
                  pl.BlockSpec((1, tm, dim), row), pl.BlockSpec((1, tm, dim), row), pl.BlockSpec((1, tm, md), row),
                  pl.BlockSpec((1, tm, d), lambda i, j: (i, j, 0)), pl.BlockSpec((1, tm, d), lambda i, j: (i, j, 1))]
                 + [_resident(c.shape) for c in (lng, lnb, e, et, wr, wm)],
        out_specs=pl.BlockSpec((1, tm, d), row),
        out_shape=jax.ShapeDtypeStruct((b, s, d), BF16),
        compiler_params=_cp("parallel", "parallel"),
        name="rwkv_post_mix",
    )(o, bonus, g, ymla, gates, gates, lng, lnb, e, et, wr, wm)


def _outproj_kernel(x_ref, m_ref, wo_ref, gc_ref, wq_ref, h_ref, q_ref):
    h = x_ref[...] + _dot(m_ref[...], wo_ref[...])
    h_ref[...] = h
    q_ref[...] = _dot(_rms(h, gc_ref[...]).astype(BF16), wq_ref[...]).astype(q_ref.dtype)


def _outproj(x, merged, wo, gc, wq, tm=256):
    t, d = x.shape
    tm = _tile(t, tm)
    row = lambda i: (i, 0)
    return pl.pallas_call(
        _outproj_kernel,
        grid=(t // tm,),
        in_specs=[pl.BlockSpec((tm, d), row), pl.BlockSpec((tm, d), row),
                  _resident(wo.shape), _resident(gc.shape), _resident(wq.shape)],
        out_specs=[pl.BlockSpec((tm, d), row), pl.BlockSpec((tm, d), row)],
        out_shape=[jax.ShapeDtypeStruct((t, d), F32), jax.ShapeDtypeStruct((t, d), BF16)],
        compiler_params=_cp("parallel"),
        name="outproj_crossq",
    )(x, merged, wo, gc, wq)


def _route_tile(lg, n_groups, per_group):
    n_exp = n_groups * per_group
    lane = lax.broadcasted_iota(jnp.int32, lg.shape, 1)
    first = lambda mask: jnp.min(jnp.where(mask, lane, 2 * LANE), axis=-1, keepdims=True)
    top = lambda mask: jnp.max(jnp.where(mask, lg, NEG), axis=-1, keepdims=True)
    is_g = lane < n_groups
    gmax = top(is_g)
    g_sel = first(is_g & (lg == gmax))
    p_grp = 1.0 / jnp.sum(jnp.where(is_g, jnp.exp(lg - gmax), 0.0), axis=-1, keepdims=True)
    eid = lane - n_groups
    in_grp = (eid >= g_sel * per_group) & (eid < (g_sel + 1) * per_group)
    v1 = top(in_grp)
    e1 = first(in_grp & (lg == v1))
    rest = in_grp & (lane != e1)
    v2 = top(rest)
    e2 = first(rest & (lg == v2))
    t = jnp.exp(v2 - v1)
    w1 = p_grp / (1.0 + t)
    w2 = w1 * t
    route = jnp.where(lane == 0, (e1 - n_groups).astype(F32),
                      jnp.where(lane == 1, (e2 - n_groups).astype(F32),
                                jnp.where(lane == 2, w1, jnp.where(lane == 3, w2, 0.0))))
    hits = jnp.where((lane == e1 - n_groups) | (lane == e2 - n_groups), 1.0, 0.0)
    assert n_exp <= LANE
    return route, jnp.sum(hits, axis=0, keepdims=True)


def _cross_kernel(q_ref, mk_ref, mv_ref, h_ref, wo_ref, gm_ref, wr_ref, br_ref, h2_ref, hn_ref, rt_ref, cnt_ref,
                  *, heads, scale, n_groups, per_group):
    q = q_ref[0]
    mk = mk_ref[0]
    mv = mv_ref[0]
    hd = q.shape[1] // heads
    outs = []
    for h in range(heads):
        sl = slice(h * hd, (h + 1) * hd)
        s = _dot_t(q[:, sl], mk[:, sl]) * scale
        p = jnp.exp(s - jnp.max(s, axis=-1, keepdims=True))
        p = p / jnp.sum(p, axis=-1, keepdims=True)
        outs.append(_dot(p.astype(BF16), mv[:, sl]).astype(BF16))
    o = jnp.concatenate(outs, axis=1)
    h2 = h_ref[0] + _dot(o, wo_ref[...])
    h2_ref[0] = h2
    hn = _rms(h2, gm_ref[...]).astype(BF16)
    hn_ref[0] = hn
    route, hist = _route_tile(_dot(hn, wr_ref[...]) + br_ref[...], n_groups, per_group)
    rt_ref[0] = route
    cnt_ref[0, 0] = jnp.broadcast_to(hist, cnt_ref.shape[2:])


def _cross(q, mk, mv, h, wo, gm, wr, br, heads, n_groups, per_group, tm=256):
    b, s, d = h.shape
    n_mem = mk.shape[1]
    tm = _tile(s, tm)
    row = lambda i, j: (i, j, 0)
    mem = pl.BlockSpec((1, n_mem, d), lambda i, j: (i, 0, 0))
    kern = functools.partial(_cross_kernel, heads=heads, scale=float(d // heads) ** -0.5,
                             n_groups=n_groups, per_group=per_group)
    return pl.pallas_call(
        kern,
        grid=(b, s // tm),
        in_specs=[pl.BlockSpec((1, tm, d), row), mem, mem, pl.BlockSpec((1, tm, d), row),
                  _resident(wo.shape), _resident(gm.shape), _resident(wr.shape), _resident(br.shape)],
        out_specs=[pl.BlockSpec((1, tm, d), row), pl.BlockSpec((1, tm, d), row), pl.BlockSpec((1, tm, LANE), row),
                   pl.BlockSpec((1, 1, 8, LANE), lambda i, j: (i, j, 0, 0))],
        out_shape=[jax.ShapeDtypeStruct((b, s, d), F32), jax.ShapeDtypeStruct((b, s, d), BF16),
                   jax.ShapeDtypeStruct((b, s, LANE), F32), jax.ShapeDtypeStruct((b, s // tm, 8, LANE), F32)],
        compiler_params=_cp("parallel", "parallel"),
        name="cross_attn_router",
    )(q, mk, mv, h, wo, gm, wr, br)


def _moe_kernel(be_ref, on_ref, x_ref, sw_ref, wg_ref, wu_ref, wd_ref, y_ref, wgb_ref, wub_ref, wdb_ref):
    i = pl.program_id(0)

    @pl.when((i == 0) | (be_ref[i] != be_ref[jnp.maximum(i - 1, 0)]))
    def _():
        wgb_ref[...] = wg_ref[0].astype(BF16)
        wub_ref[...] = wu_ref[0].astype(BF16)
        wdb_ref[...] = wd_ref[0].astype(BF16)

    @pl.when(on_ref[i] != 0)
    def _():
        xb = x_ref[...]
        g = _dot(xb, wgb_ref[...])
        u = _dot(xb, wub_ref[...])
        hb = (g * _sigmoid(g) * u).astype(BF16)
        y_ref[...] = _dot(hb, wdb_ref[...]) * sw_ref[...]

    @pl.when(on_ref[i] == 0)
    def _():
        y_ref[...] = jnp.zeros(y_ref.shape, y_ref.dtype)


def _moe_experts(blk_e, blk_on, x_slot, slot_w, wg, wu, wd):
    ns, d = x_slot.shape
    f = wg.shape[2]
    n_blk = ns // MOE_ROWS
    wspec = lambda shp: pl.BlockSpec((1,) + shp, lambda i, be, on: (be[i], 0, 0))
    grid_spec = pltpu.PrefetchScalarGridSpec(
        num_scalar_prefetch=2,
        grid=(n_blk,),
        in_specs=[pl.BlockSpec((MOE_ROWS, d), lambda i, be, on: (i, 0)),
                  pl.BlockSpec((MOE_ROWS, 1), lambda i, be, on: (i, 0)),
                  wspec((d, f)), wspec((d, f)), wspec((f, d))],
        out_specs=pl.BlockSpec((MOE_ROWS, d), lambda i, be, on: (i, 0)),
        scratch_shapes=[pltpu.VMEM((d, f), BF16), pltpu.VMEM((d, f), BF16), pltpu.VMEM((f, d), BF16)],
    )
    return pl.pallas_call(
        _moe_kernel,
        grid_spec=grid_spec,
        out_shape=jax.ShapeDtypeStruct((ns, d), F32),
        compiler_params=_cp("arbitrary"),
        name="moe_experts",
    )(blk_e, blk_on, x_slot, slot_w, wg, wu, wd)


def _final_kernel(h_ref, ya_ref, yb_ref, g_ref, o_ref):
    o_ref[...] = _rms(h_ref[...] + (ya_ref[...] + yb_ref[...]), g_ref[...])


def _final(h, ya, yb, g, row_off, tm=512):
    t, d = h.shape
    tm = _tile(t, tm)
    assert row_off % tm == 0
    off = row_off // tm
    return pl.pallas_call(
        _final_kernel,
        grid=(t // tm,),
        in_specs=[pl.BlockSpec((tm, d), lambda i: (i, 0)),
                  pl.BlockSpec((tm, d), lambda i: (i + off, 0)), pl.BlockSpec((tm, d), lambda i: (i + off, 0)),
                  pl.BlockSpec((1, d), lambda i: (0, 0))],
        out_specs=pl.BlockSpec((tm, d), lambda i: (i, 0)),
        out_shape=jax.ShapeDtypeStruct((t, d), F32),
        compiler_params=_cp("parallel"),
        name="moe_combine_norm",
    )(h, ya, yb, g)


def _pad_cols(w, n):
    return jnp.pad(w, ((0, 0),) * (w.ndim - 1) + ((0, n - w.shape[-1]),))


def _pad_rows(w, n):
    return jnp.pad(w, ((0, n - w.shape[0]), (0, 0)))


def _rot_half_cols(w):
    half = w.shape[-1] // 2
    return jnp.concatenate([-w[..., half:], w[..., :half]], axis=-1)


def _rope_table(pos):
    half = ROPE // 2
    inv = ROPE_THETA ** (-jnp.arange(half, dtype=F32) / half)
    ang = pos.astype(F32)[:, None] * inv[None, :]
    cos, sin = jnp.cos(ang), jnp.sin(ang)
    return jnp.concatenate([cos, cos, sin, sin], axis=1)


def _prep_layer(lp, dims):
    ql, kl, rope, dim, dl, il, gl, d = (dims[k] for k in ("q_lora", "kv_lora", "rope", "rw_dim", "decay_lora",
                                                          "iclr_lora", "gate_lora", "d_model"))
    w_in = lp["w_in"]
    i0, i1, i2 = ql, ql + kl, ql + kl + rope
    i3 = i2 + 3 * dim + dl + il + gl
    w_kr = w_in[:, i1:i2]
    out = {}
    out["w_mla"] = jnp.concatenate(
        [w_in[:, :i0], w_kr, _rot_half_cols(w_kr), w_in[:, i0:i1]], axis=1).astype(BF16)
    rw = w_in[:, i2:i3]
    c3 = 3 * dim
    glp = -(-gl // LANE) * LANE

    def regroup(t):
        return jnp.concatenate([t[..., :c3], _pad_cols(t[..., c3:c3 + dl], LANE),
                                _pad_cols(t[..., c3 + dl:c3 + dl + il], LANE),
                                _pad_cols(t[..., c3 + dl + il:], glp)], axis=-1)

    out["regroup"] = regroup
    out["w_rw"] = regroup(rw).astype(BF16)
    out["mu"] = regroup(lp["rw_mu"][None, :])
    out["w_gates"] = w_in[:, i3:].astype(BF16)
    heads = lp["w_uq"].shape[1]
    nope = lp["w_uq"].shape[2] - rope
    wq = lp["w_uq"]
    out["w_q"] = jnp.concatenate([wq[..., :nope], wq[..., nope:], _rot_half_cols(wq[..., nope:])],
                                 axis=-1).reshape(ql, heads * 256).astype(BF16)
    out["w_uk"] = lp["w_uk"].reshape(kl, -1).astype(BF16)
    out["w_uv"] = lp["w_uv"].reshape(kl, -1).astype(BF16)
    out["w_uv_t"] = out["w_uv"].T
    out["wb"] = _pad_rows(lp["rw_wb"], LANE).astype(BF16)
    out["ab"] = _pad_rows(lp["rw_ab"], LANE).astype(BF16)
    out["gb"] = _pad_rows(lp["rw_gb"], glp).astype(BF16)
    n_heads = dims["rw_heads"]
    hd = dim // n_heads
    e = (jnp.arange(dim)[:, None] // hd == jnp.arange(LANE)[None, :]).astype(BF16)
    out["e"], out["et"] = e, e.T
    out["w_br_rwkv"] = lp["w_br_rwkv"].astype(BF16)
    out["w_br_mla"] = lp["w_br_mla"].astype(BF16)
    out["w_out"] = lp["w_out"].astype(BF16)
    out["w_mq"] = lp["w_mq"].reshape(d, d).astype(BF16)
    out["w_mkv"] = jnp.concatenate([lp["w_mk"].reshape(d, d), lp["w_mv"].reshape(d, d)], axis=1).astype(BF16)
    out["w_mo"] = lp["w_mo"].reshape(d, d).astype(BF16)
    n_g, n_e = lp["w_rg"].shape[1], lp["w_re"].shape[1]
    out["w_r"] = _pad_cols(jnp.concatenate([lp["w_rg"], lp["w_re"]], axis=1), LANE).astype(BF16)
    out["b_r"] = _pad_cols(jnp.concatenate([lp["b_rg"], lp["b_re"]])[None, :], LANE)
    out["w_eg"], out["w_eu"], out["w_ed"] = lp["w_eg"], lp["w_eu"], lp["w_ed"]
    return out


def _row(v):
    return v.reshape(1, -1)


def _mixer(x, lp, wp, dims, pos, shift0, wkv0, attend):
    b, s, d = x.shape
    heads, rw_heads = dims["mla_heads"], dims["rw_heads"]
    gmix = _row(lp["g_mix"])
    cs = _rope_table(pos)
    scale = float(dims["nope"] + dims["rope"]) ** -0.5
    q, k, v, c, kr = _mla_proj(x, gmix, wp["w_mla"], _row(lp["g_q"]), wp["w_q"], _row(lp["g_kv"]),
                               wp["w_uk"], wp["w_uv_t"], cs, heads, scale)
    y_mla = attend(q, k, v, c)
    sh0 = wp["regroup"](shift0)
    r, dec, k2, vr, kk, bb, g, bonus, sh = _rwkv_pre(
        x, gmix, wp["w_rw"], sh0, wp["mu"], _row(lp["rw_w0"]), wp["wb"], _row(lp["rw_a0"]), wp["ab"], wp["gb"],
        _row(lp["rw_kk"]), _row(lp["rw_ka"]), _row(lp["rw_rk"]), wp["e"], wp["et"])
    n = dims["rw_dim"] // rw_heads
    o, s_t = _rwkv_scan(r, dec, k2, vr, kk, bb, _state_to_streams(wkv0))
    wkv = _state_from_streams(s_t, b, rw_heads)
    gates = _mm_norm(x.reshape(b * s, d), gmix, wp["w_gates"]).reshape(b, s, 2 * d)
    dim = dims["rw_dim"]
    merged = _mix(o, bonus, g, y_mla, gates, _row(lp["rw_ln_g"]), _row(lp["rw_ln_b"]), wp["e"], wp["et"],
                  wp["w_br_rwkv"], wp["w_br_mla"], n).reshape(b * s, d)
    h, qc = _outproj(x.reshape(b * s, d), merged, wp["w_out"], _row(lp["g_cross"]), wp["w_mq"])
    c3, dl, il = 3 * dim, dims["decay_lora"], dims["iclr_lora"]
    shift = jnp.concatenate([sh[..., :c3], sh[..., c3:c3 + dl], sh[..., c3 + LANE:c3 + LANE + il],
                             sh[..., c3 + 2 * LANE:c3 + 2 * LANE + dims["gate_lora"]]], axis=-1)
    return h.reshape(b, s, d), qc.reshape(b, s, d), c, kr, k, wkv, shift


def _moe(hn, route, counts, wp):
    t, d = hn.shape
    n_exp = counts.shape[0]
    e_idx, wts = route[:, :TOP_K], route[:, TOP_K:2 * TOP_K]
    m = t * TOP_K
    n_blk = -(-(m + n_exp * (MOE_ROWS - 1)) // MOE_ROWS)
    ns = n_blk * MOE_ROWS
    flat_e = e_idx.reshape(m).astype(jnp.int32)
    order = jnp.argsort(flat_e).astype(jnp.int32)
    padded = (counts + MOE_ROWS - 1) // MOE_ROWS * MOE_ROWS
    pad_end = jnp.cumsum(padded)
    pad_start = pad_end - padded
    start = jnp.cumsum(counts) - counts
    blk_start = jnp.arange(n_blk, dtype=jnp.int32) * MOE_ROWS
    blk_e = jnp.minimum(jnp.sum(pad_end[None, :] <= blk_start[:, None], axis=1), n_exp - 1).astype(jnp.int32)
    blk_on = (blk_start < pad_end[-1]).astype(jnp.int32)
    slot_e = jnp.repeat(blk_e, MOE_ROWS)
    off = jnp.arange(ns, dtype=jnp.int32) - pad_start[slot_e]
    valid = (off < counts[slot_e]) & (jnp.repeat(blk_on, MOE_ROWS) != 0)
    asg = order[jnp.clip(start[slot_e] + off, 0, m - 1)]
    slot_tok = jnp.where(valid, asg // TOP_K, t)
    slot_w = jnp.where(valid, wts.reshape(m)[asg], 0.0)
    x_pad = jnp.concatenate([hn, jnp.zeros((1, d), hn.dtype)], axis=0)
    x_slot = x_pad[slot_tok]
    y_slot = _moe_experts(blk_e, blk_on, x_slot, slot_w[:, None], wp["w_eg"], wp["w_eu"], wp["w_ed"])
    rank = jnp.argsort(order).astype(jnp.int32)
    slot_of = (pad_start[flat_e] + rank - start[flat_e]).reshape(t, TOP_K)
    return [y_slot[slot_of[:, c]] for c in range(TOP_K)]


def kernel(x_prompt, x_sample, mem_prompt, cache_kv_latent, cache_k_rope, cache_mem_k, cache_mem_v, state_wkv, state_shift, g_mix, w_in, g_q, w_uq, g_kv, w_uk, w_uv, rw_mu, rw_w0, rw_wb, rw_a0, rw_ab, rw_gb, rw_kk, rw_ka, rw_rk, rw_ln_g, rw_ln_b, w_br_rwkv, w_br_mla, w_out, g_cross, g_mem, w_mq, w_mk, w_mv, w_mo, g_moe, w_rg, b_rg, w_re, b_re, w_eg, w_eu, w_ed, g_final):
    depth = g_mix.shape[0]
    b, s, d = x_prompt.shape
    bs, ss, _ = x_sample.shape
    past = cache_kv_latent.shape[2]
    n_mem = mem_prompt.shape[1]
    mem_heads = w_mq.shape[2]
    rw_heads, rw_hd = rw_rk.shape[1:]
    dims = dict(d_model=d, q_lora=g_q.shape[1], kv_lora=g_kv.shape[1], mla_heads=w_uq.shape[2],
                nope=w_uk.shape[3], rope=cache_k_rope.shape[3], rw_heads=rw_heads, rw_dim=rw_heads * rw_hd,
                decay_lora=rw_wb.shape[1], iclr_lora=rw_ab.shape[1], gate_lora=rw_gb.shape[1])
    assert dims["rope"] == ROPE and dims["nope"] == LANE and w_uv.shape[3] == LANE
    assert dims["decay_lora"] <= LANE and dims["iclr_lora"] <= LANE and rw_heads <= LANE
    n_groups, n_exp = w_rg.shape[2], w_re.shape[2]
    per_group = n_exp // n_groups
    stacked = dict(g_mix=g_mix, w_in=w_in, g_q=g_q, w_uq=w_uq, g_kv=g_kv, w_uk=w_uk, w_uv=w_uv, rw_mu=rw_mu,
                   rw_w0=rw_w0, rw_wb=rw_wb, rw_a0=rw_a0, rw_ab=rw_ab, rw_gb=rw_gb, rw_kk=rw_kk, rw_ka=rw_ka,
                   rw_rk=rw_rk, rw_ln_g=rw_ln_g, rw_ln_b=rw_ln_b, w_br_rwkv=w_br_rwkv, w_br_mla=w_br_mla,
                   w_out=w_out, g_cross=g_cross, g_mem=g_mem, w_mq=w_mq, w_mk=w_mk, w_mv=w_mv, w_mo=w_mo,
                   g_moe=g_moe, w_rg=w_rg, b_rg=b_rg, w_re=w_re, b_re=b_re, w_eg=w_eg, w_eu=w_eu, w_ed=w_ed)
    h_p, h_s = x_prompt, x_sample
    outs = [[] for _ in range(10)]
    for l in range(depth):
        lp = {name: val[l] for name, val in stacked.items()}
        wp = _prep_layer(lp, dims)

        h, qc, c, kr, _, wkv, shift = _mixer(
            h_p, lp, wp, dims, jnp.arange(s), jnp.zeros((b, 1, state_shift.shape[3]), F32),
            jnp.zeros((b, rw_heads, rw_hd, rw_hd), F32), lambda q, k, v, c_new: _mla_prompt_attn(q, k, v))
        mkv = _mm_norm(mem_prompt.reshape(b * n_mem, d), _row(lp["g_mem"]), wp["w_mkv"])
        mk, mv = mkv[:, :d].reshape(b, n_mem, d), mkv[:, d:].reshape(b, n_mem, d)
        gm, wr, br = _row(lp["g_moe"]), wp["w_r"], wp["b_r"]
        h2_p, hn_p, rt_p, cnt_p = _cross(qc, mk.astype(BF16), mv.astype(BF16), h, wp["w_mo"], gm, wr, br, mem_heads,
                                         n_groups, per_group)
        for lst, val in zip(outs[:6], (c, kr, mk.reshape(b, n_mem, mem_heads, -1),
                                       mv.reshape(b, n_mem, mem_heads, -1), wkv, shift)):
            lst.append(val)

        kr_past = jnp.pad(cache_k_rope[l], ((0, 0), (0, 0), (0, LANE - ROPE))).astype(BF16)

        def attend_sample(q, k, v, c_new, l=l, kr_past=kr_past, wp=wp):
            return _mla_sample_attn(q, cache_kv_latent[l], kr_past, c_new, k, wp["w_uk"], wp["w_uv"])

        h, qc, c, kr, _, wkv, shift = _mixer(h_s, lp, wp, dims, past + jnp.arange(ss), state_shift[l],
                                             state_wkv[l], attend_sample)
        h2_s, hn_s, rt_s, cnt_s = _cross(qc, cache_mem_k[l].reshape(bs, n_mem, d).astype(BF16),
                                         cache_mem_v[l].reshape(bs, n_mem, d).astype(BF16), h, wp["w_mo"], gm, wr, br,
                                         mem_heads, n_groups, per_group)
        for lst, val in zip(outs[6:], (c, kr, wkv, shift)):
            lst.append(val)

        tp, tsm = b * s, bs * ss
        hn = jnp.concatenate([hn_p.reshape(tp, d), hn_s.reshape(tsm, d)], axis=0)
        route = jnp.concatenate([rt_p.reshape(tp, LANE), rt_s.reshape(tsm, LANE)], axis=0)
        counts = (jnp.sum(cnt_p[:, :, 0, :n_exp], axis=(0, 1)) + jnp.sum(cnt_s[:, :, 0, :n_exp], axis=(0, 1)))
        ya, yb = _moe(hn, route, counts.astype(jnp.int32), wp)
        last = l == depth - 1
        g_fin = _row(g_final) if last else None
        if last:
            h_p = _final(h2_p.reshape(tp, d), ya, yb, g_fin, 0).reshape(b, s, d)
            h_s = _final(h2_s.reshape(tsm, d), ya, yb, g_fin, tp).reshape(bs, ss, d)
        else:
            h_p = (h2_p.reshape(tp, d) + ya[:tp] + yb[:tp]).reshape(b, s, d)
            h_s = (h2_s.reshape(tsm, d) + ya[tp:] + yb[tp:]).reshape(bs, ss, d)
    stacks = [jnp.stack(o) for o in outs]
    return (h_p, h_s, *stacks)
```

```python
import functools

import jax
import jax.numpy as jnp
from jax import lax
from jax.experimental import pallas as pl
from jax.experimental.pallas import tpu as pltpu

F32 = jnp.float32
BF16 = jnp.bfloat16

EPS = 1e-6
GN_EPS = 64e-5
ROPE_THETA = 10000.0
CHUNK = 64
TOP_K = 2
NEG = -1e30

LANE = 128
V7X_VMEM_LIMIT = 56 << 20
ROPE = 64
MOE_ROWS = 256
SCAN_TS = 16
SCAN_UNROLL = 16


def _cp(*sem):
    return pltpu.CompilerParams(dimension_semantics=sem, vmem_limit_bytes=V7X_VMEM_LIMIT)


def _tile(n, pref):
    if n <= pref:
        return n
    for t in range(pref, 7, -1):
        if n % t == 0 and t % 8 == 0:
            return t
    return n


def _resident(shape):
    nd = len(shape)
    return pl.BlockSpec(shape, lambda *_: (0,) * nd, pipeline_mode=pl.Buffered(1))


def _rms(x, g):
    return x * lax.rsqrt(jnp.mean(x * x, axis=-1, keepdims=True) + EPS) * g


def _sigmoid(x):
    return 1.0 / (1.0 + jnp.exp(-x))


def _dot(a, b):
    return jnp.dot(a, b, preferred_element_type=F32)


def _dot_t(a, b):
    return lax.dot_general(a, b, (((1,), (1,)), ((), ())), preferred_element_type=F32)


def _seg_sum(x, e, et):
    hi = x.astype(BF16)
    lo = (x - hi.astype(F32)).astype(BF16)
    s = _dot(hi, e) + _dot(lo, e)
    shi = s.astype(BF16)
    slo = (s - shi.astype(F32)).astype(BF16)
    return _dot(shi, et) + _dot(slo, et)


def _mm_norm_kernel(x_ref, g_ref, w_ref, o_ref, xn_ref):
    @pl.when(pl.program_id(1) == 0)
    def _():
        xn_ref[...] = _rms(x_ref[...], g_ref[...]).astype(BF16)

    o_ref[...] = _dot(xn_ref[...], w_ref[...]).astype(o_ref.dtype)


def _mm_norm(x, g, w, tm=1024, tn=512, out_dtype=F32):
    m, k = x.shape
    n = w.shape[1]
    tm, tn = _tile(m, tm), _tile(n, tn)
    return pl.pallas_call(
        _mm_norm_kernel,
        grid=(m // tm, n // tn),
        in_specs=[pl.BlockSpec((tm, k), lambda i, j: (i, 0)),
                  pl.BlockSpec((1, k), lambda i, j: (0, 0)),
                  pl.BlockSpec((k, tn), lambda i, j: (0, j))],
        out_specs=pl.BlockSpec((tm, tn), lambda i, j: (i, j)),
        out_shape=jax.ShapeDtypeStruct((m, n), out_dtype),
        scratch_shapes=[pltpu.VMEM((tm, k), BF16)],
        compiler_params=_cp("parallel", "arbitrary"),
        name="mm_norm",
    )(x, g, w)


def _mla_proj_kernel(x_ref, gmix_ref, wm_ref, gq_ref, wq_ref, gkv_ref, wuk_ref, wuv_ref, cs_ref,
                     q_ref, k_ref, v_ref, c_ref, kr_ref, *, heads, q_lora, scale):
    u = _rms(x_ref[0], gmix_ref[...]).astype(BF16)
    p = _dot(u, wm_ref[...])
    cs = cs_ref[...]
    lane = lax.broadcasted_iota(jnp.int32, cs.shape, 1)

    def rope_tile(t2):
        t = t2 * cs
        return jnp.where(lane < ROPE, t + pltpu.roll(t, ROPE, axis=1), 0.0)

    krt = rope_tile(p[:, q_lora:q_lora + LANE])
    kr_ref[0] = krt[:, :ROPE]
    krb = krt.astype(BF16)
    c = _rms(p[:, q_lora + LANE:], gkv_ref[...])
    c_ref[0] = c
    cb = c.astype(BF16)
    kn = _dot(cb, wuk_ref[...])
    vt = _dot_t(wuv_ref[...], cb)
    qn = _rms(p[:, :q_lora], gq_ref[...]).astype(BF16)
    qf = _dot(qn, wq_ref[...])
    for h in range(heads):
        q_ref[0, h, :, :LANE] = (qf[:, h * 256:h * 256 + LANE] * scale).astype(BF16)
        q_ref[0, h, :, LANE:] = (rope_tile(qf[:, h * 256 + LANE:(h + 1) * 256]) * scale).astype(BF16)
        k_ref[0, h, :, :LANE] = kn[:, h * LANE:(h + 1) * LANE].astype(BF16)
        k_ref[0, h, :, LANE:] = krb
        v_ref[0, h] = vt[h * LANE:(h + 1) * LANE, :].astype(BF16)


def _mla_proj(x, gmix, wm, gq, wq, gkv, wuk, wuv, cs, heads, scale, tm=256):
    b, s, d = x.shape
    q_lora = gq.shape[1]
    kv_lora = gkv.shape[1]
    tm = _tile(s, tm)
    kern = functools.partial(_mla_proj_kernel, heads=heads, q_lora=q_lora, scale=scale)
    return pl.pallas_call(
        kern,
        grid=(b, s // tm),
        in_specs=[pl.BlockSpec((1, tm, d), lambda i, j: (i, j, 0)),
                  _resident(gmix.shape), _resident(wm.shape), _resident(gq.shape), _resident(wq.shape),
                  _resident(gkv.shape), _resident(wuk.shape), _resident(wuv.shape),
                  pl.BlockSpec((tm, LANE), lambda i, j: (j, 0))],
        out_specs=[pl.BlockSpec((1, heads, tm, 256), lambda i, j: (i, 0, j, 0)),
                   pl.BlockSpec((1, heads, tm, 256), lambda i, j: (i, 0, j, 0)),
                   pl.BlockSpec((1, heads, LANE, tm), lambda i, j: (i, 0, 0, j)),
                   pl.BlockSpec((1, tm, kv_lora), lambda i, j: (i, j, 0)),
                   pl.BlockSpec((1, tm, ROPE), lambda i, j: (i, j, 0))],
        out_shape=[jax.ShapeDtypeStruct((b, heads, s, 256), BF16),
                   jax.ShapeDtypeStruct((b, heads, s, 256), BF16),
                   jax.ShapeDtypeStruct((b, heads, LANE, s), BF16),
                   jax.ShapeDtypeStruct((b, s, kv_lora), F32),
                   jax.ShapeDtypeStruct((b, s, ROPE), F32)],
        compiler_params=_cp("parallel", "parallel"),
        name="mla_proj",
    )(x, gmix, wm, gq, wq, gkv, wuk, wuv, cs)


def _flash_kernel(q_ref, k_ref, vt_ref, o_ref, m_ref, l_ref, acc_ref, *, tq, hps):
    qi = pl.program_id(2)
    m_ref[...] = jnp.full(m_ref.shape, NEG, F32)
    l_ref[...] = jnp.zeros(l_ref.shape, F32)
    acc_ref[...] = jnp.zeros(acc_ref.shape, F32)

    def tile(j, diag):
        start = pl.multiple_of(j * tq, tq)
        scores = [_dot_t(k_ref[0, h, pl.ds(start, tq), :], q_ref[0, h]) for h in range(hps)]
        for h in range(hps):
            s = scores[h]
            if diag:
                kc = lax.broadcasted_iota(jnp.int32, s.shape, 0) // CHUNK
                qc = lax.broadcasted_iota(jnp.int32, s.shape, 1) // CHUNK
                s = jnp.where(kc <= qc, s, NEG)
            m = m_ref[h]
            m_new = jnp.maximum(m, jnp.max(s, axis=0, keepdims=True))
            alpha = jnp.exp(m - m_new)
            p = jnp.exp(s - m_new)
            l_ref[h] = l_ref[h] * alpha + jnp.sum(p, axis=0, keepdims=True)
            acc_ref[h] = acc_ref[h] * alpha + _dot(vt_ref[0, h, :, pl.ds(start, tq)], p.astype(BF16))
            m_ref[h] = m_new

    def body(j, c):
        tile(j, False)
        return c

    lax.fori_loop(0, qi, body, 0)
    tile(qi, True)
    for h in range(hps):
        o_ref[0, :, h * LANE:(h + 1) * LANE] = (acc_ref[h] * (1.0 / l_ref[h])).T.astype(o_ref.dtype)


def _mla_prompt_attn(q, k, vt, tq=512, hps=4):
    b, heads, s, _ = q.shape
    tq = _tile(s, tq)
    hps = min(hps, heads)
    assert tq % CHUNK == 0 and heads % hps == 0
    return pl.pallas_call(
        functools.partial(_flash_kernel, tq=tq, hps=hps),
        grid=(b, heads // hps, s // tq),
        in_specs=[pl.BlockSpec((1, hps, tq, 256), lambda i, h, j: (i, h, j, 0)),
                  pl.BlockSpec((1, hps, s, 256), lambda i, h, j: (i, h, 0, 0)),
                  pl.BlockSpec((1, hps, LANE, s), lambda i, h, j: (i, h, 0, 0))],
        out_specs=pl.BlockSpec((1, tq, hps * LANE), lambda i, h, j: (i, j, h)),
        out_shape=jax.ShapeDtypeStruct((b, s, heads * LANE), BF16),
        scratch_shapes=[pltpu.VMEM((hps, 1, tq), F32), pltpu.VMEM((hps, 1, tq), F32),
                        pltpu.VMEM((hps, LANE, tq), F32)],
        compiler_params=_cp("parallel", "parallel", "arbitrary"),
        name="mla_prompt_attn",
    )(q, k, vt)


def _mla_sample_kernel(q_ref, cp_ref, krp_ref, cn_ref, krn_ref, wuk_ref, wuv_ref, o_ref,
                       qa_ref, qr_ref, m_ref, l_ref, acc_ref, *, heads, sq):
    kb = pl.program_id(1)

    @pl.when(kb == 0)
    def _():
        for h in range(heads):
            qh = q_ref[0, h]
            qa_ref[h * sq:(h + 1) * sq, :] = _dot_t(qh[:, :LANE], wuk_ref[:, h * LANE:(h + 1) * LANE]).astype(BF16)
            qr_ref[h * sq:(h + 1) * sq, :] = qh[:, LANE:]
        m_ref[...] = jnp.full(m_ref.shape, NEG, F32)
        l_ref[...] = jnp.zeros(l_ref.shape, F32)
        acc_ref[...] = jnp.zeros(acc_ref.shape, F32)

    def update(cb, krb):
        s = _dot_t(qa_ref[...], cb) + _dot_t(qr_ref[...], krb)
        m_new = jnp.maximum(m_ref[...], jnp.max(s, axis=-1, keepdims=True))
        alpha = jnp.exp(m_ref[...] - m_new)
        p = jnp.exp(s - m_new)
        l_ref[...] = l_ref[...] * alpha + jnp.sum(p, axis=-1, keepdims=True)
        acc_ref[...] = acc_ref[...] * alpha + _dot(p.astype(BF16), cb)
        m_ref[...] = m_new

    update(cp_ref[0].astype(BF16), krp_ref[0])

    @pl.when(kb == pl.num_programs(1) - 1)
    def _():
        update(cn_ref[0].astype(BF16), krn_ref[0, 0])
        o_lat = (acc_ref[...] / l_ref[...]).astype(BF16)
        for h in range(heads):
            o_ref[0, :, h * LANE:(h + 1) * LANE] = _dot(
                o_lat[h * sq:(h + 1) * sq], wuv_ref[:, h * LANE:(h + 1) * LANE]).astype(o_ref.dtype)


def _mla_sample_attn(q, c_past, kr_past, c_new, k_new, wuk, wuv, tk=1024):
    b, heads, sq, _ = q.shape
    past, kv_lora = c_past.shape[1:]
    tk = _tile(past, tk)
    kern = functools.partial(_mla_sample_kernel, heads=heads, sq=sq)
    return pl.pallas_call(
        kern,
        grid=(b, past // tk),
        in_specs=[pl.BlockSpec((1, heads, sq, 256), lambda i, j: (i, 0, 0, 0)),
                  pl.BlockSpec((1, tk, kv_lora), lambda i, j: (i, j, 0)),
                  pl.BlockSpec((1, tk, LANE), lambda i, j: (i, j, 0)),
                  pl.BlockSpec((1, sq, kv_lora), lambda i, j: (i, 0, 0)),
                  pl.BlockSpec((1, 1, sq, LANE), lambda i, j: (i, 0, 0, 1)),
                  _resident(wuk.shape), _resident(wuv.shape)],
        out_specs=pl.BlockSpec((1, sq, heads * LANE), lambda i, j: (i, 0, 0)),
        out_shape=jax.ShapeDtypeStruct((b, sq, heads * LANE), BF16),
        scratch_shapes=[pltpu.VMEM((heads * sq, kv_lora), BF16), pltpu.VMEM((heads * sq, LANE), BF16),
                        pltpu.VMEM((heads * sq, 1), F32), pltpu.VMEM((heads * sq, 1), F32),
                        pltpu.VMEM((heads * sq, kv_lora), F32)],
        compiler_params=_cp("parallel", "arbitrary"),
        name="mla_sample_attn",
    )(q, c_past, kr_past, c_new, k_new, wuk, wuv)


def _rwkv_pre_kernel(x_ref, gmix_ref, w_ref, sh0_ref, mu_ref, w0_ref, wb_ref, a0_ref, ab_ref, gb_ref,
                     kkp_ref, ka_ref, rk_ref, e_ref, et_ref,
                     r_ref, dec_ref, k_ref, v_ref, kk_ref, b_ref, g_ref, bonus_ref, sh_ref, last_ref, *, dim):
    si = pl.program_id(1)

    def put(ref, val):
        nb, _, tiles, ts, _ = ref.shape
        for tl in range(tiles):
            ref[:, 0, tl] = val[:, tl * LANE:(tl + 1) * LANE].reshape(nb, ts, LANE)

    u = _rms(x_ref[0], gmix_ref[...]).astype(BF16)
    p = _dot(u, w_ref[...])
    tm = p.shape[0]

    @pl.when(si == 0)
    def _():
        last_ref[...] = sh0_ref[0]

    row = lax.broadcasted_iota(jnp.int32, p.shape, 0)
    prev = jnp.where(row == 0, last_ref[...], pltpu.roll(p, 1, axis=0))
    last_ref[...] = p[tm - 1:tm, :]
    sh_ref[0] = p[tm - 1:tm, :]
    xs = p + mu_ref[...] * (prev - p)
    r = xs[:, :dim]
    k = xs[:, dim:2 * dim]
    v = xs[:, 2 * dim:3 * dim]
    dw = xs[:, 3 * dim:3 * dim + LANE]
    da = xs[:, 3 * dim + LANE:3 * dim + 2 * LANE]
    dg = xs[:, 3 * dim + 2 * LANE:]
    z = -(w0_ref[...] + _dot(jnp.tanh(dw).astype(BF16), wb_ref[...]))
    softplus = jnp.maximum(z, 0.0) + jnp.log(1.0 + jnp.exp(-jnp.abs(z)))
    put(dec_ref, jnp.exp(-jnp.exp(-softplus - 0.5)))
    a = _sigmoid(a0_ref[...] + _dot(da.astype(BF16), ab_ref[...]))
    g_ref[0] = _dot(_sigmoid(dg).astype(BF16), gb_ref[...])
    kk = k * kkp_ref[...]
    nrm = jnp.sqrt(_seg_sum(kk * kk, e_ref[...], et_ref[...]))
    kk = kk / jnp.maximum(nrm, 1e-12)
    k2 = k * (1.0 + (a - 1.0) * ka_ref[...])
    put(r_ref, r)
    put(k_ref, k2)
    put(v_ref, v)
    put(kk_ref, kk)
    put(b_ref, kk * a)
    bonus_ref[0] = _seg_sum(r * k2 * rk_ref[...], e_ref[...], et_ref[...]) * v


def _rwkv_pre(x, gmix, w, sh0, mu, w0, wb, a0, ab, gb, kkp, ka, rk, e, et, tm=256):
    b, s, d = x.shape
    dim = w0.shape[1]
    width = w.shape[1]
    tm = _tile(s, tm)
    ts = min(SCAN_TS, s)
    assert tm % ts == 0 and dim % LANE == 0
    row = lambda i, j: (i, j, 0)
    act = pl.BlockSpec((1, tm, dim), row)
    act_shape = jax.ShapeDtypeStruct((b, s, dim), F32)
    seq = pl.BlockSpec((tm // ts, 1, dim // LANE, ts, LANE), lambda i, j: (j, i, 0, 0, 0))
    seq_shape = jax.ShapeDtypeStruct((s // ts, b, dim // LANE, ts, LANE), F32)
    consts = (gmix, w, mu, w0, wb, a0, ab, gb, kkp, ka, rk, e, et)
    return pl.pallas_call(
        functools.partial(_rwkv_pre_kernel, dim=dim),
        grid=(b, s // tm),
        in_specs=[pl.BlockSpec((1, tm, d), row), _resident(gmix.shape), _resident(w.shape),
                  pl.BlockSpec((1, 1, width), lambda i, j: (i, 0, 0))] + [_resident(c.shape) for c in consts[2:]],
        out_specs=[seq] * 6 + [act] * 2 + [pl.BlockSpec((1, 1, width), lambda i, j: (i, 0, 0))],
        out_shape=[seq_shape] * 6 + [act_shape] * 2 + [jax.ShapeDtypeStruct((b, 1, width), F32)],
        scratch_shapes=[pltpu.VMEM((1, width), F32)],
        compiler_params=_cp("parallel", "arbitrary"),
        name="rwkv_pre",
    )(x, gmix, w, sh0, *consts[2:])


def _rwkv_scan_kernel(r_ref, dec_ref, k_ref, v_ref, kk_ref, b_ref, s0_ref, o_ref, st_ref, xt_ref, ot_ref,
                      *, ts, n, unroll):
    tb = pl.program_id(1)
    hpl = LANE // n

    @pl.when(tb == 0)
    def _():
        st_ref[...] = s0_ref[...]

    srcs = (r_ref, dec_ref, k_ref, v_ref, kk_ref, b_ref)
    for i in range(ts):
        for a, ref in enumerate(srcs):
            xt_ref[a, i] = ref[pl.ds(i, LANE, stride=ts), :].T

    def step(i, c):
        for h2 in range(hpl):
            base = h2 * n
            vv = xt_ref[3, i, base:base + n, :]

            def p1(kx, sa):
                return sa + st_ref[h2, kx] * xt_ref[4, i, pl.ds(base + kx, 1), :]

            sa = lax.fori_loop(0, n, p1, jnp.zeros_like(vv), unroll=unroll)

            def p2(kx, o):
                s_new = (st_ref[h2, kx] * xt_ref[1, i, pl.ds(base + kx, 1), :]
                         - sa * xt_ref[5, i, pl.ds(base + kx, 1), :] + vv * xt_ref[2, i, pl.ds(base + kx, 1), :])
                st_ref[h2, kx] = s_new
                return o + s_new * xt_ref[0, i, pl.ds(base + kx, 1), :]

            ot_ref[i, base:base + n, :] = lax.fori_loop(0, n, p2, jnp.zeros_like(vv), unroll=unroll)
        return c

    lax.fori_loop(0, ts, step, 0)
    for i in range(ts):
        o_ref[pl.ds(i, LANE, stride=ts), :] = ot_ref[i].T


def _rwkv_scan(r, dec, k, v, kk, bb, s0):
    nt, b, tiles, ts, _ = r.shape
    n = s0.shape[2]
    flat = lambda t: t.reshape(nt, b * tiles * ts, LANE)
    bg = LANE // tiles
    assert b % bg == 0 and LANE % tiles == 0 and bg % 8 == 0
    groups = b // bg
    hpl = LANE // n
    seq = pl.BlockSpec((None, bg * tiles * ts, LANE), lambda g, t: (t, g, 0))
    state = pl.BlockSpec((None, hpl, n, n, LANE), lambda g, t: (g, 0, 0, 0, 0))
    o, st = pl.pallas_call(
        functools.partial(_rwkv_scan_kernel, ts=ts, n=n, unroll=SCAN_UNROLL),
        grid=(groups, nt),
        in_specs=[seq] * 6 + [state],
        out_specs=[seq, state],
        out_shape=[jax.ShapeDtypeStruct((nt, b * tiles * ts, LANE), F32), jax.ShapeDtypeStruct(s0.shape, F32)],
        scratch_shapes=[pltpu.VMEM((6, ts, LANE, LANE), F32), pltpu.VMEM((ts, LANE, LANE), F32)],
        compiler_params=_cp("parallel", "arbitrary"),
        name="rwkv_scan",
    )(*(flat(t) for t in (r, dec, k, v, kk, bb)), s0)
    return o.reshape(r.shape), st


def _state_to_streams(wkv):
    b, h, n = wkv.shape[:3]
    hpl = LANE // n
    tiles = h // hpl
    bg = LANE // tiles
    g = b // bg
    t = wkv.reshape(g, bg, tiles, hpl, n, n)
    return t.transpose(0, 3, 5, 4, 1, 2).reshape(g, hpl, n, n, LANE)


def _state_from_streams(st, b, h):
    g, hpl, n = st.shape[:3]
    tiles = h // hpl
    bg = LANE // tiles
    t = st.reshape(g, hpl, n, n, bg, tiles)
    return t.transpose(0, 4, 5, 1, 3, 2).reshape(b, h, n, n)


def _mix_kernel(o_ref, bonus_ref, g_ref, ymla_ref, grw_ref, gmla_ref, lng_ref, lnb_ref, e_ref, et_ref,
                wr_ref, wm_ref, out_ref, *, inv_n):
    nb, _, tiles, ts, _ = o_ref.shape
    o = jnp.concatenate([o_ref[:, 0, tl].reshape(nb * ts, LANE) for tl in range(tiles)], axis=1)
    e, et = e_ref[...], et_ref[...]
    d = o - _seg_sum(o, e, et) * inv_n
    var = _seg_sum(d * d, e, et) * inv_n
    on = d * lax.rsqrt(var + GN_EPS) * lng_ref[...] + lnb_ref[...]
    y = ((on + bonus_ref[0]) * g_ref[0]).astype(BF16)
    merged = _sigmoid(grw_ref[0]) * _dot(y, wr_ref[...]) + _sigmoid(gmla_ref[0]) * _dot(ymla_ref[0], wm_ref[...])
    out_ref[0] = merged.astype(out_ref.dtype)


def _mix(o, bonus, g, ymla, gates, lng, lnb, e, et, wr, wm, head_dim, tm=256):
    nt, b, tiles, ts, _ = o.shape
    s = nt * ts
    dim = tiles * LANE
    d = wr.shape[1]
    md = ymla.shape[2]
    tm = _tile(s, tm)
    assert tm % ts == 0
    row = lambda i, j: (i, j, 0)
    return pl.pallas_call(
        functools.partial(_mix_kernel, inv_n=1.0 / head_dim),
        grid=(b, s // tm),
        in_specs=[pl.BlockSpec((tm // ts, 1, tiles, ts, LANE), lambda i, j: (j, i, 0, 0, 0)),
                  pl.BlockSpec((1, tm, dim), row), pl.BlockSpec((1, tm, dim), row), pl.BlockSpec((1, tm, md), row),
                  pl.BlockSpec((1, tm, d), lambda i, j: (i, j, 0)), pl.BlockSpec((1, tm, d), lambda i, j: (i, j, 1))]
                 + [_resident(c.shape) for c in (lng, lnb, e, et, wr, wm)],
        out_specs=pl.BlockSpec((1, tm, d), row),
        out_shape=jax.ShapeDtypeStruct((b, s, d), BF16),
        compiler_params=_cp("parallel", "parallel"),
        name="rwkv_post_mix",
    )(o, bonus, g, ymla, gates, gates, lng, lnb, e, et, wr, wm)


def _outproj_kernel(x_ref, m_ref, wo_ref, gc_ref, wq_ref, h_ref, q_ref):
    h = x_ref[...] + _dot(m_ref[...], wo_ref[...])
    h_ref[...] = h
    q_ref[...] = _dot(_rms(h, gc_ref[...]).astype(BF16), wq_ref[...]).astype(q_ref.dtype)


def _outproj(x, merged, wo, gc, wq, tm=256):
    t, d = x.shape
    tm = _tile(t, tm)
    row = lambda i: (i, 0)
    return pl.pallas_call(
        _outproj_kernel,
        grid=(t // tm,),
        in_specs=[pl.BlockSpec((tm, d), row), pl.BlockSpec((tm, d), row),
                  _resident(wo.shape), _resident(gc.shape), _resident(wq.shape)],
        out_specs=[pl.BlockSpec((tm, d), row), pl.BlockSpec((tm, d), row)],
        out_shape=[jax.ShapeDtypeStruct((t, d), F32), jax.ShapeDtypeStruct((t, d), BF16)],
        compiler_params=_cp("parallel"),
        name="outproj_crossq",
    )(x, merged, wo, gc, wq)


def _route_tile(lg, n_groups, per_group):
    n_exp = n_groups * per_group
    lane = lax.broadcasted_iota(jnp.int32, lg.shape, 1)
    first = lambda mask: jnp.min(jnp.where(mask, lane, 2 * LANE), axis=-1, keepdims=True)
    top = lambda mask: jnp.max(jnp.where(mask, lg, NEG), axis=-1, keepdims=True)
    is_g = lane < n_groups
    gmax = top(is_g)
    g_sel = first(is_g & (lg == gmax))
    p_grp = 1.0 / jnp.sum(jnp.where(is_g, jnp.exp(lg - gmax), 0.0), axis=-1, keepdims=True)
    eid = lane - n_groups
    in_grp = (eid >= g_sel * per_group) & (eid < (g_sel + 1) * per_group)
    v1 = top(in_grp)
    e1 = first(in_grp & (lg == v1))
    rest = in_grp & (lane != e1)
    v2 = top(rest)
    e2 = first(rest & (lg == v2))
    t = jnp.exp(v2 - v1)
    w1 = p_grp / (1.0 + t)
    w2 = w1 * t
    route = jnp.where(lane == 0, (e1 - n_groups).astype(F32),
                      jnp.where(lane == 1, (e2 - n_groups).astype(F32),
                                jnp.where(lane == 2, w1, jnp.where(lane == 3, w2, 0.0))))
    hits = jnp.where((lane == e1 - n_groups) | (lane == e2 - n_groups), 1.0, 0.0)
    assert n_exp <= LANE
    return route, jnp.sum(hits, axis=0, keepdims=True)


def _cross_kernel(q_ref, mk_ref, mv_ref, h_ref, wo_ref, gm_ref, wr_ref, br_ref, h2_ref, hn_ref, rt_ref, cnt_ref,
                  *, heads, scale, n_groups, per_group):
    q = q_ref[0]
    mk = mk_ref[0]
    mv = mv_ref[0]
    hd = q.shape[1] // heads
    outs = []
    for h in range(heads):
        sl = slice(h * hd, (h + 1) * hd)
        s = _dot_t(q[:, sl], mk[:, sl]) * scale
        p = jnp.exp(s - jnp.max(s, axis=-1, keepdims=True))
        p = p / jnp.sum(p, axis=-1, keepdims=True)
        outs.append(_dot(p.astype(BF16), mv[:, sl]).astype(BF16))
    o = jnp.concatenate(outs, axis=1)
    h2 = h_ref[0] + _dot(o, wo_ref[...])
    h2_ref[0] = h2
    hn = _rms(h2, gm_ref[...]).astype(BF16)
    hn_ref[0] = hn
    route, hist = _route_tile(_dot(hn, wr_ref[...]) + br_ref[...], n_groups, per_group)
    rt_ref[0] = route
    cnt_ref[0, 0] = jnp.broadcast_to(hist, cnt_ref.shape[2:])


def _cross(q, mk, mv, h, wo, gm, wr, br, heads, n_groups, per_group, tm=256):
    b, s, d = h.shape
    n_mem = mk.shape[1]
    tm = _tile(s, tm)
    row = lambda i, j: (i, j, 0)
    mem = pl.BlockSpec((1, n_mem, d), lambda i, j: (i, 0, 0))
    kern = functools.partial(_cross_kernel, heads=heads, scale=float(d // heads) ** -0.5,
                             n_groups=n_groups, per_group=per_group)
    return pl.pallas_call(
        kern,
        grid=(b, s // tm),
        in_specs=[pl.BlockSpec((1, tm, d), row), mem, mem, pl.BlockSpec((1, tm, d), row),
                  _resident(wo.shape), _resident(gm.shape), _resident(wr.shape), _resident(br.shape)],
        out_specs=[pl.BlockSpec((1, tm, d), row), pl.BlockSpec((1, tm, d), row), pl.BlockSpec((1, tm, LANE), row),
                   pl.BlockSpec((1, 1, 8, LANE), lambda i, j: (i, j, 0, 0))],
        out_shape=[jax.ShapeDtypeStruct((b, s, d), F32), jax.ShapeDtypeStruct((b, s, d), BF16),
                   jax.ShapeDtypeStruct((b, s, LANE), F32), jax.ShapeDtypeStruct((b, s // tm, 8, LANE), F32)],
        compiler_params=_cp("parallel", "parallel"),
        name="cross_attn_router",
    )(q, mk, mv, h, wo, gm, wr, br)


def _moe_kernel(be_ref, on_ref, x_ref, sw_ref, wg_ref, wu_ref, wd_ref, y_ref, wgb_ref, wub_ref, wdb_ref):
    i = pl.program_id(0)

    @pl.when((i == 0) | (be_ref[i] != be_ref[jnp.maximum(i - 1, 0)]))
    def _():
        wgb_ref[...] = wg_ref[0].astype(BF16)
        wub_ref[...] = wu_ref[0].astype(BF16)
        wdb_ref[...] = wd_ref[0].astype(BF16)

    @pl.when(on_ref[i] != 0)
    def _():
        xb = x_ref[...]
        g = _dot(xb, wgb_ref[...])
        u = _dot(xb, wub_ref[...])
        hb = (g * _sigmoid(g) * u).astype(BF16)
        y_ref[...] = _dot(hb, wdb_ref[...]) * sw_ref[...]

    @pl.when(on_ref[i] == 0)
    def _():
        y_ref[...] = jnp.zeros(y_ref.shape, y_ref.dtype)


def _moe_experts(blk_e, blk_on, x_slot, slot_w, wg, wu, wd):
    ns, d = x_slot.shape
    f = wg.shape[2]
    n_blk = ns // MOE_ROWS
    wspec = lambda shp: pl.BlockSpec((1,) + shp, lambda i, be, on: (be[i], 0, 0))
    grid_spec = pltpu.PrefetchScalarGridSpec(
        num_scalar_prefetch=2,
        grid=(n_blk,),
        in_specs=[pl.BlockSpec((MOE_ROWS, d), lambda i, be, on: (i, 0)),
                  pl.BlockSpec((MOE_ROWS, 1), lambda i, be, on: (i, 0)),
                  wspec((d, f)), wspec((d, f)), wspec((f, d))],
        out_specs=pl.BlockSpec((MOE_ROWS, d), lambda i, be, on: (i, 0)),
        scratch_shapes=[pltpu.VMEM((d, f), BF16), pltpu.VMEM((d, f), BF16), pltpu.VMEM((f, d), BF16)],
    )
    return pl.pallas_call(
        _moe_kernel,
        grid_spec=grid_spec,
        out_shape=jax.ShapeDtypeStruct((ns, d), F32),
        compiler_params=_cp("arbitrary"),
        name="moe_experts",
    )(blk_e, blk_on, x_slot, slot_w, wg, wu, wd)


def _combine_kernel(idx_ref, h_ref, g_ref, y_hbm, o_ref, buf_ref, sem_ref, *, tm, norm):
    def issue(r, c):
        for ch in range(TOP_K):
            slot = idx_ref[0, TOP_K * r + ch]
            pltpu.make_async_copy(y_hbm.at[pl.ds(slot, 1)], buf_ref.at[ch, pl.ds(r, 1)], sem_ref.at[ch]).start()
        return c

    lax.fori_loop(0, tm, issue, 0, unroll=8)
    acc = h_ref[...]
    for ch in range(TOP_K):
        pltpu.make_async_copy(y_hbm.at[pl.ds(0, tm)], buf_ref.at[ch], sem_ref.at[ch]).wait()
        acc = acc + buf_ref[ch]
    o_ref[...] = _rms(acc, g_ref[...]) if norm else acc


def _combine(h, y_slot, slot_of, g, tm=256):
    t, d = h.shape
    tm = _tile(t, tm)
    idx = slot_of.reshape(t // tm, 1, TOP_K * tm)
    norm = g is not None
    g = g if norm else jnp.ones((1, d), F32)
    return pl.pallas_call(
        functools.partial(_combine_kernel, tm=tm, norm=norm),
        grid=(t // tm,),
        in_specs=[pl.BlockSpec((None, 1, TOP_K * tm), lambda i: (i, 0, 0), memory_space=pltpu.SMEM),
                  pl.BlockSpec((tm, d), lambda i: (i, 0)),
                  pl.BlockSpec((1, d), lambda i: (0, 0)),
                  pl.BlockSpec(memory_space=pl.ANY)],
        out_specs=pl.BlockSpec((tm, d), lambda i: (i, 0)),
        out_shape=jax.ShapeDtypeStruct((t, d), F32),
        scratch_shapes=[pltpu.VMEM((TOP_K, tm, d), F32), pltpu.SemaphoreType.DMA((TOP_K,))],
        compiler_params=_cp("arbitrary"),
        name="moe_combine_norm",
    )(idx, h, g, y_slot)


def _pad_cols(w, n):
    return jnp.pad(w, ((0, 0),) * (w.ndim - 1) + ((0, n - w.shape[-1]),))


def _pad_rows(w, n):
    return jnp.pad(w, ((0, n - w.shape[0]), (0, 0)))


def _rot_half_cols(w):
    half = w.shape[-1] // 2
    return jnp.concatenate([-w[..., half:], w[..., :half]], axis=-1)


def _rope_table(pos):
    half = ROPE // 2
    inv = ROPE_THETA ** (-jnp.arange(half, dtype=F32) / half)
    ang = pos.astype(F32)[:, None] * inv[None, :]
    cos, sin = jnp.cos(ang), jnp.sin(ang)
    return jnp.concatenate([cos, cos, sin, sin], axis=1)


def _prep_layer(lp, dims):
    ql, kl, rope, dim, dl, il, gl, d = (dims[k] for k in ("q_lora", "kv_lora", "rope", "rw_dim", "decay_lora",
                                                          "iclr_lora", "gate_lora", "d_model"))
    w_in = lp["w_in"]
    i0, i1, i2 = ql, ql + kl, ql + kl + rope
    i3 = i2 + 3 * dim + dl + il + gl
    w_kr = w_in[:, i1:i2]
    out = {}
    out["w_mla"] = jnp.concatenate(
        [w_in[:, :i0], w_kr, _rot_half_cols(w_kr), w_in[:, i0:i1]], axis=1).astype(BF16)
    rw = w_in[:, i2:i3]
    c3 = 3 * dim
    glp = -(-gl // LANE) * LANE

    def regroup(t):
        return jnp.concatenate([t[..., :c3], _pad_cols(t[..., c3:c3 + dl], LANE),
                                _pad_cols(t[..., c3 + dl:c3 + dl + il], LANE),
                                _pad_cols(t[..., c3 + dl + il:], glp)], axis=-1)

    out["regroup"] = regroup
    out["w_rw"] = regroup(rw).astype(BF16)
    out["mu"] = regroup(lp["rw_mu"][None, :])
    out["w_gates"] = w_in[:, i3:].astype(BF16)
    heads = lp["w_uq"].shape[1]
    nope = lp["w_uq"].shape[2] - rope
    wq = lp["w_uq"]
    out["w_q"] = jnp.concatenate([wq[..., :nope], wq[..., nope:], _rot_half_cols(wq[..., nope:])],
                                 axis=-1).reshape(ql, heads * 256).astype(BF16)
    out["w_uk"] = lp["w_uk"].reshape(kl, -1).astype(BF16)
    out["w_uv"] = lp["w_uv"].reshape(kl, -1).astype(BF16)
    out["w_uv_t"] = out["w_uv"].T
    out["wb"] = _pad_rows(lp["rw_wb"], LANE).astype(BF16)
    out["ab"] = _pad_rows(lp["rw_ab"], LANE).astype(BF16)
    out["gb"] = _pad_rows(lp["rw_gb"], glp).astype(BF16)
    n_heads = dims["rw_heads"]
    hd = dim // n_heads
    e = (jnp.arange(dim)[:, None] // hd == jnp.arange(LANE)[None, :]).astype(BF16)
    out["e"], out["et"] = e, e.T
    out["w_br_rwkv"] = lp["w_br_rwkv"].astype(BF16)
    out["w_br_mla"] = lp["w_br_mla"].astype(BF16)
    out["w_out"] = lp["w_out"].astype(BF16)
    out["w_mq"] = lp["w_mq"].reshape(d, d).astype(BF16)
    out["w_mkv"] = jnp.concatenate([lp["w_mk"].reshape(d, d), lp["w_mv"].reshape(d, d)], axis=1).astype(BF16)
    out["w_mo"] = lp["w_mo"].reshape(d, d).astype(BF16)
    n_g, n_e = lp["w_rg"].shape[1], lp["w_re"].shape[1]
    out["w_r"] = _pad_cols(jnp.concatenate([lp["w_rg"], lp["w_re"]], axis=1), LANE).astype(BF16)
    out["b_r"] = _pad_cols(jnp.concatenate([lp["b_rg"], lp["b_re"]])[None, :], LANE)
    out["w_eg"], out["w_eu"], out["w_ed"] = lp["w_eg"], lp["w_eu"], lp["w_ed"]
    return out


def _row(v):
    return v.reshape(1, -1)


def _mixer(x, lp, wp, dims, pos, shift0, wkv0, attend):
    b, s, d = x.shape
    heads, rw_heads = dims["mla_heads"], dims["rw_heads"]
    gmix = _row(lp["g_mix"])
    cs = _rope_table(pos)
    scale = float(dims["nope"] + dims["rope"]) ** -0.5
    q, k, v, c, kr = _mla_proj(x, gmix, wp["w_mla"], _row(lp["g_q"]), wp["w_q"], _row(lp["g_kv"]),
                               wp["w_uk"], wp["w_uv_t"], cs, heads, scale)
    y_mla = attend(q, k, v, c)
    sh0 = wp["regroup"](shift0)
    r, dec, k2, vr, kk, bb, g, bonus, sh = _rwkv_pre(
        x, gmix, wp["w_rw"], sh0, wp["mu"], _row(lp["rw_w0"]), wp["wb"], _row(lp["rw_a0"]), wp["ab"], wp["gb"],
        _row(lp["rw_kk"]), _row(lp["rw_ka"]), _row(lp["rw_rk"]), wp["e"], wp["et"])
    n = dims["rw_dim"] // rw_heads
    o, s_t = _rwkv_scan(r, dec, k2, vr, kk, bb, _state_to_streams(wkv0))
    wkv = _state_from_streams(s_t, b, rw_heads)
    gates = _mm_norm(x.reshape(b * s, d), gmix, wp["w_gates"]).reshape(b, s, 2 * d)
    dim = dims["rw_dim"]
    merged = _mix(o, bonus, g, y_mla, gates, _row(lp["rw_ln_g"]), _row(lp["rw_ln_b"]), wp["e"], wp["et"],
                  wp["w_br_rwkv"], wp["w_br_mla"], n).reshape(b * s, d)
    h, qc = _outproj(x.reshape(b * s, d), merged, wp["w_out"], _row(lp["g_cross"]), wp["w_mq"])
    c3, dl, il = 3 * dim, dims["decay_lora"], dims["iclr_lora"]
    shift = jnp.concatenate([sh[..., :c3], sh[..., c3:c3 + dl], sh[..., c3 + LANE:c3 + LANE + il],
                             sh[..., c3 + 2 * LANE:c3 + 2 * LANE + dims["gate_lora"]]], axis=-1)
    return h.reshape(b, s, d), qc.reshape(b, s, d), c, kr, k, wkv, shift


def _moe(hn, route, counts, wp):
    t, d = hn.shape
    n_exp = counts.shape[0]
    e_idx, wts = route[:, :TOP_K], route[:, TOP_K:2 * TOP_K]
    m = t * TOP_K
    n_blk = -(-(m + n_exp * (MOE_ROWS - 1)) // MOE_ROWS)
    ns = n_blk * MOE_ROWS
    flat_e = e_idx.reshape(m).astype(jnp.int32)
    order = jnp.argsort(flat_e).astype(jnp.int32)
    padded = (counts + MOE_ROWS - 1) // MOE_ROWS * MOE_ROWS
    pad_end = jnp.cumsum(padded)
    pad_start = pad_end - padded
    start = jnp.cumsum(counts) - counts
    blk_start = jnp.arange(n_blk, dtype=jnp.int32) * MOE_ROWS
    blk_e = jnp.minimum(jnp.sum(pad_end[None, :] <= blk_start[:, None], axis=1), n_exp - 1).astype(jnp.int32)
    blk_on = (blk_start < pad_end[-1]).astype(jnp.int32)
    off = (blk_start - pad_start[blk_e])[:, None] + jnp.arange(MOE_ROWS, dtype=jnp.int32)[None, :]
    valid = ((off < counts[blk_e][:, None]) & (blk_on[:, None] != 0)).reshape(ns)
    asg = order[jnp.clip(start[blk_e][:, None] + off, 0, m - 1).reshape(ns)]
    slot_tok = jnp.where(valid, asg // TOP_K, 0)
    slot_w = jnp.where(valid, wts.reshape(m)[asg], 0.0)
    y_slot = _moe_experts(blk_e, blk_on, hn[slot_tok], slot_w[:, None], wp["w_eg"], wp["w_eu"], wp["w_ed"])
    key = jnp.where(valid, asg, m + jnp.arange(ns, dtype=jnp.int32))
    slot_of = jnp.argsort(key)[:m].astype(jnp.int32)
    return y_slot, slot_of


def kernel(x_prompt, x_sample, mem_prompt, cache_kv_latent, cache_k_rope, cache_mem_k, cache_mem_v, state_wkv, state_shift, g_mix, w_in, g_q, w_uq, g_kv, w_uk, w_uv, rw_mu, rw_w0, rw_wb, rw_a0, rw_ab, rw_gb, rw_kk, rw_ka, rw_rk, rw_ln_g, rw_ln_b, w_br_rwkv, w_br_mla, w_out, g_cross, g_mem, w_mq, w_mk, w_mv, w_mo, g_moe, w_rg, b_rg, w_re, b_re, w_eg, w_eu, w_ed, g_final):
    depth = g_mix.shape[0]
    b, s, d = x_prompt.shape
    bs, ss, _ = x_sample.shape
    past = cache_kv_latent.shape[2]
    n_mem = mem_prompt.shape[1]
    mem_heads = w_mq.shape[2]
    rw_heads, rw_hd = rw_rk.shape[1:]
    dims = dict(d_model=d, q_lora=g_q.shape[1], kv_lora=g_kv.shape[1], mla_heads=w_uq.shape[2],
                nope=w_uk.shape[3], rope=cache_k_rope.shape[3], rw_heads=rw_heads, rw_dim=rw_heads * rw_hd,
                decay_lora=rw_wb.shape[1], iclr_lora=rw_ab.shape[1], gate_lora=rw_gb.shape[1])
    assert dims["rope"] == ROPE and dims["nope"] == LANE and w_uv.shape[3] == LANE
    assert dims["decay_lora"] <= LANE and dims["iclr_lora"] <= LANE and rw_heads <= LANE
    n_groups, n_exp = w_rg.shape[2], w_re.shape[2]
    per_group = n_exp // n_groups
    stacked = dict(g_mix=g_mix, w_in=w_in, g_q=g_q, w_uq=w_uq, g_kv=g_kv, w_uk=w_uk, w_uv=w_uv, rw_mu=rw_mu,
                   rw_w0=rw_w0, rw_wb=rw_wb, rw_a0=rw_a0, rw_ab=rw_ab, rw_gb=rw_gb, rw_kk=rw_kk, rw_ka=rw_ka,
                   rw_rk=rw_rk, rw_ln_g=rw_ln_g, rw_ln_b=rw_ln_b, w_br_rwkv=w_br_rwkv, w_br_mla=w_br_mla,
                   w_out=w_out, g_cross=g_cross, g_mem=g_mem, w_mq=w_mq, w_mk=w_mk, w_mv=w_mv, w_mo=w_mo,
                   g_moe=g_moe, w_rg=w_rg, b_rg=b_rg, w_re=w_re, b_re=b_re, w_eg=w_eg, w_eu=w_eu, w_ed=w_ed)
    h_p, h_s = x_prompt, x_sample
    outs = [[] for _ in range(10)]
    for l in range(depth):
        lp = {name: val[l] for name, val in stacked.items()}
        wp = _prep_layer(lp, dims)

        h, qc, c, kr, _, wkv, shift = _mixer(
            h_p, lp, wp, dims, jnp.arange(s), jnp.zeros((b, 1, state_shift.shape[3]), F32),
            jnp.zeros((b, rw_heads, rw_hd, rw_hd), F32), lambda q, k, v, c_new: _mla_prompt_attn(q, k, v))
        mkv = _mm_norm(mem_prompt.reshape(b * n_mem, d), _row(lp["g_mem"]), wp["w_mkv"])
        mk, mv = mkv[:, :d].reshape(b, n_mem, d), mkv[:, d:].reshape(b, n_mem, d)
        gm, wr, br = _row(lp["g_moe"]), wp["w_r"], wp["b_r"]
        h2_p, hn_p, rt_p, cnt_p = _cross(qc, mk.astype(BF16), mv.astype(BF16), h, wp["w_mo"], gm, wr, br, mem_heads,
                                         n_groups, per_group)
        for lst, val in zip(outs[:6], (c, kr, mk.reshape(b, n_mem, mem_heads, -1),
                                       mv.reshape(b, n_mem, mem_heads, -1), wkv, shift)):
            lst.append(val)

        kr_past = jnp.pad(cache_k_rope[l], ((0, 0), (0, 0), (0, LANE - ROPE))).astype(BF16)

        def attend_sample(q, k, v, c_new, l=l, kr_past=kr_past, wp=wp):
            return _mla_sample_attn(q, cache_kv_latent[l], kr_past, c_new, k, wp["w_uk"], wp["w_uv"])

        h, qc, c, kr, _, wkv, shift = _mixer(h_s, lp, wp, dims, past + jnp.arange(ss), state_shift[l],
                                             state_wkv[l], attend_sample)
        h2_s, hn_s, rt_s, cnt_s = _cross(qc, cache_mem_k[l].reshape(bs, n_mem, d).astype(BF16),
                                         cache_mem_v[l].reshape(bs, n_mem, d).astype(BF16), h, wp["w_mo"], gm, wr, br,
                                         mem_heads, n_groups, per_group)
        for lst, val in zip(outs[6:], (c, kr, wkv, shift)):
            lst.append(val)

        tp, tsm = b * s, bs * ss
        hn = jnp.concatenate([hn_p.reshape(tp, d), hn_s.reshape(tsm, d)], axis=0)
        route = jnp.concatenate([rt_p.reshape(tp, LANE), rt_s.reshape(tsm, LANE)], axis=0)
        counts = (jnp.sum(cnt_p[:, :, 0, :n_exp], axis=(0, 1)) + jnp.sum(cnt_s[:, :, 0, :n_exp], axis=(0, 1)))
        y_slot, slot_of = _moe(hn, route, counts.astype(jnp.int32), wp)
        g_fin = _row(g_final) if l == depth - 1 else None
        h_p = _combine(h2_p.reshape(tp, d), y_slot, slot_of[:tp * TOP_K], g_fin).reshape(b, s, d)
        h_s = _combine(h2_s.reshape(tsm, d), y_slot, slot_of[tp * TOP_K:], g_fin).reshape(bs, ss, d)
    stacks = [jnp.stack(o) for o in outs]
    return (h_p, h_s, *stacks)
```

```python
import functools

import jax
import jax.numpy as jnp
from jax import lax
from jax.experimental import pallas as pl
from jax.experimental.pallas import tpu as pltpu

F32 = jnp.float32
BF16 = jnp.bfloat16

EPS = 1e-6
GN_EPS = 64e-5
ROPE_THETA = 10000.0
CHUNK = 64
TOP_K = 2
NEG = -1e30

LANE = 128
V7X_VMEM_LIMIT = 56 << 20
ROPE = 64
MOE_ROWS = 256
SCAN_TS = 16
SCAN_UNROLL = 16


def _cp(*sem):
    return pltpu.CompilerParams(dimension_semantics=sem, vmem_limit_bytes=V7X_VMEM_LIMIT)


def _tile(n, pref):
    if n <= pref:
        return n
    for t in range(pref, 7, -1):
        if n % t == 0 and t % 8 == 0:
            return t
    return n


def _resident(shape):
    nd = len(shape)
    return pl.BlockSpec(shape, lambda *_: (0,) * nd, pipeline_mode=pl.Buffered(1))


def _rms(x, g):
    return x * lax.rsqrt(jnp.mean(x * x, axis=-1, keepdims=True) + EPS) * g


def _sigmoid(x):
    return 1.0 / (1.0 + jnp.exp(-x))


def _dot(a, b):
    return jnp.dot(a, b, preferred_element_type=F32)


def _dot_t(a, b):
    return lax.dot_general(a, b, (((1,), (1,)), ((), ())), preferred_element_type=F32)


def _seg_sum(x, e, et):
    hi = x.astype(BF16)
    lo = (x - hi.astype(F32)).astype(BF16)
    s = _dot(hi, e) + _dot(lo, e)
    shi = s.astype(BF16)
    slo = (s - shi.astype(F32)).astype(BF16)
    return _dot(shi, et) + _dot(slo, et)


def _mm_norm_kernel(x_ref, g_ref, w_ref, o_ref, xn_ref):
    @pl.when(pl.program_id(1) == 0)
    def _():
        xn_ref[...] = _rms(x_ref[...], g_ref[...]).astype(BF16)

    o_ref[...] = _dot(xn_ref[...], w_ref[...]).astype(o_ref.dtype)


def _mm_norm(x, g, w, tm=1024, tn=512, out_dtype=F32):
    m, k = x.shape
    n = w.shape[1]
    tm, tn = _tile(m, tm), _tile(n, tn)
    return pl.pallas_call(
        _mm_norm_kernel,
        grid=(m // tm, n // tn),
        in_specs=[pl.BlockSpec((tm, k), lambda i, j: (i, 0)),
                  pl.BlockSpec((1, k), lambda i, j: (0, 0)),
                  pl.BlockSpec((k, tn), lambda i, j: (0, j))],
        out_specs=pl.BlockSpec((tm, tn), lambda i, j: (i, j)),
        out_shape=jax.ShapeDtypeStruct((m, n), out_dtype),
        scratch_shapes=[pltpu.VMEM((tm, k), BF16)],
        compiler_params=_cp("parallel", "arbitrary"),
        name="mm_norm",
    )(x, g, w)


def _mla_proj_kernel(x_ref, gmix_ref, wm_ref, gq_ref, wq_ref, gkv_ref, wuk_ref, wuv_ref, cs_ref,
                     q_ref, k_ref, v_ref, c_ref, kr_ref, *, heads, q_lora, scale):
    u = _rms(x_ref[0], gmix_ref[...]).astype(BF16)
    p = _dot(u, wm_ref[...])
    cs = cs_ref[...]
    lane = lax.broadcasted_iota(jnp.int32, cs.shape, 1)

    def rope_tile(t2):
        t = t2 * cs
        return jnp.where(lane < ROPE, t + pltpu.roll(t, ROPE, axis=1), 0.0)

    krt = rope_tile(p[:, q_lora:q_lora + LANE])
    kr_ref[0] = krt[:, :ROPE]
    krb = krt.astype(BF16)
    c = _rms(p[:, q_lora + LANE:], gkv_ref[...])
    c_ref[0] = c
    cb = c.astype(BF16)
    kn = _dot(cb, wuk_ref[...])
    vt = _dot_t(wuv_ref[...], cb)
    qn = _rms(p[:, :q_lora], gq_ref[...]).astype(BF16)
    qf = _dot(qn, wq_ref[...])
    for h in range(heads):
        q_ref[0, h, :, :LANE] = (qf[:, h * 256:h * 256 + LANE] * scale).astype(BF16)
        q_ref[0, h, :, LANE:] = (rope_tile(qf[:, h * 256 + LANE:(h + 1) * 256]) * scale).astype(BF16)
        k_ref[0, h, :, :LANE] = kn[:, h * LANE:(h + 1) * LANE].astype(BF16)
        k_ref[0, h, :, LANE:] = krb
        v_ref[0, h] = vt[h * LANE:(h + 1) * LANE, :].astype(BF16)


def _mla_proj(x, gmix, wm, gq, wq, gkv, wuk, wuv, cs, heads, scale, tm=256):
    b, s, d = x.shape
    q_lora = gq.shape[1]
    kv_lora = gkv.shape[1]
    tm = _tile(s, tm)
    kern = functools.partial(_mla_proj_kernel, heads=heads, q_lora=q_lora, scale=scale)
    return pl.pallas_call(
        kern,
        grid=(b, s // tm),
        in_specs=[pl.BlockSpec((1, tm, d), lambda i, j: (i, j, 0)),
                  _resident(gmix.shape), _resident(wm.shape), _resident(gq.shape), _resident(wq.shape),
                  _resident(gkv.shape), _resident(wuk.shape), _resident(wuv.shape),
                  pl.BlockSpec((tm, LANE), lambda i, j: (j, 0))],
        out_specs=[pl.BlockSpec((1, heads, tm, 256), lambda i, j: (i, 0, j, 0)),
                   pl.BlockSpec((1, heads, tm, 256), lambda i, j: (i, 0, j, 0)),
                   pl.BlockSpec((1, heads, LANE, tm), lambda i, j: (i, 0, 0, j)),
                   pl.BlockSpec((1, tm, kv_lora), lambda i, j: (i, j, 0)),
                   pl.BlockSpec((1, tm, ROPE), lambda i, j: (i, j, 0))],
        out_shape=[jax.ShapeDtypeStruct((b, heads, s, 256), BF16),
                   jax.ShapeDtypeStruct((b, heads, s, 256), BF16),
                   jax.ShapeDtypeStruct((b, heads, LANE, s), BF16),
                   jax.ShapeDtypeStruct((b, s, kv_lora), F32),
                   jax.ShapeDtypeStruct((b, s, ROPE), F32)],
        compiler_params=_cp("parallel", "parallel"),
        name="mla_proj",
    )(x, gmix, wm, gq, wq, gkv, wuk, wuv, cs)


def _flash_kernel(q_ref, k_ref, vt_ref, o_ref, m_ref, l_ref, acc_ref, *, tq, hps):
    qi = pl.program_id(2)
    m_ref[...] = jnp.full(m_ref.shape, NEG, F32)
    l_ref[...] = jnp.zeros(l_ref.shape, F32)
    acc_ref[...] = jnp.zeros(acc_ref.shape, F32)

    def tile(j, diag):
        start = pl.multiple_of(j * tq, tq)
        scores = [_dot_t(k_ref[0, h, pl.ds(start, tq), :], q_ref[0, h]) for h in range(hps)]
        for h in range(hps):
            s = scores[h]
            if diag:
                kc = lax.broadcasted_iota(jnp.int32, s.shape, 0) // CHUNK
                qc = lax.broadcasted_iota(jnp.int32, s.shape, 1) // CHUNK
                s = jnp.where(kc <= qc, s, NEG)
            m = m_ref[h]
            m_new = jnp.maximum(m, jnp.max(s, axis=0, keepdims=True))
            alpha = jnp.exp(m - m_new)
            p = jnp.exp(s - m_new)
            l_ref[h] = l_ref[h] * alpha + jnp.sum(p, axis=0, keepdims=True)
            acc_ref[h] = acc_ref[h] * alpha + _dot(vt_ref[0, h, :, pl.ds(start, tq)], p.astype(BF16))
            m_ref[h] = m_new

    def body(j, c):
        tile(j, False)
        return c

    lax.fori_loop(0, qi, body, 0)
    tile(qi, True)
    for h in range(hps):
        o_ref[0, :, h * LANE:(h + 1) * LANE] = (acc_ref[h] * (1.0 / l_ref[h])).T.astype(o_ref.dtype)


def _mla_prompt_attn(q, k, vt, tq=512, hps=4):
    b, heads, s, _ = q.shape
    tq = _tile(s, tq)
    hps = min(hps, heads)
    assert tq % CHUNK == 0 and heads % hps == 0
    return pl.pallas_call(
        functools.partial(_flash_kernel, tq=tq, hps=hps),
        grid=(b, heads // hps, s // tq),
        in_specs=[pl.BlockSpec((1, hps, tq, 256), lambda i, h, j: (i, h, j, 0)),
                  pl.BlockSpec((1, hps, s, 256), lambda i, h, j: (i, h, 0, 0)),
                  pl.BlockSpec((1, hps, LANE, s), lambda i, h, j: (i, h, 0, 0))],
        out_specs=pl.BlockSpec((1, tq, hps * LANE), lambda i, h, j: (i, j, h)),
        out_shape=jax.ShapeDtypeStruct((b, s, heads * LANE), BF16),
        scratch_shapes=[pltpu.VMEM((hps, 1, tq), F32), pltpu.VMEM((hps, 1, tq), F32),
                        pltpu.VMEM((hps, LANE, tq), F32)],
        compiler_params=_cp("parallel", "parallel", "arbitrary"),
        name="mla_prompt_attn",
    )(q, k, vt)


def _mla_sample_kernel(q_ref, cp_ref, krp_ref, cn_ref, krn_ref, wuk_ref, wuv_ref, o_ref,
                       qa_ref, qr_ref, m_ref, l_ref, acc_ref, *, heads, sq):
    kb = pl.program_id(1)

    @pl.when(kb == 0)
    def _():
        for h in range(heads):
            qh = q_ref[0, h]
            qa_ref[h * sq:(h + 1) * sq, :] = _dot_t(qh[:, :LANE], wuk_ref[:, h * LANE:(h + 1) * LANE]).astype(BF16)
            qr_ref[h * sq:(h + 1) * sq, :] = qh[:, LANE:]
        m_ref[...] = jnp.full(m_ref.shape, NEG, F32)
        l_ref[...] = jnp.zeros(l_ref.shape, F32)
        acc_ref[...] = jnp.zeros(acc_ref.shape, F32)

    def update(cb, krb):
        s = _dot_t(qa_ref[...], cb) + _dot_t(qr_ref[...], krb)
        m_new = jnp.maximum(m_ref[...], jnp.max(s, axis=-1, keepdims=True))
        alpha = jnp.exp(m_ref[...] - m_new)
        p = jnp.exp(s - m_new)
        l_ref[...] = l_ref[...] * alpha + jnp.sum(p, axis=-1, keepdims=True)
        acc_ref[...] = acc_ref[...] * alpha + _dot(p.astype(BF16), cb)
        m_ref[...] = m_new

    update(cp_ref[0].astype(BF16), krp_ref[0])

    @pl.when(kb == pl.num_programs(1) - 1)
    def _():
        update(cn_ref[0].astype(BF16), krn_ref[0, 0])
        o_lat = (acc_ref[...] / l_ref[...]).astype(BF16)
        for h in range(heads):
            o_ref[0, :, h * LANE:(h + 1) * LANE] = _dot(
                o_lat[h * sq:(h + 1) * sq], wuv_ref[:, h * LANE:(h + 1) * LANE]).astype(o_ref.dtype)


def _mla_sample_attn(q, c_past, kr_past, c_new, k_new, wuk, wuv, tk=1024):
    b, heads, sq, _ = q.shape
    past, kv_lora = c_past.shape[1:]
    tk = _tile(past, tk)
    kern = functools.partial(_mla_sample_kernel, heads=heads, sq=sq)
    return pl.pallas_call(
        kern,
        grid=(b, past // tk),
        in_specs=[pl.BlockSpec((1, heads, sq, 256), lambda i, j: (i, 0, 0, 0)),
                  pl.BlockSpec((1, tk, kv_lora), lambda i, j: (i, j, 0)),
                  pl.BlockSpec((1, tk, LANE), lambda i, j: (i, j, 0)),
                  pl.BlockSpec((1, sq, kv_lora), lambda i, j: (i, 0, 0)),
                  pl.BlockSpec((1, 1, sq, LANE), lambda i, j: (i, 0, 0, 1)),
                  _resident(wuk.shape), _resident(wuv.shape)],
        out_specs=pl.BlockSpec((1, sq, heads * LANE), lambda i, j: (i, 0, 0)),
        out_shape=jax.ShapeDtypeStruct((b, sq, heads * LANE), BF16),
        scratch_shapes=[pltpu.VMEM((heads * sq, kv_lora), BF16), pltpu.VMEM((heads * sq, LANE), BF16),
                        pltpu.VMEM((heads * sq, 1), F32), pltpu.VMEM((heads * sq, 1), F32),
                        pltpu.VMEM((heads * sq, kv_lora), F32)],
        compiler_params=_cp("parallel", "arbitrary"),
        name="mla_sample_attn",
    )(q, c_past, kr_past, c_new, k_new, wuk, wuv)


def _rwkv_pre_kernel(x_ref, gmix_ref, w_ref, sh0_ref, mu_ref, w0_ref, wb_ref, a0_ref, ab_ref, gb_ref,
                     kkp_ref, ka_ref, rk_ref, e_ref, et_ref,
                     r_ref, dec_ref, k_ref, v_ref, kk_ref, b_ref, g_ref, bonus_ref, sh_ref, last_ref, *, dim):
    si = pl.program_id(1)

    def put(ref, val):
        nb, ts, tiles, _ = ref.shape
        ref[...] = val.reshape(nb * ts, tiles, LANE).reshape(nb, ts, tiles, LANE)

    u = _rms(x_ref[0], gmix_ref[...]).astype(BF16)
    p = _dot(u, w_ref[...])
    tm = p.shape[0]

    @pl.when(si == 0)
    def _():
        last_ref[...] = sh0_ref[0]

    row = lax.broadcasted_iota(jnp.int32, p.shape, 0)
    prev = jnp.where(row == 0, last_ref[...], pltpu.roll(p, 1, axis=0))
    last_ref[...] = p[tm - 1:tm, :]
    sh_ref[0] = p[tm - 1:tm, :]
    xs = p + mu_ref[...] * (prev - p)
    r = xs[:, :dim]
    k = xs[:, dim:2 * dim]
    v = xs[:, 2 * dim:3 * dim]
    dw = xs[:, 3 * dim:3 * dim + LANE]
    da = xs[:, 3 * dim + LANE:3 * dim + 2 * LANE]
    dg = xs[:, 3 * dim + 2 * LANE:]
    z = -(w0_ref[...] + _dot(jnp.tanh(dw).astype(BF16), wb_ref[...]))
    softplus = jnp.maximum(z, 0.0) + jnp.log(1.0 + jnp.exp(-jnp.abs(z)))
    put(dec_ref, jnp.exp(-jnp.exp(-softplus - 0.5)))
    a = _sigmoid(a0_ref[...] + _dot(da.astype(BF16), ab_ref[...]))
    g_ref[0] = _dot(_sigmoid(dg).astype(BF16), gb_ref[...])
    kk = k * kkp_ref[...]
    nrm = jnp.sqrt(_seg_sum(kk * kk, e_ref[...], et_ref[...]))
    kk = kk / jnp.maximum(nrm, 1e-12)
    k2 = k * (1.0 + (a - 1.0) * ka_ref[...])
    put(r_ref, r)
    put(k_ref, k2)
    put(v_ref, v)
    put(kk_ref, kk)
    put(b_ref, kk * a)
    bonus_ref[0] = _seg_sum(r * k2 * rk_ref[...], e_ref[...], et_ref[...]) * v


def _rwkv_pre(x, gmix, w, sh0, mu, w0, wb, a0, ab, gb, kkp, ka, rk, e, et, tm=256):
    b, s, d = x.shape
    dim = w0.shape[1]
    width = w.shape[1]
    tm = _tile(s, tm)
    ts = min(SCAN_TS, s)
    assert tm % ts == 0 and dim % LANE == 0
    row = lambda i, j: (i, j, 0)
    act = pl.BlockSpec((1, tm, dim), row)
    act_shape = jax.ShapeDtypeStruct((b, s, dim), F32)
    tiles = dim // LANE
    assert tiles % 8 == 0
    seq = pl.BlockSpec((tm // ts, ts, tiles, LANE), lambda i, j: (j, 0, i, 0))
    seq_shape = jax.ShapeDtypeStruct((s // ts, ts, b * tiles, LANE), F32)
    consts = (gmix, w, mu, w0, wb, a0, ab, gb, kkp, ka, rk, e, et)
    return pl.pallas_call(
        functools.partial(_rwkv_pre_kernel, dim=dim),
        grid=(b, s // tm),
        in_specs=[pl.BlockSpec((1, tm, d), row), _resident(gmix.shape), _resident(w.shape),
                  pl.BlockSpec((1, 1, width), lambda i, j: (i, 0, 0))] + [_resident(c.shape) for c in consts[2:]],
        out_specs=[seq] * 6 + [act] * 2 + [pl.BlockSpec((1, 1, width), lambda i, j: (i, 0, 0))],
        out_shape=[seq_shape] * 6 + [act_shape] * 2 + [jax.ShapeDtypeStruct((b, 1, width), F32)],
        scratch_shapes=[pltpu.VMEM((1, width), F32)],
        compiler_params=_cp("parallel", "arbitrary"),
        name="rwkv_pre",
    )(x, gmix, w, sh0, *consts[2:])


def _rwkv_scan_kernel(r_ref, dec_ref, k_ref, v_ref, kk_ref, b_ref, s0_ref, o_ref, st_ref, xt_ref, ot_ref,
                      *, ts, n, unroll):
    tb = pl.program_id(1)
    hpl = LANE // n

    @pl.when(tb == 0)
    def _():
        st_ref[...] = s0_ref[...]

    srcs = (r_ref, dec_ref, k_ref, v_ref, kk_ref, b_ref)
    for i in range(ts):
        for a, ref in enumerate(srcs):
            xt_ref[a, i] = ref[i].T

    def step(i, c):
        for h2 in range(hpl):
            base = h2 * n
            vv = xt_ref[3, i, base:base + n, :]

            def p1(kx, sa):
                return sa + st_ref[h2, kx] * xt_ref[4, i, pl.ds(base + kx, 1), :]

            sa = lax.fori_loop(0, n, p1, jnp.zeros_like(vv), unroll=unroll)

            def p2(kx, o):
                s_new = (st_ref[h2, kx] * xt_ref[1, i, pl.ds(base + kx, 1), :]
                         - sa * xt_ref[5, i, pl.ds(base + kx, 1), :] + vv * xt_ref[2, i, pl.ds(base + kx, 1), :])
                st_ref[h2, kx] = s_new
                return o + s_new * xt_ref[0, i, pl.ds(base + kx, 1), :]

            ot_ref[i, base:base + n, :] = lax.fori_loop(0, n, p2, jnp.zeros_like(vv), unroll=unroll)
        return c

    lax.fori_loop(0, ts, step, 0)
    for i in range(ts):
        o_ref[i] = ot_ref[i].T


def _rwkv_scan(r, dec, k, v, kk, bb, s0):
    nt, ts, streams, _ = r.shape
    groups, hpl, n = s0.shape[:3]
    assert streams == groups * LANE
    seq = pl.BlockSpec((None, ts, LANE, LANE), lambda g, t: (t, 0, g, 0))
    state = pl.BlockSpec((None, hpl, n, n, LANE), lambda g, t: (g, 0, 0, 0, 0))
    return pl.pallas_call(
        functools.partial(_rwkv_scan_kernel, ts=ts, n=n, unroll=SCAN_UNROLL),
        grid=(groups, nt),
        in_specs=[seq] * 6 + [state],
        out_specs=[seq, state],
        out_shape=[jax.ShapeDtypeStruct(r.shape, F32), jax.ShapeDtypeStruct(s0.shape, F32)],
        scratch_shapes=[pltpu.VMEM((6, ts, LANE, LANE), F32), pltpu.VMEM((ts, LANE, LANE), F32)],
        compiler_params=_cp("parallel", "arbitrary"),
        name="rwkv_scan",
    )(r, dec, k, v, kk, bb, s0)


def _state_to_streams(wkv):
    b, h, n = wkv.shape[:3]
    hpl = LANE // n
    tiles = h // hpl
    bg = LANE // tiles
    g = b // bg
    t = wkv.reshape(g, bg, tiles, hpl, n, n)
    return t.transpose(0, 3, 5, 4, 1, 2).reshape(g, hpl, n, n, LANE)


def _state_from_streams(st, b, h):
    g, hpl, n = st.shape[:3]
    tiles = h // hpl
    bg = LANE // tiles
    t = st.reshape(g, hpl, n, n, bg, tiles)
    return t.transpose(0, 4, 5, 1, 3, 2).reshape(b, h, n, n)


def _mix_kernel(o_ref, bonus_ref, g_ref, ymla_ref, grw_ref, gmla_ref, lng_ref, lnb_ref, e_ref, et_ref,
                wr_ref, wm_ref, out_ref, *, inv_n):
    nb, ts, tiles, _ = o_ref.shape
    o = o_ref[...].reshape(nb * ts, tiles, LANE).reshape(nb * ts, tiles * LANE)
    e, et = e_ref[...], et_ref[...]
    d = o - _seg_sum(o, e, et) * inv_n
    var = _seg_sum(d * d, e, et) * inv_n
    on = d * lax.rsqrt(var + GN_EPS) * lng_ref[...] + lnb_ref[...]
    y = ((on + bonus_ref[0]) * g_ref[0]).astype(BF16)
    merged = _sigmoid(grw_ref[0]) * _dot(y, wr_ref[...]) + _sigmoid(gmla_ref[0]) * _dot(ymla_ref[0], wm_ref[...])
    out_ref[0] = merged.astype(out_ref.dtype)


def _mix(o, bonus, g, ymla, gates, lng, lnb, e, et, wr, wm, head_dim, tm=256):
    nt, ts = o.shape[:2]
    b, s, dim = bonus.shape
    tiles = dim // LANE
    d = wr.shape[1]
    md = ymla.shape[2]
    tm = _tile(s, tm)
    assert tm % ts == 0
    row = lambda i, j: (i, j, 0)
    return pl.pallas_call(
        functools.partial(_mix_kernel, inv_n=1.0 / head_dim),
        grid=(b, s // tm),
        in_specs=[pl.BlockSpec((tm // ts, ts, tiles, LANE), lambda i, j: (j, 0, i, 0)),
                  pl.BlockSpec((1, tm, dim), row), pl.BlockSpec((1, tm, dim), row), pl.BlockSpec((1, tm, md), row),
                  pl.BlockSpec((1, tm, d), lambda i, j: (i, j, 0)), pl.BlockSpec((1, tm, d), lambda i, j: (i, j, 1))]
                 + [_resident(c.shape) for c in (lng, lnb, e, et, wr, wm)],
        out_specs=pl.BlockSpec((1, tm, d), row),
        out_shape=jax.ShapeDtypeStruct((b, s, d), BF16),
        compiler_params=_cp("parallel", "parallel"),
        name="rwkv_post_mix",
    )(o, bonus, g, ymla, gates, gates, lng, lnb, e, et, wr, wm)


def _outproj_kernel(x_ref, m_ref, wo_ref, gc_ref, wq_ref, h_ref, q_ref):
    h = x_ref[...] + _dot(m_ref[...], wo_ref[...])
    h_ref[...] = h
    q_ref[...] = _dot(_rms(h, gc_ref[...]).astype(BF16), wq_ref[...]).astype(q_ref.dtype)


def _outproj(x, merged, wo, gc, wq, tm=256):
    t, d = x.shape
    tm = _tile(t, tm)
    row = lambda i: (i, 0)
    return pl.pallas_call(
        _outproj_kernel,
        grid=(t // tm,),
        in_specs=[pl.BlockSpec((tm, d), row), pl.BlockSpec((tm, d), row),
                  _resident(wo.shape), _resident(gc.shape), _resident(wq.shape)],
        out_specs=[pl.BlockSpec((tm, d), row), pl.BlockSpec((tm, d), row)],
        out_shape=[jax.ShapeDtypeStruct((t, d), F32), jax.ShapeDtypeStruct((t, d), BF16)],
        compiler_params=_cp("parallel"),
        name="outproj_crossq",
    )(x, merged, wo, gc, wq)


def _route_tile(lg, n_groups, per_group):
    n_exp = n_groups * per_group
    lane = lax.broadcasted_iota(jnp.int32, lg.shape, 1)
    first = lambda mask: jnp.min(jnp.where(mask, lane, 2 * LANE), axis=-1, keepdims=True)
    top = lambda mask: jnp.max(jnp.where(mask, lg, NEG), axis=-1, keepdims=True)
    is_g = lane < n_groups
    gmax = top(is_g)
    g_sel = first(is_g & (lg == gmax))
    p_grp = 1.0 / jnp.sum(jnp.where(is_g, jnp.exp(lg - gmax), 0.0), axis=-1, keepdims=True)
    eid = lane - n_groups
    in_grp = (eid >= g_sel * per_group) & (eid < (g_sel + 1) * per_group)
    v1 = top(in_grp)
    e1 = first(in_grp & (lg == v1))
    rest = in_grp & (lane != e1)
    v2 = top(rest)
    e2 = first(rest & (lg == v2))
    t = jnp.exp(v2 - v1)
    w1 = p_grp / (1.0 + t)
    w2 = w1 * t
    route = jnp.where(lane == 0, (e1 - n_groups).astype(F32),
                      jnp.where(lane == 1, (e2 - n_groups).astype(F32),
                                jnp.where(lane == 2, w1, jnp.where(lane == 3, w2, 0.0))))
    hits = jnp.where((lane == e1 - n_groups) | (lane == e2 - n_groups), 1.0, 0.0)
    assert n_exp <= LANE
    return route, jnp.sum(hits, axis=0, keepdims=True)


def _cross_kernel(q_ref, mk_ref, mv_ref, h_ref, wo_ref, gm_ref, wr_ref, br_ref, h2_ref, hn_ref, rt_ref, cnt_ref,
                  *, heads, scale, n_groups, per_group):
    q = q_ref[0]
    mk = mk_ref[0]
    mv = mv_ref[0]
    hd = q.shape[1] // heads
    outs = []
    for h in range(heads):
        sl = slice(h * hd, (h + 1) * hd)
        s = _dot_t(q[:, sl], mk[:, sl]) * scale
        p = jnp.exp(s - jnp.max(s, axis=-1, keepdims=True))
        p = p / jnp.sum(p, axis=-1, keepdims=True)
        outs.append(_dot(p.astype(BF16), mv[:, sl]).astype(BF16))
    o = jnp.concatenate(outs, axis=1)
    h2 = h_ref[0] + _dot(o, wo_ref[...])
    h2_ref[0] = h2
    hn = _rms(h2, gm_ref[...]).astype(BF16)
    hn_ref[0] = hn
    route, hist = _route_tile(_dot(hn, wr_ref[...]) + br_ref[...], n_groups, per_group)
    rt_ref[0] = route
    cnt_ref[0, 0] = jnp.broadcast_to(hist, cnt_ref.shape[2:])


def _cross(q, mk, mv, h, wo, gm, wr, br, heads, n_groups, per_group, tm=256):
    b, s, d = h.shape
    n_mem = mk.shape[1]
    tm = _tile(s, tm)
    row = lambda i, j: (i, j, 0)
    mem = pl.BlockSpec((1, n_mem, d), lambda i, j: (i, 0, 0))
    kern = functools.partial(_cross_kernel, heads=heads, scale=float(d // heads) ** -0.5,
                             n_groups=n_groups, per_group=per_group)
    return pl.pallas_call(
        kern,
        grid=(b, s // tm),
        in_specs=[pl.BlockSpec((1, tm, d), row), mem, mem, pl.BlockSpec((1, tm, d), row),
                  _resident(wo.shape), _resident(gm.shape), _resident(wr.shape), _resident(br.shape)],
        out_specs=[pl.BlockSpec((1, tm, d), row), pl.BlockSpec((1, tm, d), row), pl.BlockSpec((1, tm, LANE), row),
                   pl.BlockSpec((1, 1, 8, LANE), lambda i, j: (i, j, 0, 0))],
        out_shape=[jax.ShapeDtypeStruct((b, s, d), F32), jax.ShapeDtypeStruct((b, s, d), BF16),
                   jax.ShapeDtypeStruct((b, s, LANE), F32), jax.ShapeDtypeStruct((b, s // tm, 8, LANE), F32)],
        compiler_params=_cp("parallel", "parallel"),
        name="cross_attn_router",
    )(q, mk, mv, h, wo, gm, wr, br)


def _moe_kernel(be_ref, on_ref, x_ref, sw_ref, wg_ref, wu_ref, wd_ref, y_ref, wgb_ref, wub_ref, wdb_ref):
    i = pl.program_id(0)

    @pl.when((i == 0) | (be_ref[i] != be_ref[jnp.maximum(i - 1, 0)]))
    def _():
        wgb_ref[...] = wg_ref[0].astype(BF16)
        wub_ref[...] = wu_ref[0].astype(BF16)
        wdb_ref[...] = wd_ref[0].astype(BF16)

    @pl.when(on_ref[i] != 0)
    def _():
        xb = x_ref[...]
        g = _dot(xb, wgb_ref[...])
        u = _dot(xb, wub_ref[...])
        hb = (g * _sigmoid(g) * u).astype(BF16)
        y_ref[...] = _dot(hb, wdb_ref[...]) * sw_ref[...]

    @pl.when(on_ref[i] == 0)
    def _():
        y_ref[...] = jnp.zeros(y_ref.shape, y_ref.dtype)


def _moe_experts(blk_e, blk_on, x_slot, slot_w, wg, wu, wd):
    ns, d = x_slot.shape
    f = wg.shape[2]
    n_blk = ns // MOE_ROWS
    wspec = lambda shp: pl.BlockSpec((1,) + shp, lambda i, be, on: (be[i], 0, 0))
    grid_spec = pltpu.PrefetchScalarGridSpec(
        num_scalar_prefetch=2,
        grid=(n_blk,),
        in_specs=[pl.BlockSpec((MOE_ROWS, d), lambda i, be, on: (i, 0)),
                  pl.BlockSpec((MOE_ROWS, 1), lambda i, be, on: (i, 0)),
                  wspec((d, f)), wspec((d, f)), wspec((f, d))],
        out_specs=pl.BlockSpec((MOE_ROWS, d), lambda i, be, on: (i, 0)),
        scratch_shapes=[pltpu.VMEM((d, f), BF16), pltpu.VMEM((d, f), BF16), pltpu.VMEM((f, d), BF16)],
    )
    return pl.pallas_call(
        _moe_kernel,
        grid_spec=grid_spec,
        out_shape=jax.ShapeDtypeStruct((ns, d), F32),
        compiler_params=_cp("arbitrary"),
        name="moe_experts",
    )(blk_e, blk_on, x_slot, slot_w, wg, wu, wd)


def _combine_kernel(idx_ref, h_ref, g_ref, y_hbm, o_ref, buf_ref, sem_ref, *, tm, norm):
    def issue(r, c):
        for ch in range(TOP_K):
            slot = idx_ref[0, TOP_K * r + ch]
            pltpu.make_async_copy(y_hbm.at[pl.ds(slot, 1)], buf_ref.at[ch, pl.ds(r, 1)], sem_ref.at[ch]).start()
        return c

    lax.fori_loop(0, tm, issue, 0, unroll=8)
    acc = h_ref[...]
    for ch in range(TOP_K):
        pltpu.make_async_copy(y_hbm.at[pl.ds(0, tm)], buf_ref.at[ch], sem_ref.at[ch]).wait()
        acc = acc + buf_ref[ch]
    o_ref[...] = _rms(acc, g_ref[...]) if norm else acc


def _combine(h, y_slot, slot_of, g, tm=256):
    t, d = h.shape
    tm = _tile(t, tm)
    idx = slot_of.reshape(t // tm, 1, TOP_K * tm)
    norm = g is not None
    g = g if norm else jnp.ones((1, d), F32)
    return pl.pallas_call(
        functools.partial(_combine_kernel, tm=tm, norm=norm),
        grid=(t // tm,),
        in_specs=[pl.BlockSpec((None, 1, TOP_K * tm), lambda i: (i, 0, 0), memory_space=pltpu.SMEM),
                  pl.BlockSpec((tm, d), lambda i: (i, 0)),
                  pl.BlockSpec((1, d), lambda i: (0, 0)),
                  pl.BlockSpec(memory_space=pl.ANY)],
        out_specs=pl.BlockSpec((tm, d), lambda i: (i, 0)),
        out_shape=jax.ShapeDtypeStruct((t, d), F32),
        scratch_shapes=[pltpu.VMEM((TOP_K, tm, d), F32), pltpu.SemaphoreType.DMA((TOP_K,))],
        compiler_params=_cp("arbitrary"),
        name="moe_combine_norm",
    )(idx, h, g, y_slot)


def _pad_cols(w, n):
    return jnp.pad(w, ((0, 0),) * (w.ndim - 1) + ((0, n - w.shape[-1]),))


def _pad_rows(w, n):
    return jnp.pad(w, ((0, n - w.shape[0]), (0, 0)))


def _rot_half_cols(w):
    half = w.shape[-1] // 2
    return jnp.concatenate([-w[..., half:], w[..., :half]], axis=-1)


def _rope_table(pos):
    half = ROPE // 2
    inv = ROPE_THETA ** (-jnp.arange(half, dtype=F32) / half)
    ang = pos.astype(F32)[:, None] * inv[None, :]
    cos, sin = jnp.cos(ang), jnp.sin(ang)
    return jnp.concatenate([cos, cos, sin, sin], axis=1)


def _prep_layer(lp, dims):
    ql, kl, rope, dim, dl, il, gl, d = (dims[k] for k in ("q_lora", "kv_lora", "rope", "rw_dim", "decay_lora",
                                                          "iclr_lora", "gate_lora", "d_model"))
    w_in = lp["w_in"]
    i0, i1, i2 = ql, ql + kl, ql + kl + rope
    i3 = i2 + 3 * dim + dl + il + gl
    w_kr = w_in[:, i1:i2]
    out = {}
    out["w_mla"] = jnp.concatenate(
        [w_in[:, :i0], w_kr, _rot_half_cols(w_kr), w_in[:, i0:i1]], axis=1).astype(BF16)
    rw = w_in[:, i2:i3]
    c3 = 3 * dim
    glp = -(-gl // LANE) * LANE

    def regroup(t):
        return jnp.concatenate([t[..., :c3], _pad_cols(t[..., c3:c3 + dl], LANE),
                                _pad_cols(t[..., c3 + dl:c3 + dl + il], LANE),
                                _pad_cols(t[..., c3 + dl + il:], glp)], axis=-1)

    out["regroup"] = regroup
    out["w_rw"] = regroup(rw).astype(BF16)
    out["mu"] = regroup(lp["rw_mu"][None, :])
    out["w_gates"] = w_in[:, i3:].astype(BF16)
    heads = lp["w_uq"].shape[1]
    nope = lp["w_uq"].shape[2] - rope
    wq = lp["w_uq"]
    out["w_q"] = jnp.concatenate([wq[..., :nope], wq[..., nope:], _rot_half_cols(wq[..., nope:])],
                                 axis=-1).reshape(ql, heads * 256).astype(BF16)
    out["w_uk"] = lp["w_uk"].reshape(kl, -1).astype(BF16)
    out["w_uv"] = lp["w_uv"].reshape(kl, -1).astype(BF16)
    out["w_uv_t"] = out["w_uv"].T
    out["wb"] = _pad_rows(lp["rw_wb"], LANE).astype(BF16)
    out["ab"] = _pad_rows(lp["rw_ab"], LANE).astype(BF16)
    out["gb"] = _pad_rows(lp["rw_gb"], glp).astype(BF16)
    n_heads = dims["rw_heads"]
    hd = dim // n_heads
    e = (jnp.arange(dim)[:, None] // hd == jnp.arange(LANE)[None, :]).astype(BF16)
    out["e"], out["et"] = e, e.T
    out["w_br_rwkv"] = lp["w_br_rwkv"].astype(BF16)
    out["w_br_mla"] = lp["w_br_mla"].astype(BF16)
    out["w_out"] = lp["w_out"].astype(BF16)
    out["w_mq"] = lp["w_mq"].reshape(d, d).astype(BF16)
    out["w_mkv"] = jnp.concatenate([lp["w_mk"].reshape(d, d), lp["w_mv"].reshape(d, d)], axis=1).astype(BF16)
    out["w_mo"] = lp["w_mo"].reshape(d, d).astype(BF16)
    n_g, n_e = lp["w_rg"].shape[1], lp["w_re"].shape[1]
    out["w_r"] = _pad_cols(jnp.concatenate([lp["w_rg"], lp["w_re"]], axis=1), LANE).astype(BF16)
    out["b_r"] = _pad_cols(jnp.concatenate([lp["b_rg"], lp["b_re"]])[None, :], LANE)
    out["w_eg"], out["w_eu"], out["w_ed"] = lp["w_eg"], lp["w_eu"], lp["w_ed"]
    return out


def _row(v):
    return v.reshape(1, -1)


def _mixer(x, lp, wp, dims, pos, shift0, wkv0, attend):
    b, s, d = x.shape
    heads, rw_heads = dims["mla_heads"], dims["rw_heads"]
    gmix = _row(lp["g_mix"])
    cs = _rope_table(pos)
    scale = float(dims["nope"] + dims["rope"]) ** -0.5
    q, k, v, c, kr = _mla_proj(x, gmix, wp["w_mla"], _row(lp["g_q"]), wp["w_q"], _row(lp["g_kv"]),
                               wp["w_uk"], wp["w_uv_t"], cs, heads, scale)
    y_mla = attend(q, k, v, c)
    sh0 = wp["regroup"](shift0)
    r, dec, k2, vr, kk, bb, g, bonus, sh = _rwkv_pre(
        x, gmix, wp["w_rw"], sh0, wp["mu"], _row(lp["rw_w0"]), wp["wb"], _row(lp["rw_a0"]), wp["ab"], wp["gb"],
        _row(lp["rw_kk"]), _row(lp["rw_ka"]), _row(lp["rw_rk"]), wp["e"], wp["et"])
    n = dims["rw_dim"] // rw_heads
    o, s_t = _rwkv_scan(r, dec, k2, vr, kk, bb, _state_to_streams(wkv0))
    wkv = _state_from_streams(s_t, b, rw_heads)
    gates = _mm_norm(x.reshape(b * s, d), gmix, wp["w_gates"]).reshape(b, s, 2 * d)
    dim = dims["rw_dim"]
    merged = _mix(o, bonus, g, y_mla, gates, _row(lp["rw_ln_g"]), _row(lp["rw_ln_b"]), wp["e"], wp["et"],
                  wp["w_br_rwkv"], wp["w_br_mla"], n).reshape(b * s, d)
    h, qc = _outproj(x.reshape(b * s, d), merged, wp["w_out"], _row(lp["g_cross"]), wp["w_mq"])
    c3, dl, il = 3 * dim, dims["decay_lora"], dims["iclr_lora"]
    shift = jnp.concatenate([sh[..., :c3], sh[..., c3:c3 + dl], sh[..., c3 + LANE:c3 + LANE + il],
                             sh[..., c3 + 2 * LANE:c3 + 2 * LANE + dims["gate_lora"]]], axis=-1)
    return h.reshape(b, s, d), qc.reshape(b, s, d), c, kr, k, wkv, shift


def _moe(hn, route, counts, wp):
    t, d = hn.shape
    n_exp = counts.shape[0]
    e_idx, wts = route[:, :TOP_K], route[:, TOP_K:2 * TOP_K]
    m = t * TOP_K
    n_blk = -(-(m + n_exp * (MOE_ROWS - 1)) // MOE_ROWS)
    ns = n_blk * MOE_ROWS
    flat_e = e_idx.reshape(m).astype(jnp.int32)
    order = jnp.argsort(flat_e).astype(jnp.int32)
    padded = (counts + MOE_ROWS - 1) // MOE_ROWS * MOE_ROWS
    pad_end = jnp.cumsum(padded)
    pad_start = pad_end - padded
    start = jnp.cumsum(counts) - counts
    blk_start = jnp.arange(n_blk, dtype=jnp.int32) * MOE_ROWS
    blk_e = jnp.minimum(jnp.sum(pad_end[None, :] <= blk_start[:, None], axis=1), n_exp - 1).astype(jnp.int32)
    blk_on = (blk_start < pad_end[-1]).astype(jnp.int32)
    off = (blk_start - pad_start[blk_e])[:, None] + jnp.arange(MOE_ROWS, dtype=jnp.int32)[None, :]
    valid = ((off < counts[blk_e][:, None]) & (blk_on[:, None] != 0)).reshape(ns)
    asg = order[jnp.clip(start[blk_e][:, None] + off, 0, m - 1).reshape(ns)]
    slot_tok = jnp.where(valid, asg // TOP_K, jnp.arange(ns, dtype=jnp.int32) % t)
    slot_w = jnp.where(valid, wts.reshape(m)[asg], 0.0)
    y_slot = _moe_experts(blk_e, blk_on, hn[slot_tok], slot_w[:, None], wp["w_eg"], wp["w_eu"], wp["w_ed"])
    key = jnp.where(valid, asg, m + jnp.arange(ns, dtype=jnp.int32))
    slot_of = jnp.argsort(key)[:m].astype(jnp.int32)
    return y_slot, slot_of


def kernel(x_prompt, x_sample, mem_prompt, cache_kv_latent, cache_k_rope, cache_mem_k, cache_mem_v, state_wkv, state_shift, g_mix, w_in, g_q, w_uq, g_kv, w_uk, w_uv, rw_mu, rw_w0, rw_wb, rw_a0, rw_ab, rw_gb, rw_kk, rw_ka, rw_rk, rw_ln_g, rw_ln_b, w_br_rwkv, w_br_mla, w_out, g_cross, g_mem, w_mq, w_mk, w_mv, w_mo, g_moe, w_rg, b_rg, w_re, b_re, w_eg, w_eu, w_ed, g_final):
    depth = g_mix.shape[0]
    b, s, d = x_prompt.shape
    bs, ss, _ = x_sample.shape
    past = cache_kv_latent.shape[2]
    n_mem = mem_prompt.shape[1]
    mem_heads = w_mq.shape[2]
    rw_heads, rw_hd = rw_rk.shape[1:]
    dims = dict(d_model=d, q_lora=g_q.shape[1], kv_lora=g_kv.shape[1], mla_heads=w_uq.shape[2],
                nope=w_uk.shape[3], rope=cache_k_rope.shape[3], rw_heads=rw_heads, rw_dim=rw_heads * rw_hd,
                decay_lora=rw_wb.shape[1], iclr_lora=rw_ab.shape[1], gate_lora=rw_gb.shape[1])
    assert dims["rope"] == ROPE and dims["nope"] == LANE and w_uv.shape[3] == LANE
    assert dims["decay_lora"] <= LANE and dims["iclr_lora"] <= LANE and rw_heads <= LANE
    n_groups, n_exp = w_rg.shape[2], w_re.shape[2]
    per_group = n_exp // n_groups
    stacked = dict(g_mix=g_mix, w_in=w_in, g_q=g_q, w_uq=w_uq, g_kv=g_kv, w_uk=w_uk, w_uv=w_uv, rw_mu=rw_mu,
                   rw_w0=rw_w0, rw_wb=rw_wb, rw_a0=rw_a0, rw_ab=rw_ab, rw_gb=rw_gb, rw_kk=rw_kk, rw_ka=rw_ka,
                   rw_rk=rw_rk, rw_ln_g=rw_ln_g, rw_ln_b=rw_ln_b, w_br_rwkv=w_br_rwkv, w_br_mla=w_br_mla,
                   w_out=w_out, g_cross=g_cross, g_mem=g_mem, w_mq=w_mq, w_mk=w_mk, w_mv=w_mv, w_mo=w_mo,
                   g_moe=g_moe, w_rg=w_rg, b_rg=b_rg, w_re=w_re, b_re=b_re, w_eg=w_eg, w_eu=w_eu, w_ed=w_ed)
    h_p, h_s = x_prompt, x_sample
    outs = [[] for _ in range(10)]
    for l in range(depth):
        lp = {name: val[l] for name, val in stacked.items()}
        wp = _prep_layer(lp, dims)

        h, qc, c, kr, _, wkv, shift = _mixer(
            h_p, lp, wp, dims, jnp.arange(s), jnp.zeros((b, 1, state_shift.shape[3]), F32),
            jnp.zeros((b, rw_heads, rw_hd, rw_hd), F32), lambda q, k, v, c_new: _mla_prompt_attn(q, k, v))
        mkv = _mm_norm(mem_prompt.reshape(b * n_mem, d), _row(lp["g_mem"]), wp["w_mkv"])
        mk, mv = mkv[:, :d].reshape(b, n_mem, d), mkv[:, d:].reshape(b, n_mem, d)
        gm, wr, br = _row(lp["g_moe"]), wp["w_r"], wp["b_r"]
        h2_p, hn_p, rt_p, cnt_p = _cross(qc, mk.astype(BF16), mv.astype(BF16), h, wp["w_mo"], gm, wr, br, mem_heads,
                                         n_groups, per_group)
        for lst, val in zip(outs[:6], (c, kr, mk.reshape(b, n_mem, mem_heads, -1),
                                       mv.reshape(b, n_mem, mem_heads, -1), wkv, shift)):
            lst.append(val)

        kr_past = jnp.pad(cache_k_rope[l], ((0, 0), (0, 0), (0, LANE - ROPE))).astype(BF16)

        def attend_sample(q, k, v, c_new, l=l, kr_past=kr_past, wp=wp):
            return _mla_sample_attn(q, cache_kv_latent[l], kr_past, c_new, k, wp["w_uk"], wp["w_uv"])

        h, qc, c, kr, _, wkv, shift = _mixer(h_s, lp, wp, dims, past + jnp.arange(ss), state_shift[l],
                                             state_wkv[l], attend_sample)
        h2_s, hn_s, rt_s, cnt_s = _cross(qc, cache_mem_k[l].reshape(bs, n_mem, d).astype(BF16),
                                         cache_mem_v[l].reshape(bs, n_mem, d).astype(BF16), h, wp["w_mo"], gm, wr, br,
                                         mem_heads, n_groups, per_group)
        for lst, val in zip(outs[6:], (c, kr, wkv, shift)):
            lst.append(val)

        tp, tsm = b * s, bs * ss
        hn = jnp.concatenate([hn_p.reshape(tp, d), hn_s.reshape(tsm, d)], axis=0)
        route = jnp.concatenate([rt_p.reshape(tp, LANE), rt_s.reshape(tsm, LANE)], axis=0)
        counts = (jnp.sum(cnt_p[:, :, 0, :n_exp], axis=(0, 1)) + jnp.sum(cnt_s[:, :, 0, :n_exp], axis=(0, 1)))
        y_slot, slot_of = _moe(hn, route, counts.astype(jnp.int32), wp)
        g_fin = _row(g_final) if l == depth - 1 else None
        h_p = _combine(h2_p.reshape(tp, d), y_slot, slot_of[:tp * TOP_K], g_fin).reshape(b, s, d)
        h_s = _combine(h2_s.reshape(tsm, d), y_slot, slot_of[tp * TOP_K:], g_fin).reshape(bs, ss, d)
    stacks = [jnp.stack(o) for o in outs]
    return (h_p, h_s, *stacks)
```

```python
import functools

import jax
import jax.numpy as jnp
from jax import lax
from jax.experimental import pallas as pl
from jax.experimental.pallas import tpu as pltpu

F32 = jnp.float32
BF16 = jnp.bfloat16

EPS = 1e-6
GN_EPS = 64e-5
ROPE_THETA = 10000.0
CHUNK = 64
TOP_K = 2
NEG = -1e30

LANE = 128
V7X_VMEM_LIMIT = 56 << 20
ROPE = 64
MOE_ROWS = 256
SCAN_TS = 16
SCAN_UNROLL = 16


def _cp(*sem):
    return pltpu.CompilerParams(dimension_semantics=sem, vmem_limit_bytes=V7X_VMEM_LIMIT)


def _tile(n, pref):
    if n <= pref:
        return n
    for t in range(pref, 7, -1):
        if n % t == 0 and t % 8 == 0:
            return t
    return n


def _resident(shape):
    nd = len(shape)
    return pl.BlockSpec(shape, lambda *_: (0,) * nd, pipeline_mode=pl.Buffered(1))


def _rms(x, g):
    return x * lax.rsqrt(jnp.mean(x * x, axis=-1, keepdims=True) + EPS) * g


def _sigmoid(x):
    return 1.0 / (1.0 + jnp.exp(-x))


def _dot(a, b):
    return jnp.dot(a, b, preferred_element_type=F32)


def _dot_t(a, b):
    return lax.dot_general(a, b, (((1,), (1,)), ((), ())), preferred_element_type=F32)


def _seg_sum(x, e, et):
    hi = x.astype(BF16)
    lo = (x - hi.astype(F32)).astype(BF16)
    s = _dot(hi, e) + _dot(lo, e)
    shi = s.astype(BF16)
    slo = (s - shi.astype(F32)).astype(BF16)
    return _dot(shi, et) + _dot(slo, et)


def _mm_norm_kernel(x_ref, g_ref, w_ref, o_ref, xn_ref, *, gate):
    @pl.when(pl.program_id(1) == 0)
    def _():
        xn_ref[...] = _rms(x_ref[...], g_ref[...]).astype(BF16)

    y = _dot(xn_ref[...], w_ref[...])
    o_ref[...] = (_sigmoid(y) if gate else y).astype(o_ref.dtype)


def _mm_norm(x, g, w, tm=1024, tn=512, out_dtype=F32, gate=False):
    m, k = x.shape
    n = w.shape[1]
    tm, tn = _tile(m, tm), _tile(n, tn)
    return pl.pallas_call(
        functools.partial(_mm_norm_kernel, gate=gate),
        grid=(m // tm, n // tn),
        in_specs=[pl.BlockSpec((tm, k), lambda i, j: (i, 0)),
                  pl.BlockSpec((1, k), lambda i, j: (0, 0)),
                  pl.BlockSpec((k, tn), lambda i, j: (0, j))],
        out_specs=pl.BlockSpec((tm, tn), lambda i, j: (i, j)),
        out_shape=jax.ShapeDtypeStruct((m, n), out_dtype),
        scratch_shapes=[pltpu.VMEM((tm, k), BF16)],
        compiler_params=_cp("parallel", "arbitrary"),
        name="mm_norm",
    )(x, g, w)


def _mla_proj_kernel(x_ref, gmix_ref, wm_ref, gq_ref, wq_ref, gkv_ref, wuk_ref, wuv_ref, cs_ref,
                     q_ref, k_ref, v_ref, c_ref, kr_ref, *, heads, q_lora, scale):
    u = _rms(x_ref[0], gmix_ref[...]).astype(BF16)
    p = _dot(u, wm_ref[...])
    cs = cs_ref[...]
    lane = lax.broadcasted_iota(jnp.int32, cs.shape, 1)

    def rope_tile(t2):
        t = t2 * cs
        return jnp.where(lane < ROPE, t + pltpu.roll(t, ROPE, axis=1), 0.0)

    krt = rope_tile(p[:, q_lora:q_lora + LANE])
    kr_ref[0] = krt[:, :ROPE]
    krb = krt.astype(BF16)
    c = _rms(p[:, q_lora + LANE:], gkv_ref[...])
    c_ref[0] = c
    cb = c.astype(BF16)
    kn = _dot(cb, wuk_ref[...])
    vt = _dot_t(wuv_ref[...], cb)
    qn = _rms(p[:, :q_lora], gq_ref[...]).astype(BF16)
    qf = _dot(qn, wq_ref[...])
    for h in range(heads):
        q_ref[0, h, :, :LANE] = (qf[:, h * 256:h * 256 + LANE] * scale).astype(BF16)
        q_ref[0, h, :, LANE:] = (rope_tile(qf[:, h * 256 + LANE:(h + 1) * 256]) * scale).astype(BF16)
        k_ref[0, h, :, :LANE] = kn[:, h * LANE:(h + 1) * LANE].astype(BF16)
        k_ref[0, h, :, LANE:] = krb
        v_ref[0, h] = vt[h * LANE:(h + 1) * LANE, :].astype(BF16)


def _mla_proj(x, gmix, wm, gq, wq, gkv, wuk, wuv, cs, heads, scale, tm=256):
    b, s, d = x.shape
    q_lora = gq.shape[1]
    kv_lora = gkv.shape[1]
    tm = _tile(s, tm)
    kern = functools.partial(_mla_proj_kernel, heads=heads, q_lora=q_lora, scale=scale)
    return pl.pallas_call(
        kern,
        grid=(b, s // tm),
        in_specs=[pl.BlockSpec((1, tm, d), lambda i, j: (i, j, 0)),
                  _resident(gmix.shape), _resident(wm.shape), _resident(gq.shape), _resident(wq.shape),
                  _resident(gkv.shape), _resident(wuk.shape), _resident(wuv.shape),
                  pl.BlockSpec((tm, LANE), lambda i, j: (j, 0))],
        out_specs=[pl.BlockSpec((1, heads, tm, 256), lambda i, j: (i, 0, j, 0)),
                   pl.BlockSpec((1, heads, tm, 256), lambda i, j: (i, 0, j, 0)),
                   pl.BlockSpec((1, heads, LANE, tm), lambda i, j: (i, 0, 0, j)),
                   pl.BlockSpec((1, tm, kv_lora), lambda i, j: (i, j, 0)),
                   pl.BlockSpec((1, tm, ROPE), lambda i, j: (i, j, 0))],
        out_shape=[jax.ShapeDtypeStruct((b, heads, s, 256), BF16),
                   jax.ShapeDtypeStruct((b, heads, s, 256), BF16),
                   jax.ShapeDtypeStruct((b, heads, LANE, s), BF16),
                   jax.ShapeDtypeStruct((b, s, kv_lora), F32),
                   jax.ShapeDtypeStruct((b, s, ROPE), F32)],
        compiler_params=_cp("parallel", "parallel"),
        name="mla_proj",
    )(x, gmix, wm, gq, wq, gkv, wuk, wuv, cs)


def _flash_kernel(q_ref, k_ref, vt_ref, o_ref, m_ref, l_ref, acc_ref, *, tq, hps):
    qi = pl.program_id(2)
    m_ref[...] = jnp.full(m_ref.shape, NEG, F32)
    l_ref[...] = jnp.zeros(l_ref.shape, F32)
    acc_ref[...] = jnp.zeros(acc_ref.shape, F32)

    def tile(j, diag):
        start = pl.multiple_of(j * tq, tq)
        scores = [_dot_t(k_ref[0, h, pl.ds(start, tq), :], q_ref[0, h]) for h in range(hps)]
        for h in range(hps):
            s = scores[h]
            if diag:
                kc = lax.broadcasted_iota(jnp.int32, s.shape, 0) // CHUNK
                qc = lax.broadcasted_iota(jnp.int32, s.shape, 1) // CHUNK
                s = jnp.where(kc <= qc, s, NEG)
            m = m_ref[h]
            m_new = jnp.maximum(m, jnp.max(s, axis=0, keepdims=True))
            alpha = jnp.exp(m - m_new)
            p = jnp.exp(s - m_new)
            l_ref[h] = l_ref[h] * alpha + jnp.sum(p, axis=0, keepdims=True)
            acc_ref[h] = acc_ref[h] * alpha + _dot(vt_ref[0, h, :, pl.ds(start, tq)], p.astype(BF16))
            m_ref[h] = m_new

    def body(j, c):
        tile(j, False)
        return c

    lax.fori_loop(0, qi, body, 0)
    tile(qi, True)
    for h in range(hps):
        o_ref[0, :, h * LANE:(h + 1) * LANE] = (acc_ref[h] * (1.0 / l_ref[h])).T.astype(o_ref.dtype)


def _mla_prompt_attn(q, k, vt, tq=512, hps=8):
    b, heads, s, _ = q.shape
    tq = _tile(s, tq)
    hps = min(hps, heads)
    assert tq % CHUNK == 0 and heads % hps == 0
    return pl.pallas_call(
        functools.partial(_flash_kernel, tq=tq, hps=hps),
        grid=(b, heads // hps, s // tq),
        in_specs=[pl.BlockSpec((1, hps, tq, 256), lambda i, h, j: (i, h, j, 0)),
                  pl.BlockSpec((1, hps, s, 256), lambda i, h, j: (i, h, 0, 0)),
                  pl.BlockSpec((1, hps, LANE, s), lambda i, h, j: (i, h, 0, 0))],
        out_specs=pl.BlockSpec((1, tq, hps * LANE), lambda i, h, j: (i, j, h)),
        out_shape=jax.ShapeDtypeStruct((b, s, heads * LANE), BF16),
        scratch_shapes=[pltpu.VMEM((hps, 1, tq), F32), pltpu.VMEM((hps, 1, tq), F32),
                        pltpu.VMEM((hps, LANE, tq), F32)],
        compiler_params=_cp("parallel", "parallel", "arbitrary"),
        name="mla_prompt_attn",
    )(q, k, vt)


def _mla_sample_kernel(q_ref, cp_ref, krp_ref, cn_ref, krn_ref, wuk_ref, wuv_ref, o_ref,
                       qa_ref, qr_ref, m_ref, l_ref, acc_ref, *, heads, sq):
    kb = pl.program_id(1)

    @pl.when(kb == 0)
    def _():
        for h in range(heads):
            qh = q_ref[0, h]
            qa_ref[h * sq:(h + 1) * sq, :] = _dot_t(qh[:, :LANE], wuk_ref[:, h * LANE:(h + 1) * LANE]).astype(BF16)
            qr_ref[h * sq:(h + 1) * sq, :] = qh[:, LANE:]
        m_ref[...] = jnp.full(m_ref.shape, NEG, F32)
        l_ref[...] = jnp.zeros(l_ref.shape, F32)
        acc_ref[...] = jnp.zeros(acc_ref.shape, F32)

    def update(cb, krb):
        s = _dot_t(qa_ref[...], cb) + _dot_t(qr_ref[...], krb)
        m_new = jnp.maximum(m_ref[...], jnp.max(s, axis=-1, keepdims=True))
        alpha = jnp.exp(m_ref[...] - m_new)
        p = jnp.exp(s - m_new)
        l_ref[...] = l_ref[...] * alpha + jnp.sum(p, axis=-1, keepdims=True)
        acc_ref[...] = acc_ref[...] * alpha + _dot(p.astype(BF16), cb)
        m_ref[...] = m_new

    update(cp_ref[0].astype(BF16), krp_ref[0])

    @pl.when(kb == pl.num_programs(1) - 1)
    def _():
        update(cn_ref[0].astype(BF16), krn_ref[0, 0])
        o_lat = (acc_ref[...] / l_ref[...]).astype(BF16)
        for h in range(heads):
            o_ref[0, :, h * LANE:(h + 1) * LANE] = _dot(
                o_lat[h * sq:(h + 1) * sq], wuv_ref[:, h * LANE:(h + 1) * LANE]).astype(o_ref.dtype)


def _mla_sample_attn(q, c_past, kr_past, c_new, k_new, wuk, wuv, tk=1024):
    b, heads, sq, _ = q.shape
    past, kv_lora = c_past.shape[1:]
    tk = _tile(past, tk)
    kern = functools.partial(_mla_sample_kernel, heads=heads, sq=sq)
    return pl.pallas_call(
        kern,
        grid=(b, past // tk),
        in_specs=[pl.BlockSpec((1, heads, sq, 256), lambda i, j: (i, 0, 0, 0)),
                  pl.BlockSpec((1, tk, kv_lora), lambda i, j: (i, j, 0)),
                  pl.BlockSpec((1, tk, LANE), lambda i, j: (i, j, 0)),
                  pl.BlockSpec((1, sq, kv_lora), lambda i, j: (i, 0, 0)),
                  pl.BlockSpec((1, 1, sq, LANE), lambda i, j: (i, 0, 0, 1)),
                  _resident(wuk.shape), _resident(wuv.shape)],
        out_specs=pl.BlockSpec((1, sq, heads * LANE), lambda i, j: (i, 0, 0)),
        out_shape=jax.ShapeDtypeStruct((b, sq, heads * LANE), BF16),
        scratch_shapes=[pltpu.VMEM((heads * sq, kv_lora), BF16), pltpu.VMEM((heads * sq, LANE), BF16),
                        pltpu.VMEM((heads * sq, 1), F32), pltpu.VMEM((heads * sq, 1), F32),
                        pltpu.VMEM((heads * sq, kv_lora), F32)],
        compiler_params=_cp("parallel", "arbitrary"),
        name="mla_sample_attn",
    )(q, c_past, kr_past, c_new, k_new, wuk, wuv)


def _rwkv_pre_kernel(x_ref, gmix_ref, w_ref, sh0_ref, mu_ref, w0_ref, wb_ref, a0_ref, ab_ref, gb_ref,
                     kkp_ref, ka_ref, rk_ref, e_ref, et_ref,
                     r_ref, dec_ref, k_ref, v_ref, kk_ref, b_ref, g_ref, bonus_ref, sh_ref, last_ref, *, dim):
    si = pl.program_id(1)

    def put(ref, val):
        nb, ts, tiles, _ = ref.shape
        ref[...] = val.reshape(nb * ts, tiles, LANE).reshape(nb, ts, tiles, LANE)

    u = _rms(x_ref[0], gmix_ref[...]).astype(BF16)
    p = _dot(u, w_ref[...])
    tm = p.shape[0]

    @pl.when(si == 0)
    def _():
        last_ref[...] = sh0_ref[0]

    row = lax.broadcasted_iota(jnp.int32, p.shape, 0)
    prev = jnp.where(row == 0, last_ref[...], pltpu.roll(p, 1, axis=0))
    last_ref[...] = p[tm - 1:tm, :]
    sh_ref[0] = p[tm - 1:tm, :]
    xs = p + mu_ref[...] * (prev - p)
    r = xs[:, :dim]
    k = xs[:, dim:2 * dim]
    v = xs[:, 2 * dim:3 * dim]
    dw = xs[:, 3 * dim:3 * dim + LANE]
    da = xs[:, 3 * dim + LANE:3 * dim + 2 * LANE]
    dg = xs[:, 3 * dim + 2 * LANE:]
    z = -(w0_ref[...] + _dot(jnp.tanh(dw).astype(BF16), wb_ref[...]))
    softplus = jnp.maximum(z, 0.0) + jnp.log(1.0 + jnp.exp(-jnp.abs(z)))
    put(dec_ref, jnp.exp(-jnp.exp(-softplus - 0.5)))
    a = _sigmoid(a0_ref[...] + _dot(da.astype(BF16), ab_ref[...]))
    g_ref[0] = _dot(_sigmoid(dg).astype(BF16), gb_ref[...])
    kk = k * kkp_ref[...]
    nrm = jnp.sqrt(_seg_sum(kk * kk, e_ref[...], et_ref[...]))
    kk = kk / jnp.maximum(nrm, 1e-12)
    k2 = k * (1.0 + (a - 1.0) * ka_ref[...])
    put(r_ref, r)
    put(k_ref, k2)
    put(v_ref, v)
    put(kk_ref, kk)
    put(b_ref, kk * a)
    bonus_ref[0] = _seg_sum(r * k2 * rk_ref[...], e_ref[...], et_ref[...]) * v


def _rwkv_pre(x, gmix, w, sh0, mu, w0, wb, a0, ab, gb, kkp, ka, rk, e, et, tm=256):
    b, s, d = x.shape
    dim = w0.shape[1]
    width = w.shape[1]
    tm = _tile(s, tm)
    ts = min(SCAN_TS, s)
    assert tm % ts == 0 and dim % LANE == 0
    row = lambda i, j: (i, j, 0)
    act = pl.BlockSpec((1, tm, dim), row)
    act_shape = jax.ShapeDtypeStruct((b, s, dim), F32)
    tiles = dim // LANE
    assert tiles % 8 == 0
    seq = pl.BlockSpec((tm // ts, ts, tiles, LANE), lambda i, j: (j, 0, i, 0))
    seq_shape = jax.ShapeDtypeStruct((s // ts, ts, b * tiles, LANE), F32)
    consts = (gmix, w, mu, w0, wb, a0, ab, gb, kkp, ka, rk, e, et)
    return pl.pallas_call(
        functools.partial(_rwkv_pre_kernel, dim=dim),
        grid=(b, s // tm),
        in_specs=[pl.BlockSpec((1, tm, d), row), _resident(gmix.shape), _resident(w.shape),
                  pl.BlockSpec((1, 1, width), lambda i, j: (i, 0, 0))] + [_resident(c.shape) for c in consts[2:]],
        out_specs=[seq] * 6 + [act] * 2 + [pl.BlockSpec((1, 1, width), lambda i, j: (i, 0, 0))],
        out_shape=[seq_shape] * 6 + [act_shape] * 2 + [jax.ShapeDtypeStruct((b, 1, width), F32)],
        scratch_shapes=[pltpu.VMEM((1, width), F32)],
        compiler_params=_cp("parallel", "arbitrary"),
        name="rwkv_pre",
    )(x, gmix, w, sh0, *consts[2:])


def _rwkv_scan_kernel(r_ref, dec_ref, k_ref, v_ref, kk_ref, b_ref, s0_ref, o_ref, st_ref, xt_ref, ot_ref,
                      *, ts, n, unroll):
    tb = pl.program_id(1)
    hpl = LANE // n

    @pl.when(tb == 0)
    def _():
        st_ref[...] = s0_ref[...]

    srcs = (r_ref, dec_ref, k_ref, v_ref, kk_ref, b_ref)
    for i in range(ts):
        for a, ref in enumerate(srcs):
            xt_ref[a, i] = ref[i].T

    def step(i, c):
        for h2 in range(hpl):
            base = h2 * n
            vv = xt_ref[3, i, base:base + n, :]

            def p1(kx, sa):
                return sa + st_ref[h2, kx] * xt_ref[4, i, pl.ds(base + kx, 1), :]

            sa = lax.fori_loop(0, n, p1, jnp.zeros_like(vv), unroll=unroll)

            def p2(kx, o):
                s_new = (st_ref[h2, kx] * xt_ref[1, i, pl.ds(base + kx, 1), :]
                         - sa * xt_ref[5, i, pl.ds(base + kx, 1), :] + vv * xt_ref[2, i, pl.ds(base + kx, 1), :])
                st_ref[h2, kx] = s_new
                return o + s_new * xt_ref[0, i, pl.ds(base + kx, 1), :]

            ot_ref[i, base:base + n, :] = lax.fori_loop(0, n, p2, jnp.zeros_like(vv), unroll=unroll)
        return c

    lax.fori_loop(0, ts, step, 0)
    for i in range(ts):
        o_ref[i] = ot_ref[i].T


def _rwkv_scan(r, dec, k, v, kk, bb, s0):
    nt, ts, streams, _ = r.shape
    groups, hpl, n = s0.shape[:3]
    assert streams == groups * LANE
    seq = pl.BlockSpec((None, ts, LANE, LANE), lambda g, t: (t, 0, g, 0))
    state = pl.BlockSpec((None, hpl, n, n, LANE), lambda g, t: (g, 0, 0, 0, 0))
    return pl.pallas_call(
        functools.partial(_rwkv_scan_kernel, ts=ts, n=n, unroll=SCAN_UNROLL),
        grid=(groups, nt),
        in_specs=[seq] * 6 + [state],
        out_specs=[seq, state],
        out_shape=[jax.ShapeDtypeStruct(r.shape, F32), jax.ShapeDtypeStruct(s0.shape, F32)],
        scratch_shapes=[pltpu.VMEM((6, ts, LANE, LANE), F32), pltpu.VMEM((ts, LANE, LANE), F32)],
        compiler_params=_cp("parallel", "arbitrary"),
        name="rwkv_scan",
    )(r, dec, k, v, kk, bb, s0)


def _state_to_streams(wkv):
    b, h, n = wkv.shape[:3]
    hpl = LANE // n
    tiles = h // hpl
    bg = LANE // tiles
    g = b // bg
    t = wkv.reshape(g, bg, tiles, hpl, n, n)
    return t.transpose(0, 3, 5, 4, 1, 2).reshape(g, hpl, n, n, LANE)


def _state_from_streams(st, b, h):
    g, hpl, n = st.shape[:3]
    tiles = h // hpl
    bg = LANE // tiles
    t = st.reshape(g, hpl, n, n, bg, tiles)
    return t.transpose(0, 4, 5, 1, 3, 2).reshape(b, h, n, n)


def _mix_kernel(o_ref, bonus_ref, g_ref, ymla_ref, grw_ref, gmla_ref, lng_ref, lnb_ref, e_ref, et_ref,
                wr_ref, wm_ref, out_ref, *, inv_n):
    nb, ts, tiles, _ = o_ref.shape
    o = o_ref[...].reshape(nb * ts, tiles, LANE).reshape(nb * ts, tiles * LANE)
    e, et = e_ref[...], et_ref[...]
    d = o - _seg_sum(o, e, et) * inv_n
    var = _seg_sum(d * d, e, et) * inv_n
    on = d * lax.rsqrt(var + GN_EPS) * lng_ref[...] + lnb_ref[...]
    y = ((on + bonus_ref[0]) * g_ref[0]).astype(BF16)
    merged = grw_ref[0] * _dot(y, wr_ref[...]) + gmla_ref[0] * _dot(ymla_ref[0], wm_ref[...])
    out_ref[0] = merged.astype(out_ref.dtype)


def _mix(o, bonus, g, ymla, gates, lng, lnb, e, et, wr, wm, head_dim, tm=256):
    nt, ts = o.shape[:2]
    b, s, dim = bonus.shape
    tiles = dim // LANE
    d = wr.shape[1]
    md = ymla.shape[2]
    tm = _tile(s, tm)
    assert tm % ts == 0
    row = lambda i, j: (i, j, 0)
    return pl.pallas_call(
        functools.partial(_mix_kernel, inv_n=1.0 / head_dim),
        grid=(b, s // tm),
        in_specs=[pl.BlockSpec((tm // ts, ts, tiles, LANE), lambda i, j: (j, 0, i, 0)),
                  pl.BlockSpec((1, tm, dim), row), pl.BlockSpec((1, tm, dim), row), pl.BlockSpec((1, tm, md), row),
                  pl.BlockSpec((1, tm, d), lambda i, j: (i, j, 0)), pl.BlockSpec((1, tm, d), lambda i, j: (i, j, 1))]
                 + [_resident(c.shape) for c in (lng, lnb, e, et, wr, wm)],
        out_specs=pl.BlockSpec((1, tm, d), row),
        out_shape=jax.ShapeDtypeStruct((b, s, d), BF16),
        compiler_params=_cp("parallel", "parallel"),
        name="rwkv_post_mix",
    )(o, bonus, g, ymla, gates, gates, lng, lnb, e, et, wr, wm)


def _outproj_kernel(x_ref, m_ref, wo_ref, gc_ref, wq_ref, h_ref, q_ref):
    h = x_ref[...] + _dot(m_ref[...], wo_ref[...])
    h_ref[...] = h
    q_ref[...] = _dot(_rms(h, gc_ref[...]).astype(BF16), wq_ref[...]).astype(q_ref.dtype)


def _outproj(x, merged, wo, gc, wq, tm=256):
    t, d = x.shape
    tm = _tile(t, tm)
    row = lambda i: (i, 0)
    return pl.pallas_call(
        _outproj_kernel,
        grid=(t // tm,),
        in_specs=[pl.BlockSpec((tm, d), row), pl.BlockSpec((tm, d), row),
                  _resident(wo.shape), _resident(gc.shape), _resident(wq.shape)],
        out_specs=[pl.BlockSpec((tm, d), row), pl.BlockSpec((tm, d), row)],
        out_shape=[jax.ShapeDtypeStruct((t, d), F32), jax.ShapeDtypeStruct((t, d), BF16)],
        compiler_params=_cp("parallel"),
        name="outproj_crossq",
    )(x, merged, wo, gc, wq)


def _route_tile(lg, n_groups, per_group):
    n_exp = n_groups * per_group
    lane = lax.broadcasted_iota(jnp.int32, lg.shape, 1)
    first = lambda mask: jnp.min(jnp.where(mask, lane, 2 * LANE), axis=-1, keepdims=True)
    top = lambda mask: jnp.max(jnp.where(mask, lg, NEG), axis=-1, keepdims=True)
    is_g = lane < n_groups
    gmax = top(is_g)
    g_sel = first(is_g & (lg == gmax))
    p_grp = 1.0 / jnp.sum(jnp.where(is_g, jnp.exp(lg - gmax), 0.0), axis=-1, keepdims=True)
    eid = lane - n_groups
    in_grp = (eid >= g_sel * per_group) & (eid < (g_sel + 1) * per_group)
    v1 = top(in_grp)
    e1 = first(in_grp & (lg == v1))
    rest = in_grp & (lane != e1)
    v2 = top(rest)
    e2 = first(rest & (lg == v2))
    t = jnp.exp(v2 - v1)
    w1 = p_grp / (1.0 + t)
    w2 = w1 * t
    route = jnp.where(lane == 0, (e1 - n_groups).astype(F32),
                      jnp.where(lane == 1, (e2 - n_groups).astype(F32),
                                jnp.where(lane == 2, w1, jnp.where(lane == 3, w2, 0.0))))
    hits = jnp.where((lane == e1 - n_groups) | (lane == e2 - n_groups), 1.0, 0.0)
    assert n_exp <= LANE
    return route, jnp.sum(hits, axis=0, keepdims=True)


def _cross_kernel(q_ref, mk_ref, mv_ref, h_ref, wo_ref, gm_ref, wr_ref, br_ref, h2_ref, hn_ref, rt_ref, cnt_ref,
                  *, heads, scale, n_groups, per_group):
    q = q_ref[0]
    mk = mk_ref[0]
    mv = mv_ref[0]
    hd = q.shape[1] // heads
    outs = []
    for h in range(heads):
        sl = slice(h * hd, (h + 1) * hd)
        s = _dot_t(q[:, sl], mk[:, sl]) * scale
        p = jnp.exp(s - jnp.max(s, axis=-1, keepdims=True))
        p = p / jnp.sum(p, axis=-1, keepdims=True)
        outs.append(_dot(p.astype(BF16), mv[:, sl]).astype(BF16))
    o = jnp.concatenate(outs, axis=1)
    h2 = h_ref[0] + _dot(o, wo_ref[...])
    h2_ref[0] = h2
    hn = _rms(h2, gm_ref[...]).astype(BF16)
    hn_ref[0] = hn
    route, hist = _route_tile(_dot(hn, wr_ref[...]) + br_ref[...], n_groups, per_group)
    rt_ref[0] = route
    cnt_ref[0, 0] = jnp.broadcast_to(hist, cnt_ref.shape[2:])


def _cross(q, mk, mv, h, wo, gm, wr, br, heads, n_groups, per_group, tm=256):
    b, s, d = h.shape
    n_mem = mk.shape[1]
    tm = _tile(s, tm)
    row = lambda i, j: (i, j, 0)
    mem = pl.BlockSpec((1, n_mem, d), lambda i, j: (i, 0, 0))
    kern = functools.partial(_cross_kernel, heads=heads, scale=float(d // heads) ** -0.5,
                             n_groups=n_groups, per_group=per_group)
    return pl.pallas_call(
        kern,
        grid=(b, s // tm),
        in_specs=[pl.BlockSpec((1, tm, d), row), mem, mem, pl.BlockSpec((1, tm, d), row),
                  _resident(wo.shape), _resident(gm.shape), _resident(wr.shape), _resident(br.shape)],
        out_specs=[pl.BlockSpec((1, tm, d), row), pl.BlockSpec((1, tm, d), row), pl.BlockSpec((1, tm, LANE), row),
                   pl.BlockSpec((1, 1, 8, LANE), lambda i, j: (i, j, 0, 0))],
        out_shape=[jax.ShapeDtypeStruct((b, s, d), F32), jax.ShapeDtypeStruct((b, s, d), BF16),
                   jax.ShapeDtypeStruct((b, s, LANE), F32), jax.ShapeDtypeStruct((b, s // tm, 8, LANE), F32)],
        compiler_params=_cp("parallel", "parallel"),
        name="cross_attn_router",
    )(q, mk, mv, h, wo, gm, wr, br)


def _moe_kernel(be_ref, on_ref, x_ref, wg_ref, wu_ref, wd_ref, y_ref, wgb_ref, wub_ref, wdb_ref):
    i = pl.program_id(0)

    @pl.when((i == 0) | (be_ref[i] != be_ref[jnp.maximum(i - 1, 0)]))
    def _():
        wgb_ref[...] = wg_ref[0].astype(BF16)
        wub_ref[...] = wu_ref[0].astype(BF16)
        wdb_ref[...] = wd_ref[0].astype(BF16)

    @pl.when(on_ref[i] != 0)
    def _():
        xb = x_ref[...]
        g = _dot(xb, wgb_ref[...])
        u = _dot(xb, wub_ref[...])
        hb = (g * _sigmoid(g) * u).astype(BF16)
        y_ref[...] = _dot(hb, wdb_ref[...])

    @pl.when(on_ref[i] == 0)
    def _():
        y_ref[...] = jnp.zeros(y_ref.shape, y_ref.dtype)


def _moe_experts(blk_e, blk_on, x_slot, wg, wu, wd):
    ns, d = x_slot.shape
    f = wg.shape[2]
    n_blk = ns // MOE_ROWS
    wspec = lambda shp: pl.BlockSpec((1,) + shp, lambda i, be, on: (be[i], 0, 0))
    grid_spec = pltpu.PrefetchScalarGridSpec(
        num_scalar_prefetch=2,
        grid=(n_blk,),
        in_specs=[pl.BlockSpec((MOE_ROWS, d), lambda i, be, on: (i, 0)),
                  wspec((d, f)), wspec((d, f)), wspec((f, d))],
        out_specs=pl.BlockSpec((MOE_ROWS, d), lambda i, be, on: (i, 0)),
        scratch_shapes=[pltpu.VMEM((d, f), BF16), pltpu.VMEM((d, f), BF16), pltpu.VMEM((f, d), BF16)],
    )
    return pl.pallas_call(
        _moe_kernel,
        grid_spec=grid_spec,
        out_shape=jax.ShapeDtypeStruct((ns, d), F32),
        compiler_params=_cp("arbitrary"),
        name="moe_experts",
    )(blk_e, blk_on, x_slot, wg, wu, wd)


def _combine_kernel(idx_ref, h_ref, rt_ref, g_ref, y_hbm, o_ref, buf_ref, sem_ref, *, tm, norm):
    def issue(r, c):
        for ch in range(TOP_K):
            slot = idx_ref[0, TOP_K * r + ch]
            pltpu.make_async_copy(y_hbm.at[pl.ds(slot, 1)], buf_ref.at[ch, pl.ds(r, 1)], sem_ref.at[ch]).start()
        return c

    lax.fori_loop(0, tm, issue, 0, unroll=8)
    acc = h_ref[...]
    for ch in range(TOP_K):
        pltpu.make_async_copy(y_hbm.at[pl.ds(0, tm)], buf_ref.at[ch], sem_ref.at[ch]).wait()
        acc = acc + buf_ref[ch] * rt_ref[:, TOP_K + ch:TOP_K + ch + 1]
    o_ref[...] = _rms(acc, g_ref[...]) if norm else acc


def _combine(h, y_slot, slot_of, route, row_off, g, tm=512):
    t, d = h.shape
    tm = _tile(t, tm)
    assert row_off % tm == 0
    off = row_off // tm
    idx = slot_of[row_off * TOP_K:(row_off + t) * TOP_K].reshape(t // tm, 1, TOP_K * tm)
    norm = g is not None
    g = g if norm else jnp.ones((1, d), F32)
    return pl.pallas_call(
        functools.partial(_combine_kernel, tm=tm, norm=norm),
        grid=(t // tm,),
        in_specs=[pl.BlockSpec((None, 1, TOP_K * tm), lambda i: (i, 0, 0), memory_space=pltpu.SMEM),
                  pl.BlockSpec((tm, d), lambda i: (i, 0)),
                  pl.BlockSpec((tm, LANE), lambda i: (i + off, 0)),
                  pl.BlockSpec((1, d), lambda i: (0, 0)),
                  pl.BlockSpec(memory_space=pl.ANY)],
        out_specs=pl.BlockSpec((tm, d), lambda i: (i, 0)),
        out_shape=jax.ShapeDtypeStruct((t, d), F32),
        scratch_shapes=[pltpu.VMEM((TOP_K, tm, d), F32), pltpu.SemaphoreType.DMA((TOP_K,))],
        compiler_params=_cp("arbitrary"),
        name="moe_combine_norm",
    )(idx, h, route, g, y_slot)


def _pad_cols(w, n):
    return jnp.pad(w, ((0, 0),) * (w.ndim - 1) + ((0, n - w.shape[-1]),))


def _pad_rows(w, n):
    return jnp.pad(w, ((0, n - w.shape[0]), (0, 0)))


def _rot_half_cols(w):
    half = w.shape[-1] // 2
    return jnp.concatenate([-w[..., half:], w[..., :half]], axis=-1)


def _rope_table(pos):
    half = ROPE // 2
    inv = ROPE_THETA ** (-jnp.arange(half, dtype=F32) / half)
    ang = pos.astype(F32)[:, None] * inv[None, :]
    cos, sin = jnp.cos(ang), jnp.sin(ang)
    return jnp.concatenate([cos, cos, sin, sin], axis=1)


def _prep_layer(lp, dims):
    ql, kl, rope, dim, dl, il, gl, d = (dims[k] for k in ("q_lora", "kv_lora", "rope", "rw_dim", "decay_lora",
                                                          "iclr_lora", "gate_lora", "d_model"))
    w_in = lp["w_in"]
    i0, i1, i2 = ql, ql + kl, ql + kl + rope
    i3 = i2 + 3 * dim + dl + il + gl
    w_kr = w_in[:, i1:i2]
    out = {}
    out["w_mla"] = jnp.concatenate(
        [w_in[:, :i0], w_kr, _rot_half_cols(w_kr), w_in[:, i0:i1]], axis=1).astype(BF16)
    rw = w_in[:, i2:i3]
    c3 = 3 * dim
    glp = -(-gl // LANE) * LANE

    def regroup(t):
        return jnp.concatenate([t[..., :c3], _pad_cols(t[..., c3:c3 + dl], LANE),
                                _pad_cols(t[..., c3 + dl:c3 + dl + il], LANE),
                                _pad_cols(t[..., c3 + dl + il:], glp)], axis=-1)

    out["regroup"] = regroup
    out["w_rw"] = regroup(rw).astype(BF16)
    out["mu"] = regroup(lp["rw_mu"][None, :])
    out["w_gates"] = w_in[:, i3:].astype(BF16)
    heads = lp["w_uq"].shape[1]
    nope = lp["w_uq"].shape[2] - rope
    wq = lp["w_uq"]
    out["w_q"] = jnp.concatenate([wq[..., :nope], wq[..., nope:], _rot_half_cols(wq[..., nope:])],
                                 axis=-1).reshape(ql, heads * 256).astype(BF16)
    out["w_uk"] = lp["w_uk"].reshape(kl, -1).astype(BF16)
    out["w_uv"] = lp["w_uv"].reshape(kl, -1).astype(BF16)
    out["w_uv_t"] = out["w_uv"].T
    out["wb"] = _pad_rows(lp["rw_wb"], LANE).astype(BF16)
    out["ab"] = _pad_rows(lp["rw_ab"], LANE).astype(BF16)
    out["gb"] = _pad_rows(lp["rw_gb"], glp).astype(BF16)
    n_heads = dims["rw_heads"]
    hd = dim // n_heads
    e = (jnp.arange(dim)[:, None] // hd == jnp.arange(LANE)[None, :]).astype(BF16)
    out["e"], out["et"] = e, e.T
    out["w_br_rwkv"] = lp["w_br_rwkv"].astype(BF16)
    out["w_br_mla"] = lp["w_br_mla"].astype(BF16)
    out["w_out"] = lp["w_out"].astype(BF16)
    out["w_mq"] = lp["w_mq"].reshape(d, d).astype(BF16)
    out["w_mkv"] = jnp.concatenate([lp["w_mk"].reshape(d, d), lp["w_mv"].reshape(d, d)], axis=1).astype(BF16)
    out["w_mo"] = lp["w_mo"].reshape(d, d).astype(BF16)
    n_g, n_e = lp["w_rg"].shape[1], lp["w_re"].shape[1]
    out["w_r"] = _pad_cols(jnp.concatenate([lp["w_rg"], lp["w_re"]], axis=1), LANE).astype(BF16)
    out["b_r"] = _pad_cols(jnp.concatenate([lp["b_rg"], lp["b_re"]])[None, :], LANE)
    out["w_eg"], out["w_eu"], out["w_ed"] = lp["w_eg"], lp["w_eu"], lp["w_ed"]
    return out


def _row(v):
    return v.reshape(1, -1)


def _mixer(x, lp, wp, dims, pos, shift0, wkv0, attend):
    b, s, d = x.shape
    heads, rw_heads = dims["mla_heads"], dims["rw_heads"]
    gmix = _row(lp["g_mix"])
    cs = _rope_table(pos)
    scale = float(dims["nope"] + dims["rope"]) ** -0.5
    q, k, v, c, kr = _mla_proj(x, gmix, wp["w_mla"], _row(lp["g_q"]), wp["w_q"], _row(lp["g_kv"]),
                               wp["w_uk"], wp["w_uv_t"], cs, heads, scale)
    y_mla = attend(q, k, v, c)
    sh0 = wp["regroup"](shift0)
    r, dec, k2, vr, kk, bb, g, bonus, sh = _rwkv_pre(
        x, gmix, wp["w_rw"], sh0, wp["mu"], _row(lp["rw_w0"]), wp["wb"], _row(lp["rw_a0"]), wp["ab"], wp["gb"],
        _row(lp["rw_kk"]), _row(lp["rw_ka"]), _row(lp["rw_rk"]), wp["e"], wp["et"])
    n = dims["rw_dim"] // rw_heads
    o, s_t = _rwkv_scan(r, dec, k2, vr, kk, bb, _state_to_streams(wkv0))
    wkv = _state_from_streams(s_t, b, rw_heads)
    gates = _mm_norm(x.reshape(b * s, d), gmix, wp["w_gates"], out_dtype=BF16, gate=True).reshape(b, s, 2 * d)
    dim = dims["rw_dim"]
    merged = _mix(o, bonus, g, y_mla, gates, _row(lp["rw_ln_g"]), _row(lp["rw_ln_b"]), wp["e"], wp["et"],
                  wp["w_br_rwkv"], wp["w_br_mla"], n).reshape(b * s, d)
    h, qc = _outproj(x.reshape(b * s, d), merged, wp["w_out"], _row(lp["g_cross"]), wp["w_mq"])
    c3, dl, il = 3 * dim, dims["decay_lora"], dims["iclr_lora"]
    shift = jnp.concatenate([sh[..., :c3], sh[..., c3:c3 + dl], sh[..., c3 + LANE:c3 + LANE + il],
                             sh[..., c3 + 2 * LANE:c3 + 2 * LANE + dims["gate_lora"]]], axis=-1)
    return h.reshape(b, s, d), qc.reshape(b, s, d), c, kr, k, wkv, shift


def _moe(hn, route, counts, wp):
    t, d = hn.shape
    n_exp = counts.shape[0]
    e_idx = route[:, :TOP_K]
    m = t * TOP_K
    n_blk = -(-(m + n_exp * (MOE_ROWS - 1)) // MOE_ROWS)
    ns = n_blk * MOE_ROWS
    flat_e = e_idx.reshape(m).astype(jnp.int32)
    order = jnp.argsort(flat_e).astype(jnp.int32)
    padded = (counts + MOE_ROWS - 1) // MOE_ROWS * MOE_ROWS
    pad_end = jnp.cumsum(padded)
    pad_start = pad_end - padded
    start = jnp.cumsum(counts) - counts
    blk_start = jnp.arange(n_blk, dtype=jnp.int32) * MOE_ROWS
    blk_e = jnp.minimum(jnp.sum(pad_end[None, :] <= blk_start[:, None], axis=1), n_exp - 1).astype(jnp.int32)
    blk_on = (blk_start < pad_end[-1]).astype(jnp.int32)
    off = (blk_start - pad_start[blk_e])[:, None] + jnp.arange(MOE_ROWS, dtype=jnp.int32)[None, :]
    valid = ((off < counts[blk_e][:, None]) & (blk_on[:, None] != 0)).reshape(ns)
    asg = order[jnp.clip(start[blk_e][:, None] + off, 0, m - 1).reshape(ns)]
    slot_tok = jnp.where(valid, asg // TOP_K, jnp.arange(ns, dtype=jnp.int32) % t)
    y_slot = _moe_experts(blk_e, blk_on, hn[slot_tok], wp["w_eg"], wp["w_eu"], wp["w_ed"])
    key = jnp.where(valid, asg, m + jnp.arange(ns, dtype=jnp.int32))
    slot_of = jnp.argsort(key)[:m].astype(jnp.int32)
    return y_slot, slot_of


def kernel(x_prompt, x_sample, mem_prompt, cache_kv_latent, cache_k_rope, cache_mem_k, cache_mem_v, state_wkv, state_shift, g_mix, w_in, g_q, w_uq, g_kv, w_uk, w_uv, rw_mu, rw_w0, rw_wb, rw_a0, rw_ab, rw_gb, rw_kk, rw_ka, rw_rk, rw_ln_g, rw_ln_b, w_br_rwkv, w_br_mla, w_out, g_cross, g_mem, w_mq, w_mk, w_mv, w_mo, g_moe, w_rg, b_rg, w_re, b_re, w_eg, w_eu, w_ed, g_final):
    depth = g_mix.shape[0]
    b, s, d = x_prompt.shape
    bs, ss, _ = x_sample.shape
    past = cache_kv_latent.shape[2]
    n_mem = mem_prompt.shape[1]
    mem_heads = w_mq.shape[2]
    rw_heads, rw_hd = rw_rk.shape[1:]
    dims = dict(d_model=d, q_lora=g_q.shape[1], kv_lora=g_kv.shape[1], mla_heads=w_uq.shape[2],
                nope=w_uk.shape[3], rope=cache_k_rope.shape[3], rw_heads=rw_heads, rw_dim=rw_heads * rw_hd,
                decay_lora=rw_wb.shape[1], iclr_lora=rw_ab.shape[1], gate_lora=rw_gb.shape[1])
    assert dims["rope"] == ROPE and dims["nope"] == LANE and w_uv.shape[3] == LANE
    assert dims["decay_lora"] <= LANE and dims["iclr_lora"] <= LANE and rw_heads <= LANE
    n_groups, n_exp = w_rg.shape[2], w_re.shape[2]
    per_group = n_exp // n_groups
    stacked = dict(g_mix=g_mix, w_in=w_in, g_q=g_q, w_uq=w_uq, g_kv=g_kv, w_uk=w_uk, w_uv=w_uv, rw_mu=rw_mu,
                   rw_w0=rw_w0, rw_wb=rw_wb, rw_a0=rw_a0, rw_ab=rw_ab, rw_gb=rw_gb, rw_kk=rw_kk, rw_ka=rw_ka,
                   rw_rk=rw_rk, rw_ln_g=rw_ln_g, rw_ln_b=rw_ln_b, w_br_rwkv=w_br_rwkv, w_br_mla=w_br_mla,
                   w_out=w_out, g_cross=g_cross, g_mem=g_mem, w_mq=w_mq, w_mk=w_mk, w_mv=w_mv, w_mo=w_mo,
                   g_moe=g_moe, w_rg=w_rg, b_rg=b_rg, w_re=w_re, b_re=b_re, w_eg=w_eg, w_eu=w_eu, w_ed=w_ed)
    h_p, h_s = x_prompt, x_sample
    outs = [[] for _ in range(10)]
    for l in range(depth):
        lp = {name: val[l] for name, val in stacked.items()}
        wp = _prep_layer(lp, dims)

        h, qc, c, kr, _, wkv, shift = _mixer(
            h_p, lp, wp, dims, jnp.arange(s), jnp.zeros((b, 1, state_shift.shape[3]), F32),
            jnp.zeros((b, rw_heads, rw_hd, rw_hd), F32), lambda q, k, v, c_new: _mla_prompt_attn(q, k, v))
        mkv = _mm_norm(mem_prompt.reshape(b * n_mem, d), _row(lp["g_mem"]), wp["w_mkv"])
        mk, mv = mkv[:, :d].reshape(b, n_mem, d), mkv[:, d:].reshape(b, n_mem, d)
        gm, wr, br = _row(lp["g_moe"]), wp["w_r"], wp["b_r"]
        h2_p, hn_p, rt_p, cnt_p = _cross(qc, mk.astype(BF16), mv.astype(BF16), h, wp["w_mo"], gm, wr, br, mem_heads,
                                         n_groups, per_group)
        for lst, val in zip(outs[:6], (c, kr, mk.reshape(b, n_mem, mem_heads, -1),
                                       mv.reshape(b, n_mem, mem_heads, -1), wkv, shift)):
            lst.append(val)

        kr_past = jnp.pad(cache_k_rope[l], ((0, 0), (0, 0), (0, LANE - ROPE))).astype(BF16)

        def attend_sample(q, k, v, c_new, l=l, kr_past=kr_past, wp=wp):
            return _mla_sample_attn(q, cache_kv_latent[l], kr_past, c_new, k, wp["w_uk"], wp["w_uv"])

        h, qc, c, kr, _, wkv, shift = _mixer(h_s, lp, wp, dims, past + jnp.arange(ss), state_shift[l],
                                             state_wkv[l], attend_sample)
        h2_s, hn_s, rt_s, cnt_s = _cross(qc, cache_mem_k[l].reshape(bs, n_mem, d).astype(BF16),
                                         cache_mem_v[l].reshape(bs, n_mem, d).astype(BF16), h, wp["w_mo"], gm, wr, br,
                                         mem_heads, n_groups, per_group)
        for lst, val in zip(outs[6:], (c, kr, wkv, shift)):
            lst.append(val)

        tp, tsm = b * s, bs * ss
        hn = jnp.concatenate([hn_p.reshape(tp, d), hn_s.reshape(tsm, d)], axis=0)
        route = jnp.concatenate([rt_p.reshape(tp, LANE), rt_s.reshape(tsm, LANE)], axis=0)
        counts = (jnp.sum(cnt_p[:, :, 0, :n_exp], axis=(0, 1)) + jnp.sum(cnt_s[:, :, 0, :n_exp], axis=(0, 1)))
        y_slot, slot_of = _moe(hn, route, counts.astype(jnp.int32), wp)
        g_fin = _row(g_final) if l == depth - 1 else None
        h_p = _combine(h2_p.reshape(tp, d), y_slot, slot_of, route, 0, g_fin).reshape(b, s, d)
        h_s = _combine(h2_s.reshape(tsm, d), y_slot, slot_of, route, tp, g_fin).reshape(bs, ss, d)
    stacks = [jnp.stack(o) for o in outs]
    return (h_p, h_s, *stacks)
```

```python
import functools

import jax
import jax.numpy as jnp
from jax import lax
from jax.experimental import pallas as pl
from jax.experimental.pallas import tpu as pltpu

F32 = jnp.float32
BF16 = jnp.bfloat16

EPS = 1e-6
GN_EPS = 64e-5
ROPE_THETA = 10000.0
CHUNK = 64
TOP_K = 2
NEG = -1e30

LANE = 128
V7X_VMEM_LIMIT = 56 << 20
ROPE = 64
MOE_ROWS = 256
SCAN_TS = 16
SCAN_UNROLL = 16


def _cp(*sem):
    return pltpu.CompilerParams(dimension_semantics=sem, vmem_limit_bytes=V7X_VMEM_LIMIT)


def _tile(n, pref):
    if n <= pref:
        return n
    for t in range(pref, 7, -1):
        if n % t == 0 and t % 8 == 0:
            return t
    return n


def _resident(shape):
    nd = len(shape)
    return pl.BlockSpec(shape, lambda *_: (0,) * nd, pipeline_mode=pl.Buffered(1))


def _rms(x, g):
    return x * lax.rsqrt(jnp.mean(x * x, axis=-1, keepdims=True) + EPS) * g


def _sigmoid(x):
    return 1.0 / (1.0 + jnp.exp(-x))


def _dot(a, b):
    return jnp.dot(a, b, preferred_element_type=F32)


def _dot_t(a, b):
    return lax.dot_general(a, b, (((1,), (1,)), ((), ())), preferred_element_type=F32)


def _seg_sum(x, e, et):
    hi = x.astype(BF16)
    lo = (x - hi.astype(F32)).astype(BF16)
    s = _dot(hi, e) + _dot(lo, e)
    shi = s.astype(BF16)
    slo = (s - shi.astype(F32)).astype(BF16)
    return _dot(shi, et) + _dot(slo, et)


def _mm_norm_kernel(x_ref, g_ref, w_ref, o_ref, xn_ref, *, gate):
    @pl.when(pl.program_id(1) == 0)
    def _():
        xn_ref[...] = _rms(x_ref[...], g_ref[...]).astype(BF16)

    y = _dot(xn_ref[...], w_ref[...])
    o_ref[...] = (_sigmoid(y) if gate else y).astype(o_ref.dtype)


def _mm_norm(x, g, w, tm=1024, tn=512, out_dtype=F32, gate=False):
    m, k = x.shape
    n = w.shape[1]
    tm, tn = _tile(m, tm), _tile(n, tn)
    return pl.pallas_call(
        functools.partial(_mm_norm_kernel, gate=gate),
        grid=(m // tm, n // tn),
        in_specs=[pl.BlockSpec((tm, k), lambda i, j: (i, 0)),
                  pl.BlockSpec((1, k), lambda i, j: (0, 0)),
                  pl.BlockSpec((k, tn), lambda i, j: (0, j))],
        out_specs=pl.BlockSpec((tm, tn), lambda i, j: (i, j)),
        out_shape=jax.ShapeDtypeStruct((m, n), out_dtype),
        scratch_shapes=[pltpu.VMEM((tm, k), BF16)],
        compiler_params=_cp("parallel", "arbitrary"),
        name="mm_norm",
    )(x, g, w)


def _mla_proj_kernel(x_ref, gmix_ref, wm_ref, gq_ref, wq_ref, gkv_ref, wuk_ref, wuv_ref, cs_ref,
                     q_ref, k_ref, v_ref, c_ref, kr_ref, *, heads, q_lora, scale):
    u = _rms(x_ref[0], gmix_ref[...]).astype(BF16)
    p = _dot(u, wm_ref[...])
    cs = cs_ref[...]
    lane = lax.broadcasted_iota(jnp.int32, cs.shape, 1)

    def rope_tile(t2):
        t = t2 * cs
        return jnp.where(lane < ROPE, t + pltpu.roll(t, ROPE, axis=1), 0.0)

    krt = rope_tile(p[:, q_lora:q_lora + LANE])
    kr_ref[0] = krt[:, :ROPE]
    krb = krt.astype(BF16)
    c = _rms(p[:, q_lora + LANE:], gkv_ref[...])
    c_ref[0] = c
    cb = c.astype(BF16)
    kn = _dot(cb, wuk_ref[...])
    vt = _dot_t(wuv_ref[...], cb)
    qn = _rms(p[:, :q_lora], gq_ref[...]).astype(BF16)
    qf = _dot(qn, wq_ref[...])
    for h in range(heads):
        q_ref[0, h, :, :LANE] = (qf[:, h * 256:h * 256 + LANE] * scale).astype(BF16)
        q_ref[0, h, :, LANE:] = (rope_tile(qf[:, h * 256 + LANE:(h + 1) * 256]) * scale).astype(BF16)
        k_ref[0, h, :, :LANE] = kn[:, h * LANE:(h + 1) * LANE].astype(BF16)
        k_ref[0, h, :, LANE:] = krb
        v_ref[0, h] = vt[h * LANE:(h + 1) * LANE, :].astype(BF16)


def _mla_proj(x, gmix, wm, gq, wq, gkv, wuk, wuv, cs, heads, scale, tm=256):
    b, s, d = x.shape
    q_lora = gq.shape[1]
    kv_lora = gkv.shape[1]
    tm = _tile(s, tm)
    kern = functools.partial(_mla_proj_kernel, heads=heads, q_lora=q_lora, scale=scale)
    return pl.pallas_call(
        kern,
        grid=(b, s // tm),
        in_specs=[pl.BlockSpec((1, tm, d), lambda i, j: (i, j, 0)),
                  _resident(gmix.shape), _resident(wm.shape), _resident(gq.shape), _resident(wq.shape),
                  _resident(gkv.shape), _resident(wuk.shape), _resident(wuv.shape),
                  pl.BlockSpec((tm, LANE), lambda i, j: (j, 0))],
        out_specs=[pl.BlockSpec((1, heads, tm, 256), lambda i, j: (i, 0, j, 0)),
                   pl.BlockSpec((1, heads, tm, 256), lambda i, j: (i, 0, j, 0)),
                   pl.BlockSpec((1, heads, LANE, tm), lambda i, j: (i, 0, 0, j)),
                   pl.BlockSpec((1, tm, kv_lora), lambda i, j: (i, j, 0)),
                   pl.BlockSpec((1, tm, ROPE), lambda i, j: (i, j, 0))],
        out_shape=[jax.ShapeDtypeStruct((b, heads, s, 256), BF16),
                   jax.ShapeDtypeStruct((b, heads, s, 256), BF16),
                   jax.ShapeDtypeStruct((b, heads, LANE, s), BF16),
                   jax.ShapeDtypeStruct((b, s, kv_lora), F32),
                   jax.ShapeDtypeStruct((b, s, ROPE), F32)],
        compiler_params=_cp("parallel", "parallel"),
        name="mla_proj",
    )(x, gmix, wm, gq, wq, gkv, wuk, wuv, cs)


def _flash_kernel(q_ref, k_ref, vt_ref, o_ref, m_ref, l_ref, acc_ref, sc_ref, *, tq, hps):
    qi = pl.program_id(2)
    m_ref[...] = jnp.full(m_ref.shape, NEG, F32)
    l_ref[...] = jnp.zeros(l_ref.shape, F32)
    acc_ref[...] = jnp.zeros(acc_ref.shape, F32)

    def scores(j, slot):
        start = pl.multiple_of(j * tq, tq)
        for h in range(hps):
            sc_ref[slot, h] = _dot_t(k_ref[0, h, pl.ds(start, tq), :], q_ref[0, h])

    def consume(j, slot, diag):
        start = pl.multiple_of(j * tq, tq)
        for h in range(hps):
            s = sc_ref[slot, h]
            if diag:
                kc = lax.broadcasted_iota(jnp.int32, s.shape, 0) // CHUNK
                qc = lax.broadcasted_iota(jnp.int32, s.shape, 1) // CHUNK
                s = jnp.where(kc <= qc, s, NEG)
            m = m_ref[h]
            m_new = jnp.maximum(m, jnp.max(s, axis=0, keepdims=True))
            alpha = jnp.exp(m - m_new)
            p = jnp.exp(s - m_new)
            l_ref[h] = l_ref[h] * alpha + jnp.sum(p, axis=0, keepdims=True)
            acc_ref[h] = acc_ref[h] * alpha + _dot(vt_ref[0, h, :, pl.ds(start, tq)], p.astype(BF16))
            m_ref[h] = m_new

    scores(0, 0)

    def body(j, c):
        scores(2 * j + 1, 1)
        consume(2 * j, 0, False)
        scores(2 * j + 2, 0)
        consume(2 * j + 1, 1, False)
        return c

    lax.fori_loop(0, qi // 2, body, 0)

    @pl.when(qi % 2 == 1)
    def _():
        scores(qi, 1)
        consume(qi - 1, 0, False)
        consume(qi, 1, True)

    @pl.when(qi % 2 == 0)
    def _():
        consume(qi, 0, True)

    for h in range(hps):
        o_ref[0, :, h * LANE:(h + 1) * LANE] = (acc_ref[h] * (1.0 / l_ref[h])).T.astype(o_ref.dtype)


def _mla_prompt_attn(q, k, vt, tq=512, hps=8):
    b, heads, s, _ = q.shape
    tq = _tile(s, tq)
    hps = min(hps, heads)
    assert tq % CHUNK == 0 and heads % hps == 0
    return pl.pallas_call(
        functools.partial(_flash_kernel, tq=tq, hps=hps),
        grid=(b, heads // hps, s // tq),
        in_specs=[pl.BlockSpec((1, hps, tq, 256), lambda i, h, j: (i, h, j, 0)),
                  pl.BlockSpec((1, hps, s, 256), lambda i, h, j: (i, h, 0, 0)),
                  pl.BlockSpec((1, hps, LANE, s), lambda i, h, j: (i, h, 0, 0))],
        out_specs=pl.BlockSpec((1, tq, hps * LANE), lambda i, h, j: (i, j, h)),
        out_shape=jax.ShapeDtypeStruct((b, s, heads * LANE), BF16),
        scratch_shapes=[pltpu.VMEM((hps, 1, tq), F32), pltpu.VMEM((hps, 1, tq), F32),
                        pltpu.VMEM((hps, LANE, tq), F32), pltpu.VMEM((2, hps, tq, tq), F32)],
        compiler_params=_cp("parallel", "parallel", "arbitrary"),
        name="mla_prompt_attn",
    )(q, k, vt)


def _mla_sample_kernel(q_ref, cp_ref, krp_ref, cn_ref, krn_ref, wuk_ref, wuv_ref, o_ref,
                       qa_ref, qr_ref, m_ref, l_ref, acc_ref, *, heads, sq):
    kb = pl.program_id(1)

    @pl.when(kb == 0)
    def _():
        for h in range(heads):
            qh = q_ref[0, h]
            qa_ref[h * sq:(h + 1) * sq, :] = _dot_t(qh[:, :LANE], wuk_ref[:, h * LANE:(h + 1) * LANE]).astype(BF16)
            qr_ref[h * sq:(h + 1) * sq, :] = qh[:, LANE:]
        m_ref[...] = jnp.full(m_ref.shape, NEG, F32)
        l_ref[...] = jnp.zeros(l_ref.shape, F32)
        acc_ref[...] = jnp.zeros(acc_ref.shape, F32)

    def update(cb, krb):
        s = _dot_t(qa_ref[...], cb) + _dot_t(qr_ref[...], krb)
        m_new = jnp.maximum(m_ref[...], jnp.max(s, axis=-1, keepdims=True))
        alpha = jnp.exp(m_ref[...] - m_new)
        p = jnp.exp(s - m_new)
        l_ref[...] = l_ref[...] * alpha + jnp.sum(p, axis=-1, keepdims=True)
        acc_ref[...] = acc_ref[...] * alpha + _dot(p.astype(BF16), cb)
        m_ref[...] = m_new

    update(cp_ref[0].astype(BF16), krp_ref[0])

    @pl.when(kb == pl.num_programs(1) - 1)
    def _():
        update(cn_ref[0].astype(BF16), krn_ref[0, 0])
        o_lat = (acc_ref[...] / l_ref[...]).astype(BF16)
        for h in range(heads):
            o_ref[0, :, h * LANE:(h + 1) * LANE] = _dot(
                o_lat[h * sq:(h + 1) * sq], wuv_ref[:, h * LANE:(h + 1) * LANE]).astype(o_ref.dtype)


def _mla_sample_attn(q, c_past, kr_past, c_new, k_new, wuk, wuv, tk=1024):
    b, heads, sq, _ = q.shape
    past, kv_lora = c_past.shape[1:]
    tk = _tile(past, tk)
    kern = functools.partial(_mla_sample_kernel, heads=heads, sq=sq)
    return pl.pallas_call(
        kern,
        grid=(b, past // tk),
        in_specs=[pl.BlockSpec((1, heads, sq, 256), lambda i, j: (i, 0, 0, 0)),
                  pl.BlockSpec((1, tk, kv_lora), lambda i, j: (i, j, 0)),
                  pl.BlockSpec((1, tk, LANE), lambda i, j: (i, j, 0)),
                  pl.BlockSpec((1, sq, kv_lora), lambda i, j: (i, 0, 0)),
                  pl.BlockSpec((1, 1, sq, LANE), lambda i, j: (i, 0, 0, 1)),
                  _resident(wuk.shape), _resident(wuv.shape)],
        out_specs=pl.BlockSpec((1, sq, heads * LANE), lambda i, j: (i, 0, 0)),
        out_shape=jax.ShapeDtypeStruct((b, sq, heads * LANE), BF16),
        scratch_shapes=[pltpu.VMEM((heads * sq, kv_lora), BF16), pltpu.VMEM((heads * sq, LANE), BF16),
                        pltpu.VMEM((heads * sq, 1), F32), pltpu.VMEM((heads * sq, 1), F32),
                        pltpu.VMEM((heads * sq, kv_lora), F32)],
        compiler_params=_cp("parallel", "arbitrary"),
        name="mla_sample_attn",
    )(q, c_past, kr_past, c_new, k_new, wuk, wuv)


def _rwkv_pre_kernel(x_ref, gmix_ref, w_ref, sh0_ref, mu_ref, w0_ref, wb_ref, a0_ref, ab_ref, gb_ref,
                     kkp_ref, ka_ref, rk_ref, e_ref, et_ref,
                     r_ref, dec_ref, k_ref, v_ref, kk_ref, b_ref, g_ref, bonus_ref, sh_ref, last_ref, *, dim):
    si = pl.program_id(1)

    def put(ref, val):
        nb, ts, tiles, _ = ref.shape
        ref[...] = val.reshape(nb * ts, tiles, LANE).reshape(nb, ts, tiles, LANE)

    u = _rms(x_ref[0], gmix_ref[...]).astype(BF16)
    p = _dot(u, w_ref[...])
    tm = p.shape[0]

    @pl.when(si == 0)
    def _():
        last_ref[...] = sh0_ref[0]

    row = lax.broadcasted_iota(jnp.int32, p.shape, 0)
    prev = jnp.where(row == 0, last_ref[...], pltpu.roll(p, 1, axis=0))
    last_ref[...] = p[tm - 1:tm, :]
    sh_ref[0] = p[tm - 1:tm, :]
    xs = p + mu_ref[...] * (prev - p)
    r = xs[:, :dim]
    k = xs[:, dim:2 * dim]
    v = xs[:, 2 * dim:3 * dim]
    dw = xs[:, 3 * dim:3 * dim + LANE]
    da = xs[:, 3 * dim + LANE:3 * dim + 2 * LANE]
    dg = xs[:, 3 * dim + 2 * LANE:]
    z = -(w0_ref[...] + _dot(jnp.tanh(dw).astype(BF16), wb_ref[...]))
    softplus = jnp.maximum(z, 0.0) + jnp.log(1.0 + jnp.exp(-jnp.abs(z)))
    put(dec_ref, jnp.exp(-jnp.exp(-softplus - 0.5)))
    a = _sigmoid(a0_ref[...] + _dot(da.astype(BF16), ab_ref[...]))
    g_ref[0] = _dot(_sigmoid(dg).astype(BF16), gb_ref[...])
    kk = k * kkp_ref[...]
    nrm = jnp.sqrt(_seg_sum(kk * kk, e_ref[...], et_ref[...]))
    kk = kk / jnp.maximum(nrm, 1e-12)
    k2 = k * (1.0 + (a - 1.0) * ka_ref[...])
    put(r_ref, r)
    put(k_ref, k2)
    put(v_ref, v)
    put(kk_ref, kk)
    put(b_ref, kk * a)
    bonus_ref[0] = _seg_sum(r * k2 * rk_ref[...], e_ref[...], et_ref[...]) * v


def _rwkv_pre(x, gmix, w, sh0, mu, w0, wb, a0, ab, gb, kkp, ka, rk, e, et, tm=256):
    b, s, d = x.shape
    dim = w0.shape[1]
    width = w.shape[1]
    tm = _tile(s, tm)
    ts = min(SCAN_TS, s)
    assert tm % ts == 0 and dim % LANE == 0
    row = lambda i, j: (i, j, 0)
    act = pl.BlockSpec((1, tm, dim), row)
    act_shape = jax.ShapeDtypeStruct((b, s, dim), F32)
    tiles = dim // LANE
    assert tiles % 8 == 0
    seq = pl.BlockSpec((tm // ts, ts, tiles, LANE), lambda i, j: (j, 0, i, 0))
    seq_shape = jax.ShapeDtypeStruct((s // ts, ts, b * tiles, LANE), F32)
    consts = (gmix, w, mu, w0, wb, a0, ab, gb, kkp, ka, rk, e, et)
    return pl.pallas_call(
        functools.partial(_rwkv_pre_kernel, dim=dim),
        grid=(b, s // tm),
        in_specs=[pl.BlockSpec((1, tm, d), row), _resident(gmix.shape), _resident(w.shape),
                  pl.BlockSpec((1, 1, width), lambda i, j: (i, 0, 0))] + [_resident(c.shape) for c in consts[2:]],
        out_specs=[seq] * 6 + [act] * 2 + [pl.BlockSpec((1, 1, width), lambda i, j: (i, 0, 0))],
        out_shape=[seq_shape] * 6 + [act_shape] * 2 + [jax.ShapeDtypeStruct((b, 1, width), F32)],
        scratch_shapes=[pltpu.VMEM((1, width), F32)],
        compiler_params=_cp("parallel", "arbitrary"),
        name="rwkv_pre",
    )(x, gmix, w, sh0, *consts[2:])


def _rwkv_scan_kernel(r_ref, dec_ref, k_ref, v_ref, kk_ref, b_ref, s0_ref, o_ref, st_ref, xt_ref, ot_ref,
                      *, ts, n, unroll):
    tb = pl.program_id(1)
    hpl = LANE // n

    @pl.when(tb == 0)
    def _():
        st_ref[...] = s0_ref[...]

    srcs = (r_ref, dec_ref, k_ref, v_ref, kk_ref, b_ref)
    for i in range(ts):
        for a, ref in enumerate(srcs):
            xt_ref[a, i] = ref[i].T

    def step(i, c):
        for h2 in range(hpl):
            base = h2 * n
            vv = xt_ref[3, i, base:base + n, :]

            def p1(kx, sa):
                return sa + st_ref[h2, kx] * xt_ref[4, i, pl.ds(base + kx, 1), :]

            sa = lax.fori_loop(0, n, p1, jnp.zeros_like(vv), unroll=unroll)

            def p2(kx, o):
                s_new = (st_ref[h2, kx] * xt_ref[1, i, pl.ds(base + kx, 1), :]
                         - sa * xt_ref[5, i, pl.ds(base + kx, 1), :] + vv * xt_ref[2, i, pl.ds(base + kx, 1), :])
                st_ref[h2, kx] = s_new
                return o + s_new * xt_ref[0, i, pl.ds(base + kx, 1), :]

            ot_ref[i, base:base + n, :] = lax.fori_loop(0, n, p2, jnp.zeros_like(vv), unroll=unroll)
        return c

    lax.fori_loop(0, ts, step, 0)
    for i in range(ts):
        o_ref[i] = ot_ref[i].T


def _rwkv_scan(r, dec, k, v, kk, bb, s0):
    nt, ts, streams, _ = r.shape
    groups, hpl, n = s0.shape[:3]
    assert streams == groups * LANE
    seq = pl.BlockSpec((None, ts, LANE, LANE), lambda g, t: (t, 0, g, 0))
    state = pl.BlockSpec((None, hpl, n, n, LANE), lambda g, t: (g, 0, 0, 0, 0))
    return pl.pallas_call(
        functools.partial(_rwkv_scan_kernel, ts=ts, n=n, unroll=SCAN_UNROLL),
        grid=(groups, nt),
        in_specs=[seq] * 6 + [state],
        out_specs=[seq, state],
        out_shape=[jax.ShapeDtypeStruct(r.shape, F32), jax.ShapeDtypeStruct(s0.shape, F32)],
        scratch_shapes=[pltpu.VMEM((6, ts, LANE, LANE), F32), pltpu.VMEM((ts, LANE, LANE), F32)],
        compiler_params=_cp("parallel", "arbitrary"),
        name="rwkv_scan",
    )(r, dec, k, v, kk, bb, s0)


def _state_to_streams(wkv):
    b, h, n = wkv.shape[:3]
    hpl = LANE // n
    tiles = h // hpl
    bg = LANE // tiles
    g = b // bg
    t = wkv.reshape(g, bg, tiles, hpl, n, n)
    return t.transpose(0, 3, 5, 4, 1, 2).reshape(g, hpl, n, n, LANE)


def _state_from_streams(st, b, h):
    g, hpl, n = st.shape[:3]
    tiles = h // hpl
    bg = LANE // tiles
    t = st.reshape(g, hpl, n, n, bg, tiles)
    return t.transpose(0, 4, 5, 1, 3, 2).reshape(b, h, n, n)


def _mix_kernel(o_ref, bonus_ref, g_ref, ymla_ref, grw_ref, gmla_ref, lng_ref, lnb_ref, e_ref, et_ref,
                wr_ref, wm_ref, out_ref, *, inv_n):
    nb, ts, tiles, _ = o_ref.shape
    o = o_ref[...].reshape(nb * ts, tiles, LANE).reshape(nb * ts, tiles * LANE)
    e, et = e_ref[...], et_ref[...]
    d = o - _seg_sum(o, e, et) * inv_n
    var = _seg_sum(d * d, e, et) * inv_n
    on = d * lax.rsqrt(var + GN_EPS) * lng_ref[...] + lnb_ref[...]
    y = ((on + bonus_ref[0]) * g_ref[0]).astype(BF16)
    merged = grw_ref[0] * _dot(y, wr_ref[...]) + gmla_ref[0] * _dot(ymla_ref[0], wm_ref[...])
    out_ref[0] = merged.astype(out_ref.dtype)


def _mix(o, bonus, g, ymla, gates, lng, lnb, e, et, wr, wm, head_dim, tm=256):
    nt, ts = o.shape[:2]
    b, s, dim = bonus.shape
    tiles = dim // LANE
    d = wr.shape[1]
    md = ymla.shape[2]
    tm = _tile(s, tm)
    assert tm % ts == 0
    row = lambda i, j: (i, j, 0)
    return pl.pallas_call(
        functools.partial(_mix_kernel, inv_n=1.0 / head_dim),
        grid=(b, s // tm),
        in_specs=[pl.BlockSpec((tm // ts, ts, tiles, LANE), lambda i, j: (j, 0, i, 0)),
                  pl.BlockSpec((1, tm, dim), row), pl.BlockSpec((1, tm, dim), row), pl.BlockSpec((1, tm, md), row),
                  pl.BlockSpec((1, tm, d), lambda i, j: (i, j, 0)), pl.BlockSpec((1, tm, d), lambda i, j: (i, j, 1))]
                 + [_resident(c.shape) for c in (lng, lnb, e, et, wr, wm)],
        out_specs=pl.BlockSpec((1, tm, d), row),
        out_shape=jax.ShapeDtypeStruct((b, s, d), BF16),
        compiler_params=_cp("parallel", "parallel"),
        name="rwkv_post_mix",
    )(o, bonus, g, ymla, gates, gates, lng, lnb, e, et, wr, wm)


def _outproj_kernel(x_ref, m_ref, wo_ref, gc_ref, wq_ref, h_ref, q_ref):
    h = x_ref[...] + _dot(m_ref[...], wo_ref[...])
    h_ref[...] = h
    q_ref[...] = _dot(_rms(h, gc_ref[...]).astype(BF16), wq_ref[...]).astype(q_ref.dtype)


def _outproj(x, merged, wo, gc, wq, tm=256):
    t, d = x.shape
    tm = _tile(t, tm)
    row = lambda i: (i, 0)
    return pl.pallas_call(
        _outproj_kernel,
        grid=(t // tm,),
        in_specs=[pl.BlockSpec((tm, d), row), pl.BlockSpec((tm, d), row),
                  _resident(wo.shape), _resident(gc.shape), _resident(wq.shape)],
        out_specs=[pl.BlockSpec((tm, d), row), pl.BlockSpec((tm, d), row)],
        out_shape=[jax.ShapeDtypeStruct((t, d), F32), jax.ShapeDtypeStruct((t, d), BF16)],
        compiler_params=_cp("parallel"),
        name="outproj_crossq",
    )(x, merged, wo, gc, wq)


def _route_tile(lg, n_groups, per_group):
    n_exp = n_groups * per_group
    lane = lax.broadcasted_iota(jnp.int32, lg.shape, 1)
    first = lambda mask: jnp.min(jnp.where(mask, lane, 2 * LANE), axis=-1, keepdims=True)
    top = lambda mask: jnp.max(jnp.where(mask, lg, NEG), axis=-1, keepdims=True)
    is_g = lane < n_groups
    gmax = top(is_g)
    g_sel = first(is_g & (lg == gmax))
    p_grp = 1.0 / jnp.sum(jnp.where(is_g, jnp.exp(lg - gmax), 0.0), axis=-1, keepdims=True)
    eid = lane - n_groups
    in_grp = (eid >= g_sel * per_group) & (eid < (g_sel + 1) * per_group)
    v1 = top(in_grp)
    e1 = first(in_grp & (lg == v1))
    rest = in_grp & (lane != e1)
    v2 = top(rest)
    e2 = first(rest & (lg == v2))
    t = jnp.exp(v2 - v1)
    w1 = p_grp / (1.0 + t)
    w2 = w1 * t
    route = jnp.where(lane == 0, (e1 - n_groups).astype(F32),
                      jnp.where(lane == 1, (e2 - n_groups).astype(F32),
                                jnp.where(lane == 2, w1, jnp.where(lane == 3, w2, 0.0))))
    hits = jnp.where((lane == e1 - n_groups) | (lane == e2 - n_groups), 1.0, 0.0)
    assert n_exp <= LANE
    return route, jnp.sum(hits, axis=0, keepdims=True)


def _cross_kernel(q_ref, mk_ref, mv_ref, h_ref, wo_ref, gm_ref, wr_ref, br_ref, h2_ref, hn_ref, rt_ref, cnt_ref,
                  *, heads, scale, n_groups, per_group):
    q = q_ref[0]
    mk = mk_ref[0]
    mv = mv_ref[0]
    hd = q.shape[1] // heads
    outs = []
    for h in range(heads):
        sl = slice(h * hd, (h + 1) * hd)
        s = _dot_t(q[:, sl], mk[:, sl]) * scale
        p = jnp.exp(s - jnp.max(s, axis=-1, keepdims=True))
        p = p / jnp.sum(p, axis=-1, keepdims=True)
        outs.append(_dot(p.astype(BF16), mv[:, sl]).astype(BF16))
    o = jnp.concatenate(outs, axis=1)
    h2 = h_ref[0] + _dot(o, wo_ref[...])
    h2_ref[0] = h2
    hn = _rms(h2, gm_ref[...]).astype(BF16)
    hn_ref[0] = hn
    route, hist = _route_tile(_dot(hn, wr_ref[...]) + br_ref[...], n_groups, per_group)
    rt_ref[0] = route
    cnt_ref[0, 0] = jnp.broadcast_to(hist, cnt_ref.shape[2:])


def _cross(q, mk, mv, h, wo, gm, wr, br, heads, n_groups, per_group, tm=256):
    b, s, d = h.shape
    n_mem = mk.shape[1]
    tm = _tile(s, tm)
    row = lambda i, j: (i, j, 0)
    mem = pl.BlockSpec((1, n_mem, d), lambda i, j: (i, 0, 0))
    kern = functools.partial(_cross_kernel, heads=heads, scale=float(d // heads) ** -0.5,
                             n_groups=n_groups, per_group=per_group)
    return pl.pallas_call(
        kern,
        grid=(b, s // tm),
        in_specs=[pl.BlockSpec((1, tm, d), row), mem, mem, pl.BlockSpec((1, tm, d), row),
                  _resident(wo.shape), _resident(gm.shape), _resident(wr.shape), _resident(br.shape)],
        out_specs=[pl.BlockSpec((1, tm, d), row), pl.BlockSpec((1, tm, d), row), pl.BlockSpec((1, tm, LANE), row),
                   pl.BlockSpec((1, 1, 8, LANE), lambda i, j: (i, j, 0, 0))],
        out_shape=[jax.ShapeDtypeStruct((b, s, d), F32), jax.ShapeDtypeStruct((b, s, d), BF16),
                   jax.ShapeDtypeStruct((b, s, LANE), F32), jax.ShapeDtypeStruct((b, s // tm, 8, LANE), F32)],
        compiler_params=_cp("parallel", "parallel"),
        name="cross_attn_router",
    )(q, mk, mv, h, wo, gm, wr, br)


def _moe_kernel(be_ref, on_ref, x_ref, wg_ref, wu_ref, wd_ref, y_ref, wgb_ref, wub_ref, wdb_ref):
    i = pl.program_id(0)

    @pl.when((i == 0) | (be_ref[i] != be_ref[jnp.maximum(i - 1, 0)]))
    def _():
        wgb_ref[...] = wg_ref[0].astype(BF16)
        wub_ref[...] = wu_ref[0].astype(BF16)
        wdb_ref[...] = wd_ref[0].astype(BF16)

    @pl.when(on_ref[i] != 0)
    def _():
        xb = x_ref[...]
        g = _dot(xb, wgb_ref[...])
        u = _dot(xb, wub_ref[...])
        hb = (g * _sigmoid(g) * u).astype(BF16)
        y_ref[...] = _dot(hb, wdb_ref[...])

    @pl.when(on_ref[i] == 0)
    def _():
        y_ref[...] = jnp.zeros(y_ref.shape, y_ref.dtype)


def _moe_experts(blk_e, blk_on, x_slot, wg, wu, wd):
    ns, d = x_slot.shape
    f = wg.shape[2]
    n_blk = ns // MOE_ROWS
    wspec = lambda shp: pl.BlockSpec((1,) + shp, lambda i, be, on: (be[i], 0, 0))
    grid_spec = pltpu.PrefetchScalarGridSpec(
        num_scalar_prefetch=2,
        grid=(n_blk,),
        in_specs=[pl.BlockSpec((MOE_ROWS, d), lambda i, be, on: (i, 0)),
                  wspec((d, f)), wspec((d, f)), wspec((f, d))],
        out_specs=pl.BlockSpec((MOE_ROWS, d), lambda i, be, on: (i, 0)),
        scratch_shapes=[pltpu.VMEM((d, f), BF16), pltpu.VMEM((d, f), BF16), pltpu.VMEM((f, d), BF16)],
    )
    return pl.pallas_call(
        _moe_kernel,
        grid_spec=grid_spec,
        out_shape=jax.ShapeDtypeStruct((ns, d), F32),
        compiler_params=_cp("arbitrary"),
        name="moe_experts",
    )(blk_e, blk_on, x_slot, wg, wu, wd)


def _combine_kernel(idx_ref, h_ref, rt_ref, g_ref, y_hbm, o_ref, buf_ref, sem_ref, *, tm, norm):
    def issue(r, c):
        for ch in range(TOP_K):
            slot = idx_ref[0, TOP_K * r + ch]
            pltpu.make_async_copy(y_hbm.at[pl.ds(slot, 1)], buf_ref.at[ch, pl.ds(r, 1)], sem_ref.at[ch]).start()
        return c

    lax.fori_loop(0, tm, issue, 0, unroll=8)
    acc = h_ref[...]
    for ch in range(TOP_K):
        pltpu.make_async_copy(y_hbm.at[pl.ds(0, tm)], buf_ref.at[ch], sem_ref.at[ch]).wait()
        acc = acc + buf_ref[ch] * rt_ref[:, TOP_K + ch:TOP_K + ch + 1]
    o_ref[...] = _rms(acc, g_ref[...]) if norm else acc


def _combine(h, y_slot, slot_of, route, row_off, g, tm=512):
    t, d = h.shape
    tm = _tile(t, tm)
    assert row_off % tm == 0
    off = row_off // tm
    idx = slot_of[row_off * TOP_K:(row_off + t) * TOP_K].reshape(t // tm, 1, TOP_K * tm)
    norm = g is not None
    g = g if norm else jnp.ones((1, d), F32)
    return pl.pallas_call(
        functools.partial(_combine_kernel, tm=tm, norm=norm),
        grid=(t // tm,),
        in_specs=[pl.BlockSpec((None, 1, TOP_K * tm), lambda i: (i, 0, 0), memory_space=pltpu.SMEM),
                  pl.BlockSpec((tm, d), lambda i: (i, 0)),
                  pl.BlockSpec((tm, LANE), lambda i: (i + off, 0)),
                  pl.BlockSpec((1, d), lambda i: (0, 0)),
                  pl.BlockSpec(memory_space=pl.ANY)],
        out_specs=pl.BlockSpec((tm, d), lambda i: (i, 0)),
        out_shape=jax.ShapeDtypeStruct((t, d), F32),
        scratch_shapes=[pltpu.VMEM((TOP_K, tm, d), F32), pltpu.SemaphoreType.DMA((TOP_K,))],
        compiler_params=_cp("arbitrary"),
        name="moe_combine_norm",
    )(idx, h, route, g, y_slot)


def _pad_cols(w, n):
    return jnp.pad(w, ((0, 0),) * (w.ndim - 1) + ((0, n - w.shape[-1]),))


def _pad_rows(w, n):
    return jnp.pad(w, ((0, n - w.shape[0]), (0, 0)))


def _rot_half_cols(w):
    half = w.shape[-1] // 2
    return jnp.concatenate([-w[..., half:], w[..., :half]], axis=-1)


def _rope_table(pos):
    half = ROPE // 2
    inv = ROPE_THETA ** (-jnp.arange(half, dtype=F32) / half)
    ang = pos.astype(F32)[:, None] * inv[None, :]
    cos, sin = jnp.cos(ang), jnp.sin(ang)
    return jnp.concatenate([cos, cos, sin, sin], axis=1)


def _prep_layer(lp, dims):
    ql, kl, rope, dim, dl, il, gl, d = (dims[k] for k in ("q_lora", "kv_lora", "rope", "rw_dim", "decay_lora",
                                                          "iclr_lora", "gate_lora", "d_model"))
    w_in = lp["w_in"]
    i0, i1, i2 = ql, ql + kl, ql + kl + rope
    i3 = i2 + 3 * dim + dl + il + gl
    w_kr = w_in[:, i1:i2]
    out = {}
    out["w_mla"] = jnp.concatenate(
        [w_in[:, :i0], w_kr, _rot_half_cols(w_kr), w_in[:, i0:i1]], axis=1).astype(BF16)
    rw = w_in[:, i2:i3]
    c3 = 3 * dim
    glp = -(-gl // LANE) * LANE

    def regroup(t):
        return jnp.concatenate([t[..., :c3], _pad_cols(t[..., c3:c3 + dl], LANE),
                                _pad_cols(t[..., c3 + dl:c3 + dl + il], LANE),
                                _pad_cols(t[..., c3 + dl + il:], glp)], axis=-1)

    out["regroup"] = regroup
    out["w_rw"] = regroup(rw).astype(BF16)
    out["mu"] = regroup(lp["rw_mu"][None, :])
    out["w_gates"] = w_in[:, i3:].astype(BF16)
    heads = lp["w_uq"].shape[1]
    nope = lp["w_uq"].shape[2] - rope
    wq = lp["w_uq"]
    out["w_q"] = jnp.concatenate([wq[..., :nope], wq[..., nope:], _rot_half_cols(wq[..., nope:])],
                                 axis=-1).reshape(ql, heads * 256).astype(BF16)
    out["w_uk"] = lp["w_uk"].reshape(kl, -1).astype(BF16)
    out["w_uv"] = lp["w_uv"].reshape(kl, -1).astype(BF16)
    out["w_uv_t"] = out["w_uv"].T
    out["wb"] = _pad_rows(lp["rw_wb"], LANE).astype(BF16)
    out["ab"] = _pad_rows(lp["rw_ab"], LANE).astype(BF16)
    out["gb"] = _pad_rows(lp["rw_gb"], glp).astype(BF16)
    n_heads = dims["rw_heads"]
    hd = dim // n_heads
    e = (jnp.arange(dim)[:, None] // hd == jnp.arange(LANE)[None, :]).astype(BF16)
    out["e"], out["et"] = e, e.T
    out["w_br_rwkv"] = lp["w_br_rwkv"].astype(BF16)
    out["w_br_mla"] = lp["w_br_mla"].astype(BF16)
    out["w_out"] = lp["w_out"].astype(BF16)
    out["w_mq"] = lp["w_mq"].reshape(d, d).astype(BF16)
    out["w_mkv"] = jnp.concatenate([lp["w_mk"].reshape(d, d), lp["w_mv"].reshape(d, d)], axis=1).astype(BF16)
    out["w_mo"] = lp["w_mo"].reshape(d, d).astype(BF16)
    n_g, n_e = lp["w_rg"].shape[1], lp["w_re"].shape[1]
    out["w_r"] = _pad_cols(jnp.concatenate([lp["w_rg"], lp["w_re"]], axis=1), LANE).astype(BF16)
    out["b_r"] = _pad_cols(jnp.concatenate([lp["b_rg"], lp["b_re"]])[None, :], LANE)
    out["w_eg"], out["w_eu"], out["w_ed"] = lp["w_eg"], lp["w_eu"], lp["w_ed"]
    return out


def _row(v):
    return v.reshape(1, -1)


def _mixer(x, lp, wp, dims, pos, shift0, wkv0, attend):
    b, s, d = x.shape
    heads, rw_heads = dims["mla_heads"], dims["rw_heads"]
    gmix = _row(lp["g_mix"])
    cs = _rope_table(pos)
    scale = float(dims["nope"] + dims["rope"]) ** -0.5
    q, k, v, c, kr = _mla_proj(x, gmix, wp["w_mla"], _row(lp["g_q"]), wp["w_q"], _row(lp["g_kv"]),
                               wp["w_uk"], wp["w_uv_t"], cs, heads, scale)
    y_mla = attend(q, k, v, c)
    sh0 = wp["regroup"](shift0)
    r, dec, k2, vr, kk, bb, g, bonus, sh = _rwkv_pre(
        x, gmix, wp["w_rw"], sh0, wp["mu"], _row(lp["rw_w0"]), wp["wb"], _row(lp["rw_a0"]), wp["ab"], wp["gb"],
        _row(lp["rw_kk"]), _row(lp["rw_ka"]), _row(lp["rw_rk"]), wp["e"], wp["et"])
    n = dims["rw_dim"] // rw_heads
    o, s_t = _rwkv_scan(r, dec, k2, vr, kk, bb, _state_to_streams(wkv0))
    wkv = _state_from_streams(s_t, b, rw_heads)
    gates = _mm_norm(x.reshape(b * s, d), gmix, wp["w_gates"], out_dtype=BF16, gate=True).reshape(b, s, 2 * d)
    dim = dims["rw_dim"]
    merged = _mix(o, bonus, g, y_mla, gates, _row(lp["rw_ln_g"]), _row(lp["rw_ln_b"]), wp["e"], wp["et"],
                  wp["w_br_rwkv"], wp["w_br_mla"], n).reshape(b * s, d)
    h, qc = _outproj(x.reshape(b * s, d), merged, wp["w_out"], _row(lp["g_cross"]), wp["w_mq"])
    c3, dl, il = 3 * dim, dims["decay_lora"], dims["iclr_lora"]
    shift = jnp.concatenate([sh[..., :c3], sh[..., c3:c3 + dl], sh[..., c3 + LANE:c3 + LANE + il],
                             sh[..., c3 + 2 * LANE:c3 + 2 * LANE + dims["gate_lora"]]], axis=-1)
    return h.reshape(b, s, d), qc.reshape(b, s, d), c, kr, k, wkv, shift


def _moe(hn, route, counts, wp):
    t, d = hn.shape
    n_exp = counts.shape[0]
    e_idx = route[:, :TOP_K]
    m = t * TOP_K
    n_blk = -(-(m + n_exp * (MOE_ROWS - 1)) // MOE_ROWS)
    ns = n_blk * MOE_ROWS
    flat_e = e_idx.reshape(m).astype(jnp.int32)
    order = jnp.argsort(flat_e).astype(jnp.int32)
    padded = (counts + MOE_ROWS - 1) // MOE_ROWS * MOE_ROWS
    pad_end = jnp.cumsum(padded)
    pad_start = pad_end - padded
    start = jnp.cumsum(counts) - counts
    blk_start = jnp.arange(n_blk, dtype=jnp.int32) * MOE_ROWS
    blk_e = jnp.minimum(jnp.sum(pad_end[None, :] <= blk_start[:, None], axis=1), n_exp - 1).astype(jnp.int32)
    blk_on = (blk_start < pad_end[-1]).astype(jnp.int32)
    off = (blk_start - pad_start[blk_e])[:, None] + jnp.arange(MOE_ROWS, dtype=jnp.int32)[None, :]
    valid = ((off < counts[blk_e][:, None]) & (blk_on[:, None] != 0)).reshape(ns)
    asg = order[jnp.clip(start[blk_e][:, None] + off, 0, m - 1).reshape(ns)]
    slot_tok = jnp.where(valid, asg // TOP_K, jnp.arange(ns, dtype=jnp.int32) % t)
    y_slot = _moe_experts(blk_e, blk_on, hn[slot_tok], wp["w_eg"], wp["w_eu"], wp["w_ed"])
    key = jnp.where(valid, asg, m + jnp.arange(ns, dtype=jnp.int32))
    slot_of = jnp.argsort(key)[:m].astype(jnp.int32)
    return y_slot, slot_of


def kernel(x_prompt, x_sample, mem_prompt, cache_kv_latent, cache_k_rope, cache_mem_k, cache_mem_v, state_wkv, state_shift, g_mix, w_in, g_q, w_uq, g_kv, w_uk, w_uv, rw_mu, rw_w0, rw_wb, rw_a0, rw_ab, rw_gb, rw_kk, rw_ka, rw_rk, rw_ln_g, rw_ln_b, w_br_rwkv, w_br_mla, w_out, g_cross, g_mem, w_mq, w_mk, w_mv, w_mo, g_moe, w_rg, b_rg, w_re, b_re, w_eg, w_eu, w_ed, g_final):
    depth = g_mix.shape[0]
    b, s, d = x_prompt.shape
    bs, ss, _ = x_sample.shape
    past = cache_kv_latent.shape[2]
    n_mem = mem_prompt.shape[1]
    mem_heads = w_mq.shape[2]
    rw_heads, rw_hd = rw_rk.shape[1:]
    dims = dict(d_model=d, q_lora=g_q.shape[1], kv_lora=g_kv.shape[1], mla_heads=w_uq.shape[2],
                nope=w_uk.shape[3], rope=cache_k_rope.shape[3], rw_heads=rw_heads, rw_dim=rw_heads * rw_hd,
                decay_lora=rw_wb.shape[1], iclr_lora=rw_ab.shape[1], gate_lora=rw_gb.shape[1])
    assert dims["rope"] == ROPE and dims["nope"] == LANE and w_uv.shape[3] == LANE
    assert dims["decay_lora"] <= LANE and dims["iclr_lora"] <= LANE and rw_heads <= LANE
    n_groups, n_exp = w_rg.shape[2], w_re.shape[2]
    per_group = n_exp // n_groups
    stacked = dict(g_mix=g_mix, w_in=w_in, g_q=g_q, w_uq=w_uq, g_kv=g_kv, w_uk=w_uk, w_uv=w_uv, rw_mu=rw_mu,
                   rw_w0=rw_w0, rw_wb=rw_wb, rw_a0=rw_a0, rw_ab=rw_ab, rw_gb=rw_gb, rw_kk=rw_kk, rw_ka=rw_ka,
                   rw_rk=rw_rk, rw_ln_g=rw_ln_g, rw_ln_b=rw_ln_b, w_br_rwkv=w_br_rwkv, w_br_mla=w_br_mla,
                   w_out=w_out, g_cross=g_cross, g_mem=g_mem, w_mq=w_mq, w_mk=w_mk, w_mv=w_mv, w_mo=w_mo,
                   g_moe=g_moe, w_rg=w_rg, b_rg=b_rg, w_re=w_re, b_re=b_re, w_eg=w_eg, w_eu=w_eu, w_ed=w_ed)
    h_p, h_s = x_prompt, x_sample
    outs = [[] for _ in range(10)]
    for l in range(depth):
        lp = {name: val[l] for name, val in stacked.items()}
        wp = _prep_layer(lp, dims)

        h, qc, c, kr, _, wkv, shift = _mixer(
            h_p, lp, wp, dims, jnp.arange(s), jnp.zeros((b, 1, state_shift.shape[3]), F32),
            jnp.zeros((b, rw_heads, rw_hd, rw_hd), F32), lambda q, k, v, c_new: _mla_prompt_attn(q, k, v))
        mkv = _mm_norm(mem_prompt.reshape(b * n_mem, d), _row(lp["g_mem"]), wp["w_mkv"])
        mk, mv = mkv[:, :d].reshape(b, n_mem, d), mkv[:, d:].reshape(b, n_mem, d)
        gm, wr, br = _row(lp["g_moe"]), wp["w_r"], wp["b_r"]
        h2_p, hn_p, rt_p, cnt_p = _cross(qc, mk.astype(BF16), mv.astype(BF16), h, wp["w_mo"], gm, wr, br, mem_heads,
                                         n_groups, per_group)
        for lst, val in zip(outs[:6], (c, kr, mk.reshape(b, n_mem, mem_heads, -1),
                                       mv.reshape(b, n_mem, mem_heads, -1), wkv, shift)):
            lst.append(val)

        kr_past = jnp.pad(cache_k_rope[l], ((0, 0), (0, 0), (0, LANE - ROPE))).astype(BF16)

        def attend_sample(q, k, v, c_new, l=l, kr_past=kr_past, wp=wp):
            return _mla_sample_attn(q, cache_kv_latent[l], kr_past, c_new, k, wp["w_uk"], wp["w_uv"])

        h, qc, c, kr, _, wkv, shift = _mixer(h_s, lp, wp, dims, past + jnp.arange(ss), state_shift[l],
                                             state_wkv[l], attend_sample)
        h2_s, hn_s, rt_s, cnt_s = _cross(qc, cache_mem_k[l].reshape(bs, n_mem, d).astype(BF16),
                                         cache_mem_v[l].reshape(bs, n_mem, d).astype(BF16), h, wp["w_mo"], gm, wr, br,
                                         mem_heads, n_groups, per_group)
        for lst, val in zip(outs[6:], (c, kr, wkv, shift)):
            lst.append(val)

        tp, tsm = b * s, bs * ss
        hn = jnp.concatenate([hn_p.reshape(tp, d), hn_s.reshape(tsm, d)], axis=0)
        route = jnp.concatenate([rt_p.reshape(tp, LANE), rt_s.reshape(tsm, LANE)], axis=0)
        counts = (jnp.sum(cnt_p[:, :, 0, :n_exp], axis=(0, 1)) + jnp.sum(cnt_s[:, :, 0, :n_exp], axis=(0, 1)))
        y_slot, slot_of = _moe(hn, route, counts.astype(jnp.int32), wp)
        g_fin = _row(g_final) if l == depth - 1 else None
        h_p = _combine(h2_p.reshape(tp, d), y_slot, slot_of, route, 0, g_fin).reshape(b, s, d)
        h_s = _combine(h2_s.reshape(tsm, d), y_slot, slot_of, route, tp, g_fin).reshape(bs, ss, d)
    stacks = [jnp.stack(o) for o in outs]
    return (h_p, h_s, *stacks)
```

```python
import functools

import jax
import jax.numpy as jnp
from jax import lax
from jax.experimental import pallas as pl
from jax.experimental.pallas import tpu as pltpu

F32 = jnp.float32
BF16 = jnp.bfloat16

EPS = 1e-6
GN_EPS = 64e-5
ROPE_THETA = 10000.0
CHUNK = 64
TOP_K = 2
NEG = -1e30

LANE = 128
V7X_VMEM_LIMIT = 56 << 20
ROPE = 64
MOE_ROWS = 512
SCAN_TS = 16
SCAN_UNROLL = 16


def _cp(*sem):
    return pltpu.CompilerParams(dimension_semantics=sem, vmem_limit_bytes=V7X_VMEM_LIMIT)


def _tile(n, pref):
    if n <= pref:
        return n
    for t in range(pref, 7, -1):
        if n % t == 0 and t % 8 == 0:
            return t
    return n


def _resident(shape):
    nd = len(shape)
    return pl.BlockSpec(shape, lambda *_: (0,) * nd, pipeline_mode=pl.Buffered(1))


def _rms(x, g):
    return x * lax.rsqrt(jnp.mean(x * x, axis=-1, keepdims=True) + EPS) * g


def _sigmoid(x):
    return 1.0 / (1.0 + jnp.exp(-x))


def _dot(a, b):
    return jnp.dot(a, b, preferred_element_type=F32)


def _dot_t(a, b):
    return lax.dot_general(a, b, (((1,), (1,)), ((), ())), preferred_element_type=F32)


def _seg_sum(x, e, et):
    hi = x.astype(BF16)
    lo = (x - hi.astype(F32)).astype(BF16)
    s = _dot(hi, e) + _dot(lo, e)
    shi = s.astype(BF16)
    slo = (s - shi.astype(F32)).astype(BF16)
    return _dot(shi, et) + _dot(slo, et)


def _proj_kernel(x_ref, g_ref, *refs, n_w, gate, twin, cols):
    w_refs, o_refs = refs[:n_w], refs[n_w:]
    xn = _rms(x_ref[...], g_ref[...]).astype(BF16)
    per_w = 2 if twin else 1
    for i, w_ref in enumerate(w_refs):
        o_ref = o_refs[per_w * i]
        n = w_ref.shape[1]
        for c in range(0, n, cols):
            y = _dot(xn, w_ref[:, c:c + cols])
            if gate:
                y = _sigmoid(y)
            if len(o_ref.shape) == 3:
                hd = o_ref.shape[2]
                o_ref[:, c // hd:(c + cols) // hd, :] = y.reshape(y.shape[0], cols // hd, hd).astype(o_ref.dtype)
            else:
                o_ref[:, c:c + cols] = y.astype(o_ref.dtype)
            if twin:
                o_refs[per_w * i + 1][:, c:c + cols] = y.astype(BF16)


def _proj(x, g, ws, tm, out_dtype=F32, gate=False, heads=None, twin=False, cols=1024):
    m, k = x.shape
    tm = _tile(m, tm)
    out_specs, out_shape = [], []
    for w in ws:
        n = w.shape[1]
        if heads is None:
            out_specs.append(pl.BlockSpec((tm, n), lambda i: (i, 0)))
            out_shape.append(jax.ShapeDtypeStruct((m, n), out_dtype))
        else:
            out_specs.append(pl.BlockSpec((tm, heads, n // heads), lambda i: (i, 0, 0)))
            out_shape.append(jax.ShapeDtypeStruct((m, heads, n // heads), out_dtype))
        if twin:
            out_specs.append(pl.BlockSpec((tm, n), lambda i: (i, 0)))
            out_shape.append(jax.ShapeDtypeStruct((m, n), BF16))
    cols = min(cols, min(w.shape[1] for w in ws))
    return pl.pallas_call(
        functools.partial(_proj_kernel, n_w=len(ws), gate=gate, twin=twin, cols=cols),
        grid=(m // tm,),
        in_specs=[pl.BlockSpec((tm, k), lambda i: (i, 0)), _resident(g.shape)] + [_resident(w.shape) for w in ws],
        out_specs=out_specs,
        out_shape=out_shape,
        compiler_params=_cp("parallel"),
        name="norm_proj",
    )(x, g, *ws)


def _mla_proj_kernel(x_ref, gmix_ref, wm_ref, gq_ref, wq_ref, gkv_ref, wuk_ref, wuv_ref, cs_ref,
                     q_ref, k_ref, v_ref, c_ref, kr_ref, *, heads, q_lora, scale):
    u = _rms(x_ref[0], gmix_ref[...]).astype(BF16)
    p = _dot(u, wm_ref[...])
    cs = cs_ref[...]
    lane = lax.broadcasted_iota(jnp.int32, cs.shape, 1)

    def rope_tile(t2):
        t = t2 * cs
        return jnp.where(lane < ROPE, t + pltpu.roll(t, ROPE, axis=1), 0.0)

    krt = rope_tile(p[:, q_lora:q_lora + LANE])
    kr_ref[0] = krt[:, :ROPE]
    krb = krt.astype(BF16)
    c = _rms(p[:, q_lora + LANE:], gkv_ref[...])
    c_ref[0] = c
    cb = c.astype(BF16)
    kn = _dot(cb, wuk_ref[...])
    vt = _dot_t(wuv_ref[...], cb)
    qn = _rms(p[:, :q_lora], gq_ref[...]).astype(BF16)
    qf = _dot(qn, wq_ref[...])
    for h in range(heads):
        q_ref[0, h, :, :LANE] = (qf[:, h * 256:h * 256 + LANE] * scale).astype(BF16)
        q_ref[0, h, :, LANE:] = (rope_tile(qf[:, h * 256 + LANE:(h + 1) * 256]) * scale).astype(BF16)
        k_ref[0, h, :, :LANE] = kn[:, h * LANE:(h + 1) * LANE].astype(BF16)
        k_ref[0, h, :, LANE:] = krb
        v_ref[0, h] = vt[h * LANE:(h + 1) * LANE, :].astype(BF16)


def _mla_proj(x, gmix, wm, gq, wq, gkv, wuk, wuv, cs, heads, scale, tm=256):
    b, s, d = x.shape
    q_lora = gq.shape[1]
    kv_lora = gkv.shape[1]
    tm = _tile(s, tm)
    kern = functools.partial(_mla_proj_kernel, heads=heads, q_lora=q_lora, scale=scale)
    return pl.pallas_call(
        kern,
        grid=(b, s // tm),
        in_specs=[pl.BlockSpec((1, tm, d), lambda i, j: (i, j, 0)),
                  _resident(gmix.shape), _resident(wm.shape), _resident(gq.shape), _resident(wq.shape),
                  _resident(gkv.shape), _resident(wuk.shape), _resident(wuv.shape),
                  pl.BlockSpec((tm, LANE), lambda i, j: (j, 0))],
        out_specs=[pl.BlockSpec((1, heads, tm, 256), lambda i, j: (i, 0, j, 0)),
                   pl.BlockSpec((1, heads, tm, 256), lambda i, j: (i, 0, j, 0)),
                   pl.BlockSpec((1, heads, LANE, tm), lambda i, j: (i, 0, 0, j)),
                   pl.BlockSpec((1, tm, kv_lora), lambda i, j: (i, j, 0)),
                   pl.BlockSpec((1, tm, ROPE), lambda i, j: (i, j, 0))],
        out_shape=[jax.ShapeDtypeStruct((b, heads, s, 256), BF16),
                   jax.ShapeDtypeStruct((b, heads, s, 256), BF16),
                   jax.ShapeDtypeStruct((b, heads, LANE, s), BF16),
                   jax.ShapeDtypeStruct((b, s, kv_lora), F32),
                   jax.ShapeDtypeStruct((b, s, ROPE), F32)],
        compiler_params=_cp("parallel", "parallel"),
        name="mla_proj",
    )(x, gmix, wm, gq, wq, gkv, wuk, wuv, cs)


def _flash_kernel(q_ref, k_ref, vt_ref, o_ref, m_ref, l_ref, acc_ref, sc_ref, *, tq, hps):
    qi = pl.program_id(2)
    m_ref[...] = jnp.full(m_ref.shape, NEG, F32)
    l_ref[...] = jnp.zeros(l_ref.shape, F32)
    acc_ref[...] = jnp.zeros(acc_ref.shape, F32)

    def scores(j, slot):
        start = pl.multiple_of(j * tq, tq)
        for h in range(hps):
            sc_ref[slot, h] = _dot_t(k_ref[0, h, pl.ds(start, tq), :], q_ref[0, h])

    def consume(j, slot, diag):
        start = pl.multiple_of(j * tq, tq)
        for h in range(hps):
            s = sc_ref[slot, h]
            if diag:
                kc = lax.broadcasted_iota(jnp.int32, s.shape, 0) // CHUNK
                qc = lax.broadcasted_iota(jnp.int32, s.shape, 1) // CHUNK
                s = jnp.where(kc <= qc, s, NEG)
            m = m_ref[h]
            m_new = jnp.maximum(m, jnp.max(s, axis=0, keepdims=True))
            alpha = jnp.exp(m - m_new)
            p = jnp.exp(s - m_new)
            l_ref[h] = l_ref[h] * alpha + jnp.sum(p, axis=0, keepdims=True)
            acc_ref[h] = acc_ref[h] * alpha + _dot(vt_ref[0, h, :, pl.ds(start, tq)], p.astype(BF16))
            m_ref[h] = m_new

    scores(0, 0)

    def body(j, c):
        scores(2 * j + 1, 1)
        consume(2 * j, 0, False)
        scores(2 * j + 2, 0)
        consume(2 * j + 1, 1, False)
        return c

    lax.fori_loop(0, qi // 2, body, 0)

    @pl.when(qi % 2 == 1)
    def _():
        scores(qi, 1)
        consume(qi - 1, 0, False)
        consume(qi, 1, True)

    @pl.when(qi % 2 == 0)
    def _():
        consume(qi, 0, True)

    for h in range(hps):
        o_ref[0, :, h * LANE:(h + 1) * LANE] = (acc_ref[h] * (1.0 / l_ref[h])).T.astype(o_ref.dtype)


def _mla_prompt_attn(q, k, vt, tq=512, hps=8):
    b, heads, s, _ = q.shape
    tq = _tile(s, tq)
    hps = min(hps, heads)
    assert tq % CHUNK == 0 and heads % hps == 0
    return pl.pallas_call(
        functools.partial(_flash_kernel, tq=tq, hps=hps),
        grid=(b, heads // hps, s // tq),
        in_specs=[pl.BlockSpec((1, hps, tq, 256), lambda i, h, j: (i, h, j, 0)),
                  pl.BlockSpec((1, hps, s, 256), lambda i, h, j: (i, h, 0, 0)),
                  pl.BlockSpec((1, hps, LANE, s), lambda i, h, j: (i, h, 0, 0))],
        out_specs=pl.BlockSpec((1, tq, hps * LANE), lambda i, h, j: (i, j, h)),
        out_shape=jax.ShapeDtypeStruct((b, s, heads * LANE), BF16),
        scratch_shapes=[pltpu.VMEM((hps, 1, tq), F32), pltpu.VMEM((hps, 1, tq), F32),
                        pltpu.VMEM((hps, LANE, tq), F32), pltpu.VMEM((2, hps, tq, tq), F32)],
        compiler_params=_cp("parallel", "parallel", "arbitrary"),
        name="mla_prompt_attn",
    )(q, k, vt)


def _mla_sample_kernel(q_ref, cp_ref, krp_ref, cn_ref, krn_ref, wuk_ref, wuv_ref, o_ref,
                       qa_ref, qr_ref, m_ref, l_ref, acc_ref, *, heads, sq):
    kb = pl.program_id(1)

    @pl.when(kb == 0)
    def _():
        for h in range(heads):
            qh = q_ref[0, h]
            qa_ref[h * sq:(h + 1) * sq, :] = _dot_t(qh[:, :LANE], wuk_ref[:, h * LANE:(h + 1) * LANE]).astype(BF16)
            qr_ref[h * sq:(h + 1) * sq, :] = qh[:, LANE:]
        m_ref[...] = jnp.full(m_ref.shape, NEG, F32)
        l_ref[...] = jnp.zeros(l_ref.shape, F32)
        acc_ref[...] = jnp.zeros(acc_ref.shape, F32)

    def update(cb, krb):
        s = _dot_t(qa_ref[...], cb) + _dot_t(qr_ref[...], krb)
        m_new = jnp.maximum(m_ref[...], jnp.max(s, axis=-1, keepdims=True))
        alpha = jnp.exp(m_ref[...] - m_new)
        p = jnp.exp(s - m_new)
        l_ref[...] = l_ref[...] * alpha + jnp.sum(p, axis=-1, keepdims=True)
        acc_ref[...] = acc_ref[...] * alpha + _dot(p.astype(BF16), cb)
        m_ref[...] = m_new

    update(cp_ref[0].astype(BF16), krp_ref[0])

    @pl.when(kb == pl.num_programs(1) - 1)
    def _():
        update(cn_ref[0].astype(BF16), krn_ref[0, 0])
        o_lat = (acc_ref[...] / l_ref[...]).astype(BF16)
        for h in range(heads):
            o_ref[0, :, h * LANE:(h + 1) * LANE] = _dot(
                o_lat[h * sq:(h + 1) * sq], wuv_ref[:, h * LANE:(h + 1) * LANE]).astype(o_ref.dtype)


def _mla_sample_attn(q, c_past, kr_past, c_new, k_new, wuk, wuv, tk=1024):
    b, heads, sq, _ = q.shape
    past, kv_lora = c_past.shape[1:]
    tk = _tile(past, tk)
    kern = functools.partial(_mla_sample_kernel, heads=heads, sq=sq)
    return pl.pallas_call(
        kern,
        grid=(b, past // tk),
        in_specs=[pl.BlockSpec((1, heads, sq, 256), lambda i, j: (i, 0, 0, 0)),
                  pl.BlockSpec((1, tk, kv_lora), lambda i, j: (i, j, 0)),
                  pl.BlockSpec((1, tk, LANE), lambda i, j: (i, j, 0)),
                  pl.BlockSpec((1, sq, kv_lora), lambda i, j: (i, 0, 0)),
                  pl.BlockSpec((1, 1, sq, LANE), lambda i, j: (i, 0, 0, 1)),
                  _resident(wuk.shape), _resident(wuv.shape)],
        out_specs=pl.BlockSpec((1, sq, heads * LANE), lambda i, j: (i, 0, 0)),
        out_shape=jax.ShapeDtypeStruct((b, sq, heads * LANE), BF16),
        scratch_shapes=[pltpu.VMEM((heads * sq, kv_lora), BF16), pltpu.VMEM((heads * sq, LANE), BF16),
                        pltpu.VMEM((heads * sq, 1), F32), pltpu.VMEM((heads * sq, 1), F32),
                        pltpu.VMEM((heads * sq, kv_lora), F32)],
        compiler_params=_cp("parallel", "arbitrary"),
        name="mla_sample_attn",
    )(q, c_past, kr_past, c_new, k_new, wuk, wuv)


def _rwkv_pre_kernel(x_ref, gmix_ref, w_ref, sh0_ref, mu_ref, w0_ref, wb_ref, a0_ref, ab_ref, gb_ref,
                     kkp_ref, ka_ref, rk_ref, e_ref, et_ref,
                     r_ref, dec_ref, k_ref, v_ref, kk_ref, b_ref, g_ref, bonus_ref, sh_ref, last_ref, *, dim):
    si = pl.program_id(1)

    def put(ref, val):
        nb, ts, tiles, _ = ref.shape
        ref[...] = val.reshape(nb * ts, tiles, LANE).reshape(nb, ts, tiles, LANE)

    u = _rms(x_ref[0], gmix_ref[...]).astype(BF16)
    p = _dot(u, w_ref[...])
    tm = p.shape[0]

    @pl.when(si == 0)
    def _():
        last_ref[...] = sh0_ref[0]

    row = lax.broadcasted_iota(jnp.int32, p.shape, 0)
    prev = jnp.where(row == 0, last_ref[...], pltpu.roll(p, 1, axis=0))
    last_ref[...] = p[tm - 1:tm, :]
    sh_ref[0] = p[tm - 1:tm, :]
    xs = p + mu_ref[...] * (prev - p)
    r = xs[:, :dim]
    k = xs[:, dim:2 * dim]
    v = xs[:, 2 * dim:3 * dim]
    dw = xs[:, 3 * dim:3 * dim + LANE]
    da = xs[:, 3 * dim + LANE:3 * dim + 2 * LANE]
    dg = xs[:, 3 * dim + 2 * LANE:]
    z = -(w0_ref[...] + _dot(jnp.tanh(dw).astype(BF16), wb_ref[...]))
    softplus = jnp.maximum(z, 0.0) + jnp.log(1.0 + jnp.exp(-jnp.abs(z)))
    put(dec_ref, jnp.exp(-jnp.exp(-softplus - 0.5)))
    a = _sigmoid(a0_ref[...] + _dot(da.astype(BF16), ab_ref[...]))
    g_ref[0] = _dot(_sigmoid(dg).astype(BF16), gb_ref[...])
    kk = k * kkp_ref[...]
    nrm = jnp.sqrt(_seg_sum(kk * kk, e_ref[...], et_ref[...]))
    kk = kk / jnp.maximum(nrm, 1e-12)
    k2 = k * (1.0 + (a - 1.0) * ka_ref[...])
    put(r_ref, r)
    put(k_ref, k2)
    put(v_ref, v)
    put(kk_ref, kk)
    put(b_ref, kk * a)
    bonus_ref[0] = _seg_sum(r * k2 * rk_ref[...], e_ref[...], et_ref[...]) * v


def _rwkv_pre(x, gmix, w, sh0, mu, w0, wb, a0, ab, gb, kkp, ka, rk, e, et, tm=256):
    b, s, d = x.shape
    dim = w0.shape[1]
    width = w.shape[1]
    tm = _tile(s, tm)
    ts = min(SCAN_TS, s)
    assert tm % ts == 0 and dim % LANE == 0
    row = lambda i, j: (i, j, 0)
    act = pl.BlockSpec((1, tm, dim), row)
    act_shape = jax.ShapeDtypeStruct((b, s, dim), F32)
    tiles = dim // LANE
    assert tiles % 8 == 0
    seq = pl.BlockSpec((tm // ts, ts, tiles, LANE), lambda i, j: (j, 0, i, 0))
    seq_shape = jax.ShapeDtypeStruct((s // ts, ts, b * tiles, LANE), F32)
    consts = (gmix, w, mu, w0, wb, a0, ab, gb, kkp, ka, rk, e, et)
    return pl.pallas_call(
        functools.partial(_rwkv_pre_kernel, dim=dim),
        grid=(b, s // tm),
        in_specs=[pl.BlockSpec((1, tm, d), row), _resident(gmix.shape), _resident(w.shape),
                  pl.BlockSpec((1, 1, width), lambda i, j: (i, 0, 0))] + [_resident(c.shape) for c in consts[2:]],
        out_specs=[seq] * 6 + [act] * 2 + [pl.BlockSpec((1, 1, width), lambda i, j: (i, 0, 0))],
        out_shape=[seq_shape] * 6 + [act_shape] * 2 + [jax.ShapeDtypeStruct((b, 1, width), F32)],
        scratch_shapes=[pltpu.VMEM((1, width), F32)],
        compiler_params=_cp("parallel", "arbitrary"),
        name="rwkv_pre",
    )(x, gmix, w, sh0, *consts[2:])


def _rwkv_scan_kernel(r_ref, dec_ref, k_ref, v_ref, kk_ref, b_ref, s0_ref, o_ref, st_ref, xt_ref, ot_ref,
                      *, ts, n, unroll):
    tb = pl.program_id(1)
    hpl = LANE // n

    @pl.when(tb == 0)
    def _():
        st_ref[...] = s0_ref[...]

    srcs = (r_ref, dec_ref, k_ref, v_ref, kk_ref, b_ref)
    for i in range(ts):
        for a, ref in enumerate(srcs):
            xt_ref[a, i] = ref[i].T

    def step(i, c):
        for h2 in range(hpl):
            base = h2 * n
            vv = xt_ref[3, i, base:base + n, :]

            def p1(kx, sa):
                return sa + st_ref[h2, kx] * xt_ref[4, i, pl.ds(base + kx, 1), :]

            sa = lax.fori_loop(0, n, p1, jnp.zeros_like(vv), unroll=unroll)

            def p2(kx, o):
                s_new = (st_ref[h2, kx] * xt_ref[1, i, pl.ds(base + kx, 1), :]
                         - sa * xt_ref[5, i, pl.ds(base + kx, 1), :] + vv * xt_ref[2, i, pl.ds(base + kx, 1), :])
                st_ref[h2, kx] = s_new
                return o + s_new * xt_ref[0, i, pl.ds(base + kx, 1), :]

            ot_ref[i, base:base + n, :] = lax.fori_loop(0, n, p2, jnp.zeros_like(vv), unroll=unroll)
        return c

    lax.fori_loop(0, ts, step, 0)
    for i in range(ts):
        o_ref[i] = ot_ref[i].T


def _rwkv_scan(r, dec, k, v, kk, bb, s0):
    nt, ts, streams, _ = r.shape
    groups, hpl, n = s0.shape[:3]
    assert streams == groups * LANE
    seq = pl.BlockSpec((None, ts, LANE, LANE), lambda g, t: (t, 0, g, 0))
    state = pl.BlockSpec((None, hpl, n, n, LANE), lambda g, t: (g, 0, 0, 0, 0))
    return pl.pallas_call(
        functools.partial(_rwkv_scan_kernel, ts=ts, n=n, unroll=SCAN_UNROLL),
        grid=(groups, nt),
        in_specs=[seq] * 6 + [state],
        out_specs=[seq, state],
        out_shape=[jax.ShapeDtypeStruct(r.shape, F32), jax.ShapeDtypeStruct(s0.shape, F32)],
        scratch_shapes=[pltpu.VMEM((6, ts, LANE, LANE), F32), pltpu.VMEM((ts, LANE, LANE), F32)],
        compiler_params=_cp("parallel", "arbitrary"),
        name="rwkv_scan",
    )(r, dec, k, v, kk, bb, s0)


def _state_to_streams(wkv):
    b, h, n = wkv.shape[:3]
    hpl = LANE // n
    tiles = h // hpl
    bg = LANE // tiles
    g = b // bg
    t = wkv.reshape(g, bg, tiles, hpl, n, n)
    return t.transpose(0, 3, 5, 4, 1, 2).reshape(g, hpl, n, n, LANE)


def _state_from_streams(st, b, h):
    g, hpl, n = st.shape[:3]
    tiles = h // hpl
    bg = LANE // tiles
    t = st.reshape(g, hpl, n, n, bg, tiles)
    return t.transpose(0, 4, 5, 1, 3, 2).reshape(b, h, n, n)


def _mix_kernel(o_ref, bonus_ref, g_ref, ymla_ref, grw_ref, gmla_ref, lng_ref, lnb_ref, e_ref, et_ref,
                wr_ref, wm_ref, out_ref, *, inv_n):
    nb, ts, tiles, _ = o_ref.shape
    o = o_ref[...].reshape(nb * ts, tiles, LANE).reshape(nb * ts, tiles * LANE)
    e, et = e_ref[...], et_ref[...]
    d = o - _seg_sum(o, e, et) * inv_n
    var = _seg_sum(d * d, e, et) * inv_n
    on = d * lax.rsqrt(var + GN_EPS) * lng_ref[...] + lnb_ref[...]
    y = ((on + bonus_ref[0]) * g_ref[0]).astype(BF16)
    merged = grw_ref[0] * _dot(y, wr_ref[...]) + gmla_ref[0] * _dot(ymla_ref[0], wm_ref[...])
    out_ref[0] = merged.astype(out_ref.dtype)


def _mix(o, bonus, g, ymla, gates, lng, lnb, e, et, wr, wm, head_dim, tm=256):
    nt, ts = o.shape[:2]
    b, s, dim = bonus.shape
    tiles = dim // LANE
    d = wr.shape[1]
    md = ymla.shape[2]
    tm = _tile(s, tm)
    assert tm % ts == 0
    row = lambda i, j: (i, j, 0)
    return pl.pallas_call(
        functools.partial(_mix_kernel, inv_n=1.0 / head_dim),
        grid=(b, s // tm),
        in_specs=[pl.BlockSpec((tm // ts, ts, tiles, LANE), lambda i, j: (j, 0, i, 0)),
                  pl.BlockSpec((1, tm, dim), row), pl.BlockSpec((1, tm, dim), row), pl.BlockSpec((1, tm, md), row),
                  pl.BlockSpec((1, tm, d), lambda i, j: (i, j, 0)), pl.BlockSpec((1, tm, d), lambda i, j: (i, j, 1))]
                 + [_resident(c.shape) for c in (lng, lnb, e, et, wr, wm)],
        out_specs=pl.BlockSpec((1, tm, d), row),
        out_shape=jax.ShapeDtypeStruct((b, s, d), BF16),
        compiler_params=_cp("parallel", "parallel"),
        name="rwkv_post_mix",
    )(o, bonus, g, ymla, gates, gates, lng, lnb, e, et, wr, wm)


def _outproj_kernel(x_ref, m_ref, wo_ref, gc_ref, wq_ref, h_ref, q_ref):
    h = x_ref[...] + _dot(m_ref[...], wo_ref[...])
    h_ref[...] = h
    q_ref[...] = _dot(_rms(h, gc_ref[...]).astype(BF16), wq_ref[...]).astype(q_ref.dtype)


def _outproj(x, merged, wo, gc, wq, tm=256):
    t, d = x.shape
    tm = _tile(t, tm)
    row = lambda i: (i, 0)
    return pl.pallas_call(
        _outproj_kernel,
        grid=(t // tm,),
        in_specs=[pl.BlockSpec((tm, d), row), pl.BlockSpec((tm, d), row),
                  _resident(wo.shape), _resident(gc.shape), _resident(wq.shape)],
        out_specs=[pl.BlockSpec((tm, d), row), pl.BlockSpec((tm, d), row)],
        out_shape=[jax.ShapeDtypeStruct((t, d), F32), jax.ShapeDtypeStruct((t, d), BF16)],
        compiler_params=_cp("parallel"),
        name="outproj_crossq",
    )(x, merged, wo, gc, wq)


def _route_tile(lg, n_groups, per_group):
    n_exp = n_groups * per_group
    lane = lax.broadcasted_iota(jnp.int32, lg.shape, 1)
    first = lambda mask: jnp.min(jnp.where(mask, lane, 2 * LANE), axis=-1, keepdims=True)
    top = lambda mask: jnp.max(jnp.where(mask, lg, NEG), axis=-1, keepdims=True)
    is_g = lane < n_groups
    gmax = top(is_g)
    g_sel = first(is_g & (lg == gmax))
    p_grp = 1.0 / jnp.sum(jnp.where(is_g, jnp.exp(lg - gmax), 0.0), axis=-1, keepdims=True)
    eid = lane - n_groups
    in_grp = (eid >= g_sel * per_group) & (eid < (g_sel + 1) * per_group)
    v1 = top(in_grp)
    e1 = first(in_grp & (lg == v1))
    rest = in_grp & (lane != e1)
    v2 = top(rest)
    e2 = first(rest & (lg == v2))
    t = jnp.exp(v2 - v1)
    w1 = p_grp / (1.0 + t)
    w2 = w1 * t
    route = jnp.where(lane == 0, (e1 - n_groups).astype(F32),
                      jnp.where(lane == 1, (e2 - n_groups).astype(F32),
                                jnp.where(lane == 2, w1, jnp.where(lane == 3, w2, 0.0))))
    hits = jnp.where((lane == e1 - n_groups) | (lane == e2 - n_groups), 1.0, 0.0)
    assert n_exp <= LANE
    return route, jnp.sum(hits, axis=0, keepdims=True)


def _cross_kernel(q_ref, mk_ref, mv_ref, h_ref, wo_ref, gm_ref, wr_ref, br_ref, h2_ref, hn_ref, rt_ref, cnt_ref,
                  *, heads, scale, n_groups, per_group):
    q = q_ref[0]
    mk = mk_ref[0]
    mv = mv_ref[0]
    hd = q.shape[1] // heads
    outs = []
    for h in range(heads):
        sl = slice(h * hd, (h + 1) * hd)
        s = _dot_t(q[:, sl], mk[:, sl]) * scale
        p = jnp.exp(s - jnp.max(s, axis=-1, keepdims=True))
        p = p / jnp.sum(p, axis=-1, keepdims=True)
        outs.append(_dot(p.astype(BF16), mv[:, sl]).astype(BF16))
    o = jnp.concatenate(outs, axis=1)
    h2 = h_ref[0] + _dot(o, wo_ref[...])
    h2_ref[0] = h2
    hn = _rms(h2, gm_ref[...]).astype(BF16)
    hn_ref[0] = hn
    route, hist = _route_tile(_dot(hn, wr_ref[...]) + br_ref[...], n_groups, per_group)
    rt_ref[0] = route
    cnt_ref[0, 0] = jnp.broadcast_to(hist, cnt_ref.shape[2:])


def _cross(q, mk, mv, h, wo, gm, wr, br, heads, n_groups, per_group, tm=256):
    b, s, d = h.shape
    n_mem = mk.shape[1]
    tm = _tile(s, tm)
    row = lambda i, j: (i, j, 0)
    mem = pl.BlockSpec((1, n_mem, d), lambda i, j: (i, 0, 0))
    kern = functools.partial(_cross_kernel, heads=heads, scale=float(d // heads) ** -0.5,
                             n_groups=n_groups, per_group=per_group)
    return pl.pallas_call(
        kern,
        grid=(b, s // tm),
        in_specs=[pl.BlockSpec((1, tm, d), row), mem, mem, pl.BlockSpec((1, tm, d), row),
                  _resident(wo.shape), _resident(gm.shape), _resident(wr.shape), _resident(br.shape)],
        out_specs=[pl.BlockSpec((1, tm, d), row), pl.BlockSpec((1, tm, d), row), pl.BlockSpec((1, tm, LANE), row),
                   pl.BlockSpec((1, 1, 8, LANE), lambda i, j: (i, j, 0, 0))],
        out_shape=[jax.ShapeDtypeStruct((b, s, d), F32), jax.ShapeDtypeStruct((b, s, d), BF16),
                   jax.ShapeDtypeStruct((b, s, LANE), F32), jax.ShapeDtypeStruct((b, s // tm, 8, LANE), F32)],
        compiler_params=_cp("parallel", "parallel"),
        name="cross_attn_router",
    )(q, mk, mv, h, wo, gm, wr, br)


def _moe_kernel(be_ref, on_ref, x_ref, wg_ref, wu_ref, wd_ref, y_ref, wgb_ref, wub_ref, wdb_ref):
    i = pl.program_id(0)

    @pl.when((i == 0) | (be_ref[i] != be_ref[jnp.maximum(i - 1, 0)]))
    def _():
        wgb_ref[...] = wg_ref[0].astype(BF16)
        wub_ref[...] = wu_ref[0].astype(BF16)
        wdb_ref[...] = wd_ref[0].astype(BF16)

    @pl.when(on_ref[i] != 0)
    def _():
        xb = x_ref[...]
        g = _dot(xb, wgb_ref[...])
        u = _dot(xb, wub_ref[...])
        hb = (g * _sigmoid(g) * u).astype(BF16)
        y_ref[...] = _dot(hb, wdb_ref[...])

    @pl.when(on_ref[i] == 0)
    def _():
        y_ref[...] = jnp.zeros(y_ref.shape, y_ref.dtype)


def _moe_experts(blk_e, blk_on, x_slot, wg, wu, wd):
    ns, d = x_slot.shape
    f = wg.shape[2]
    n_blk = ns // MOE_ROWS
    wspec = lambda shp: pl.BlockSpec((1,) + shp, lambda i, be, on: (be[i], 0, 0))
    grid_spec = pltpu.PrefetchScalarGridSpec(
        num_scalar_prefetch=2,
        grid=(n_blk,),
        in_specs=[pl.BlockSpec((MOE_ROWS, d), lambda i, be, on: (i, 0)),
                  wspec((d, f)), wspec((d, f)), wspec((f, d))],
        out_specs=pl.BlockSpec((MOE_ROWS, d), lambda i, be, on: (i, 0)),
        scratch_shapes=[pltpu.VMEM((d, f), BF16), pltpu.VMEM((d, f), BF16), pltpu.VMEM((f, d), BF16)],
    )
    return pl.pallas_call(
        _moe_kernel,
        grid_spec=grid_spec,
        out_shape=jax.ShapeDtypeStruct((ns, d), F32),
        compiler_params=_cp("arbitrary"),
        name="moe_experts",
    )(blk_e, blk_on, x_slot, wg, wu, wd)


def _combine_kernel(idx_ref, h_ref, rt_ref, g_ref, y_hbm, o_ref, buf_ref, sem_ref, *, tm, norm):
    def issue(r, c):
        for ch in range(TOP_K):
            slot = idx_ref[0, TOP_K * r + ch]
            pltpu.make_async_copy(y_hbm.at[pl.ds(slot, 1)], buf_ref.at[ch, pl.ds(r, 1)], sem_ref.at[ch]).start()
        return c

    lax.fori_loop(0, tm, issue, 0, unroll=8)
    acc = h_ref[...]
    for ch in range(TOP_K):
        pltpu.make_async_copy(y_hbm.at[pl.ds(0, tm)], buf_ref.at[ch], sem_ref.at[ch]).wait()
        acc = acc + buf_ref[ch] * rt_ref[:, TOP_K + ch:TOP_K + ch + 1]
    o_ref[...] = _rms(acc, g_ref[...]) if norm else acc


def _combine(h, y_slot, slot_of, route, row_off, g, tm=512):
    t, d = h.shape
    tm = _tile(t, tm)
    assert row_off % tm == 0
    off = row_off // tm
    idx = slot_of[row_off * TOP_K:(row_off + t) * TOP_K].reshape(t // tm, 1, TOP_K * tm)
    norm = g is not None
    g = g if norm else jnp.ones((1, d), F32)
    return pl.pallas_call(
        functools.partial(_combine_kernel, tm=tm, norm=norm),
        grid=(t // tm,),
        in_specs=[pl.BlockSpec((None, 1, TOP_K * tm), lambda i: (i, 0, 0), memory_space=pltpu.SMEM),
                  pl.BlockSpec((tm, d), lambda i: (i, 0)),
                  pl.BlockSpec((tm, LANE), lambda i: (i + off, 0)),
                  pl.BlockSpec((1, d), lambda i: (0, 0)),
                  pl.BlockSpec(memory_space=pl.ANY)],
        out_specs=pl.BlockSpec((tm, d), lambda i: (i, 0)),
        out_shape=jax.ShapeDtypeStruct((t, d), F32),
        scratch_shapes=[pltpu.VMEM((TOP_K, tm, d), F32), pltpu.SemaphoreType.DMA((TOP_K,))],
        compiler_params=_cp("arbitrary"),
        name="moe_combine_norm",
    )(idx, h, route, g, y_slot)


def _pad_cols(w, n):
    return jnp.pad(w, ((0, 0),) * (w.ndim - 1) + ((0, n - w.shape[-1]),))


def _pad_rows(w, n):
    return jnp.pad(w, ((0, n - w.shape[0]), (0, 0)))


def _rot_half_cols(w):
    half = w.shape[-1] // 2
    return jnp.concatenate([-w[..., half:], w[..., :half]], axis=-1)


def _rope_table(pos):
    half = ROPE // 2
    inv = ROPE_THETA ** (-jnp.arange(half, dtype=F32) / half)
    ang = pos.astype(F32)[:, None] * inv[None, :]
    cos, sin = jnp.cos(ang), jnp.sin(ang)
    return jnp.concatenate([cos, cos, sin, sin], axis=1)


def _prep_layer(lp, dims):
    ql, kl, rope, dim, dl, il, gl, d = (dims[k] for k in ("q_lora", "kv_lora", "rope", "rw_dim", "decay_lora",
                                                          "iclr_lora", "gate_lora", "d_model"))
    w_in = lp["w_in"]
    i0, i1, i2 = ql, ql + kl, ql + kl + rope
    i3 = i2 + 3 * dim + dl + il + gl
    w_kr = w_in[:, i1:i2]
    out = {}
    out["w_mla"] = jnp.concatenate(
        [w_in[:, :i0], w_kr, _rot_half_cols(w_kr), w_in[:, i0:i1]], axis=1).astype(BF16)
    rw = w_in[:, i2:i3]
    c3 = 3 * dim
    glp = -(-gl // LANE) * LANE

    def regroup(t):
        return jnp.concatenate([t[..., :c3], _pad_cols(t[..., c3:c3 + dl], LANE),
                                _pad_cols(t[..., c3 + dl:c3 + dl + il], LANE),
                                _pad_cols(t[..., c3 + dl + il:], glp)], axis=-1)

    out["regroup"] = regroup
    out["w_rw"] = regroup(rw).astype(BF16)
    out["mu"] = regroup(lp["rw_mu"][None, :])
    out["w_gates"] = w_in[:, i3:].astype(BF16)
    heads = lp["w_uq"].shape[1]
    nope = lp["w_uq"].shape[2] - rope
    wq = lp["w_uq"]
    out["w_q"] = jnp.concatenate([wq[..., :nope], wq[..., nope:], _rot_half_cols(wq[..., nope:])],
                                 axis=-1).reshape(ql, heads * 256).astype(BF16)
    out["w_uk"] = lp["w_uk"].reshape(kl, -1).astype(BF16)
    out["w_uv"] = lp["w_uv"].reshape(kl, -1).astype(BF16)
    out["w_uv_t"] = out["w_uv"].T
    out["wb"] = _pad_rows(lp["rw_wb"], LANE).astype(BF16)
    out["ab"] = _pad_rows(lp["rw_ab"], LANE).astype(BF16)
    out["gb"] = _pad_rows(lp["rw_gb"], glp).astype(BF16)
    n_heads = dims["rw_heads"]
    hd = dim // n_heads
    e = (jnp.arange(dim)[:, None] // hd == jnp.arange(LANE)[None, :]).astype(BF16)
    out["e"], out["et"] = e, e.T
    out["w_br_rwkv"] = lp["w_br_rwkv"].astype(BF16)
    out["w_br_mla"] = lp["w_br_mla"].astype(BF16)
    out["w_out"] = lp["w_out"].astype(BF16)
    out["w_mq"] = lp["w_mq"].reshape(d, d).astype(BF16)
    out["w_mk"] = lp["w_mk"].reshape(d, d).astype(BF16)
    out["w_mv"] = lp["w_mv"].reshape(d, d).astype(BF16)
    out["w_mo"] = lp["w_mo"].reshape(d, d).astype(BF16)
    n_g, n_e = lp["w_rg"].shape[1], lp["w_re"].shape[1]
    out["w_r"] = _pad_cols(jnp.concatenate([lp["w_rg"], lp["w_re"]], axis=1), LANE).astype(BF16)
    out["b_r"] = _pad_cols(jnp.concatenate([lp["b_rg"], lp["b_re"]])[None, :], LANE)
    out["w_eg"], out["w_eu"], out["w_ed"] = lp["w_eg"], lp["w_eu"], lp["w_ed"]
    return out


def _row(v):
    return v.reshape(1, -1)


def _mixer(x, lp, wp, dims, pos, shift0, wkv0, attend):
    b, s, d = x.shape
    heads, rw_heads = dims["mla_heads"], dims["rw_heads"]
    gmix = _row(lp["g_mix"])
    cs = _rope_table(pos)
    scale = float(dims["nope"] + dims["rope"]) ** -0.5
    q, k, v, c, kr = _mla_proj(x, gmix, wp["w_mla"], _row(lp["g_q"]), wp["w_q"], _row(lp["g_kv"]),
                               wp["w_uk"], wp["w_uv_t"], cs, heads, scale)
    y_mla = attend(q, k, v, c)
    sh0 = wp["regroup"](shift0)
    r, dec, k2, vr, kk, bb, g, bonus, sh = _rwkv_pre(
        x, gmix, wp["w_rw"], sh0, wp["mu"], _row(lp["rw_w0"]), wp["wb"], _row(lp["rw_a0"]), wp["ab"], wp["gb"],
        _row(lp["rw_kk"]), _row(lp["rw_ka"]), _row(lp["rw_rk"]), wp["e"], wp["et"])
    n = dims["rw_dim"] // rw_heads
    o, s_t = _rwkv_scan(r, dec, k2, vr, kk, bb, _state_to_streams(wkv0))
    wkv = _state_from_streams(s_t, b, rw_heads)
    gates, = _proj(x.reshape(b * s, d), gmix, [wp["w_gates"]], 512, out_dtype=BF16, gate=True)
    gates = gates.reshape(b, s, 2 * d)
    dim = dims["rw_dim"]
    merged = _mix(o, bonus, g, y_mla, gates, _row(lp["rw_ln_g"]), _row(lp["rw_ln_b"]), wp["e"], wp["et"],
                  wp["w_br_rwkv"], wp["w_br_mla"], n).reshape(b * s, d)
    h, qc = _outproj(x.reshape(b * s, d), merged, wp["w_out"], _row(lp["g_cross"]), wp["w_mq"])
    c3, dl, il = 3 * dim, dims["decay_lora"], dims["iclr_lora"]
    shift = jnp.concatenate([sh[..., :c3], sh[..., c3:c3 + dl], sh[..., c3 + LANE:c3 + LANE + il],
                             sh[..., c3 + 2 * LANE:c3 + 2 * LANE + dims["gate_lora"]]], axis=-1)
    return h.reshape(b, s, d), qc.reshape(b, s, d), c, kr, k, wkv, shift


def _moe(hn, route, counts, wp):
    t, d = hn.shape
    n_exp = counts.shape[0]
    e_idx = route[:, :TOP_K]
    m = t * TOP_K
    n_blk = -(-(m + n_exp * (MOE_ROWS - 1)) // MOE_ROWS)
    ns = n_blk * MOE_ROWS
    flat_e = e_idx.reshape(m).astype(jnp.int32)
    order = jnp.argsort(flat_e).astype(jnp.int32)
    padded = (counts + MOE_ROWS - 1) // MOE_ROWS * MOE_ROWS
    pad_end = jnp.cumsum(padded)
    pad_start = pad_end - padded
    start = jnp.cumsum(counts) - counts
    blk_start = jnp.arange(n_blk, dtype=jnp.int32) * MOE_ROWS
    blk_e = jnp.minimum(jnp.sum(pad_end[None, :] <= blk_start[:, None], axis=1), n_exp - 1).astype(jnp.int32)
    blk_on = (blk_start < pad_end[-1]).astype(jnp.int32)
    off = (blk_start - pad_start[blk_e])[:, None] + jnp.arange(MOE_ROWS, dtype=jnp.int32)[None, :]
    valid = ((off < counts[blk_e][:, None]) & (blk_on[:, None] != 0)).reshape(ns)
    asg = order[jnp.clip(start[blk_e][:, None] + off, 0, m - 1).reshape(ns)]
    slot_tok = jnp.where(valid, asg // TOP_K, jnp.arange(ns, dtype=jnp.int32) % t)
    y_slot = _moe_experts(blk_e, blk_on, hn[slot_tok], wp["w_eg"], wp["w_eu"], wp["w_ed"])
    key = jnp.where(valid, asg, m + jnp.arange(ns, dtype=jnp.int32))
    slot_of = jnp.argsort(key)[:m].astype(jnp.int32)
    return y_slot, slot_of


def kernel(x_prompt, x_sample, mem_prompt, cache_kv_latent, cache_k_rope, cache_mem_k, cache_mem_v, state_wkv, state_shift, g_mix, w_in, g_q, w_uq, g_kv, w_uk, w_uv, rw_mu, rw_w0, rw_wb, rw_a0, rw_ab, rw_gb, rw_kk, rw_ka, rw_rk, rw_ln_g, rw_ln_b, w_br_rwkv, w_br_mla, w_out, g_cross, g_mem, w_mq, w_mk, w_mv, w_mo, g_moe, w_rg, b_rg, w_re, b_re, w_eg, w_eu, w_ed, g_final):
    depth = g_mix.shape[0]
    b, s, d = x_prompt.shape
    bs, ss, _ = x_sample.shape
    past = cache_kv_latent.shape[2]
    n_mem = mem_prompt.shape[1]
    mem_heads = w_mq.shape[2]
    rw_heads, rw_hd = rw_rk.shape[1:]
    dims = dict(d_model=d, q_lora=g_q.shape[1], kv_lora=g_kv.shape[1], mla_heads=w_uq.shape[2],
                nope=w_uk.shape[3], rope=cache_k_rope.shape[3], rw_heads=rw_heads, rw_dim=rw_heads * rw_hd,
                decay_lora=rw_wb.shape[1], iclr_lora=rw_ab.shape[1], gate_lora=rw_gb.shape[1])
    assert dims["rope"] == ROPE and dims["nope"] == LANE and w_uv.shape[3] == LANE
    assert dims["decay_lora"] <= LANE and dims["iclr_lora"] <= LANE and rw_heads <= LANE
    n_groups, n_exp = w_rg.shape[2], w_re.shape[2]
    per_group = n_exp // n_groups
    stacked = dict(g_mix=g_mix, w_in=w_in, g_q=g_q, w_uq=w_uq, g_kv=g_kv, w_uk=w_uk, w_uv=w_uv, rw_mu=rw_mu,
                   rw_w0=rw_w0, rw_wb=rw_wb, rw_a0=rw_a0, rw_ab=rw_ab, rw_gb=rw_gb, rw_kk=rw_kk, rw_ka=rw_ka,
                   rw_rk=rw_rk, rw_ln_g=rw_ln_g, rw_ln_b=rw_ln_b, w_br_rwkv=w_br_rwkv, w_br_mla=w_br_mla,
                   w_out=w_out, g_cross=g_cross, g_mem=g_mem, w_mq=w_mq, w_mk=w_mk, w_mv=w_mv, w_mo=w_mo,
                   g_moe=g_moe, w_rg=w_rg, b_rg=b_rg, w_re=w_re, b_re=b_re, w_eg=w_eg, w_eu=w_eu, w_ed=w_ed)
    h_p, h_s = x_prompt, x_sample
    outs = [[] for _ in range(10)]
    for l in range(depth):
        lp = {name: val[l] for name, val in stacked.items()}
        wp = _prep_layer(lp, dims)

        h, qc, c, kr, _, wkv, shift = _mixer(
            h_p, lp, wp, dims, jnp.arange(s), jnp.zeros((b, 1, state_shift.shape[3]), F32),
            jnp.zeros((b, rw_heads, rw_hd, rw_hd), F32), lambda q, k, v, c_new: _mla_prompt_attn(q, k, v))
        mk, mk_b, mv, mv_b = _proj(mem_prompt.reshape(b * n_mem, d), _row(lp["g_mem"]), [wp["w_mk"], wp["w_mv"]], 256,
                                   heads=mem_heads, twin=True)
        gm, wr, br = _row(lp["g_moe"]), wp["w_r"], wp["b_r"]
        h2_p, hn_p, rt_p, cnt_p = _cross(qc, mk_b.reshape(b, n_mem, d), mv_b.reshape(b, n_mem, d), h, wp["w_mo"], gm, wr,
                                         br, mem_heads, n_groups, per_group)
        for lst, val in zip(outs[:6], (c, kr, mk.reshape(b, n_mem, mem_heads, -1),
                                       mv.reshape(b, n_mem, mem_heads, -1), wkv, shift)):
            lst.append(val)

        kr_past = jnp.pad(cache_k_rope[l], ((0, 0), (0, 0), (0, LANE - ROPE))).astype(BF16)

        def attend_sample(q, k, v, c_new, l=l, kr_past=kr_past, wp=wp):
            return _mla_sample_attn(q, cache_kv_latent[l], kr_past, c_new, k, wp["w_uk"], wp["w_uv"])

        h, qc, c, kr, _, wkv, shift = _mixer(h_s, lp, wp, dims, past + jnp.arange(ss), state_shift[l],
                                             state_wkv[l], attend_sample)
        h2_s, hn_s, rt_s, cnt_s = _cross(qc, cache_mem_k[l].reshape(bs, n_mem, d).astype(BF16),
                                         cache_mem_v[l].reshape(bs, n_mem, d).astype(BF16), h, wp["w_mo"], gm, wr, br,
                                         mem_heads, n_groups, per_group)
        for lst, val in zip(outs[6:], (c, kr, wkv, shift)):
            lst.append(val)

        tp, tsm = b * s, bs * ss
        hn = jnp.concatenate([hn_p.reshape(tp, d), hn_s.reshape(tsm, d)], axis=0)
        route = jnp.concatenate([rt_p.reshape(tp, LANE), rt_s.reshape(tsm, LANE)], axis=0)
        counts = (jnp.sum(cnt_p[:, :, 0, :n_exp], axis=(0, 1)) + jnp.sum(cnt_s[:, :, 0, :n_exp], axis=(0, 1)))
        y_slot, slot_of = _moe(hn, route, counts.astype(jnp.int32), wp)
        g_fin = _row(g_final) if l == depth - 1 else None
        h_p = _combine(h2_p.reshape(tp, d), y_slot, slot_of, route, 0, g_fin).reshape(b, s, d)
        h_s = _combine(h2_s.reshape(tsm, d), y_slot, slot_of, route, tp, g_fin).reshape(bs, ss, d)
    stacks = [jnp.stack(o) for o in outs]
    return (h_p, h_s, *stacks)
```

```python
import functools

import jax
import jax.numpy as jnp
from jax import lax
from jax.experimental import pallas as pl
from jax.experimental.pallas import tpu as pltpu

F32 = jnp.float32
BF16 = jnp.bfloat16

EPS = 1e-6
GN_EPS = 64e-5
ROPE_THETA = 10000.0
CHUNK = 64
TOP_K = 2
NEG = -1e30

LANE = 128
V7X_VMEM_LIMIT = 56 << 20
ROPE = 64
MOE_ROWS = 256
SCAN_TS = 16
SCAN_UNROLL = 16


def _cp(*sem):
    return pltpu.CompilerParams(dimension_semantics=sem, vmem_limit_bytes=V7X_VMEM_LIMIT)


def _tile(n, pref):
    if n <= pref:
        return n
    for t in range(pref, 7, -1):
        if n % t == 0 and t % 8 == 0:
            return t
    return n


def _resident(shape):
    nd = len(shape)
    return pl.BlockSpec(shape, lambda *_: (0,) * nd, pipeline_mode=pl.Buffered(1))


def _rms(x, g):
    return x * lax.rsqrt(jnp.mean(x * x, axis=-1, keepdims=True) + EPS) * g


def _sigmoid(x):
    return 1.0 / (1.0 + jnp.exp(-x))


def _dot(a, b):
    return jnp.dot(a, b, preferred_element_type=F32)


def _dot_t(a, b):
    return lax.dot_general(a, b, (((1,), (1,)), ((), ())), preferred_element_type=F32)


def _seg_sum(x, e, et):
    hi = x.astype(BF16)
    lo = (x - hi.astype(F32)).astype(BF16)
    s = _dot(hi, e) + _dot(lo, e)
    shi = s.astype(BF16)
    slo = (s - shi.astype(F32)).astype(BF16)
    return _dot(shi, et) + _dot(slo, et)


def _proj_kernel(x_ref, g_ref, *refs, n_w, gate, twin, cols):
    w_refs, o_refs = refs[:n_w], refs[n_w:]
    xn = _rms(x_ref[...], g_ref[...]).astype(BF16)
    per_w = 2 if twin else 1
    for i, w_ref in enumerate(w_refs):
        o_ref = o_refs[per_w * i]
        n = w_ref.shape[1]
        for c in range(0, n, cols):
            y = _dot(xn, w_ref[:, c:c + cols])
            if gate:
                y = _sigmoid(y)
            if len(o_ref.shape) == 3:
                hd = o_ref.shape[2]
                o_ref[:, c // hd:(c + cols) // hd, :] = y.reshape(y.shape[0], cols // hd, hd).astype(o_ref.dtype)
            else:
                o_ref[:, c:c + cols] = y.astype(o_ref.dtype)
            if twin:
                o_refs[per_w * i + 1][:, c:c + cols] = y.astype(BF16)


def _proj(x, g, ws, tm, out_dtype=F32, gate=False, heads=None, twin=False, cols=1024):
    m, k = x.shape
    tm = _tile(m, tm)
    out_specs, out_shape = [], []
    for w in ws:
        n = w.shape[1]
        if heads is None:
            out_specs.append(pl.BlockSpec((tm, n), lambda i: (i, 0)))
            out_shape.append(jax.ShapeDtypeStruct((m, n), out_dtype))
        else:
            out_specs.append(pl.BlockSpec((tm, heads, n // heads), lambda i: (i, 0, 0)))
            out_shape.append(jax.ShapeDtypeStruct((m, heads, n // heads), out_dtype))
        if twin:
            out_specs.append(pl.BlockSpec((tm, n), lambda i: (i, 0)))
            out_shape.append(jax.ShapeDtypeStruct((m, n), BF16))
    cols = min(cols, min(w.shape[1] for w in ws))
    return pl.pallas_call(
        functools.partial(_proj_kernel, n_w=len(ws), gate=gate, twin=twin, cols=cols),
        grid=(m // tm,),
        in_specs=[pl.BlockSpec((tm, k), lambda i: (i, 0)), _resident(g.shape)] + [_resident(w.shape) for w in ws],
        out_specs=out_specs,
        out_shape=out_shape,
        compiler_params=_cp("parallel"),
        name="norm_proj",
    )(x, g, *ws)


def _mla_proj_kernel(x_ref, gmix_ref, wm_ref, gq_ref, wq_ref, gkv_ref, wuk_ref, wuv_ref, cs_ref,
                     q_ref, k_ref, v_ref, c_ref, kr_ref, *, heads, q_lora, scale):
    u = _rms(x_ref[0], gmix_ref[...]).astype(BF16)
    p = _dot(u, wm_ref[...])
    cs = cs_ref[...]
    lane = lax.broadcasted_iota(jnp.int32, cs.shape, 1)

    def rope_tile(t2):
        t = t2 * cs
        return jnp.where(lane < ROPE, t + pltpu.roll(t, ROPE, axis=1), 0.0)

    krt = rope_tile(p[:, q_lora:q_lora + LANE])
    kr_ref[0] = krt[:, :ROPE]
    krb = krt.astype(BF16)
    c = _rms(p[:, q_lora + LANE:], gkv_ref[...])
    c_ref[0] = c
    cb = c.astype(BF16)
    kn = _dot(cb, wuk_ref[...])
    vt = _dot_t(wuv_ref[...], cb)
    qn = _rms(p[:, :q_lora], gq_ref[...]).astype(BF16)
    qf = _dot(qn, wq_ref[...])
    for h in range(heads):
        q_ref[0, h, :, :LANE] = (qf[:, h * 256:h * 256 + LANE] * scale).astype(BF16)
        q_ref[0, h, :, LANE:] = (rope_tile(qf[:, h * 256 + LANE:(h + 1) * 256]) * scale).astype(BF16)
        k_ref[0, h, :, :LANE] = kn[:, h * LANE:(h + 1) * LANE].astype(BF16)
        k_ref[0, h, :, LANE:] = krb
        v_ref[0, h] = vt[h * LANE:(h + 1) * LANE, :].astype(BF16)


def _mla_proj(x, gmix, wm, gq, wq, gkv, wuk, wuv, cs, heads, scale, tm=256):
    b, s, d = x.shape
    q_lora = gq.shape[1]
    kv_lora = gkv.shape[1]
    tm = _tile(s, tm)
    kern = functools.partial(_mla_proj_kernel, heads=heads, q_lora=q_lora, scale=scale)
    return pl.pallas_call(
        kern,
        grid=(b, s // tm),
        in_specs=[pl.BlockSpec((1, tm, d), lambda i, j: (i, j, 0)),
                  _resident(gmix.shape), _resident(wm.shape), _resident(gq.shape), _resident(wq.shape),
                  _resident(gkv.shape), _resident(wuk.shape), _resident(wuv.shape),
                  pl.BlockSpec((tm, LANE), lambda i, j: (j, 0))],
        out_specs=[pl.BlockSpec((1, heads, tm, 256), lambda i, j: (i, 0, j, 0)),
                   pl.BlockSpec((1, heads, tm, 256), lambda i, j: (i, 0, j, 0)),
                   pl.BlockSpec((1, heads, LANE, tm), lambda i, j: (i, 0, 0, j)),
                   pl.BlockSpec((1, tm, kv_lora), lambda i, j: (i, j, 0)),
                   pl.BlockSpec((1, tm, ROPE), lambda i, j: (i, j, 0))],
        out_shape=[jax.ShapeDtypeStruct((b, heads, s, 256), BF16),
                   jax.ShapeDtypeStruct((b, heads, s, 256), BF16),
                   jax.ShapeDtypeStruct((b, heads, LANE, s), BF16),
                   jax.ShapeDtypeStruct((b, s, kv_lora), F32),
                   jax.ShapeDtypeStruct((b, s, ROPE), F32)],
        compiler_params=_cp("parallel", "parallel"),
        name="mla_proj",
    )(x, gmix, wm, gq, wq, gkv, wuk, wuv, cs)


def _flash_kernel(q_ref, k_ref, vt_ref, o_ref, m_ref, l_ref, acc_ref, sc_ref, *, tq, hps):
    qi = pl.program_id(2)
    m_ref[...] = jnp.full(m_ref.shape, NEG, F32)
    l_ref[...] = jnp.zeros(l_ref.shape, F32)
    acc_ref[...] = jnp.zeros(acc_ref.shape, F32)

    def scores(j, slot):
        start = pl.multiple_of(j * tq, tq)
        for h in range(hps):
            sc_ref[slot, h] = _dot_t(k_ref[0, h, pl.ds(start, tq), :], q_ref[0, h])

    def consume(j, slot, diag):
        start = pl.multiple_of(j * tq, tq)
        for h in range(hps):
            s = sc_ref[slot, h]
            if diag:
                kc = lax.broadcasted_iota(jnp.int32, s.shape, 0) // CHUNK
                qc = lax.broadcasted_iota(jnp.int32, s.shape, 1) // CHUNK
                s = jnp.where(kc <= qc, s, NEG)
            m = m_ref[h]
            m_new = jnp.maximum(m, jnp.max(s, axis=0, keepdims=True))
            alpha = jnp.exp(m - m_new)
            p = jnp.exp(s - m_new)
            l_ref[h] = l_ref[h] * alpha + jnp.sum(p, axis=0, keepdims=True)
            acc_ref[h] = acc_ref[h] * alpha + _dot(vt_ref[0, h, :, pl.ds(start, tq)], p.astype(BF16))
            m_ref[h] = m_new

    scores(0, 0)

    def body(j, c):
        scores(2 * j + 1, 1)
        consume(2 * j, 0, False)
        scores(2 * j + 2, 0)
        consume(2 * j + 1, 1, False)
        return c

    lax.fori_loop(0, qi // 2, body, 0)

    @pl.when(qi % 2 == 1)
    def _():
        scores(qi, 1)
        consume(qi - 1, 0, False)
        consume(qi, 1, True)

    @pl.when(qi % 2 == 0)
    def _():
        consume(qi, 0, True)

    for h in range(hps):
        o_ref[0, :, h * LANE:(h + 1) * LANE] = (acc_ref[h] * (1.0 / l_ref[h])).T.astype(o_ref.dtype)


def _mla_prompt_attn(q, k, vt, tq=512, hps=8):
    b, heads, s, _ = q.shape
    tq = _tile(s, tq)
    hps = min(hps, heads)
    assert tq % CHUNK == 0 and heads % hps == 0
    return pl.pallas_call(
        functools.partial(_flash_kernel, tq=tq, hps=hps),
        grid=(b, heads // hps, s // tq),
        in_specs=[pl.BlockSpec((1, hps, tq, 256), lambda i, h, j: (i, h, j, 0)),
                  pl.BlockSpec((1, hps, s, 256), lambda i, h, j: (i, h, 0, 0)),
                  pl.BlockSpec((1, hps, LANE, s), lambda i, h, j: (i, h, 0, 0))],
        out_specs=pl.BlockSpec((1, tq, hps * LANE), lambda i, h, j: (i, j, h)),
        out_shape=jax.ShapeDtypeStruct((b, s, heads * LANE), BF16),
        scratch_shapes=[pltpu.VMEM((hps, 1, tq), F32), pltpu.VMEM((hps, 1, tq), F32),
                        pltpu.VMEM((hps, LANE, tq), F32), pltpu.VMEM((2, hps, tq, tq), F32)],
        compiler_params=_cp("parallel", "parallel", "arbitrary"),
        name="mla_prompt_attn",
    )(q, k, vt)


def _mla_sample_kernel(q_ref, cp_ref, krp_ref, cn_ref, krn_ref, wuk_ref, wuv_ref, o_ref,
                       qa_ref, qr_ref, m_ref, l_ref, acc_ref, *, heads, sq):
    kb = pl.program_id(1)

    @pl.when(kb == 0)
    def _():
        for h in range(heads):
            qh = q_ref[0, h]
            qa_ref[h * sq:(h + 1) * sq, :] = _dot_t(qh[:, :LANE], wuk_ref[:, h * LANE:(h + 1) * LANE]).astype(BF16)
            qr_ref[h * sq:(h + 1) * sq, :] = qh[:, LANE:]
        m_ref[...] = jnp.full(m_ref.shape, NEG, F32)
        l_ref[...] = jnp.zeros(l_ref.shape, F32)
        acc_ref[...] = jnp.zeros(acc_ref.shape, F32)

    def update(cb, krb):
        s = _dot_t(qa_ref[...], cb) + _dot_t(qr_ref[...], krb)
        m_new = jnp.maximum(m_ref[...], jnp.max(s, axis=-1, keepdims=True))
        alpha = jnp.exp(m_ref[...] - m_new)
        p = jnp.exp(s - m_new)
        l_ref[...] = l_ref[...] * alpha + jnp.sum(p, axis=-1, keepdims=True)
        acc_ref[...] = acc_ref[...] * alpha + _dot(p.astype(BF16), cb)
        m_ref[...] = m_new

    update(cp_ref[0].astype(BF16), krp_ref[0])

    @pl.when(kb == pl.num_programs(1) - 1)
    def _():
        update(cn_ref[0].astype(BF16), krn_ref[0, 0])
        o_lat = (acc_ref[...] / l_ref[...]).astype(BF16)
        for h in range(heads):
            o_ref[0, :, h * LANE:(h + 1) * LANE] = _dot(
                o_lat[h * sq:(h + 1) * sq], wuv_ref[:, h * LANE:(h + 1) * LANE]).astype(o_ref.dtype)


def _mla_sample_attn(q, c_past, kr_past, c_new, k_new, wuk, wuv, tk=1024):
    b, heads, sq, _ = q.shape
    past, kv_lora = c_past.shape[1:]
    tk = _tile(past, tk)
    kern = functools.partial(_mla_sample_kernel, heads=heads, sq=sq)
    return pl.pallas_call(
        kern,
        grid=(b, past // tk),
        in_specs=[pl.BlockSpec((1, heads, sq, 256), lambda i, j: (i, 0, 0, 0)),
                  pl.BlockSpec((1, tk, kv_lora), lambda i, j: (i, j, 0)),
                  pl.BlockSpec((1, tk, LANE), lambda i, j: (i, j, 0)),
                  pl.BlockSpec((1, sq, kv_lora), lambda i, j: (i, 0, 0)),
                  pl.BlockSpec((1, 1, sq, LANE), lambda i, j: (i, 0, 0, 1)),
                  _resident(wuk.shape), _resident(wuv.shape)],
        out_specs=pl.BlockSpec((1, sq, heads * LANE), lambda i, j: (i, 0, 0)),
        out_shape=jax.ShapeDtypeStruct((b, sq, heads * LANE), BF16),
        scratch_shapes=[pltpu.VMEM((heads * sq, kv_lora), BF16), pltpu.VMEM((heads * sq, LANE), BF16),
                        pltpu.VMEM((heads * sq, 1), F32), pltpu.VMEM((heads * sq, 1), F32),
                        pltpu.VMEM((heads * sq, kv_lora), F32)],
        compiler_params=_cp("parallel", "arbitrary"),
        name="mla_sample_attn",
    )(q, c_past, kr_past, c_new, k_new, wuk, wuv)


def _rwkv_pre_kernel(x_ref, gmix_ref, w_ref, sh0_ref, mu_ref, w0_ref, wb_ref, a0_ref, ab_ref, gb_ref,
                     kkp_ref, ka_ref, rk_ref, e_ref, et_ref,
                     r_ref, dec_ref, k_ref, v_ref, kk_ref, b_ref, g_ref, bonus_ref, sh_ref, last_ref, *, dim):
    si = pl.program_id(1)

    def put(ref, val):
        nb, ts, tiles, _ = ref.shape
        ref[...] = val.reshape(nb * ts, tiles, LANE).reshape(nb, ts, tiles, LANE)

    u = _rms(x_ref[0], gmix_ref[...]).astype(BF16)
    p = _dot(u, w_ref[...])
    tm = p.shape[0]

    @pl.when(si == 0)
    def _():
        last_ref[...] = sh0_ref[0]

    row = lax.broadcasted_iota(jnp.int32, p.shape, 0)
    prev = jnp.where(row == 0, last_ref[...], pltpu.roll(p, 1, axis=0))
    last_ref[...] = p[tm - 1:tm, :]
    sh_ref[0] = p[tm - 1:tm, :]
    xs = p + mu_ref[...] * (prev - p)
    r = xs[:, :dim]
    k = xs[:, dim:2 * dim]
    v = xs[:, 2 * dim:3 * dim]
    dw = xs[:, 3 * dim:3 * dim + LANE]
    da = xs[:, 3 * dim + LANE:3 * dim + 2 * LANE]
    dg = xs[:, 3 * dim + 2 * LANE:]
    z = -(w0_ref[...] + _dot(jnp.tanh(dw).astype(BF16), wb_ref[...]))
    softplus = jnp.maximum(z, 0.0) + jnp.log(1.0 + jnp.exp(-jnp.abs(z)))
    put(dec_ref, jnp.exp(-jnp.exp(-softplus - 0.5)))
    a = _sigmoid(a0_ref[...] + _dot(da.astype(BF16), ab_ref[...]))
    g_ref[0] = _dot(_sigmoid(dg).astype(BF16), gb_ref[...])
    kk = k * kkp_ref[...]
    nrm = jnp.sqrt(_seg_sum(kk * kk, e_ref[...], et_ref[...]))
    kk = kk / jnp.maximum(nrm, 1e-12)
    k2 = k * (1.0 + (a - 1.0) * ka_ref[...])
    put(r_ref, r)
    put(k_ref, k2)
    put(v_ref, v)
    put(kk_ref, kk)
    put(b_ref, kk * a)
    bonus_ref[0] = _seg_sum(r * k2 * rk_ref[...], e_ref[...], et_ref[...]) * v


def _rwkv_pre(x, gmix, w, sh0, mu, w0, wb, a0, ab, gb, kkp, ka, rk, e, et, tm=256):
    b, s, d = x.shape
    dim = w0.shape[1]
    width = w.shape[1]
    tm = _tile(s, tm)
    ts = min(SCAN_TS, s)
    assert tm % ts == 0 and dim % LANE == 0
    row = lambda i, j: (i, j, 0)
    act = pl.BlockSpec((1, tm, dim), row)
    act_shape = jax.ShapeDtypeStruct((b, s, dim), F32)
    tiles = dim // LANE
    assert tiles % 8 == 0
    seq = pl.BlockSpec((tm // ts, ts, tiles, LANE), lambda i, j: (j, 0, i, 0))
    seq_shape = jax.ShapeDtypeStruct((s // ts, ts, b * tiles, LANE), F32)
    consts = (gmix, w, mu, w0, wb, a0, ab, gb, kkp, ka, rk, e, et)
    return pl.pallas_call(
        functools.partial(_rwkv_pre_kernel, dim=dim),
        grid=(b, s // tm),
        in_specs=[pl.BlockSpec((1, tm, d), row), _resident(gmix.shape), _resident(w.shape),
                  pl.BlockSpec((1, 1, width), lambda i, j: (i, 0, 0))] + [_resident(c.shape) for c in consts[2:]],
        out_specs=[seq] * 6 + [act] * 2 + [pl.BlockSpec((1, 1, width), lambda i, j: (i, 0, 0))],
        out_shape=[seq_shape] * 6 + [act_shape] * 2 + [jax.ShapeDtypeStruct((b, 1, width), F32)],
        scratch_shapes=[pltpu.VMEM((1, width), F32)],
        compiler_params=_cp("parallel", "arbitrary"),
        name="rwkv_pre",
    )(x, gmix, w, sh0, *consts[2:])


def _rwkv_scan_kernel(r_ref, dec_ref, k_ref, v_ref, kk_ref, b_ref, s0_ref, o_ref, st_ref, xt_ref, ot_ref,
                      *, ts, n, unroll):
    tb = pl.program_id(1)
    hpl = LANE // n

    @pl.when(tb == 0)
    def _():
        st_ref[...] = s0_ref[...]

    srcs = (r_ref, dec_ref, k_ref, v_ref, kk_ref, b_ref)
    for i in range(ts):
        for a, ref in enumerate(srcs):
            xt_ref[a, i] = ref[i].T

    def step(i, c):
        for h2 in range(hpl):
            base = h2 * n
            vv = xt_ref[3, i, base:base + n, :]

            def p1(kx, sa):
                return sa + st_ref[h2, kx] * xt_ref[4, i, pl.ds(base + kx, 1), :]

            sa = lax.fori_loop(0, n, p1, jnp.zeros_like(vv), unroll=unroll)

            def p2(kx, o):
                s_new = (st_ref[h2, kx] * xt_ref[1, i, pl.ds(base + kx, 1), :]
                         - sa * xt_ref[5, i, pl.ds(base + kx, 1), :] + vv * xt_ref[2, i, pl.ds(base + kx, 1), :])
                st_ref[h2, kx] = s_new
                return o + s_new * xt_ref[0, i, pl.ds(base + kx, 1), :]

            ot_ref[i, base:base + n, :] = lax.fori_loop(0, n, p2, jnp.zeros_like(vv), unroll=unroll)
        return c

    lax.fori_loop(0, ts, step, 0)
    for i in range(ts):
        o_ref[i] = ot_ref[i].T


def _rwkv_scan(r, dec, k, v, kk, bb, s0):
    nt, ts, streams, _ = r.shape
    groups, hpl, n = s0.shape[:3]
    assert streams == groups * LANE
    seq = pl.BlockSpec((None, ts, LANE, LANE), lambda g, t: (t, 0, g, 0))
    state = pl.BlockSpec((None, hpl, n, n, LANE), lambda g, t: (g, 0, 0, 0, 0))
    return pl.pallas_call(
        functools.partial(_rwkv_scan_kernel, ts=ts, n=n, unroll=SCAN_UNROLL),
        grid=(groups, nt),
        in_specs=[seq] * 6 + [state],
        out_specs=[seq, state],
        out_shape=[jax.ShapeDtypeStruct(r.shape, F32), jax.ShapeDtypeStruct(s0.shape, F32)],
        scratch_shapes=[pltpu.VMEM((6, ts, LANE, LANE), F32), pltpu.VMEM((ts, LANE, LANE), F32)],
        compiler_params=_cp("parallel", "arbitrary"),
        name="rwkv_scan",
    )(r, dec, k, v, kk, bb, s0)


def _state_to_streams(wkv):
    b, h, n = wkv.shape[:3]
    hpl = LANE // n
    tiles = h // hpl
    bg = LANE // tiles
    g = b // bg
    t = wkv.reshape(g, bg, tiles, hpl, n, n)
    return t.transpose(0, 3, 5, 4, 1, 2).reshape(g, hpl, n, n, LANE)


def _state_from_streams(st, b, h):
    g, hpl, n = st.shape[:3]
    tiles = h // hpl
    bg = LANE // tiles
    t = st.reshape(g, hpl, n, n, bg, tiles)
    return t.transpose(0, 4, 5, 1, 3, 2).reshape(b, h, n, n)


def _mix_kernel(o_ref, bonus_ref, g_ref, ymla_ref, grw_ref, gmla_ref, lng_ref, lnb_ref, e_ref, et_ref,
                wr_ref, wm_ref, out_ref, *, inv_n):
    nb, ts, tiles, _ = o_ref.shape
    o = o_ref[...].reshape(nb * ts, tiles, LANE).reshape(nb * ts, tiles * LANE)
    e, et = e_ref[...], et_ref[...]
    d = o - _seg_sum(o, e, et) * inv_n
    var = _seg_sum(d * d, e, et) * inv_n
    on = d * lax.rsqrt(var + GN_EPS) * lng_ref[...] + lnb_ref[...]
    y = ((on + bonus_ref[0]) * g_ref[0]).astype(BF16)
    merged = grw_ref[0] * _dot(y, wr_ref[...]) + gmla_ref[0] * _dot(ymla_ref[0], wm_ref[...])
    out_ref[0] = merged.astype(out_ref.dtype)


def _mix(o, bonus, g, ymla, gates, lng, lnb, e, et, wr, wm, head_dim, tm=256):
    nt, ts = o.shape[:2]
    b, s, dim = bonus.shape
    tiles = dim // LANE
    d = wr.shape[1]
    md = ymla.shape[2]
    tm = _tile(s, tm)
    assert tm % ts == 0
    row = lambda i, j: (i, j, 0)
    return pl.pallas_call(
        functools.partial(_mix_kernel, inv_n=1.0 / head_dim),
        grid=(b, s // tm),
        in_specs=[pl.BlockSpec((tm // ts, ts, tiles, LANE), lambda i, j: (j, 0, i, 0)),
                  pl.BlockSpec((1, tm, dim), row), pl.BlockSpec((1, tm, dim), row), pl.BlockSpec((1, tm, md), row),
                  pl.BlockSpec((1, tm, d), lambda i, j: (i, j, 0)), pl.BlockSpec((1, tm, d), lambda i, j: (i, j, 1))]
                 + [_resident(c.shape) for c in (lng, lnb, e, et, wr, wm)],
        out_specs=pl.BlockSpec((1, tm, d), row),
        out_shape=jax.ShapeDtypeStruct((b, s, d), BF16),
        compiler_params=_cp("parallel", "parallel"),
        name="rwkv_post_mix",
    )(o, bonus, g, ymla, gates, gates, lng, lnb, e, et, wr, wm)


def _outproj_kernel(x_ref, m_ref, wo_ref, gc_ref, wq_ref, h_ref, q_ref):
    h = x_ref[...] + _dot(m_ref[...], wo_ref[...])
    h_ref[...] = h
    q_ref[...] = _dot(_rms(h, gc_ref[...]).astype(BF16), wq_ref[...]).astype(q_ref.dtype)


def _outproj(x, merged, wo, gc, wq, tm=256):
    t, d = x.shape
    tm = _tile(t, tm)
    row = lambda i: (i, 0)
    return pl.pallas_call(
        _outproj_kernel,
        grid=(t // tm,),
        in_specs=[pl.BlockSpec((tm, d), row), pl.BlockSpec((tm, d), row),
                  _resident(wo.shape), _resident(gc.shape), _resident(wq.shape)],
        out_specs=[pl.BlockSpec((tm, d), row), pl.BlockSpec((tm, d), row)],
        out_shape=[jax.ShapeDtypeStruct((t, d), F32), jax.ShapeDtypeStruct((t, d), BF16)],
        compiler_params=_cp("parallel"),
        name="outproj_crossq",
    )(x, merged, wo, gc, wq)


def _route_tile(lg, n_groups, per_group):
    n_exp = n_groups * per_group
    lane = lax.broadcasted_iota(jnp.int32, lg.shape, 1)
    first = lambda mask: jnp.min(jnp.where(mask, lane, 2 * LANE), axis=-1, keepdims=True)
    top = lambda mask: jnp.max(jnp.where(mask, lg, NEG), axis=-1, keepdims=True)
    is_g = lane < n_groups
    gmax = top(is_g)
    g_sel = first(is_g & (lg == gmax))
    p_grp = 1.0 / jnp.sum(jnp.where(is_g, jnp.exp(lg - gmax), 0.0), axis=-1, keepdims=True)
    eid = lane - n_groups
    in_grp = (eid >= g_sel * per_group) & (eid < (g_sel + 1) * per_group)
    v1 = top(in_grp)
    e1 = first(in_grp & (lg == v1))
    rest = in_grp & (lane != e1)
    v2 = top(rest)
    e2 = first(rest & (lg == v2))
    t = jnp.exp(v2 - v1)
    w1 = p_grp / (1.0 + t)
    w2 = w1 * t
    route = jnp.where(lane == 0, (e1 - n_groups).astype(F32),
                      jnp.where(lane == 1, (e2 - n_groups).astype(F32),
                                jnp.where(lane == 2, w1, jnp.where(lane == 3, w2, 0.0))))
    hits = jnp.where((lane == e1 - n_groups) | (lane == e2 - n_groups), 1.0, 0.0)
    assert n_exp <= LANE
    return route, jnp.sum(hits, axis=0, keepdims=True)


def _cross_kernel(q_ref, mk_ref, mv_ref, h_ref, wo_ref, gm_ref, wr_ref, br_ref, h2_ref, hn_ref, rt_ref, cnt_ref,
                  *, heads, scale, n_groups, per_group):
    q = q_ref[0]
    mk = mk_ref[0]
    mv = mv_ref[0]
    hd = q.shape[1] // heads
    outs = []
    for h in range(heads):
        sl = slice(h * hd, (h + 1) * hd)
        s = _dot_t(q[:, sl], mk[:, sl]) * scale
        p = jnp.exp(s - jnp.max(s, axis=-1, keepdims=True))
        p = p / jnp.sum(p, axis=-1, keepdims=True)
        outs.append(_dot(p.astype(BF16), mv[:, sl]).astype(BF16))
    o = jnp.concatenate(outs, axis=1)
    h2 = h_ref[0] + _dot(o, wo_ref[...])
    h2_ref[0] = h2
    hn = _rms(h2, gm_ref[...]).astype(BF16)
    hn_ref[0] = hn
    route, hist = _route_tile(_dot(hn, wr_ref[...]) + br_ref[...], n_groups, per_group)
    rt_ref[0] = route
    cnt_ref[0, 0] = jnp.broadcast_to(hist, cnt_ref.shape[2:])


def _cross(q, mk, mv, h, wo, gm, wr, br, heads, n_groups, per_group, tm=256):
    b, s, d = h.shape
    n_mem = mk.shape[1]
    tm = _tile(s, tm)
    row = lambda i, j: (i, j, 0)
    mem = pl.BlockSpec((1, n_mem, d), lambda i, j: (i, 0, 0))
    kern = functools.partial(_cross_kernel, heads=heads, scale=float(d // heads) ** -0.5,
                             n_groups=n_groups, per_group=per_group)
    return pl.pallas_call(
        kern,
        grid=(b, s // tm),
        in_specs=[pl.BlockSpec((1, tm, d), row), mem, mem, pl.BlockSpec((1, tm, d), row),
                  _resident(wo.shape), _resident(gm.shape), _resident(wr.shape), _resident(br.shape)],
        out_specs=[pl.BlockSpec((1, tm, d), row), pl.BlockSpec((1, tm, d), row), pl.BlockSpec((1, tm, LANE), row),
                   pl.BlockSpec((1, 1, 8, LANE), lambda i, j: (i, j, 0, 0))],
        out_shape=[jax.ShapeDtypeStruct((b, s, d), F32), jax.ShapeDtypeStruct((b, s, d), BF16),
                   jax.ShapeDtypeStruct((b, s, LANE), F32), jax.ShapeDtypeStruct((b, s // tm, 8, LANE), F32)],
        compiler_params=_cp("parallel", "parallel"),
        name="cross_attn_router",
    )(q, mk, mv, h, wo, gm, wr, br)


def _moe_kernel(be_ref, on_ref, first_ref, nxt_ref, par_ref, x_ref, wg_hbm, wu_hbm, wd_hbm, y_ref,
                wgf_ref, wuf_ref, wdf_ref, wgb_ref, wub_ref, wdb_ref, sem_ref):
    i = pl.program_id(0)

    def fetch(e, slot):
        return (pltpu.make_async_copy(wg_hbm.at[e], wgf_ref.at[slot], sem_ref.at[slot, 0]),
                pltpu.make_async_copy(wu_hbm.at[e], wuf_ref.at[slot], sem_ref.at[slot, 1]),
                pltpu.make_async_copy(wd_hbm.at[e], wdf_ref.at[slot], sem_ref.at[slot, 2]))

    @pl.when(first_ref[i] != 0)
    def _():
        slot = par_ref[i]

        @pl.when(i == 0)
        def _():
            for c in fetch(be_ref[0], 0):
                c.start()

        for c in fetch(be_ref[i], slot):
            c.wait()
        wgb_ref[...] = wgf_ref[slot].astype(BF16)
        wub_ref[...] = wuf_ref[slot].astype(BF16)
        wdb_ref[...] = wdf_ref[slot].astype(BF16)

        @pl.when(nxt_ref[i] >= 0)
        def _():
            for c in fetch(nxt_ref[i], 1 - slot):
                c.start()

    @pl.when(on_ref[i] != 0)
    def _():
        xb = x_ref[...]
        g = _dot(xb, wgb_ref[...])
        u = _dot(xb, wub_ref[...])
        hb = (g * _sigmoid(g) * u).astype(BF16)
        y_ref[...] = _dot(hb, wdb_ref[...])

    @pl.when(on_ref[i] == 0)
    def _():
        y_ref[...] = jnp.zeros(y_ref.shape, y_ref.dtype)


def _moe_experts(blk_e, blk_on, counts, x_slot, wg, wu, wd):
    ns, d = x_slot.shape
    n_exp, _, f = wg.shape
    n_blk = ns // MOE_ROWS
    prev_e = jnp.concatenate([jnp.full((1,), -1, jnp.int32), blk_e[:-1]])
    first = ((blk_on != 0) & (blk_e != prev_e)).astype(jnp.int32)
    par = ((jnp.cumsum(first) - 1) % 2).astype(jnp.int32)
    ids = jnp.where(counts > 0, jnp.arange(n_exp, dtype=jnp.int32), n_exp)
    nxt_tab = jnp.concatenate([lax.cummin(ids[::-1])[::-1][1:], jnp.full((1,), n_exp, jnp.int32)])
    nxt = nxt_tab[blk_e]
    nxt = jnp.where(nxt < n_exp, nxt, -1).astype(jnp.int32)
    hbm = pl.BlockSpec(memory_space=pl.ANY)
    grid_spec = pltpu.PrefetchScalarGridSpec(
        num_scalar_prefetch=5,
        grid=(n_blk,),
        in_specs=[pl.BlockSpec((MOE_ROWS, d), lambda i, *_: (i, 0)), hbm, hbm, hbm],
        out_specs=pl.BlockSpec((MOE_ROWS, d), lambda i, *_: (i, 0)),
        scratch_shapes=[pltpu.VMEM((2, d, f), F32), pltpu.VMEM((2, d, f), F32), pltpu.VMEM((2, f, d), F32),
                        pltpu.VMEM((d, f), BF16), pltpu.VMEM((d, f), BF16), pltpu.VMEM((f, d), BF16),
                        pltpu.SemaphoreType.DMA((2, 3))],
    )
    return pl.pallas_call(
        _moe_kernel,
        grid_spec=grid_spec,
        out_shape=jax.ShapeDtypeStruct((ns, d), F32),
        compiler_params=_cp("arbitrary"),
        name="moe_experts",
    )(blk_e, blk_on, first, nxt, par, x_slot, wg, wu, wd)


def _combine_kernel(idx_ref, h_ref, rt_ref, g_ref, y_hbm, o_ref, buf_ref, sem_ref, *, tm, norm):
    def issue(r, c):
        for ch in range(TOP_K):
            slot = idx_ref[0, TOP_K * r + ch]
            pltpu.make_async_copy(y_hbm.at[pl.ds(slot, 1)], buf_ref.at[ch, pl.ds(r, 1)], sem_ref.at[ch]).start()
        return c

    lax.fori_loop(0, tm, issue, 0, unroll=8)
    acc = h_ref[...]
    for ch in range(TOP_K):
        pltpu.make_async_copy(y_hbm.at[pl.ds(0, tm)], buf_ref.at[ch], sem_ref.at[ch]).wait()
        acc = acc + buf_ref[ch] * rt_ref[:, TOP_K + ch:TOP_K + ch + 1]
    o_ref[...] = _rms(acc, g_ref[...]) if norm else acc


def _combine(h, y_slot, slot_of, route, row_off, g, tm=512):
    t, d = h.shape
    tm = _tile(t, tm)
    assert row_off % tm == 0
    off = row_off // tm
    idx = slot_of[row_off * TOP_K:(row_off + t) * TOP_K].reshape(t // tm, 1, TOP_K * tm)
    norm = g is not None
    g = g if norm else jnp.ones((1, d), F32)
    return pl.pallas_call(
        functools.partial(_combine_kernel, tm=tm, norm=norm),
        grid=(t // tm,),
        in_specs=[pl.BlockSpec((None, 1, TOP_K * tm), lambda i: (i, 0, 0), memory_space=pltpu.SMEM),
                  pl.BlockSpec((tm, d), lambda i: (i, 0)),
                  pl.BlockSpec((tm, LANE), lambda i: (i + off, 0)),
                  pl.BlockSpec((1, d), lambda i: (0, 0)),
                  pl.BlockSpec(memory_space=pl.ANY)],
        out_specs=pl.BlockSpec((tm, d), lambda i: (i, 0)),
        out_shape=jax.ShapeDtypeStruct((t, d), F32),
        scratch_shapes=[pltpu.VMEM((TOP_K, tm, d), F32), pltpu.SemaphoreType.DMA((TOP_K,))],
        compiler_params=_cp("arbitrary"),
        name="moe_combine_norm",
    )(idx, h, route, g, y_slot)


def _pad_cols(w, n):
    return jnp.pad(w, ((0, 0),) * (w.ndim - 1) + ((0, n - w.shape[-1]),))


def _pad_rows(w, n):
    return jnp.pad(w, ((0, n - w.shape[0]), (0, 0)))


def _rot_half_cols(w):
    half = w.shape[-1] // 2
    return jnp.concatenate([-w[..., half:], w[..., :half]], axis=-1)


def _rope_table(pos):
    half = ROPE // 2
    inv = ROPE_THETA ** (-jnp.arange(half, dtype=F32) / half)
    ang = pos.astype(F32)[:, None] * inv[None, :]
    cos, sin = jnp.cos(ang), jnp.sin(ang)
    return jnp.concatenate([cos, cos, sin, sin], axis=1)


def _prep_layer(lp, dims):
    ql, kl, rope, dim, dl, il, gl, d = (dims[k] for k in ("q_lora", "kv_lora", "rope", "rw_dim", "decay_lora",
                                                          "iclr_lora", "gate_lora", "d_model"))
    w_in = lp["w_in"]
    i0, i1, i2 = ql, ql + kl, ql + kl + rope
    i3 = i2 + 3 * dim + dl + il + gl
    w_kr = w_in[:, i1:i2]
    out = {}
    out["w_mla"] = jnp.concatenate(
        [w_in[:, :i0], w_kr, _rot_half_cols(w_kr), w_in[:, i0:i1]], axis=1).astype(BF16)
    rw = w_in[:, i2:i3]
    c3 = 3 * dim
    glp = -(-gl // LANE) * LANE

    def regroup(t):
        return jnp.concatenate([t[..., :c3], _pad_cols(t[..., c3:c3 + dl], LANE),
                                _pad_cols(t[..., c3 + dl:c3 + dl + il], LANE),
                                _pad_cols(t[..., c3 + dl + il:], glp)], axis=-1)

    out["regroup"] = regroup
    out["w_rw"] = regroup(rw).astype(BF16)
    out["mu"] = regroup(lp["rw_mu"][None, :])
    out["w_gates"] = w_in[:, i3:].astype(BF16)
    heads = lp["w_uq"].shape[1]
    nope = lp["w_uq"].shape[2] - rope
    wq = lp["w_uq"]
    out["w_q"] = jnp.concatenate([wq[..., :nope], wq[..., nope:], _rot_half_cols(wq[..., nope:])],
                                 axis=-1).reshape(ql, heads * 256).astype(BF16)
    out["w_uk"] = lp["w_uk"].reshape(kl, -1).astype(BF16)
    out["w_uv"] = lp["w_uv"].reshape(kl, -1).astype(BF16)
    out["w_uv_t"] = out["w_uv"].T
    out["wb"] = _pad_rows(lp["rw_wb"], LANE).astype(BF16)
    out["ab"] = _pad_rows(lp["rw_ab"], LANE).astype(BF16)
    out["gb"] = _pad_rows(lp["rw_gb"], glp).astype(BF16)
    n_heads = dims["rw_heads"]
    hd = dim // n_heads
    e = (jnp.arange(dim)[:, None] // hd == jnp.arange(LANE)[None, :]).astype(BF16)
    out["e"], out["et"] = e, e.T
    out["w_br_rwkv"] = lp["w_br_rwkv"].astype(BF16)
    out["w_br_mla"] = lp["w_br_mla"].astype(BF16)
    out["w_out"] = lp["w_out"].astype(BF16)
    out["w_mq"] = lp["w_mq"].reshape(d, d).astype(BF16)
    out["w_mk"] = lp["w_mk"].reshape(d, d).astype(BF16)
    out["w_mv"] = lp["w_mv"].reshape(d, d).astype(BF16)
    out["w_mo"] = lp["w_mo"].reshape(d, d).astype(BF16)
    n_g, n_e = lp["w_rg"].shape[1], lp["w_re"].shape[1]
    out["w_r"] = _pad_cols(jnp.concatenate([lp["w_rg"], lp["w_re"]], axis=1), LANE).astype(BF16)
    out["b_r"] = _pad_cols(jnp.concatenate([lp["b_rg"], lp["b_re"]])[None, :], LANE)
    out["w_eg"], out["w_eu"], out["w_ed"] = lp["w_eg"], lp["w_eu"], lp["w_ed"]
    return out


def _row(v):
    return v.reshape(1, -1)


def _mixer(x, lp, wp, dims, pos, shift0, wkv0, attend):
    b, s, d = x.shape
    heads, rw_heads = dims["mla_heads"], dims["rw_heads"]
    gmix = _row(lp["g_mix"])
    cs = _rope_table(pos)
    scale = float(dims["nope"] + dims["rope"]) ** -0.5
    q, k, v, c, kr = _mla_proj(x, gmix, wp["w_mla"], _row(lp["g_q"]), wp["w_q"], _row(lp["g_kv"]),
                               wp["w_uk"], wp["w_uv_t"], cs, heads, scale)
    y_mla = attend(q, k, v, c)
    sh0 = wp["regroup"](shift0)
    r, dec, k2, vr, kk, bb, g, bonus, sh = _rwkv_pre(
        x, gmix, wp["w_rw"], sh0, wp["mu"], _row(lp["rw_w0"]), wp["wb"], _row(lp["rw_a0"]), wp["ab"], wp["gb"],
        _row(lp["rw_kk"]), _row(lp["rw_ka"]), _row(lp["rw_rk"]), wp["e"], wp["et"])
    n = dims["rw_dim"] // rw_heads
    o, s_t = _rwkv_scan(r, dec, k2, vr, kk, bb, _state_to_streams(wkv0))
    wkv = _state_from_streams(s_t, b, rw_heads)
    gates, = _proj(x.reshape(b * s, d), gmix, [wp["w_gates"]], 512, out_dtype=BF16, gate=True)
    gates = gates.reshape(b, s, 2 * d)
    dim = dims["rw_dim"]
    merged = _mix(o, bonus, g, y_mla, gates, _row(lp["rw_ln_g"]), _row(lp["rw_ln_b"]), wp["e"], wp["et"],
                  wp["w_br_rwkv"], wp["w_br_mla"], n).reshape(b * s, d)
    h, qc = _outproj(x.reshape(b * s, d), merged, wp["w_out"], _row(lp["g_cross"]), wp["w_mq"])
    c3, dl, il = 3 * dim, dims["decay_lora"], dims["iclr_lora"]
    shift = jnp.concatenate([sh[..., :c3], sh[..., c3:c3 + dl], sh[..., c3 + LANE:c3 + LANE + il],
                             sh[..., c3 + 2 * LANE:c3 + 2 * LANE + dims["gate_lora"]]], axis=-1)
    return h.reshape(b, s, d), qc.reshape(b, s, d), c, kr, k, wkv, shift


def _moe(hn, route, counts, wp):
    t, d = hn.shape
    n_exp = counts.shape[0]
    e_idx = route[:, :TOP_K]
    m = t * TOP_K
    n_blk = -(-(m + n_exp * (MOE_ROWS - 1)) // MOE_ROWS)
    ns = n_blk * MOE_ROWS
    flat_e = e_idx.reshape(m).astype(jnp.int32)
    order = jnp.argsort(flat_e).astype(jnp.int32)
    padded = (counts + MOE_ROWS - 1) // MOE_ROWS * MOE_ROWS
    pad_end = jnp.cumsum(padded)
    pad_start = pad_end - padded
    start = jnp.cumsum(counts) - counts
    blk_start = jnp.arange(n_blk, dtype=jnp.int32) * MOE_ROWS
    blk_e = jnp.minimum(jnp.sum(pad_end[None, :] <= blk_start[:, None], axis=1), n_exp - 1).astype(jnp.int32)
    blk_on = (blk_start < pad_end[-1]).astype(jnp.int32)
    off = (blk_start - pad_start[blk_e])[:, None] + jnp.arange(MOE_ROWS, dtype=jnp.int32)[None, :]
    valid = ((off < counts[blk_e][:, None]) & (blk_on[:, None] != 0)).reshape(ns)
    asg = order[jnp.clip(start[blk_e][:, None] + off, 0, m - 1).reshape(ns)]
    slot_tok = jnp.where(valid, asg // TOP_K, jnp.arange(ns, dtype=jnp.int32) % t)
    y_slot = _moe_experts(blk_e, blk_on, counts, hn[slot_tok], wp["w_eg"], wp["w_eu"], wp["w_ed"])
    key = jnp.where(valid, asg, m + jnp.arange(ns, dtype=jnp.int32))
    slot_of = jnp.argsort(key)[:m].astype(jnp.int32)
    return y_slot, slot_of


def kernel(x_prompt, x_sample, mem_prompt, cache_kv_latent, cache_k_rope, cache_mem_k, cache_mem_v, state_wkv, state_shift, g_mix, w_in, g_q, w_uq, g_kv, w_uk, w_uv, rw_mu, rw_w0, rw_wb, rw_a0, rw_ab, rw_gb, rw_kk, rw_ka, rw_rk, rw_ln_g, rw_ln_b, w_br_rwkv, w_br_mla, w_out, g_cross, g_mem, w_mq, w_mk, w_mv, w_mo, g_moe, w_rg, b_rg, w_re, b_re, w_eg, w_eu, w_ed, g_final):
    depth = g_mix.shape[0]
    b, s, d = x_prompt.shape
    bs, ss, _ = x_sample.shape
    past = cache_kv_latent.shape[2]
    n_mem = mem_prompt.shape[1]
    mem_heads = w_mq.shape[2]
    rw_heads, rw_hd = rw_rk.shape[1:]
    dims = dict(d_model=d, q_lora=g_q.shape[1], kv_lora=g_kv.shape[1], mla_heads=w_uq.shape[2],
                nope=w_uk.shape[3], rope=cache_k_rope.shape[3], rw_heads=rw_heads, rw_dim=rw_heads * rw_hd,
                decay_lora=rw_wb.shape[1], iclr_lora=rw_ab.shape[1], gate_lora=rw_gb.shape[1])
    assert dims["rope"] == ROPE and dims["nope"] == LANE and w_uv.shape[3] == LANE
    assert dims["decay_lora"] <= LANE and dims["iclr_lora"] <= LANE and rw_heads <= LANE
    n_groups, n_exp = w_rg.shape[2], w_re.shape[2]
    per_group = n_exp // n_groups
    stacked = dict(g_mix=g_mix, w_in=w_in, g_q=g_q, w_uq=w_uq, g_kv=g_kv, w_uk=w_uk, w_uv=w_uv, rw_mu=rw_mu,
                   rw_w0=rw_w0, rw_wb=rw_wb, rw_a0=rw_a0, rw_ab=rw_ab, rw_gb=rw_gb, rw_kk=rw_kk, rw_ka=rw_ka,
                   rw_rk=rw_rk, rw_ln_g=rw_ln_g, rw_ln_b=rw_ln_b, w_br_rwkv=w_br_rwkv, w_br_mla=w_br_mla,
                   w_out=w_out, g_cross=g_cross, g_mem=g_mem, w_mq=w_mq, w_mk=w_mk, w_mv=w_mv, w_mo=w_mo,
                   g_moe=g_moe, w_rg=w_rg, b_rg=b_rg, w_re=w_re, b_re=b_re, w_eg=w_eg, w_eu=w_eu, w_ed=w_ed)
    h_p, h_s = x_prompt, x_sample
    outs = [[] for _ in range(10)]
    for l in range(depth):
        lp = {name: val[l] for name, val in stacked.items()}
        wp = _prep_layer(lp, dims)

        h, qc, c, kr, _, wkv, shift = _mixer(
            h_p, lp, wp, dims, jnp.arange(s), jnp.zeros((b, 1, state_shift.shape[3]), F32),
            jnp.zeros((b, rw_heads, rw_hd, rw_hd), F32), lambda q, k, v, c_new: _mla_prompt_attn(q, k, v))
        mk, mk_b, mv, mv_b = _proj(mem_prompt.reshape(b * n_mem, d), _row(lp["g_mem"]), [wp["w_mk"], wp["w_mv"]], 256,
                                   heads=mem_heads, twin=True)
        gm, wr, br = _row(lp["g_moe"]), wp["w_r"], wp["b_r"]
        h2_p, hn_p, rt_p, cnt_p = _cross(qc, mk_b.reshape(b, n_mem, d), mv_b.reshape(b, n_mem, d), h, wp["w_mo"], gm, wr,
                                         br, mem_heads, n_groups, per_group)
        for lst, val in zip(outs[:6], (c, kr, mk.reshape(b, n_mem, mem_heads, -1),
                                       mv.reshape(b, n_mem, mem_heads, -1), wkv, shift)):
            lst.append(val)

        kr_past = jnp.pad(cache_k_rope[l], ((0, 0), (0, 0), (0, LANE - ROPE))).astype(BF16)

        def attend_sample(q, k, v, c_new, l=l, kr_past=kr_past, wp=wp):
            return _mla_sample_attn(q, cache_kv_latent[l], kr_past, c_new, k, wp["w_uk"], wp["w_uv"])

        h, qc, c, kr, _, wkv, shift = _mixer(h_s, lp, wp, dims, past + jnp.arange(ss), state_shift[l],
                                             state_wkv[l], attend_sample)
        h2_s, hn_s, rt_s, cnt_s = _cross(qc, cache_mem_k[l].reshape(bs, n_mem, d).astype(BF16),
                                         cache_mem_v[l].reshape(bs, n_mem, d).astype(BF16), h, wp["w_mo"], gm, wr, br,
                                         mem_heads, n_groups, per_group)
        for lst, val in zip(outs[6:], (c, kr, wkv, shift)):
            lst.append(val)

        tp, tsm = b * s, bs * ss
        hn = jnp.concatenate([hn_p.reshape(tp, d), hn_s.reshape(tsm, d)], axis=0)
        route = jnp.concatenate([rt_p.reshape(tp, LANE), rt_s.reshape(tsm, LANE)], axis=0)
        counts = (jnp.sum(cnt_p[:, :, 0, :n_exp], axis=(0, 1)) + jnp.sum(cnt_s[:, :, 0, :n_exp], axis=(0, 1)))
        y_slot, slot_of = _moe(hn, route, counts.astype(jnp.int32), wp)
        g_fin = _row(g_final) if l == depth - 1 else None
        h_p = _combine(h2_p.reshape(tp, d), y_slot, slot_of, route, 0, g_fin).reshape(b, s, d)
        h_s = _combine(h2_s.reshape(tsm, d), y_slot, slot_of, route, tp, g_fin).reshape(bs, ss, d)
    stacks = [jnp.stack(o) for o in outs]
    return (h_p, h_s, *stacks)
```

```python
import functools

import jax
import jax.numpy as jnp
from jax import lax
from jax.experimental import pallas as pl
from jax.experimental.pallas import tpu as pltpu

F32 = jnp.float32
BF16 = jnp.bfloat16

EPS = 1e-6
GN_EPS = 64e-5
ROPE_THETA = 10000.0
CHUNK = 64
TOP_K = 2
NEG = -1e30

LANE = 128
V7X_VMEM_LIMIT = 56 << 20
ROPE = 64
MOE_ROWS = 256
SCAN_TS = 16
SCAN_UNROLL = 16


def _cp(*sem):
    return pltpu.CompilerParams(dimension_semantics=sem, vmem_limit_bytes=V7X_VMEM_LIMIT)


def _tile(n, pref):
    if n <= pref:
        return n
    for t in range(pref, 7, -1):
        if n % t == 0 and t % 8 == 0:
            return t
    return n


def _resident(shape):
    nd = len(shape)
    return pl.BlockSpec(shape, lambda *_: (0,) * nd, pipeline_mode=pl.Buffered(1))


def _rms(x, g):
    return x * lax.rsqrt(jnp.mean(x * x, axis=-1, keepdims=True) + EPS) * g


def _sigmoid(x):
    return 1.0 / (1.0 + jnp.exp(-x))


def _dot(a, b):
    return jnp.dot(a, b, preferred_element_type=F32)


def _dot_t(a, b):
    return lax.dot_general(a, b, (((1,), (1,)), ((), ())), preferred_element_type=F32)


def _seg_sum(x, e, et):
    hi = x.astype(BF16)
    lo = (x - hi.astype(F32)).astype(BF16)
    s = _dot(hi, e) + _dot(lo, e)
    shi = s.astype(BF16)
    slo = (s - shi.astype(F32)).astype(BF16)
    return _dot(shi, et) + _dot(slo, et)


def _proj_kernel(x_ref, g_ref, *refs, n_w, gate, twin, cols):
    w_refs, o_refs = refs[:n_w], refs[n_w:]
    xn = _rms(x_ref[...], g_ref[...]).astype(BF16)
    per_w = 2 if twin else 1
    for i, w_ref in enumerate(w_refs):
        o_ref = o_refs[per_w * i]
        n = w_ref.shape[1]
        for c in range(0, n, cols):
            y = _dot(xn, w_ref[:, c:c + cols])
            if gate:
                y = _sigmoid(y)
            if len(o_ref.shape) == 3:
                hd = o_ref.shape[2]
                o_ref[:, c // hd:(c + cols) // hd, :] = y.reshape(y.shape[0], cols // hd, hd).astype(o_ref.dtype)
            else:
                o_ref[:, c:c + cols] = y.astype(o_ref.dtype)
            if twin:
                o_refs[per_w * i + 1][:, c:c + cols] = y.astype(BF16)


def _proj(x, g, ws, tm, out_dtype=F32, gate=False, heads=None, twin=False, cols=1024):
    m, k = x.shape
    tm = _tile(m, tm)
    out_specs, out_shape = [], []
    for w in ws:
        n = w.shape[1]
        if heads is None:
            out_specs.append(pl.BlockSpec((tm, n), lambda i: (i, 0)))
            out_shape.append(jax.ShapeDtypeStruct((m, n), out_dtype))
        else:
            out_specs.append(pl.BlockSpec((tm, heads, n // heads), lambda i: (i, 0, 0)))
            out_shape.append(jax.ShapeDtypeStruct((m, heads, n // heads), out_dtype))
        if twin:
            out_specs.append(pl.BlockSpec((tm, n), lambda i: (i, 0)))
            out_shape.append(jax.ShapeDtypeStruct((m, n), BF16))
    cols = min(cols, min(w.shape[1] for w in ws))
    return pl.pallas_call(
        functools.partial(_proj_kernel, n_w=len(ws), gate=gate, twin=twin, cols=cols),
        grid=(m // tm,),
        in_specs=[pl.BlockSpec((tm, k), lambda i: (i, 0)), _resident(g.shape)] + [_resident(w.shape) for w in ws],
        out_specs=out_specs,
        out_shape=out_shape,
        compiler_params=_cp("parallel"),
        name="norm_proj",
    )(x, g, *ws)


def _mla_proj_kernel(x_ref, gmix_ref, wm_ref, gq_ref, wq_ref, gkv_ref, wuk_ref, wuv_ref, cs_ref,
                     q_ref, k_ref, v_ref, c_ref, kr_ref, *, heads, q_lora, scale):
    u = _rms(x_ref[0], gmix_ref[...]).astype(BF16)
    p = _dot(u, wm_ref[...])
    cs = cs_ref[...]
    lane = lax.broadcasted_iota(jnp.int32, cs.shape, 1)

    def rope_tile(t2):
        t = t2 * cs
        return jnp.where(lane < ROPE, t + pltpu.roll(t, ROPE, axis=1), 0.0)

    krt = rope_tile(p[:, q_lora:q_lora + LANE])
    kr_ref[0] = krt[:, :ROPE]
    krb = krt.astype(BF16)
    c = _rms(p[:, q_lora + LANE:], gkv_ref[...])
    c_ref[0] = c
    cb = c.astype(BF16)
    kn = _dot(cb, wuk_ref[...])
    vt = _dot_t(wuv_ref[...], cb)
    qn = _rms(p[:, :q_lora], gq_ref[...]).astype(BF16)
    qf = _dot(qn, wq_ref[...])
    for h in range(heads):
        q_ref[0, h, :, :LANE] = (qf[:, h * 256:h * 256 + LANE] * scale).astype(BF16)
        q_ref[0, h, :, LANE:] = (rope_tile(qf[:, h * 256 + LANE:(h + 1) * 256]) * scale).astype(BF16)
        k_ref[0, h, :, :LANE] = kn[:, h * LANE:(h + 1) * LANE].astype(BF16)
        k_ref[0, h, :, LANE:] = krb
        v_ref[0, h] = vt[h * LANE:(h + 1) * LANE, :].astype(BF16)


def _mla_proj(x, gmix, wm, gq, wq, gkv, wuk, wuv, cs, heads, scale, tm=256):
    b, s, d = x.shape
    q_lora = gq.shape[1]
    kv_lora = gkv.shape[1]
    tm = _tile(s, tm)
    kern = functools.partial(_mla_proj_kernel, heads=heads, q_lora=q_lora, scale=scale)
    return pl.pallas_call(
        kern,
        grid=(b, s // tm),
        in_specs=[pl.BlockSpec((1, tm, d), lambda i, j: (i, j, 0)),
                  _resident(gmix.shape), _resident(wm.shape), _resident(gq.shape), _resident(wq.shape),
                  _resident(gkv.shape), _resident(wuk.shape), _resident(wuv.shape),
                  pl.BlockSpec((tm, LANE), lambda i, j: (j, 0))],
        out_specs=[pl.BlockSpec((1, heads, tm, 256), lambda i, j: (i, 0, j, 0)),
                   pl.BlockSpec((1, heads, tm, 256), lambda i, j: (i, 0, j, 0)),
                   pl.BlockSpec((1, heads, LANE, tm), lambda i, j: (i, 0, 0, j)),
                   pl.BlockSpec((1, tm, kv_lora), lambda i, j: (i, j, 0)),
                   pl.BlockSpec((1, tm, ROPE), lambda i, j: (i, j, 0))],
        out_shape=[jax.ShapeDtypeStruct((b, heads, s, 256), BF16),
                   jax.ShapeDtypeStruct((b, heads, s, 256), BF16),
                   jax.ShapeDtypeStruct((b, heads, LANE, s), BF16),
                   jax.ShapeDtypeStruct((b, s, kv_lora), F32),
                   jax.ShapeDtypeStruct((b, s, ROPE), F32)],
        compiler_params=_cp("parallel", "parallel"),
        name="mla_proj",
    )(x, gmix, wm, gq, wq, gkv, wuk, wuv, cs)


def _flash_kernel(q_ref, k_ref, vt_ref, o_ref, m_ref, l_ref, acc_ref, sc_ref, *, tq, hps):
    qi = pl.program_id(2)
    m_ref[...] = jnp.full(m_ref.shape, NEG, F32)
    l_ref[...] = jnp.zeros(l_ref.shape, F32)
    acc_ref[...] = jnp.zeros(acc_ref.shape, F32)

    def scores(j, slot):
        start = pl.multiple_of(j * tq, tq)
        for h in range(hps):
            sc_ref[slot, h] = _dot_t(k_ref[0, h, pl.ds(start, tq), :], q_ref[0, h])

    def consume(j, slot, diag):
        start = pl.multiple_of(j * tq, tq)
        for h in range(hps):
            s = sc_ref[slot, h]
            if diag:
                kc = lax.broadcasted_iota(jnp.int32, s.shape, 0) // CHUNK
                qc = lax.broadcasted_iota(jnp.int32, s.shape, 1) // CHUNK
                s = jnp.where(kc <= qc, s, NEG)
            m = m_ref[h]
            m_new = jnp.maximum(m, jnp.max(s, axis=0, keepdims=True))
            alpha = jnp.exp(m - m_new)
            p = jnp.exp(s - m_new)
            l_ref[h] = l_ref[h] * alpha + jnp.sum(p, axis=0, keepdims=True)
            acc_ref[h] = acc_ref[h] * alpha + _dot(vt_ref[0, h, :, pl.ds(start, tq)], p.astype(BF16))
            m_ref[h] = m_new

    scores(0, 0)

    def body(j, c):
        scores(2 * j + 1, 1)
        consume(2 * j, 0, False)
        scores(2 * j + 2, 0)
        consume(2 * j + 1, 1, False)
        return c

    lax.fori_loop(0, qi // 2, body, 0)

    @pl.when(qi % 2 == 1)
    def _():
        scores(qi, 1)
        consume(qi - 1, 0, False)
        consume(qi, 1, True)

    @pl.when(qi % 2 == 0)
    def _():
        consume(qi, 0, True)

    for h in range(hps):
        o_ref[0, :, h * LANE:(h + 1) * LANE] = (acc_ref[h] * (1.0 / l_ref[h])).T.astype(o_ref.dtype)


def _mla_prompt_attn(q, k, vt, tq=512, hps=8):
    b, heads, s, _ = q.shape
    tq = _tile(s, tq)
    hps = min(hps, heads)
    assert tq % CHUNK == 0 and heads % hps == 0
    return pl.pallas_call(
        functools.partial(_flash_kernel, tq=tq, hps=hps),
        grid=(b, heads // hps, s // tq),
        in_specs=[pl.BlockSpec((1, hps, tq, 256), lambda i, h, j: (i, h, j, 0)),
                  pl.BlockSpec((1, hps, s, 256), lambda i, h, j: (i, h, 0, 0)),
                  pl.BlockSpec((1, hps, LANE, s), lambda i, h, j: (i, h, 0, 0))],
        out_specs=pl.BlockSpec((1, tq, hps * LANE), lambda i, h, j: (i, j, h)),
        out_shape=jax.ShapeDtypeStruct((b, s, heads * LANE), BF16),
        scratch_shapes=[pltpu.VMEM((hps, 1, tq), F32), pltpu.VMEM((hps, 1, tq), F32),
                        pltpu.VMEM((hps, LANE, tq), F32), pltpu.VMEM((2, hps, tq, tq), F32)],
        compiler_params=_cp("parallel", "parallel", "arbitrary"),
        name="mla_prompt_attn",
    )(q, k, vt)


def _mla_sample_kernel(q_ref, cp_ref, krp_ref, cn_ref, krn_ref, wuk_ref, wuv_ref, o_ref,
                       qa_ref, qr_ref, m_ref, l_ref, acc_ref, *, heads, sq):
    kb = pl.program_id(1)

    @pl.when(kb == 0)
    def _():
        for h in range(heads):
            qh = q_ref[0, h]
            qa_ref[h * sq:(h + 1) * sq, :] = _dot_t(qh[:, :LANE], wuk_ref[:, h * LANE:(h + 1) * LANE]).astype(BF16)
            qr_ref[h * sq:(h + 1) * sq, :] = qh[:, LANE:]
        m_ref[...] = jnp.full(m_ref.shape, NEG, F32)
        l_ref[...] = jnp.zeros(l_ref.shape, F32)
        acc_ref[...] = jnp.zeros(acc_ref.shape, F32)

    def update(cb, krb):
        s = _dot_t(qa_ref[...], cb) + _dot_t(qr_ref[...], krb)
        m_new = jnp.maximum(m_ref[...], jnp.max(s, axis=-1, keepdims=True))
        alpha = jnp.exp(m_ref[...] - m_new)
        p = jnp.exp(s - m_new)
        l_ref[...] = l_ref[...] * alpha + jnp.sum(p, axis=-1, keepdims=True)
        acc_ref[...] = acc_ref[...] * alpha + _dot(p.astype(BF16), cb)
        m_ref[...] = m_new

    update(cp_ref[0].astype(BF16), krp_ref[0])

    @pl.when(kb == pl.num_programs(1) - 1)
    def _():
        update(cn_ref[0].astype(BF16), krn_ref[0, 0])
        o_lat = (acc_ref[...] / l_ref[...]).astype(BF16)
        for h in range(heads):
            o_ref[0, :, h * LANE:(h + 1) * LANE] = _dot(
                o_lat[h * sq:(h + 1) * sq], wuv_ref[:, h * LANE:(h + 1) * LANE]).astype(o_ref.dtype)


def _mla_sample_attn(q, c_past, kr_past, c_new, k_new, wuk, wuv, tk=1024):
    b, heads, sq, _ = q.shape
    past, kv_lora = c_past.shape[1:]
    tk = _tile(past, tk)
    kern = functools.partial(_mla_sample_kernel, heads=heads, sq=sq)
    return pl.pallas_call(
        kern,
        grid=(b, past // tk),
        in_specs=[pl.BlockSpec((1, heads, sq, 256), lambda i, j: (i, 0, 0, 0)),
                  pl.BlockSpec((1, tk, kv_lora), lambda i, j: (i, j, 0)),
                  pl.BlockSpec((1, tk, LANE), lambda i, j: (i, j, 0)),
                  pl.BlockSpec((1, sq, kv_lora), lambda i, j: (i, 0, 0)),
                  pl.BlockSpec((1, 1, sq, LANE), lambda i, j: (i, 0, 0, 1)),
                  _resident(wuk.shape), _resident(wuv.shape)],
        out_specs=pl.BlockSpec((1, sq, heads * LANE), lambda i, j: (i, 0, 0)),
        out_shape=jax.ShapeDtypeStruct((b, sq, heads * LANE), BF16),
        scratch_shapes=[pltpu.VMEM((heads * sq, kv_lora), BF16), pltpu.VMEM((heads * sq, LANE), BF16),
                        pltpu.VMEM((heads * sq, 1), F32), pltpu.VMEM((heads * sq, 1), F32),
                        pltpu.VMEM((heads * sq, kv_lora), F32)],
        compiler_params=_cp("parallel", "arbitrary"),
        name="mla_sample_attn",
    )(q, c_past, kr_past, c_new, k_new, wuk, wuv)


def _rwkv_pre_kernel(x_ref, gmix_ref, w_ref, sh0_ref, mu_ref, w0_ref, wb_ref, a0_ref, ab_ref, gb_ref,
                     kkp_ref, ka_ref, rk_ref, e_ref, et_ref,
                     r_ref, dec_ref, k_ref, v_ref, kk_ref, b_ref, g_ref, bonus_ref, sh_ref, last_ref, *, dim):
    si = pl.program_id(1)

    def put(ref, val):
        nb, ts, tiles, _ = ref.shape
        ref[...] = val.reshape(nb * ts, tiles, LANE).reshape(nb, ts, tiles, LANE)

    u = _rms(x_ref[0], gmix_ref[...]).astype(BF16)
    tm = u.shape[0]
    width = w_ref.shape[1]

    @pl.when(si == 0)
    def _():
        last_ref[...] = sh0_ref[0]

    def shifted(c0, c1):
        p = _dot(u, w_ref[:, c0:c1])
        row = lax.broadcasted_iota(jnp.int32, p.shape, 0)
        prev = jnp.where(row == 0, last_ref[:, c0:c1], pltpu.roll(p, 1, axis=0))
        last_ref[:, c0:c1] = p[tm - 1:tm, :]
        sh_ref[0, :, c0:c1] = p[tm - 1:tm, :]
        return p + mu_ref[:, c0:c1] * (prev - p)

    lora = shifted(3 * dim, width)
    k = shifted(dim, 2 * dim)
    dw = lora[:, :LANE]
    da = lora[:, LANE:2 * LANE]
    dg = lora[:, 2 * LANE:]
    z = -(w0_ref[...] + _dot(jnp.tanh(dw).astype(BF16), wb_ref[...]))
    a = _sigmoid(a0_ref[...] + _dot(da.astype(BF16), ab_ref[...]))
    g_ref[0] = _dot(_sigmoid(dg).astype(BF16), gb_ref[...])
    r = shifted(0, dim)
    softplus = jnp.maximum(z, 0.0) + jnp.log(1.0 + jnp.exp(-jnp.abs(z)))
    put(dec_ref, jnp.exp(-jnp.exp(-softplus - 0.5)))
    kk = k * kkp_ref[...]
    nrm = jnp.sqrt(_seg_sum(kk * kk, e_ref[...], et_ref[...]))
    v = shifted(2 * dim, 3 * dim)
    kk = kk / jnp.maximum(nrm, 1e-12)
    k2 = k * (1.0 + (a - 1.0) * ka_ref[...])
    put(k_ref, k2)
    put(kk_ref, kk)
    put(b_ref, kk * a)
    put(r_ref, r)
    put(v_ref, v)
    bonus_ref[0] = _seg_sum(r * k2 * rk_ref[...], e_ref[...], et_ref[...]) * v


def _rwkv_pre(x, gmix, w, sh0, mu, w0, wb, a0, ab, gb, kkp, ka, rk, e, et, tm=256):
    b, s, d = x.shape
    dim = w0.shape[1]
    width = w.shape[1]
    tm = _tile(s, tm)
    ts = min(SCAN_TS, s)
    assert tm % ts == 0 and dim % LANE == 0
    row = lambda i, j: (i, j, 0)
    act = pl.BlockSpec((1, tm, dim), row)
    act_shape = jax.ShapeDtypeStruct((b, s, dim), F32)
    tiles = dim // LANE
    assert tiles % 8 == 0
    seq = pl.BlockSpec((tm // ts, ts, tiles, LANE), lambda i, j: (j, 0, i, 0))
    seq_shape = jax.ShapeDtypeStruct((s // ts, ts, b * tiles, LANE), F32)
    consts = (gmix, w, mu, w0, wb, a0, ab, gb, kkp, ka, rk, e, et)
    return pl.pallas_call(
        functools.partial(_rwkv_pre_kernel, dim=dim),
        grid=(b, s // tm),
        in_specs=[pl.BlockSpec((1, tm, d), row), _resident(gmix.shape), _resident(w.shape),
                  pl.BlockSpec((1, 1, width), lambda i, j: (i, 0, 0))] + [_resident(c.shape) for c in consts[2:]],
        out_specs=[seq] * 6 + [act] * 2 + [pl.BlockSpec((1, 1, width), lambda i, j: (i, 0, 0))],
        out_shape=[seq_shape] * 6 + [act_shape] * 2 + [jax.ShapeDtypeStruct((b, 1, width), F32)],
        scratch_shapes=[pltpu.VMEM((1, width), F32)],
        compiler_params=_cp("parallel", "arbitrary"),
        name="rwkv_pre",
    )(x, gmix, w, sh0, *consts[2:])


def _rwkv_scan_kernel(r_ref, dec_ref, k_ref, v_ref, kk_ref, b_ref, s0_ref, o_ref, st_ref, xt_ref, ot_ref,
                      *, ts, n, unroll):
    tb = pl.program_id(1)
    hpl = LANE // n

    @pl.when(tb == 0)
    def _():
        st_ref[...] = s0_ref[...]

    srcs = (r_ref, dec_ref, k_ref, v_ref, kk_ref, b_ref)
    for i in range(ts):
        for a, ref in enumerate(srcs):
            xt_ref[a, i] = ref[i].T

    def step(i, c):
        for h2 in range(hpl):
            base = h2 * n
            vv = xt_ref[3, i, base:base + n, :]

            def p1(kx, sa):
                return sa + st_ref[h2, kx] * xt_ref[4, i, pl.ds(base + kx, 1), :]

            sa = lax.fori_loop(0, n, p1, jnp.zeros_like(vv), unroll=unroll)

            def p2(kx, o):
                s_new = (st_ref[h2, kx] * xt_ref[1, i, pl.ds(base + kx, 1), :]
                         - sa * xt_ref[5, i, pl.ds(base + kx, 1), :] + vv * xt_ref[2, i, pl.ds(base + kx, 1), :])
                st_ref[h2, kx] = s_new
                return o + s_new * xt_ref[0, i, pl.ds(base + kx, 1), :]

            ot_ref[i, base:base + n, :] = lax.fori_loop(0, n, p2, jnp.zeros_like(vv), unroll=unroll)
        return c

    lax.fori_loop(0, ts, step, 0)
    for i in range(ts):
        o_ref[i] = ot_ref[i].T


def _rwkv_scan(r, dec, k, v, kk, bb, s0):
    nt, ts, streams, _ = r.shape
    groups, hpl, n = s0.shape[:3]
    assert streams == groups * LANE
    seq = pl.BlockSpec((None, ts, LANE, LANE), lambda g, t: (t, 0, g, 0))
    state = pl.BlockSpec((None, hpl, n, n, LANE), lambda g, t: (g, 0, 0, 0, 0))
    return pl.pallas_call(
        functools.partial(_rwkv_scan_kernel, ts=ts, n=n, unroll=SCAN_UNROLL),
        grid=(groups, nt),
        in_specs=[seq] * 6 + [state],
        out_specs=[seq, state],
        out_shape=[jax.ShapeDtypeStruct(r.shape, F32), jax.ShapeDtypeStruct(s0.shape, F32)],
        scratch_shapes=[pltpu.VMEM((6, ts, LANE, LANE), F32), pltpu.VMEM((ts, LANE, LANE), F32)],
        compiler_params=_cp("parallel", "arbitrary"),
        name="rwkv_scan",
    )(r, dec, k, v, kk, bb, s0)


def _state_to_streams(wkv):
    b, h, n = wkv.shape[:3]
    hpl = LANE // n
    tiles = h // hpl
    bg = LANE // tiles
    g = b // bg
    t = wkv.reshape(g, bg, tiles, hpl, n, n)
    return t.transpose(0, 3, 5, 4, 1, 2).reshape(g, hpl, n, n, LANE)


def _state_from_streams(st, b, h):
    g, hpl, n = st.shape[:3]
    tiles = h // hpl
    bg = LANE // tiles
    t = st.reshape(g, hpl, n, n, bg, tiles)
    return t.transpose(0, 4, 5, 1, 3, 2).reshape(b, h, n, n)


def _mix_kernel(o_ref, bonus_ref, g_ref, ymla_ref, grw_ref, gmla_ref, lng_ref, lnb_ref, e_ref, et_ref,
                wr_ref, wm_ref, out_ref, *, inv_n):
    nb, ts, tiles, _ = o_ref.shape
    o = o_ref[...].reshape(nb * ts, tiles, LANE).reshape(nb * ts, tiles * LANE)
    e, et = e_ref[...], et_ref[...]
    half = wm_ref.shape[1] // 2
    mean = _seg_sum(o, e, et) * inv_n
    m_lo = _dot(ymla_ref[0], wm_ref[:, :half])
    d = o - mean
    var = _seg_sum(d * d, e, et) * inv_n
    m_hi = _dot(ymla_ref[0], wm_ref[:, half:])
    on = d * lax.rsqrt(var + GN_EPS) * lng_ref[...] + lnb_ref[...]
    y = ((on + bonus_ref[0]) * g_ref[0]).astype(BF16)
    yr = _dot(y, wr_ref[...])
    out_ref[0, :, :half] = (grw_ref[0, :, :half] * yr[:, :half] + gmla_ref[0, :, :half] * m_lo).astype(out_ref.dtype)
    out_ref[0, :, half:] = (grw_ref[0, :, half:] * yr[:, half:] + gmla_ref[0, :, half:] * m_hi).astype(out_ref.dtype)


def _mix(o, bonus, g, ymla, gates, lng, lnb, e, et, wr, wm, head_dim, tm=256):
    nt, ts = o.shape[:2]
    b, s, dim = bonus.shape
    tiles = dim // LANE
    d = wr.shape[1]
    md = ymla.shape[2]
    tm = _tile(s, tm)
    assert tm % ts == 0
    row = lambda i, j: (i, j, 0)
    return pl.pallas_call(
        functools.partial(_mix_kernel, inv_n=1.0 / head_dim),
        grid=(b, s // tm),
        in_specs=[pl.BlockSpec((tm // ts, ts, tiles, LANE), lambda i, j: (j, 0, i, 0)),
                  pl.BlockSpec((1, tm, dim), row), pl.BlockSpec((1, tm, dim), row), pl.BlockSpec((1, tm, md), row),
                  pl.BlockSpec((1, tm, d), lambda i, j: (i, j, 0)), pl.BlockSpec((1, tm, d), lambda i, j: (i, j, 1))]
                 + [_resident(c.shape) for c in (lng, lnb, e, et, wr, wm)],
        out_specs=pl.BlockSpec((1, tm, d), row),
        out_shape=jax.ShapeDtypeStruct((b, s, d), BF16),
        compiler_params=_cp("parallel", "parallel"),
        name="rwkv_post_mix",
    )(o, bonus, g, ymla, gates, gates, lng, lnb, e, et, wr, wm)


def _outproj_kernel(x_ref, m_ref, wo_ref, gc_ref, wq_ref, h_ref, q_ref):
    h = x_ref[...] + _dot(m_ref[...], wo_ref[...])
    h_ref[...] = h
    q_ref[...] = _dot(_rms(h, gc_ref[...]).astype(BF16), wq_ref[...]).astype(q_ref.dtype)


def _outproj(x, merged, wo, gc, wq, tm=256):
    t, d = x.shape
    tm = _tile(t, tm)
    row = lambda i: (i, 0)
    return pl.pallas_call(
        _outproj_kernel,
        grid=(t // tm,),
        in_specs=[pl.BlockSpec((tm, d), row), pl.BlockSpec((tm, d), row),
                  _resident(wo.shape), _resident(gc.shape), _resident(wq.shape)],
        out_specs=[pl.BlockSpec((tm, d), row), pl.BlockSpec((tm, d), row)],
        out_shape=[jax.ShapeDtypeStruct((t, d), F32), jax.ShapeDtypeStruct((t, d), BF16)],
        compiler_params=_cp("parallel"),
        name="outproj_crossq",
    )(x, merged, wo, gc, wq)


def _route_tile(lg, n_groups, per_group):
    n_exp = n_groups * per_group
    lane = lax.broadcasted_iota(jnp.int32, lg.shape, 1)
    first = lambda mask: jnp.min(jnp.where(mask, lane, 2 * LANE), axis=-1, keepdims=True)
    top = lambda mask: jnp.max(jnp.where(mask, lg, NEG), axis=-1, keepdims=True)
    is_g = lane < n_groups
    gmax = top(is_g)
    g_sel = first(is_g & (lg == gmax))
    p_grp = 1.0 / jnp.sum(jnp.where(is_g, jnp.exp(lg - gmax), 0.0), axis=-1, keepdims=True)
    eid = lane - n_groups
    in_grp = (eid >= g_sel * per_group) & (eid < (g_sel + 1) * per_group)
    v1 = top(in_grp)
    e1 = first(in_grp & (lg == v1))
    rest = in_grp & (lane != e1)
    v2 = top(rest)
    e2 = first(rest & (lg == v2))
    t = jnp.exp(v2 - v1)
    w1 = p_grp / (1.0 + t)
    w2 = w1 * t
    route = jnp.where(lane == 0, (e1 - n_groups).astype(F32),
                      jnp.where(lane == 1, (e2 - n_groups).astype(F32),
                                jnp.where(lane == 2, w1, jnp.where(lane == 3, w2, 0.0))))
    hits = jnp.where((lane == e1 - n_groups) | (lane == e2 - n_groups), 1.0, 0.0)
    assert n_exp <= LANE
    return route, jnp.sum(hits, axis=0, keepdims=True)


def _cross_kernel(q_ref, mk_ref, mv_ref, h_ref, wo_ref, gm_ref, wr_ref, br_ref, h2_ref, hn_ref, rt_ref, cnt_ref,
                  *, heads, scale, n_groups, per_group):
    q = q_ref[0]
    mk = mk_ref[0]
    mv = mv_ref[0]
    hd = q.shape[1] // heads
    outs = []
    for h in range(heads):
        sl = slice(h * hd, (h + 1) * hd)
        s = _dot_t(q[:, sl], mk[:, sl]) * scale
        p = jnp.exp(s - jnp.max(s, axis=-1, keepdims=True))
        p = p / jnp.sum(p, axis=-1, keepdims=True)
        outs.append(_dot(p.astype(BF16), mv[:, sl]).astype(BF16))
    o = jnp.concatenate(outs, axis=1)
    h2 = h_ref[0] + _dot(o, wo_ref[...])
    h2_ref[0] = h2
    hn = _rms(h2, gm_ref[...]).astype(BF16)
    hn_ref[0] = hn
    route, hist = _route_tile(_dot(hn, wr_ref[...]) + br_ref[...], n_groups, per_group)
    rt_ref[0] = route
    cnt_ref[0, 0] = jnp.broadcast_to(hist, cnt_ref.shape[2:])


def _cross(q, mk, mv, h, wo, gm, wr, br, heads, n_groups, per_group, tm=256):
    b, s, d = h.shape
    n_mem = mk.shape[1]
    tm = _tile(s, tm)
    row = lambda i, j: (i, j, 0)
    mem = pl.BlockSpec((1, n_mem, d), lambda i, j: (i, 0, 0))
    kern = functools.partial(_cross_kernel, heads=heads, scale=float(d // heads) ** -0.5,
                             n_groups=n_groups, per_group=per_group)
    return pl.pallas_call(
        kern,
        grid=(b, s // tm),
        in_specs=[pl.BlockSpec((1, tm, d), row), mem, mem, pl.BlockSpec((1, tm, d), row),
                  _resident(wo.shape), _resident(gm.shape), _resident(wr.shape), _resident(br.shape)],
        out_specs=[pl.BlockSpec((1, tm, d), row), pl.BlockSpec((1, tm, d), row), pl.BlockSpec((1, tm, LANE), row),
                   pl.BlockSpec((1, 1, 8, LANE), lambda i, j: (i, j, 0, 0))],
        out_shape=[jax.ShapeDtypeStruct((b, s, d), F32), jax.ShapeDtypeStruct((b, s, d), BF16),
                   jax.ShapeDtypeStruct((b, s, LANE), F32), jax.ShapeDtypeStruct((b, s // tm, 8, LANE), F32)],
        compiler_params=_cp("parallel", "parallel"),
        name="cross_attn_router",
    )(q, mk, mv, h, wo, gm, wr, br)


def _moe_kernel(be_ref, on_ref, first_ref, nxt_ref, par_ref, x_ref, wg_hbm, wu_hbm, wd_hbm, y_ref,
                wgf_ref, wuf_ref, wdf_ref, wgb_ref, wub_ref, wdb_ref, sem_ref):
    i = pl.program_id(0)

    def fetch(e, slot):
        return (pltpu.make_async_copy(wg_hbm.at[e], wgf_ref.at[slot], sem_ref.at[slot, 0]),
                pltpu.make_async_copy(wu_hbm.at[e], wuf_ref.at[slot], sem_ref.at[slot, 1]),
                pltpu.make_async_copy(wd_hbm.at[e], wdf_ref.at[slot], sem_ref.at[slot, 2]))

    @pl.when(first_ref[i] != 0)
    def _():
        slot = par_ref[i]

        @pl.when(i == 0)
        def _():
            for c in fetch(be_ref[0], 0):
                c.start()

        for c in fetch(be_ref[i], slot):
            c.wait()
        wgb_ref[...] = wgf_ref[slot].astype(BF16)
        wub_ref[...] = wuf_ref[slot].astype(BF16)
        wdb_ref[...] = wdf_ref[slot].astype(BF16)

        @pl.when(nxt_ref[i] >= 0)
        def _():
            for c in fetch(nxt_ref[i], 1 - slot):
                c.start()

    @pl.when(on_ref[i] != 0)
    def _():
        xb = x_ref[...]
        g = _dot(xb, wgb_ref[...])
        u = _dot(xb, wub_ref[...])
        hb = (g * _sigmoid(g) * u).astype(BF16)
        y_ref[...] = _dot(hb, wdb_ref[...])

    @pl.when(on_ref[i] == 0)
    def _():
        y_ref[...] = jnp.zeros(y_ref.shape, y_ref.dtype)


def _moe_experts(blk_e, blk_on, counts, x_slot, wg, wu, wd):
    ns, d = x_slot.shape
    n_exp, _, f = wg.shape
    n_blk = ns // MOE_ROWS
    prev_e = jnp.concatenate([jnp.full((1,), -1, jnp.int32), blk_e[:-1]])
    first = ((blk_on != 0) & (blk_e != prev_e)).astype(jnp.int32)
    par = ((jnp.cumsum(first) - 1) % 2).astype(jnp.int32)
    ids = jnp.where(counts > 0, jnp.arange(n_exp, dtype=jnp.int32), n_exp)
    nxt_tab = jnp.concatenate([lax.cummin(ids[::-1])[::-1][1:], jnp.full((1,), n_exp, jnp.int32)])
    nxt = nxt_tab[blk_e]
    nxt = jnp.where(nxt < n_exp, nxt, -1).astype(jnp.int32)
    hbm = pl.BlockSpec(memory_space=pl.ANY)
    grid_spec = pltpu.PrefetchScalarGridSpec(
        num_scalar_prefetch=5,
        grid=(n_blk,),
        in_specs=[pl.BlockSpec((MOE_ROWS, d), lambda i, *_: (i, 0)), hbm, hbm, hbm],
        out_specs=pl.BlockSpec((MOE_ROWS, d), lambda i, *_: (i, 0)),
        scratch_shapes=[pltpu.VMEM((2, d, f), F32), pltpu.VMEM((2, d, f), F32), pltpu.VMEM((2, f, d), F32),
                        pltpu.VMEM((d, f), BF16), pltpu.VMEM((d, f), BF16), pltpu.VMEM((f, d), BF16),
                        pltpu.SemaphoreType.DMA((2, 3))],
    )
    return pl.pallas_call(
        _moe_kernel,
        grid_spec=grid_spec,
        out_shape=jax.ShapeDtypeStruct((ns, d), F32),
        compiler_params=_cp("arbitrary"),
        name="moe_experts",
    )(blk_e, blk_on, first, nxt, par, x_slot, wg, wu, wd)


def _combine_kernel(idx_ref, nidx_ref, h_ref, rt_ref, g_ref, y_hbm, o_ref, buf_ref, sem_ref, *, tm, norm):
    i = pl.program_id(0)
    n = pl.num_programs(0)

    def issue(ids_ref, slot):
        def one(r, c):
            for ch in range(TOP_K):
                src = ids_ref[0, TOP_K * r + ch]
                pltpu.make_async_copy(y_hbm.at[pl.ds(src, 1)], buf_ref.at[slot, ch, pl.ds(r, 1)],
                                      sem_ref.at[slot, ch]).start()
            return c

        lax.fori_loop(0, tm, one, 0, unroll=8)

    cur = i % 2

    @pl.when(i == 0)
    def _():
        issue(idx_ref, 0)

    @pl.when(i + 1 < n)
    def _():
        issue(nidx_ref, 1 - cur)

    acc = h_ref[...]
    for ch in range(TOP_K):
        pltpu.make_async_copy(y_hbm.at[pl.ds(0, tm)], buf_ref.at[cur, ch], sem_ref.at[cur, ch]).wait()
        acc = acc + buf_ref[cur, ch] * rt_ref[:, TOP_K + ch:TOP_K + ch + 1]
    o_ref[...] = _rms(acc, g_ref[...]) if norm else acc


def _combine(h, y_slot, slot_of, route, row_off, g, tm=512):
    t, d = h.shape
    tm = _tile(t, tm)
    assert row_off % tm == 0
    off = row_off // tm
    nb = t // tm
    idx = slot_of[row_off * TOP_K:(row_off + t) * TOP_K].reshape(nb, 1, TOP_K * tm)
    norm = g is not None
    g = g if norm else jnp.ones((1, d), F32)
    ids = lambda f: pl.BlockSpec((None, 1, TOP_K * tm), f, memory_space=pltpu.SMEM)
    return pl.pallas_call(
        functools.partial(_combine_kernel, tm=tm, norm=norm),
        grid=(nb,),
        in_specs=[ids(lambda i: (i, 0, 0)), ids(lambda i: (jnp.minimum(i + 1, nb - 1), 0, 0)),
                  pl.BlockSpec((tm, d), lambda i: (i, 0)),
                  pl.BlockSpec((tm, LANE), lambda i: (i + off, 0)),
                  pl.BlockSpec((1, d), lambda i: (0, 0)),
                  pl.BlockSpec(memory_space=pl.ANY)],
        out_specs=pl.BlockSpec((tm, d), lambda i: (i, 0)),
        out_shape=jax.ShapeDtypeStruct((t, d), F32),
        scratch_shapes=[pltpu.VMEM((2, TOP_K, tm, d), F32), pltpu.SemaphoreType.DMA((2, TOP_K))],
        compiler_params=_cp("arbitrary"),
        name="moe_combine_norm",
    )(idx, idx, h, route, g, y_slot)


def _pad_cols(w, n):
    return jnp.pad(w, ((0, 0),) * (w.ndim - 1) + ((0, n - w.shape[-1]),))


def _pad_rows(w, n):
    return jnp.pad(w, ((0, n - w.shape[0]), (0, 0)))


def _rot_half_cols(w):
    half = w.shape[-1] // 2
    return jnp.concatenate([-w[..., half:], w[..., :half]], axis=-1)


def _rope_table(pos):
    half = ROPE // 2
    inv = ROPE_THETA ** (-jnp.arange(half, dtype=F32) / half)
    ang = pos.astype(F32)[:, None] * inv[None, :]
    cos, sin = jnp.cos(ang), jnp.sin(ang)
    return jnp.concatenate([cos, cos, sin, sin], axis=1)


def _prep_layer(lp, dims):
    ql, kl, rope, dim, dl, il, gl, d = (dims[k] for k in ("q_lora", "kv_lora", "rope", "rw_dim", "decay_lora",
                                                          "iclr_lora", "gate_lora", "d_model"))
    w_in = lp["w_in"]
    i0, i1, i2 = ql, ql + kl, ql + kl + rope
    i3 = i2 + 3 * dim + dl + il + gl
    w_kr = w_in[:, i1:i2]
    out = {}
    out["w_mla"] = jnp.concatenate(
        [w_in[:, :i0], w_kr, _rot_half_cols(w_kr), w_in[:, i0:i1]], axis=1).astype(BF16)
    rw = w_in[:, i2:i3]
    c3 = 3 * dim
    glp = -(-gl // LANE) * LANE

    def regroup(t):
        return jnp.concatenate([t[..., :c3], _pad_cols(t[..., c3:c3 + dl], LANE),
                                _pad_cols(t[..., c3 + dl:c3 + dl + il], LANE),
                                _pad_cols(t[..., c3 + dl + il:], glp)], axis=-1)

    out["regroup"] = regroup
    out["w_rw"] = regroup(rw).astype(BF16)
    out["mu"] = regroup(lp["rw_mu"][None, :])
    out["w_gates"] = w_in[:, i3:].astype(BF16)
    heads = lp["w_uq"].shape[1]
    nope = lp["w_uq"].shape[2] - rope
    wq = lp["w_uq"]
    out["w_q"] = jnp.concatenate([wq[..., :nope], wq[..., nope:], _rot_half_cols(wq[..., nope:])],
                                 axis=-1).reshape(ql, heads * 256).astype(BF16)
    out["w_uk"] = lp["w_uk"].reshape(kl, -1).astype(BF16)
    out["w_uv"] = lp["w_uv"].reshape(kl, -1).astype(BF16)
    out["w_uv_t"] = out["w_uv"].T
    out["wb"] = _pad_rows(lp["rw_wb"], LANE).astype(BF16)
    out["ab"] = _pad_rows(lp["rw_ab"], LANE).astype(BF16)
    out["gb"] = _pad_rows(lp["rw_gb"], glp).astype(BF16)
    n_heads = dims["rw_heads"]
    hd = dim // n_heads
    e = (jnp.arange(dim)[:, None] // hd == jnp.arange(LANE)[None, :]).astype(BF16)
    out["e"], out["et"] = e, e.T
    out["w_br_rwkv"] = lp["w_br_rwkv"].astype(BF16)
    out["w_br_mla"] = lp["w_br_mla"].astype(BF16)
    out["w_out"] = lp["w_out"].astype(BF16)
    out["w_mq"] = lp["w_mq"].reshape(d, d).astype(BF16)
    out["w_mk"] = lp["w_mk"].reshape(d, d).astype(BF16)
    out["w_mv"] = lp["w_mv"].reshape(d, d).astype(BF16)
    out["w_mo"] = lp["w_mo"].reshape(d, d).astype(BF16)
    n_g, n_e = lp["w_rg"].shape[1], lp["w_re"].shape[1]
    out["w_r"] = _pad_cols(jnp.concatenate([lp["w_rg"], lp["w_re"]], axis=1), LANE).astype(BF16)
    out["b_r"] = _pad_cols(jnp.concatenate([lp["b_rg"], lp["b_re"]])[None, :], LANE)
    out["w_eg"], out["w_eu"], out["w_ed"] = lp["w_eg"], lp["w_eu"], lp["w_ed"]
    return out


def _row(v):
    return v.reshape(1, -1)


def _mixer(x, lp, wp, dims, pos, shift0, wkv0, attend):
    b, s, d = x.shape
    heads, rw_heads = dims["mla_heads"], dims["rw_heads"]
    gmix = _row(lp["g_mix"])
    cs = _rope_table(pos)
    scale = float(dims["nope"] + dims["rope"]) ** -0.5
    q, k, v, c, kr = _mla_proj(x, gmix, wp["w_mla"], _row(lp["g_q"]), wp["w_q"], _row(lp["g_kv"]),
                               wp["w_uk"], wp["w_uv_t"], cs, heads, scale)
    y_mla = attend(q, k, v, c)
    sh0 = wp["regroup"](shift0)
    r, dec, k2, vr, kk, bb, g, bonus, sh = _rwkv_pre(
        x, gmix, wp["w_rw"], sh0, wp["mu"], _row(lp["rw_w0"]), wp["wb"], _row(lp["rw_a0"]), wp["ab"], wp["gb"],
        _row(lp["rw_kk"]), _row(lp["rw_ka"]), _row(lp["rw_rk"]), wp["e"], wp["et"])
    n = dims["rw_dim"] // rw_heads
    o, s_t = _rwkv_scan(r, dec, k2, vr, kk, bb, _state_to_streams(wkv0))
    wkv = _state_from_streams(s_t, b, rw_heads)
    gates, = _proj(x.reshape(b * s, d), gmix, [wp["w_gates"]], 512, out_dtype=BF16, gate=True)
    gates = gates.reshape(b, s, 2 * d)
    dim = dims["rw_dim"]
    merged = _mix(o, bonus, g, y_mla, gates, _row(lp["rw_ln_g"]), _row(lp["rw_ln_b"]), wp["e"], wp["et"],
                  wp["w_br_rwkv"], wp["w_br_mla"], n).reshape(b * s, d)
    h, qc = _outproj(x.reshape(b * s, d), merged, wp["w_out"], _row(lp["g_cross"]), wp["w_mq"])
    c3, dl, il = 3 * dim, dims["decay_lora"], dims["iclr_lora"]
    shift = jnp.concatenate([sh[..., :c3], sh[..., c3:c3 + dl], sh[..., c3 + LANE:c3 + LANE + il],
                             sh[..., c3 + 2 * LANE:c3 + 2 * LANE + dims["gate_lora"]]], axis=-1)
    return h.reshape(b, s, d), qc.reshape(b, s, d), c, kr, k, wkv, shift


def _moe(hn, route, counts, wp):
    t, d = hn.shape
    n_exp = counts.shape[0]
    e_idx = route[:, :TOP_K]
    m = t * TOP_K
    n_blk = -(-(m + n_exp * (MOE_ROWS - 1)) // MOE_ROWS)
    ns = n_blk * MOE_ROWS
    flat_e = e_idx.reshape(m).astype(jnp.int32)
    order = jnp.argsort(flat_e).astype(jnp.int32)
    padded = (counts + MOE_ROWS - 1) // MOE_ROWS * MOE_ROWS
    pad_end = jnp.cumsum(padded)
    pad_start = pad_end - padded
    start = jnp.cumsum(counts) - counts
    blk_start = jnp.arange(n_blk, dtype=jnp.int32) * MOE_ROWS
    blk_e = jnp.minimum(jnp.sum(pad_end[None, :] <= blk_start[:, None], axis=1), n_exp - 1).astype(jnp.int32)
    blk_on = (blk_start < pad_end[-1]).astype(jnp.int32)
    off = (blk_start - pad_start[blk_e])[:, None] + jnp.arange(MOE_ROWS, dtype=jnp.int32)[None, :]
    valid = ((off < counts[blk_e][:, None]) & (blk_on[:, None] != 0)).reshape(ns)
    asg = order[jnp.clip(start[blk_e][:, None] + off, 0, m - 1).reshape(ns)]
    slot_tok = jnp.where(valid, asg // TOP_K, jnp.arange(ns, dtype=jnp.int32) % t)
    y_slot = _moe_experts(blk_e, blk_on, counts, hn[slot_tok], wp["w_eg"], wp["w_eu"], wp["w_ed"])
    key = jnp.where(valid, asg, m + jnp.arange(ns, dtype=jnp.int32))
    slot_of = jnp.argsort(key)[:m].astype(jnp.int32)
    return y_slot, slot_of


def kernel(x_prompt, x_sample, mem_prompt, cache_kv_latent, cache_k_rope, cache_mem_k, cache_mem_v, state_wkv, state_shift, g_mix, w_in, g_q, w_uq, g_kv, w_uk, w_uv, rw_mu, rw_w0, rw_wb, rw_a0, rw_ab, rw_gb, rw_kk, rw_ka, rw_rk, rw_ln_g, rw_ln_b, w_br_rwkv, w_br_mla, w_out, g_cross, g_mem, w_mq, w_mk, w_mv, w_mo, g_moe, w_rg, b_rg, w_re, b_re, w_eg, w_eu, w_ed, g_final):
    depth = g_mix.shape[0]
    b, s, d = x_prompt.shape
    bs, ss, _ = x_sample.shape
    past = cache_kv_latent.shape[2]
    n_mem = mem_prompt.shape[1]
    mem_heads = w_mq.shape[2]
    rw_heads, rw_hd = rw_rk.shape[1:]
    dims = dict(d_model=d, q_lora=g_q.shape[1], kv_lora=g_kv.shape[1], mla_heads=w_uq.shape[2],
                nope=w_uk.shape[3], rope=cache_k_rope.shape[3], rw_heads=rw_heads, rw_dim=rw_heads * rw_hd,
                decay_lora=rw_wb.shape[1], iclr_lora=rw_ab.shape[1], gate_lora=rw_gb.shape[1])
    assert dims["rope"] == ROPE and dims["nope"] == LANE and w_uv.shape[3] == LANE
    assert dims["decay_lora"] <= LANE and dims["iclr_lora"] <= LANE and rw_heads <= LANE
    n_groups, n_exp = w_rg.shape[2], w_re.shape[2]
    per_group = n_exp // n_groups
    stacked = dict(g_mix=g_mix, w_in=w_in, g_q=g_q, w_uq=w_uq, g_kv=g_kv, w_uk=w_uk, w_uv=w_uv, rw_mu=rw_mu,
                   rw_w0=rw_w0, rw_wb=rw_wb, rw_a0=rw_a0, rw_ab=rw_ab, rw_gb=rw_gb, rw_kk=rw_kk, rw_ka=rw_ka,
                   rw_rk=rw_rk, rw_ln_g=rw_ln_g, rw_ln_b=rw_ln_b, w_br_rwkv=w_br_rwkv, w_br_mla=w_br_mla,
                   w_out=w_out, g_cross=g_cross, g_mem=g_mem, w_mq=w_mq, w_mk=w_mk, w_mv=w_mv, w_mo=w_mo,
                   g_moe=g_moe, w_rg=w_rg, b_rg=b_rg, w_re=w_re, b_re=b_re, w_eg=w_eg, w_eu=w_eu, w_ed=w_ed)
    h_p, h_s = x_prompt, x_sample
    outs = [[] for _ in range(10)]
    for l in range(depth):
        lp = {name: val[l] for name, val in stacked.items()}
        wp = _prep_layer(lp, dims)

        h, qc, c, kr, _, wkv, shift = _mixer(
            h_p, lp, wp, dims, jnp.arange(s), jnp.zeros((b, 1, state_shift.shape[3]), F32),
            jnp.zeros((b, rw_heads, rw_hd, rw_hd), F32), lambda q, k, v, c_new: _mla_prompt_attn(q, k, v))
        mk, mk_b, mv, mv_b = _proj(mem_prompt.reshape(b * n_mem, d), _row(lp["g_mem"]), [wp["w_mk"], wp["w_mv"]], 256,
                                   heads=mem_heads, twin=True)
        gm, wr, br = _row(lp["g_moe"]), wp["w_r"], wp["b_r"]
        h2_p, hn_p, rt_p, cnt_p = _cross(qc, mk_b.reshape(b, n_mem, d), mv_b.reshape(b, n_mem, d), h, wp["w_mo"], gm, wr,
                                         br, mem_heads, n_groups, per_group)
        for lst, val in zip(outs[:6], (c, kr, mk.reshape(b, n_mem, mem_heads, -1),
                                       mv.reshape(b, n_mem, mem_heads, -1), wkv, shift)):
            lst.append(val)

        kr_past = jnp.pad(cache_k_rope[l], ((0, 0), (0, 0), (0, LANE - ROPE))).astype(BF16)

        def attend_sample(q, k, v, c_new, l=l, kr_past=kr_past, wp=wp):
            return _mla_sample_attn(q, cache_kv_latent[l], kr_past, c_new, k, wp["w_uk"], wp["w_uv"])

        h, qc, c, kr, _, wkv, shift = _mixer(h_s, lp, wp, dims, past + jnp.arange(ss), state_shift[l],
                                             state_wkv[l], attend_sample)
        h2_s, hn_s, rt_s, cnt_s = _cross(qc, cache_mem_k[l].reshape(bs, n_mem, d).astype(BF16),
                                         cache_mem_v[l].reshape(bs, n_mem, d).astype(BF16), h, wp["w_mo"], gm, wr, br,
                                         mem_heads, n_groups, per_group)
        for lst, val in zip(outs[6:], (c, kr, wkv, shift)):
            lst.append(val)

        tp, tsm = b * s, bs * ss
        hn = jnp.concatenate([hn_p.reshape(tp, d), hn_s.reshape(tsm, d)], axis=0)
        route = jnp.concatenate([rt_p.reshape(tp, LANE), rt_s.reshape(tsm, LANE)], axis=0)
        counts = (jnp.sum(cnt_p[:, :, 0, :n_exp], axis=(0, 1)) + jnp.sum(cnt_s[:, :, 0, :n_exp], axis=(0, 1)))
        y_slot, slot_of = _moe(hn, route, counts.astype(jnp.int32), wp)
        g_fin = _row(g_final) if l == depth - 1 else None
        h_p = _combine(h2_p.reshape(tp, d), y_slot, slot_of, route, 0, g_fin).reshape(b, s, d)
        h_s = _combine(h2_s.reshape(tsm, d), y_slot, slot_of, route, tp, g_fin).reshape(bs, ss, d)
    stacks = [jnp.stack(o) for o in outs]
    return (h_p, h_s, *stacks)
```

```python
import functools

import jax
import jax.numpy as jnp
from jax import lax
from jax.experimental import pallas as pl
from jax.experimental.pallas import tpu as pltpu

F32 = jnp.float32
BF16 = jnp.bfloat16

EPS = 1e-6
GN_EPS = 64e-5
ROPE_THETA = 10000.0
CHUNK = 64
TOP_K = 2
NEG = -1e30

LANE = 128
V7X_VMEM_LIMIT = 56 << 20
ROPE = 64
MOE_ROWS = 256
SCAN_TS = 16
SCAN_UNROLL = 16


def _cp(*sem):
    return pltpu.CompilerParams(dimension_semantics=sem, vmem_limit_bytes=V7X_VMEM_LIMIT)


def _tile(n, pref):
    if n <= pref:
        return n
    for t in range(pref, 7, -1):
        if n % t == 0 and t % 8 == 0:
            return t
    return n


def _resident(shape):
    nd = len(shape)
    return pl.BlockSpec(shape, lambda *_: (0,) * nd, pipeline_mode=pl.Buffered(1))


def _rms(x, g):
    return x * lax.rsqrt(jnp.mean(x * x, axis=-1, keepdims=True) + EPS) * g


def _sigmoid(x):
    return 1.0 / (1.0 + jnp.exp(-x))


def _dot(a, b):
    return jnp.dot(a, b, preferred_element_type=F32)


def _dot_t(a, b):
    return lax.dot_general(a, b, (((1,), (1,)), ((), ())), preferred_element_type=F32)


def _seg_sum(x, e, et):
    hi = x.astype(BF16)
    lo = (x - hi.astype(F32)).astype(BF16)
    s = _dot(hi, e) + _dot(lo, e)
    shi = s.astype(BF16)
    slo = (s - shi.astype(F32)).astype(BF16)
    return _dot(shi, et) + _dot(slo, et)


def _proj_kernel(x_ref, g_ref, *refs, n_w, gate, twin, cols):
    w_refs, o_refs = refs[:n_w], refs[n_w:]
    xn = _rms(x_ref[...], g_ref[...]).astype(BF16)
    per_w = 2 if twin else 1
    for i, w_ref in enumerate(w_refs):
        o_ref = o_refs[per_w * i]
        n = w_ref.shape[1]
        for c in range(0, n, cols):
            y = _dot(xn, w_ref[:, c:c + cols])
            if gate:
                y = _sigmoid(y)
            if len(o_ref.shape) == 3:
                hd = o_ref.shape[2]
                o_ref[:, c // hd:(c + cols) // hd, :] = y.reshape(y.shape[0], cols // hd, hd).astype(o_ref.dtype)
            else:
                o_ref[:, c:c + cols] = y.astype(o_ref.dtype)
            if twin:
                o_refs[per_w * i + 1][:, c:c + cols] = y.astype(BF16)


def _proj(x, g, ws, tm, out_dtype=F32, gate=False, heads=None, twin=False, cols=1024):
    m, k = x.shape
    tm = _tile(m, tm)
    out_specs, out_shape = [], []
    for w in ws:
        n = w.shape[1]
        if heads is None:
            out_specs.append(pl.BlockSpec((tm, n), lambda i: (i, 0)))
            out_shape.append(jax.ShapeDtypeStruct((m, n), out_dtype))
        else:
            out_specs.append(pl.BlockSpec((tm, heads, n // heads), lambda i: (i, 0, 0)))
            out_shape.append(jax.ShapeDtypeStruct((m, heads, n // heads), out_dtype))
        if twin:
            out_specs.append(pl.BlockSpec((tm, n), lambda i: (i, 0)))
            out_shape.append(jax.ShapeDtypeStruct((m, n), BF16))
    cols = min(cols, min(w.shape[1] for w in ws))
    return pl.pallas_call(
        functools.partial(_proj_kernel, n_w=len(ws), gate=gate, twin=twin, cols=cols),
        grid=(m // tm,),
        in_specs=[pl.BlockSpec((tm, k), lambda i: (i, 0)), _resident(g.shape)] + [_resident(w.shape) for w in ws],
        out_specs=out_specs,
        out_shape=out_shape,
        compiler_params=_cp("parallel"),
        name="norm_proj",
    )(x, g, *ws)


def _mla_proj_kernel(x_ref, gmix_ref, wm_ref, gq_ref, wq_ref, gkv_ref, wuk_ref, wuv_ref, cs_ref,
                     q_ref, k_ref, v_ref, c_ref, kr_ref, *, heads, q_lora, scale):
    u = _rms(x_ref[0], gmix_ref[...]).astype(BF16)
    p = _dot(u, wm_ref[...])
    cs = cs_ref[...]
    lane = lax.broadcasted_iota(jnp.int32, cs.shape, 1)

    def rope_tile(t2):
        t = t2 * cs
        return jnp.where(lane < ROPE, t + pltpu.roll(t, ROPE, axis=1), 0.0)

    krt = rope_tile(p[:, q_lora:q_lora + LANE])
    kr_ref[0] = krt[:, :ROPE]
    krb = krt.astype(BF16)
    c = _rms(p[:, q_lora + LANE:], gkv_ref[...])
    c_ref[0] = c
    cb = c.astype(BF16)
    kn = _dot(cb, wuk_ref[...])
    vt = _dot_t(wuv_ref[...], cb)
    qn = _rms(p[:, :q_lora], gq_ref[...]).astype(BF16)
    qf = _dot(qn, wq_ref[...])
    for h in range(heads):
        q_ref[0, h, :, :LANE] = (qf[:, h * 256:h * 256 + LANE] * scale).astype(BF16)
        q_ref[0, h, :, LANE:] = (rope_tile(qf[:, h * 256 + LANE:(h + 1) * 256]) * scale).astype(BF16)
        k_ref[0, h, :, :LANE] = kn[:, h * LANE:(h + 1) * LANE].astype(BF16)
        k_ref[0, h, :, LANE:] = krb
        v_ref[0, h] = vt[h * LANE:(h + 1) * LANE, :].astype(BF16)


def _mla_proj(x, gmix, wm, gq, wq, gkv, wuk, wuv, cs, heads, scale, tm=256):
    b, s, d = x.shape
    q_lora = gq.shape[1]
    kv_lora = gkv.shape[1]
    tm = _tile(s, tm)
    kern = functools.partial(_mla_proj_kernel, heads=heads, q_lora=q_lora, scale=scale)
    return pl.pallas_call(
        kern,
        grid=(b, s // tm),
        in_specs=[pl.BlockSpec((1, tm, d), lambda i, j: (i, j, 0)),
                  _resident(gmix.shape), _resident(wm.shape), _resident(gq.shape), _resident(wq.shape),
                  _resident(gkv.shape), _resident(wuk.shape), _resident(wuv.shape),
                  pl.BlockSpec((tm, LANE), lambda i, j: (j, 0))],
        out_specs=[pl.BlockSpec((1, heads, tm, 256), lambda i, j: (i, 0, j, 0)),
                   pl.BlockSpec((1, heads, tm, 256), lambda i, j: (i, 0, j, 0)),
                   pl.BlockSpec((1, heads, LANE, tm), lambda i, j: (i, 0, 0, j)),
                   pl.BlockSpec((1, tm, kv_lora), lambda i, j: (i, j, 0)),
                   pl.BlockSpec((1, tm, ROPE), lambda i, j: (i, j, 0))],
        out_shape=[jax.ShapeDtypeStruct((b, heads, s, 256), BF16),
                   jax.ShapeDtypeStruct((b, heads, s, 256), BF16),
                   jax.ShapeDtypeStruct((b, heads, LANE, s), BF16),
                   jax.ShapeDtypeStruct((b, s, kv_lora), F32),
                   jax.ShapeDtypeStruct((b, s, ROPE), F32)],
        compiler_params=_cp("parallel", "parallel"),
        name="mla_proj",
    )(x, gmix, wm, gq, wq, gkv, wuk, wuv, cs)


def _flash_kernel(q_ref, k_ref, vt_ref, o_ref, m_ref, l_ref, acc_ref, sc_ref, *, tq, hps):
    qi = pl.program_id(2)
    m_ref[...] = jnp.full(m_ref.shape, NEG, F32)
    l_ref[...] = jnp.zeros(l_ref.shape, F32)
    acc_ref[...] = jnp.zeros(acc_ref.shape, F32)

    def scores(j, slot):
        start = pl.multiple_of(j * tq, tq)
        for h in range(hps):
            sc_ref[slot, h] = _dot_t(k_ref[0, h, pl.ds(start, tq), :], q_ref[0, h])

    def consume(j, slot, diag):
        start = pl.multiple_of(j * tq, tq)
        for h in range(hps):
            s = sc_ref[slot, h]
            if diag:
                kc = lax.broadcasted_iota(jnp.int32, s.shape, 0) // CHUNK
                qc = lax.broadcasted_iota(jnp.int32, s.shape, 1) // CHUNK
                s = jnp.where(kc <= qc, s, NEG)
            m = m_ref[h]
            m_new = jnp.maximum(m, jnp.max(s, axis=0, keepdims=True))
            alpha = jnp.exp(m - m_new)
            p = jnp.exp(s - m_new)
            l_ref[h] = l_ref[h] * alpha + jnp.sum(p, axis=0, keepdims=True)
            acc_ref[h] = acc_ref[h] * alpha + _dot(vt_ref[0, h, :, pl.ds(start, tq)], p.astype(BF16))
            m_ref[h] = m_new

    scores(0, 0)

    def body(j, c):
        scores(2 * j + 1, 1)
        consume(2 * j, 0, False)
        scores(2 * j + 2, 0)
        consume(2 * j + 1, 1, False)
        return c

    lax.fori_loop(0, qi // 2, body, 0)

    @pl.when(qi % 2 == 1)
    def _():
        scores(qi, 1)
        consume(qi - 1, 0, False)
        consume(qi, 1, True)

    @pl.when(qi % 2 == 0)
    def _():
        consume(qi, 0, True)

    for h in range(hps):
        o_ref[0, :, h * LANE:(h + 1) * LANE] = (acc_ref[h] * (1.0 / l_ref[h])).T.astype(o_ref.dtype)


def _mla_prompt_attn(q, k, vt, tq=512, hps=8):
    b, heads, s, _ = q.shape
    tq = _tile(s, tq)
    hps = min(hps, heads)
    assert tq % CHUNK == 0 and heads % hps == 0
    return pl.pallas_call(
        functools.partial(_flash_kernel, tq=tq, hps=hps),
        grid=(b, heads // hps, s // tq),
        in_specs=[pl.BlockSpec((1, hps, tq, 256), lambda i, h, j: (i, h, j, 0)),
                  pl.BlockSpec((1, hps, s, 256), lambda i, h, j: (i, h, 0, 0)),
                  pl.BlockSpec((1, hps, LANE, s), lambda i, h, j: (i, h, 0, 0))],
        out_specs=pl.BlockSpec((1, tq, hps * LANE), lambda i, h, j: (i, j, h)),
        out_shape=jax.ShapeDtypeStruct((b, s, heads * LANE), BF16),
        scratch_shapes=[pltpu.VMEM((hps, 1, tq), F32), pltpu.VMEM((hps, 1, tq), F32),
                        pltpu.VMEM((hps, LANE, tq), F32), pltpu.VMEM((2, hps, tq, tq), F32)],
        compiler_params=_cp("parallel", "parallel", "arbitrary"),
        name="mla_prompt_attn",
    )(q, k, vt)


def _mla_sample_kernel(q_ref, cp_ref, krp_ref, cn_ref, krn_ref, wuk_ref, wuv_ref, o_ref,
                       qa_ref, qr_ref, m_ref, l_ref, acc_ref, *, heads, sq):
    kb = pl.program_id(1)

    @pl.when(kb == 0)
    def _():
        for h in range(heads):
            qh = q_ref[0, h]
            qa_ref[h * sq:(h + 1) * sq, :] = _dot_t(qh[:, :LANE], wuk_ref[:, h * LANE:(h + 1) * LANE]).astype(BF16)
            qr_ref[h * sq:(h + 1) * sq, :] = qh[:, LANE:]
        m_ref[...] = jnp.full(m_ref.shape, NEG, F32)
        l_ref[...] = jnp.zeros(l_ref.shape, F32)
        acc_ref[...] = jnp.zeros(acc_ref.shape, F32)

    def update(cb, krb):
        s = _dot_t(qa_ref[...], cb) + _dot_t(qr_ref[...], krb)
        m_new = jnp.maximum(m_ref[...], jnp.max(s, axis=-1, keepdims=True))
        alpha = jnp.exp(m_ref[...] - m_new)
        p = jnp.exp(s - m_new)
        l_ref[...] = l_ref[...] * alpha + jnp.sum(p, axis=-1, keepdims=True)
        acc_ref[...] = acc_ref[...] * alpha + _dot(p.astype(BF16), cb)
        m_ref[...] = m_new

    update(cp_ref[0].astype(BF16), krp_ref[0])

    @pl.when(kb == pl.num_programs(1) - 1)
    def _():
        update(cn_ref[0].astype(BF16), krn_ref[0, 0])
        o_lat = (acc_ref[...] / l_ref[...]).astype(BF16)
        for h in range(heads):
            o_ref[0, :, h * LANE:(h + 1) * LANE] = _dot(
                o_lat[h * sq:(h + 1) * sq], wuv_ref[:, h * LANE:(h + 1) * LANE]).astype(o_ref.dtype)


def _mla_sample_attn(q, c_past, kr_past, c_new, k_new, wuk, wuv, tk=1024):
    b, heads, sq, _ = q.shape
    past, kv_lora = c_past.shape[1:]
    tk = _tile(past, tk)
    kern = functools.partial(_mla_sample_kernel, heads=heads, sq=sq)
    return pl.pallas_call(
        kern,
        grid=(b, past // tk),
        in_specs=[pl.BlockSpec((1, heads, sq, 256), lambda i, j: (i, 0, 0, 0)),
                  pl.BlockSpec((1, tk, kv_lora), lambda i, j: (i, j, 0)),
                  pl.BlockSpec((1, tk, LANE), lambda i, j: (i, j, 0)),
                  pl.BlockSpec((1, sq, kv_lora), lambda i, j: (i, 0, 0)),
                  pl.BlockSpec((1, 1, sq, LANE), lambda i, j: (i, 0, 0, 1)),
                  _resident(wuk.shape), _resident(wuv.shape)],
        out_specs=pl.BlockSpec((1, sq, heads * LANE), lambda i, j: (i, 0, 0)),
        out_shape=jax.ShapeDtypeStruct((b, sq, heads * LANE), BF16),
        scratch_shapes=[pltpu.VMEM((heads * sq, kv_lora), BF16), pltpu.VMEM((heads * sq, LANE), BF16),
                        pltpu.VMEM((heads * sq, 1), F32), pltpu.VMEM((heads * sq, 1), F32),
                        pltpu.VMEM((heads * sq, kv_lora), F32)],
        compiler_params=_cp("parallel", "arbitrary"),
        name="mla_sample_attn",
    )(q, c_past, kr_past, c_new, k_new, wuk, wuv)


def _rwkv_pre_kernel(x_ref, gmix_ref, w_ref, sh0_ref, mu_ref, w0_ref, wb_ref, a0_ref, ab_ref, gb_ref,
                     kkp_ref, ka_ref, rk_ref, e_ref, et_ref,
                     r_ref, dec_ref, k_ref, v_ref, kk_ref, b_ref, g_ref, bonus_ref, sh_ref, last_ref, *, dim):
    si = pl.program_id(1)

    def put(ref, val):
        nb, ts, tiles, _ = ref.shape
        ref[...] = val.reshape(nb * ts, tiles, LANE).reshape(nb, ts, tiles, LANE)

    u = _rms(x_ref[0], gmix_ref[...]).astype(BF16)
    tm = u.shape[0]
    width = w_ref.shape[1]

    @pl.when(si == 0)
    def _():
        last_ref[...] = sh0_ref[0]

    def shifted(c0, c1):
        p = _dot(u, w_ref[:, c0:c1])
        row = lax.broadcasted_iota(jnp.int32, p.shape, 0)
        prev = jnp.where(row == 0, last_ref[:, c0:c1], pltpu.roll(p, 1, axis=0))
        last_ref[:, c0:c1] = p[tm - 1:tm, :]
        sh_ref[0, :, c0:c1] = p[tm - 1:tm, :]
        return p + mu_ref[:, c0:c1] * (prev - p)

    lora = shifted(3 * dim, width)
    k = shifted(dim, 2 * dim)
    dw = lora[:, :LANE]
    da = lora[:, LANE:2 * LANE]
    dg = lora[:, 2 * LANE:]
    z = -(w0_ref[...] + _dot(jnp.tanh(dw).astype(BF16), wb_ref[...]))
    a = _sigmoid(a0_ref[...] + _dot(da.astype(BF16), ab_ref[...]))
    g_ref[0] = _dot(_sigmoid(dg).astype(BF16), gb_ref[...])
    r = shifted(0, dim)
    softplus = jnp.maximum(z, 0.0) + jnp.log(1.0 + jnp.exp(-jnp.abs(z)))
    put(dec_ref, jnp.exp(-jnp.exp(-softplus - 0.5)))
    kk = k * kkp_ref[...]
    nrm = jnp.sqrt(_seg_sum(kk * kk, e_ref[...], et_ref[...]))
    v = shifted(2 * dim, 3 * dim)
    kk = kk / jnp.maximum(nrm, 1e-12)
    k2 = k * (1.0 + (a - 1.0) * ka_ref[...])
    put(k_ref, k2)
    put(kk_ref, kk)
    put(b_ref, kk * a)
    put(r_ref, r)
    put(v_ref, v)
    bonus_ref[0] = _seg_sum(r * k2 * rk_ref[...], e_ref[...], et_ref[...]) * v


def _rwkv_pre(x, gmix, w, sh0, mu, w0, wb, a0, ab, gb, kkp, ka, rk, e, et, tm=256):
    b, s, d = x.shape
    dim = w0.shape[1]
    width = w.shape[1]
    tm = _tile(s, tm)
    ts = min(SCAN_TS, s)
    assert tm % ts == 0 and dim % LANE == 0
    row = lambda i, j: (i, j, 0)
    act = pl.BlockSpec((1, tm, dim), row)
    act_shape = jax.ShapeDtypeStruct((b, s, dim), F32)
    tiles = dim // LANE
    assert tiles % 8 == 0
    seq = pl.BlockSpec((tm // ts, ts, tiles, LANE), lambda i, j: (j, 0, i, 0))
    seq_shape = jax.ShapeDtypeStruct((s // ts, ts, b * tiles, LANE), F32)
    consts = (gmix, w, mu, w0, wb, a0, ab, gb, kkp, ka, rk, e, et)
    return pl.pallas_call(
        functools.partial(_rwkv_pre_kernel, dim=dim),
        grid=(b, s // tm),
        in_specs=[pl.BlockSpec((1, tm, d), row), _resident(gmix.shape), _resident(w.shape),
                  pl.BlockSpec((1, 1, width), lambda i, j: (i, 0, 0))] + [_resident(c.shape) for c in consts[2:]],
        out_specs=[seq] * 6 + [act] * 2 + [pl.BlockSpec((1, 1, width), lambda i, j: (i, 0, 0))],
        out_shape=[seq_shape] * 6 + [act_shape] * 2 + [jax.ShapeDtypeStruct((b, 1, width), F32)],
        scratch_shapes=[pltpu.VMEM((1, width), F32)],
        compiler_params=_cp("parallel", "arbitrary"),
        name="rwkv_pre",
    )(x, gmix, w, sh0, *consts[2:])


def _rwkv_scan_kernel(r_ref, dec_ref, k_ref, v_ref, kk_ref, b_ref, s0_ref, o_ref, st_ref, xt_ref, ot_ref,
                      *, ts, n, unroll):
    tb = pl.program_id(1)
    hpl = LANE // n

    @pl.when(tb == 0)
    def _():
        st_ref[...] = s0_ref[...]

    srcs = (r_ref, dec_ref, k_ref, v_ref, kk_ref, b_ref)
    for i in range(ts):
        for a, ref in enumerate(srcs):
            xt_ref[a, i] = ref[i].T

    def step(i, c):
        for h2 in range(hpl):
            base = h2 * n
            vv = xt_ref[3, i, base:base + n, :]

            def p1(kx, sa):
                return sa + st_ref[h2, kx] * xt_ref[4, i, pl.ds(base + kx, 1), :]

            sa = lax.fori_loop(0, n, p1, jnp.zeros_like(vv), unroll=unroll)

            def p2(kx, o):
                s_new = (st_ref[h2, kx] * xt_ref[1, i, pl.ds(base + kx, 1), :]
                         - sa * xt_ref[5, i, pl.ds(base + kx, 1), :] + vv * xt_ref[2, i, pl.ds(base + kx, 1), :])
                st_ref[h2, kx] = s_new
                return o + s_new * xt_ref[0, i, pl.ds(base + kx, 1), :]

            ot_ref[i, base:base + n, :] = lax.fori_loop(0, n, p2, jnp.zeros_like(vv), unroll=unroll)
        return c

    lax.fori_loop(0, ts, step, 0)
    for i in range(ts):
        o_ref[i] = ot_ref[i].T


def _rwkv_scan(r, dec, k, v, kk, bb, s0):
    nt, ts, streams, _ = r.shape
    groups, hpl, n = s0.shape[:3]
    assert streams == groups * LANE
    seq = pl.BlockSpec((None, ts, LANE, LANE), lambda g, t: (t, 0, g, 0))
    state = pl.BlockSpec((None, hpl, n, n, LANE), lambda g, t: (g, 0, 0, 0, 0))
    return pl.pallas_call(
        functools.partial(_rwkv_scan_kernel, ts=ts, n=n, unroll=SCAN_UNROLL),
        grid=(groups, nt),
        in_specs=[seq] * 6 + [state],
        out_specs=[seq, state],
        out_shape=[jax.ShapeDtypeStruct(r.shape, F32), jax.ShapeDtypeStruct(s0.shape, F32)],
        scratch_shapes=[pltpu.VMEM((6, ts, LANE, LANE), F32), pltpu.VMEM((ts, LANE, LANE), F32)],
        compiler_params=_cp("parallel", "arbitrary"),
        name="rwkv_scan",
    )(r, dec, k, v, kk, bb, s0)


def _state_to_streams(wkv):
    b, h, n = wkv.shape[:3]
    hpl = LANE // n
    tiles = h // hpl
    bg = LANE // tiles
    g = b // bg
    t = wkv.reshape(g, bg, tiles, hpl, n, n)
    return t.transpose(0, 3, 5, 4, 1, 2).reshape(g, hpl, n, n, LANE)


def _state_from_streams(st, b, h):
    g, hpl, n = st.shape[:3]
    tiles = h // hpl
    bg = LANE // tiles
    t = st.reshape(g, hpl, n, n, bg, tiles)
    return t.transpose(0, 4, 5, 1, 3, 2).reshape(b, h, n, n)


def _mix_kernel(o_ref, bonus_ref, g_ref, ymla_ref, grw_ref, gmla_ref, lng_ref, lnb_ref, e_ref, et_ref,
                wr_ref, wm_ref, out_ref, *, inv_n):
    nb, ts, tiles, _ = o_ref.shape
    o = o_ref[...].reshape(nb * ts, tiles, LANE).reshape(nb * ts, tiles * LANE)
    e, et = e_ref[...], et_ref[...]
    d = o - _seg_sum(o, e, et) * inv_n
    var = _seg_sum(d * d, e, et) * inv_n
    on = d * lax.rsqrt(var + GN_EPS) * lng_ref[...] + lnb_ref[...]
    y = ((on + bonus_ref[0]) * g_ref[0]).astype(BF16)
    merged = grw_ref[0] * _dot(y, wr_ref[...]) + gmla_ref[0] * _dot(ymla_ref[0], wm_ref[...])
    out_ref[0] = merged.astype(out_ref.dtype)


def _mix(o, bonus, g, ymla, gates, lng, lnb, e, et, wr, wm, head_dim, tm=256):
    nt, ts = o.shape[:2]
    b, s, dim = bonus.shape
    tiles = dim // LANE
    d = wr.shape[1]
    md = ymla.shape[2]
    tm = _tile(s, tm)
    assert tm % ts == 0
    row = lambda i, j: (i, j, 0)
    return pl.pallas_call(
        functools.partial(_mix_kernel, inv_n=1.0 / head_dim),
        grid=(b, s // tm),
        in_specs=[pl.BlockSpec((tm // ts, ts, tiles, LANE), lambda i, j: (j, 0, i, 0)),
                  pl.BlockSpec((1, tm, dim), row), pl.BlockSpec((1, tm, dim), row), pl.BlockSpec((1, tm, md), row),
                  pl.BlockSpec((1, tm, d), lambda i, j: (i, j, 0)), pl.BlockSpec((1, tm, d), lambda i, j: (i, j, 1))]
                 + [_resident(c.shape) for c in (lng, lnb, e, et, wr, wm)],
        out_specs=pl.BlockSpec((1, tm, d), row),
        out_shape=jax.ShapeDtypeStruct((b, s, d), BF16),
        compiler_params=_cp("parallel", "parallel"),
        name="rwkv_post_mix",
    )(o, bonus, g, ymla, gates, gates, lng, lnb, e, et, wr, wm)


def _outproj_kernel(x_ref, m_ref, wo_ref, gc_ref, wq_ref, h_ref, q_ref):
    h = x_ref[...] + _dot(m_ref[...], wo_ref[...])
    h_ref[...] = h
    q_ref[...] = _dot(_rms(h, gc_ref[...]).astype(BF16), wq_ref[...]).astype(q_ref.dtype)


def _outproj(x, merged, wo, gc, wq, tm=256):
    t, d = x.shape
    tm = _tile(t, tm)
    row = lambda i: (i, 0)
    return pl.pallas_call(
        _outproj_kernel,
        grid=(t // tm,),
        in_specs=[pl.BlockSpec((tm, d), row), pl.BlockSpec((tm, d), row),
                  _resident(wo.shape), _resident(gc.shape), _resident(wq.shape)],
        out_specs=[pl.BlockSpec((tm, d), row), pl.BlockSpec((tm, d), row)],
        out_shape=[jax.ShapeDtypeStruct((t, d), F32), jax.ShapeDtypeStruct((t, d), BF16)],
        compiler_params=_cp("parallel"),
        name="outproj_crossq",
    )(x, merged, wo, gc, wq)


def _route_tile(lg, n_groups, per_group):
    n_exp = n_groups * per_group
    lane = lax.broadcasted_iota(jnp.int32, lg.shape, 1)
    first = lambda mask: jnp.min(jnp.where(mask, lane, 2 * LANE), axis=-1, keepdims=True)
    top = lambda mask: jnp.max(jnp.where(mask, lg, NEG), axis=-1, keepdims=True)
    is_g = lane < n_groups
    gmax = top(is_g)
    g_sel = first(is_g & (lg == gmax))
    p_grp = 1.0 / jnp.sum(jnp.where(is_g, jnp.exp(lg - gmax), 0.0), axis=-1, keepdims=True)
    eid = lane - n_groups
    in_grp = (eid >= g_sel * per_group) & (eid < (g_sel + 1) * per_group)
    v1 = top(in_grp)
    e1 = first(in_grp & (lg == v1))
    rest = in_grp & (lane != e1)
    v2 = top(rest)
    e2 = first(rest & (lg == v2))
    t = jnp.exp(v2 - v1)
    w1 = p_grp / (1.0 + t)
    w2 = w1 * t
    route = jnp.where(lane == 0, (e1 - n_groups).astype(F32),
                      jnp.where(lane == 1, (e2 - n_groups).astype(F32),
                                jnp.where(lane == 2, w1, jnp.where(lane == 3, w2, 0.0))))
    hits = jnp.where((lane == e1 - n_groups) | (lane == e2 - n_groups), 1.0, 0.0)
    assert n_exp <= LANE
    return route, jnp.sum(hits, axis=0, keepdims=True)


def _cross_kernel(q_ref, mk_ref, mv_ref, h_ref, wo_ref, gm_ref, wr_ref, br_ref, hn_all_ref, h2_ref, hn_ref, rt_ref,
                  cnt_ref, *, heads, scale, n_groups, per_group):
    del hn_all_ref
    q = q_ref[0]
    mk = mk_ref[0]
    mv = mv_ref[0]
    hd = q.shape[1] // heads
    outs = []
    for h in range(heads):
        sl = slice(h * hd, (h + 1) * hd)
        s = _dot_t(q[:, sl], mk[:, sl]) * scale
        p = jnp.exp(s - jnp.max(s, axis=-1, keepdims=True))
        p = p / jnp.sum(p, axis=-1, keepdims=True)
        outs.append(_dot(p.astype(BF16), mv[:, sl]).astype(BF16))
    o = jnp.concatenate(outs, axis=1)
    h2 = h_ref[0] + _dot(o, wo_ref[...])
    h2_ref[0] = h2
    hn = _rms(h2, gm_ref[...]).astype(BF16)
    hn_ref[...] = hn
    route, hist = _route_tile(_dot(hn, wr_ref[...]) + br_ref[...], n_groups, per_group)
    rt_ref[0] = route
    cnt_ref[0, 0] = jnp.broadcast_to(hist, cnt_ref.shape[2:])


def _cross(q, mk, mv, h, wo, gm, wr, br, heads, n_groups, per_group, total_rows, row_off, hn_all, tm=256):
    b, s, d = h.shape
    n_mem = mk.shape[1]
    tm = _tile(s, tm)
    assert row_off % tm == 0
    off, per_b = row_off // tm, s // tm
    row = lambda i, j: (i, j, 0)
    mem = pl.BlockSpec((1, n_mem, d), lambda i, j: (i, 0, 0))
    kern = functools.partial(_cross_kernel, heads=heads, scale=float(d // heads) ** -0.5,
                             n_groups=n_groups, per_group=per_group)
    in_specs = [pl.BlockSpec((1, tm, d), row), mem, mem, pl.BlockSpec((1, tm, d), row),
                _resident(wo.shape), _resident(gm.shape), _resident(wr.shape), _resident(br.shape),
                pl.BlockSpec(memory_space=pl.ANY)]
    args = (q, mk, mv, h, wo, gm, wr, br, hn_all)
    return pl.pallas_call(
        kern,
        grid=(b, s // tm),
        in_specs=in_specs,
        out_specs=[pl.BlockSpec((1, tm, d), row), pl.BlockSpec((tm, d), lambda i, j: (off + i * per_b + j, 0)),
                   pl.BlockSpec((1, tm, LANE), row), pl.BlockSpec((1, 1, 8, LANE), lambda i, j: (i, j, 0, 0))],
        out_shape=[jax.ShapeDtypeStruct((b, s, d), F32), jax.ShapeDtypeStruct((total_rows, d), BF16),
                   jax.ShapeDtypeStruct((b, s, LANE), F32), jax.ShapeDtypeStruct((b, s // tm, 8, LANE), F32)],
        input_output_aliases={len(args) - 1: 1},
        compiler_params=_cp("parallel", "parallel"),
        name="cross_attn_router",
    )(*args)


def _moe_kernel(be_ref, on_ref, first_ref, nxt_ref, par_ref, x_ref, wg_hbm, wu_hbm, wd_hbm, y_ref,
                wgf_ref, wuf_ref, wdf_ref, wgb_ref, wub_ref, wdb_ref, sem_ref):
    i = pl.program_id(0)

    def fetch(e, slot):
        return (pltpu.make_async_copy(wg_hbm.at[e], wgf_ref.at[slot], sem_ref.at[slot, 0]),
                pltpu.make_async_copy(wu_hbm.at[e], wuf_ref.at[slot], sem_ref.at[slot, 1]),
                pltpu.make_async_copy(wd_hbm.at[e], wdf_ref.at[slot], sem_ref.at[slot, 2]))

    @pl.when(first_ref[i] != 0)
    def _():
        slot = par_ref[i]

        @pl.when(i == 0)
        def _():
            for c in fetch(be_ref[0], 0):
                c.start()

        for c in fetch(be_ref[i], slot):
            c.wait()
        wgb_ref[...] = wgf_ref[slot].astype(BF16)
        wub_ref[...] = wuf_ref[slot].astype(BF16)
        wdb_ref[...] = wdf_ref[slot].astype(BF16)

        @pl.when(nxt_ref[i] >= 0)
        def _():
            for c in fetch(nxt_ref[i], 1 - slot):
                c.start()

    @pl.when(on_ref[i] != 0)
    def _():
        xb = x_ref[...]
        g = _dot(xb, wgb_ref[...])
        u = _dot(xb, wub_ref[...])
        hb = (g * _sigmoid(g) * u).astype(BF16)
        y_ref[...] = _dot(hb, wdb_ref[...])

    @pl.when(on_ref[i] == 0)
    def _():
        y_ref[...] = jnp.zeros(y_ref.shape, y_ref.dtype)


def _moe_experts(blk_e, blk_on, counts, x_slot, wg, wu, wd):
    ns, d = x_slot.shape
    n_exp, _, f = wg.shape
    n_blk = ns // MOE_ROWS
    prev_e = jnp.concatenate([jnp.full((1,), -1, jnp.int32), blk_e[:-1]])
    first = ((blk_on != 0) & (blk_e != prev_e)).astype(jnp.int32)
    par = ((jnp.cumsum(first) - 1) % 2).astype(jnp.int32)
    ids = jnp.where(counts > 0, jnp.arange(n_exp, dtype=jnp.int32), n_exp)
    nxt_tab = jnp.concatenate([lax.cummin(ids[::-1])[::-1][1:], jnp.full((1,), n_exp, jnp.int32)])
    nxt = nxt_tab[blk_e]
    nxt = jnp.where(nxt < n_exp, nxt, -1).astype(jnp.int32)
    hbm = pl.BlockSpec(memory_space=pl.ANY)
    grid_spec = pltpu.PrefetchScalarGridSpec(
        num_scalar_prefetch=5,
        grid=(n_blk,),
        in_specs=[pl.BlockSpec((MOE_ROWS, d), lambda i, *_: (i, 0)), hbm, hbm, hbm],
        out_specs=pl.BlockSpec((MOE_ROWS, d), lambda i, *_: (i, 0)),
        scratch_shapes=[pltpu.VMEM((2, d, f), F32), pltpu.VMEM((2, d, f), F32), pltpu.VMEM((2, f, d), F32),
                        pltpu.VMEM((d, f), BF16), pltpu.VMEM((d, f), BF16), pltpu.VMEM((f, d), BF16),
                        pltpu.SemaphoreType.DMA((2, 3))],
    )
    return pl.pallas_call(
        _moe_kernel,
        grid_spec=grid_spec,
        out_shape=jax.ShapeDtypeStruct((ns, d), F32),
        compiler_params=_cp("arbitrary"),
        name="moe_experts",
    )(blk_e, blk_on, first, nxt, par, x_slot, wg, wu, wd)


def _combine_kernel(idx_ref, nidx_ref, h_ref, rt_ref, g_ref, y_hbm, o_ref, buf_ref, sem_ref, *, tm, norm):
    i = pl.program_id(0)
    n = pl.num_programs(0)

    def issue(ids_ref, slot):
        def one(r, c):
            for ch in range(TOP_K):
                src = ids_ref[0, TOP_K * r + ch]
                pltpu.make_async_copy(y_hbm.at[pl.ds(src, 1)], buf_ref.at[slot, ch, pl.ds(r, 1)],
                                      sem_ref.at[slot, ch]).start()
            return c

        lax.fori_loop(0, tm, one, 0, unroll=8)

    cur = i % 2

    @pl.when(i == 0)
    def _():
        issue(idx_ref, 0)

    @pl.when(i + 1 < n)
    def _():
        issue(nidx_ref, 1 - cur)

    acc = h_ref[...]
    for ch in range(TOP_K):
        pltpu.make_async_copy(y_hbm.at[pl.ds(0, tm)], buf_ref.at[cur, ch], sem_ref.at[cur, ch]).wait()
        acc = acc + buf_ref[cur, ch] * rt_ref[:, TOP_K + ch:TOP_K + ch + 1]
    o_ref[...] = _rms(acc, g_ref[...]) if norm else acc


def _combine(h, y_slot, slot_of, route, row_off, g, tm=512):
    t, d = h.shape
    tm = _tile(t, tm)
    assert row_off % tm == 0
    off = row_off // tm
    nb = t // tm
    idx = slot_of[row_off * TOP_K:(row_off + t) * TOP_K].reshape(nb, 1, TOP_K * tm)
    norm = g is not None
    g = g if norm else jnp.ones((1, d), F32)
    ids = lambda f: pl.BlockSpec((None, 1, TOP_K * tm), f, memory_space=pltpu.SMEM)
    return pl.pallas_call(
        functools.partial(_combine_kernel, tm=tm, norm=norm),
        grid=(nb,),
        in_specs=[ids(lambda i: (i, 0, 0)), ids(lambda i: (jnp.minimum(i + 1, nb - 1), 0, 0)),
                  pl.BlockSpec((tm, d), lambda i: (i, 0)),
                  pl.BlockSpec((tm, LANE), lambda i: (i + off, 0)),
                  pl.BlockSpec((1, d), lambda i: (0, 0)),
                  pl.BlockSpec(memory_space=pl.ANY)],
        out_specs=pl.BlockSpec((tm, d), lambda i: (i, 0)),
        out_shape=jax.ShapeDtypeStruct((t, d), F32),
        scratch_shapes=[pltpu.VMEM((2, TOP_K, tm, d), F32), pltpu.SemaphoreType.DMA((2, TOP_K))],
        compiler_params=_cp("arbitrary"),
        name="moe_combine_norm",
    )(idx, idx, h, route, g, y_slot)


def _pad_cols(w, n):
    return jnp.pad(w, ((0, 0),) * (w.ndim - 1) + ((0, n - w.shape[-1]),))


def _pad_rows(w, n):
    return jnp.pad(w, ((0, n - w.shape[0]), (0, 0)))


def _rot_half_cols(w):
    half = w.shape[-1] // 2
    return jnp.concatenate([-w[..., half:], w[..., :half]], axis=-1)


def _rope_table(pos):
    half = ROPE // 2
    inv = ROPE_THETA ** (-jnp.arange(half, dtype=F32) / half)
    ang = pos.astype(F32)[:, None] * inv[None, :]
    cos, sin = jnp.cos(ang), jnp.sin(ang)
    return jnp.concatenate([cos, cos, sin, sin], axis=1)


def _prep_layer(lp, dims):
    ql, kl, rope, dim, dl, il, gl, d = (dims[k] for k in ("q_lora", "kv_lora", "rope", "rw_dim", "decay_lora",
                                                          "iclr_lora", "gate_lora", "d_model"))
    w_in = lp["w_in"]
    i0, i1, i2 = ql, ql + kl, ql + kl + rope
    i3 = i2 + 3 * dim + dl + il + gl
    w_kr = w_in[:, i1:i2]
    out = {}
    out["w_mla"] = jnp.concatenate(
        [w_in[:, :i0], w_kr, _rot_half_cols(w_kr), w_in[:, i0:i1]], axis=1).astype(BF16)
    rw = w_in[:, i2:i3]
    c3 = 3 * dim
    glp = -(-gl // LANE) * LANE

    def regroup(t):
        return jnp.concatenate([t[..., :c3], _pad_cols(t[..., c3:c3 + dl], LANE),
                                _pad_cols(t[..., c3 + dl:c3 + dl + il], LANE),
                                _pad_cols(t[..., c3 + dl + il:], glp)], axis=-1)

    out["regroup"] = regroup
    out["w_rw"] = regroup(rw).astype(BF16)
    out["mu"] = regroup(lp["rw_mu"][None, :])
    out["w_gates"] = w_in[:, i3:].astype(BF16)
    heads = lp["w_uq"].shape[1]
    nope = lp["w_uq"].shape[2] - rope
    wq = lp["w_uq"]
    out["w_q"] = jnp.concatenate([wq[..., :nope], wq[..., nope:], _rot_half_cols(wq[..., nope:])],
                                 axis=-1).reshape(ql, heads * 256).astype(BF16)
    out["w_uk"] = lp["w_uk"].reshape(kl, -1).astype(BF16)
    out["w_uv"] = lp["w_uv"].reshape(kl, -1).astype(BF16)
    out["w_uv_t"] = out["w_uv"].T
    out["wb"] = _pad_rows(lp["rw_wb"], LANE).astype(BF16)
    out["ab"] = _pad_rows(lp["rw_ab"], LANE).astype(BF16)
    out["gb"] = _pad_rows(lp["rw_gb"], glp).astype(BF16)
    n_heads = dims["rw_heads"]
    hd = dim // n_heads
    e = (jnp.arange(dim)[:, None] // hd == jnp.arange(LANE)[None, :]).astype(BF16)
    out["e"], out["et"] = e, e.T
    out["w_br_rwkv"] = lp["w_br_rwkv"].astype(BF16)
    out["w_br_mla"] = lp["w_br_mla"].astype(BF16)
    out["w_out"] = lp["w_out"].astype(BF16)
    out["w_mq"] = lp["w_mq"].reshape(d, d).astype(BF16)
    out["w_mk"] = lp["w_mk"].reshape(d, d).astype(BF16)
    out["w_mv"] = lp["w_mv"].reshape(d, d).astype(BF16)
    out["w_mo"] = lp["w_mo"].reshape(d, d).astype(BF16)
    n_g, n_e = lp["w_rg"].shape[1], lp["w_re"].shape[1]
    out["w_r"] = _pad_cols(jnp.concatenate([lp["w_rg"], lp["w_re"]], axis=1), LANE).astype(BF16)
    out["b_r"] = _pad_cols(jnp.concatenate([lp["b_rg"], lp["b_re"]])[None, :], LANE)
    out["w_eg"], out["w_eu"], out["w_ed"] = lp["w_eg"], lp["w_eu"], lp["w_ed"]
    return out


def _row(v):
    return v.reshape(1, -1)


def _mixer(x, lp, wp, dims, pos, shift0, wkv0, attend):
    b, s, d = x.shape
    heads, rw_heads = dims["mla_heads"], dims["rw_heads"]
    gmix = _row(lp["g_mix"])
    cs = _rope_table(pos)
    scale = float(dims["nope"] + dims["rope"]) ** -0.5
    q, k, v, c, kr = _mla_proj(x, gmix, wp["w_mla"], _row(lp["g_q"]), wp["w_q"], _row(lp["g_kv"]),
                               wp["w_uk"], wp["w_uv_t"], cs, heads, scale)
    y_mla = attend(q, k, v, c)
    sh0 = wp["regroup"](shift0)
    r, dec, k2, vr, kk, bb, g, bonus, sh = _rwkv_pre(
        x, gmix, wp["w_rw"], sh0, wp["mu"], _row(lp["rw_w0"]), wp["wb"], _row(lp["rw_a0"]), wp["ab"], wp["gb"],
        _row(lp["rw_kk"]), _row(lp["rw_ka"]), _row(lp["rw_rk"]), wp["e"], wp["et"])
    n = dims["rw_dim"] // rw_heads
    o, s_t = _rwkv_scan(r, dec, k2, vr, kk, bb, _state_to_streams(wkv0))
    wkv = _state_from_streams(s_t, b, rw_heads)
    gates, = _proj(x.reshape(b * s, d), gmix, [wp["w_gates"]], 512, out_dtype=BF16, gate=True)
    gates = gates.reshape(b, s, 2 * d)
    dim = dims["rw_dim"]
    merged = _mix(o, bonus, g, y_mla, gates, _row(lp["rw_ln_g"]), _row(lp["rw_ln_b"]), wp["e"], wp["et"],
                  wp["w_br_rwkv"], wp["w_br_mla"], n).reshape(b * s, d)
    h, qc = _outproj(x.reshape(b * s, d), merged, wp["w_out"], _row(lp["g_cross"]), wp["w_mq"])
    c3, dl, il = 3 * dim, dims["decay_lora"], dims["iclr_lora"]
    shift = jnp.concatenate([sh[..., :c3], sh[..., c3:c3 + dl], sh[..., c3 + LANE:c3 + LANE + il],
                             sh[..., c3 + 2 * LANE:c3 + 2 * LANE + dims["gate_lora"]]], axis=-1)
    return h.reshape(b, s, d), qc.reshape(b, s, d), c, kr, k, wkv, shift


def _moe(hn, route, counts, wp):
    t, d = hn.shape
    n_exp = counts.shape[0]
    e_idx = route[:, :TOP_K]
    m = t * TOP_K
    n_blk = -(-(m + n_exp * (MOE_ROWS - 1)) // MOE_ROWS)
    ns = n_blk * MOE_ROWS
    flat_e = e_idx.reshape(m).astype(jnp.int32)
    order = jnp.argsort(flat_e).astype(jnp.int32)
    padded = (counts + MOE_ROWS - 1) // MOE_ROWS * MOE_ROWS
    pad_end = jnp.cumsum(padded)
    pad_start = pad_end - padded
    start = jnp.cumsum(counts) - counts
    blk_start = jnp.arange(n_blk, dtype=jnp.int32) * MOE_ROWS
    blk_e = jnp.minimum(jnp.sum(pad_end[None, :] <= blk_start[:, None], axis=1), n_exp - 1).astype(jnp.int32)
    blk_on = (blk_start < pad_end[-1]).astype(jnp.int32)
    off = (blk_start - pad_start[blk_e])[:, None] + jnp.arange(MOE_ROWS, dtype=jnp.int32)[None, :]
    valid = ((off < counts[blk_e][:, None]) & (blk_on[:, None] != 0)).reshape(ns)
    asg = order[jnp.clip(start[blk_e][:, None] + off, 0, m - 1).reshape(ns)]
    slot_tok = jnp.where(valid, asg // TOP_K, jnp.arange(ns, dtype=jnp.int32) % t)
    y_slot = _moe_experts(blk_e, blk_on, counts, hn[slot_tok], wp["w_eg"], wp["w_eu"], wp["w_ed"])
    key = jnp.where(valid, asg, m + jnp.arange(ns, dtype=jnp.int32))
    slot_of = jnp.argsort(key)[:m].astype(jnp.int32)
    return y_slot, slot_of


def kernel(x_prompt, x_sample, mem_prompt, cache_kv_latent, cache_k_rope, cache_mem_k, cache_mem_v, state_wkv, state_shift, g_mix, w_in, g_q, w_uq, g_kv, w_uk, w_uv, rw_mu, rw_w0, rw_wb, rw_a0, rw_ab, rw_gb, rw_kk, rw_ka, rw_rk, rw_ln_g, rw_ln_b, w_br_rwkv, w_br_mla, w_out, g_cross, g_mem, w_mq, w_mk, w_mv, w_mo, g_moe, w_rg, b_rg, w_re, b_re, w_eg, w_eu, w_ed, g_final):
    depth = g_mix.shape[0]
    b, s, d = x_prompt.shape
    bs, ss, _ = x_sample.shape
    past = cache_kv_latent.shape[2]
    n_mem = mem_prompt.shape[1]
    mem_heads = w_mq.shape[2]
    rw_heads, rw_hd = rw_rk.shape[1:]
    dims = dict(d_model=d, q_lora=g_q.shape[1], kv_lora=g_kv.shape[1], mla_heads=w_uq.shape[2],
                nope=w_uk.shape[3], rope=cache_k_rope.shape[3], rw_heads=rw_heads, rw_dim=rw_heads * rw_hd,
                decay_lora=rw_wb.shape[1], iclr_lora=rw_ab.shape[1], gate_lora=rw_gb.shape[1])
    assert dims["rope"] == ROPE and dims["nope"] == LANE and w_uv.shape[3] == LANE
    assert dims["decay_lora"] <= LANE and dims["iclr_lora"] <= LANE and rw_heads <= LANE
    n_groups, n_exp = w_rg.shape[2], w_re.shape[2]
    per_group = n_exp // n_groups
    stacked = dict(g_mix=g_mix, w_in=w_in, g_q=g_q, w_uq=w_uq, g_kv=g_kv, w_uk=w_uk, w_uv=w_uv, rw_mu=rw_mu,
                   rw_w0=rw_w0, rw_wb=rw_wb, rw_a0=rw_a0, rw_ab=rw_ab, rw_gb=rw_gb, rw_kk=rw_kk, rw_ka=rw_ka,
                   rw_rk=rw_rk, rw_ln_g=rw_ln_g, rw_ln_b=rw_ln_b, w_br_rwkv=w_br_rwkv, w_br_mla=w_br_mla,
                   w_out=w_out, g_cross=g_cross, g_mem=g_mem, w_mq=w_mq, w_mk=w_mk, w_mv=w_mv, w_mo=w_mo,
                   g_moe=g_moe, w_rg=w_rg, b_rg=b_rg, w_re=w_re, b_re=b_re, w_eg=w_eg, w_eu=w_eu, w_ed=w_ed)
    h_p, h_s = x_prompt, x_sample
    outs = [[] for _ in range(10)]
    for l in range(depth):
        lp = {name: val[l] for name, val in stacked.items()}
        wp = _prep_layer(lp, dims)

        h, qc, c, kr, _, wkv, shift = _mixer(
            h_p, lp, wp, dims, jnp.arange(s), jnp.zeros((b, 1, state_shift.shape[3]), F32),
            jnp.zeros((b, rw_heads, rw_hd, rw_hd), F32), lambda q, k, v, c_new: _mla_prompt_attn(q, k, v))
        mk, mk_b, mv, mv_b = _proj(mem_prompt.reshape(b * n_mem, d), _row(lp["g_mem"]), [wp["w_mk"], wp["w_mv"]], 256,
                                   heads=mem_heads, twin=True)
        gm, wr, br = _row(lp["g_moe"]), wp["w_r"], wp["b_r"]
        tp, tsm = b * s, bs * ss
        h2_p, hn, rt_p, cnt_p = _cross(qc, mk_b.reshape(b, n_mem, d), mv_b.reshape(b, n_mem, d), h, wp["w_mo"], gm, wr,
                                       br, mem_heads, n_groups, per_group, tp + tsm, 0,
                                       hn_all=jnp.zeros((tp + tsm, d), BF16))
        for lst, val in zip(outs[:6], (c, kr, mk.reshape(b, n_mem, mem_heads, -1),
                                       mv.reshape(b, n_mem, mem_heads, -1), wkv, shift)):
            lst.append(val)

        kr_past = jnp.pad(cache_k_rope[l], ((0, 0), (0, 0), (0, LANE - ROPE))).astype(BF16)

        def attend_sample(q, k, v, c_new, l=l, kr_past=kr_past, wp=wp):
            return _mla_sample_attn(q, cache_kv_latent[l], kr_past, c_new, k, wp["w_uk"], wp["w_uv"])

        h, qc, c, kr, _, wkv, shift = _mixer(h_s, lp, wp, dims, past + jnp.arange(ss), state_shift[l],
                                             state_wkv[l], attend_sample)
        h2_s, hn, rt_s, cnt_s = _cross(qc, cache_mem_k[l].reshape(bs, n_mem, d).astype(BF16),
                                       cache_mem_v[l].reshape(bs, n_mem, d).astype(BF16), h, wp["w_mo"], gm, wr, br,
                                       mem_heads, n_groups, per_group, tp + tsm, tp, hn_all=hn)
        for lst, val in zip(outs[6:], (c, kr, wkv, shift)):
            lst.append(val)

        route = jnp.concatenate([rt_p.reshape(tp, LANE), rt_s.reshape(tsm, LANE)], axis=0)
        counts = (jnp.sum(cnt_p[:, :, 0, :n_exp], axis=(0, 1)) + jnp.sum(cnt_s[:, :, 0, :n_exp], axis=(0, 1)))
        y_slot, slot_of = _moe(hn, route, counts.astype(jnp.int32), wp)
        g_fin = _row(g_final) if l == depth - 1 else None
        h_p = _combine(h2_p.reshape(tp, d), y_slot, slot_of, route, 0, g_fin).reshape(b, s, d)
        h_s = _combine(h2_s.reshape(tsm, d), y_slot, slot_of, route, tp, g_fin).reshape(bs, ss, d)
    stacks = [jnp.stack(o) for o in outs]
    return (h_p, h_s, *stacks)
```

```python
import functools

import jax
import jax.numpy as jnp
from jax import lax
from jax.experimental import pallas as pl
from jax.experimental.pallas import tpu as pltpu

F32 = jnp.float32
BF16 = jnp.bfloat16

EPS = 1e-6
GN_EPS = 64e-5
ROPE_THETA = 10000.0
CHUNK = 64
TOP_K = 2
NEG = -1e30
LOG2E = 1.4426950408889634

LANE = 128
V7X_VMEM_LIMIT = 56 << 20
ROPE = 64
MOE_ROWS = 256
SCAN_TS = 16
SCAN_UNROLL = 32


def _cp(*sem):
    return pltpu.CompilerParams(dimension_semantics=sem, vmem_limit_bytes=V7X_VMEM_LIMIT)


def _tile(n, pref):
    if n <= pref:
        return n
    for t in range(pref, 7, -1):
        if n % t == 0 and t % 8 == 0:
            return t
    return n


def _resident(shape):
    nd = len(shape)
    return pl.BlockSpec(shape, lambda *_: (0,) * nd, pipeline_mode=pl.Buffered(1))


def _rms(x, g):
    return x * lax.rsqrt(jnp.mean(x * x, axis=-1, keepdims=True) + EPS) * g


def _sigmoid(x):
    return 1.0 / (1.0 + jnp.exp(-x))


def _dot(a, b):
    return jnp.dot(a, b, preferred_element_type=F32)


def _dot_t(a, b):
    return lax.dot_general(a, b, (((1,), (1,)), ((), ())), preferred_element_type=F32)


def _seg_sum(x, e, et):
    hi = x.astype(BF16)
    lo = (x - hi.astype(F32)).astype(BF16)
    s = _dot(hi, e) + _dot(lo, e)
    shi = s.astype(BF16)
    slo = (s - shi.astype(F32)).astype(BF16)
    return _dot(shi, et) + _dot(slo, et)


def _proj_kernel(x_ref, g_ref, *refs, n_w, gate, twin, cols):
    w_refs, o_refs = refs[:n_w], refs[n_w:]
    xn = _rms(x_ref[...], g_ref[...]).astype(BF16)
    per_w = 2 if twin else 1
    for i, w_ref in enumerate(w_refs):
        o_ref = o_refs[per_w * i]
        n = w_ref.shape[1]
        for c in range(0, n, cols):
            y = _dot(xn, w_ref[:, c:c + cols])
            if gate:
                y = _sigmoid(y)
            if len(o_ref.shape) == 3:
                hd = o_ref.shape[2]
                o_ref[:, c // hd:(c + cols) // hd, :] = y.reshape(y.shape[0], cols // hd, hd).astype(o_ref.dtype)
            else:
                o_ref[:, c:c + cols] = y.astype(o_ref.dtype)
            if twin:
                o_refs[per_w * i + 1][:, c:c + cols] = y.astype(BF16)


def _proj(x, g, ws, tm, out_dtype=F32, gate=False, heads=None, twin=False, cols=1024):
    m, k = x.shape
    tm = _tile(m, tm)
    out_specs, out_shape = [], []
    for w in ws:
        n = w.shape[1]
        if heads is None:
            out_specs.append(pl.BlockSpec((tm, n), lambda i: (i, 0)))
            out_shape.append(jax.ShapeDtypeStruct((m, n), out_dtype))
        else:
            out_specs.append(pl.BlockSpec((tm, heads, n // heads), lambda i: (i, 0, 0)))
            out_shape.append(jax.ShapeDtypeStruct((m, heads, n // heads), out_dtype))
        if twin:
            out_specs.append(pl.BlockSpec((tm, n), lambda i: (i, 0)))
            out_shape.append(jax.ShapeDtypeStruct((m, n), BF16))
    cols = min(cols, min(w.shape[1] for w in ws))
    return pl.pallas_call(
        functools.partial(_proj_kernel, n_w=len(ws), gate=gate, twin=twin, cols=cols),
        grid=(m // tm,),
        in_specs=[pl.BlockSpec((tm, k), lambda i: (i, 0)), _resident(g.shape)] + [_resident(w.shape) for w in ws],
        out_specs=out_specs,
        out_shape=out_shape,
        compiler_params=_cp("parallel"),
        name="norm_proj",
    )(x, g, *ws)


def _mla_proj_kernel(x_ref, gmix_ref, wm_ref, gq_ref, wq_ref, gkv_ref, wuk_ref, wuv_ref, cs_ref,
                     q_ref, k_ref, v_ref, c_ref, kr_ref, *, heads, q_lora, scale):
    u = _rms(x_ref[0], gmix_ref[...]).astype(BF16)
    p = _dot(u, wm_ref[...])
    cs = cs_ref[...]
    lane = lax.broadcasted_iota(jnp.int32, cs.shape, 1)

    def rope_tile(t2):
        t = t2 * cs
        return jnp.where(lane < ROPE, t + pltpu.roll(t, ROPE, axis=1), 0.0)

    krt = rope_tile(p[:, q_lora:q_lora + LANE])
    kr_ref[0] = krt[:, :ROPE]
    krb = krt.astype(BF16)
    c = _rms(p[:, q_lora + LANE:], gkv_ref[...])
    c_ref[0] = c
    cb = c.astype(BF16)
    kn = _dot(cb, wuk_ref[...])
    vt = _dot_t(wuv_ref[...], cb)
    qn = _rms(p[:, :q_lora], gq_ref[...]).astype(BF16)
    qf = _dot(qn, wq_ref[...])
    for h in range(heads):
        q_ref[0, h, :, :LANE] = (qf[:, h * 256:h * 256 + LANE] * scale).astype(BF16)
        q_ref[0, h, :, LANE:] = (rope_tile(qf[:, h * 256 + LANE:(h + 1) * 256]) * scale).astype(BF16)
        k_ref[0, h, :, :LANE] = kn[:, h * LANE:(h + 1) * LANE].astype(BF16)
        k_ref[0, h, :, LANE:] = krb
        v_ref[0, h] = vt[h * LANE:(h + 1) * LANE, :].astype(BF16)


def _mla_proj(x, gmix, wm, gq, wq, gkv, wuk, wuv, cs, heads, scale, tm=256):
    b, s, d = x.shape
    q_lora = gq.shape[1]
    kv_lora = gkv.shape[1]
    tm = _tile(s, tm)
    kern = functools.partial(_mla_proj_kernel, heads=heads, q_lora=q_lora, scale=scale)
    return pl.pallas_call(
        kern,
        grid=(b, s // tm),
        in_specs=[pl.BlockSpec((1, tm, d), lambda i, j: (i, j, 0)),
                  _resident(gmix.shape), _resident(wm.shape), _resident(gq.shape), _resident(wq.shape),
                  _resident(gkv.shape), _resident(wuk.shape), _resident(wuv.shape),
                  pl.BlockSpec((tm, LANE), lambda i, j: (j, 0))],
        out_specs=[pl.BlockSpec((1, heads, tm, 256), lambda i, j: (i, 0, j, 0)),
                   pl.BlockSpec((1, heads, tm, 256), lambda i, j: (i, 0, j, 0)),
                   pl.BlockSpec((1, heads, LANE, tm), lambda i, j: (i, 0, 0, j)),
                   pl.BlockSpec((1, tm, kv_lora), lambda i, j: (i, j, 0)),
                   pl.BlockSpec((1, tm, ROPE), lambda i, j: (i, j, 0))],
        out_shape=[jax.ShapeDtypeStruct((b, heads, s, 256), BF16),
                   jax.ShapeDtypeStruct((b, heads, s, 256), BF16),
                   jax.ShapeDtypeStruct((b, heads, LANE, s), BF16),
                   jax.ShapeDtypeStruct((b, s, kv_lora), F32),
                   jax.ShapeDtypeStruct((b, s, ROPE), F32)],
        compiler_params=_cp("parallel", "parallel"),
        name="mla_proj",
    )(x, gmix, wm, gq, wq, gkv, wuk, wuv, cs)


def _flash_kernel(q_ref, k_ref, vt_ref, o_ref, m_ref, l_ref, acc_ref, sc_ref, *, tq, hps):
    qi = pl.program_id(2)
    m_ref[...] = jnp.full(m_ref.shape, NEG, F32)
    l_ref[...] = jnp.zeros(l_ref.shape, F32)
    acc_ref[...] = jnp.zeros(acc_ref.shape, F32)

    def scores(j, slot):
        start = pl.multiple_of(j * tq, tq)
        for h in range(hps):
            sc_ref[slot, h] = _dot_t(k_ref[0, h, pl.ds(start, tq), :], q_ref[0, h])

    def consume(j, slot, diag):
        start = pl.multiple_of(j * tq, tq)
        for h in range(hps):
            s = sc_ref[slot, h]
            if diag:
                kc = lax.broadcasted_iota(jnp.int32, s.shape, 0) // CHUNK
                qc = lax.broadcasted_iota(jnp.int32, s.shape, 1) // CHUNK
                s = jnp.where(kc <= qc, s, NEG)
            m = m_ref[h]
            m_new = jnp.maximum(m, jnp.max(s, axis=0, keepdims=True))
            alpha = jnp.exp2(m - m_new)
            p = jnp.exp2(s - m_new)
            l_ref[h] = l_ref[h] * alpha + jnp.sum(p, axis=0, keepdims=True)
            acc_ref[h] = acc_ref[h] * alpha + _dot(vt_ref[0, h, :, pl.ds(start, tq)], p.astype(BF16))
            m_ref[h] = m_new

    scores(0, 0)

    def body(j, c):
        scores(2 * j + 1, 1)
        consume(2 * j, 0, False)
        scores(2 * j + 2, 0)
        consume(2 * j + 1, 1, False)
        return c

    lax.fori_loop(0, qi // 2, body, 0)

    @pl.when(qi % 2 == 1)
    def _():
        scores(qi, 1)
        consume(qi - 1, 0, False)
        consume(qi, 1, True)

    @pl.when(qi % 2 == 0)
    def _():
        consume(qi, 0, True)

    for h in range(hps):
        o_ref[0, :, h * LANE:(h + 1) * LANE] = (acc_ref[h] * (1.0 / l_ref[h])).T.astype(o_ref.dtype)


def _mla_prompt_attn(q, k, vt, tq=512, hps=8):
    b, heads, s, _ = q.shape
    tq = _tile(s, tq)
    hps = min(hps, heads)
    assert tq % CHUNK == 0 and heads % hps == 0
    return pl.pallas_call(
        functools.partial(_flash_kernel, tq=tq, hps=hps),
        grid=(b, heads // hps, s // tq),
        in_specs=[pl.BlockSpec((1, hps, tq, 256), lambda i, h, j: (i, h, j, 0)),
                  pl.BlockSpec((1, hps, s, 256), lambda i, h, j: (i, h, 0, 0)),
                  pl.BlockSpec((1, hps, LANE, s), lambda i, h, j: (i, h, 0, 0))],
        out_specs=pl.BlockSpec((1, tq, hps * LANE), lambda i, h, j: (i, j, h)),
        out_shape=jax.ShapeDtypeStruct((b, s, heads * LANE), BF16),
        scratch_shapes=[pltpu.VMEM((hps, 1, tq), F32), pltpu.VMEM((hps, 1, tq), F32),
                        pltpu.VMEM((hps, LANE, tq), F32), pltpu.VMEM((2, hps, tq, tq), F32)],
        compiler_params=_cp("parallel", "parallel", "arbitrary"),
        name="mla_prompt_attn",
    )(q, k, vt)


def _mla_sample_kernel(q_ref, cp_ref, krp_ref, cn_ref, krn_ref, wuk_ref, wuv_ref, o_ref,
                       qa_ref, qr_ref, m_ref, l_ref, acc_ref, *, heads, sq):
    kb = pl.program_id(1)

    @pl.when(kb == 0)
    def _():
        for h in range(heads):
            qh = q_ref[0, h]
            qa_ref[h * sq:(h + 1) * sq, :] = _dot_t(qh[:, :LANE], wuk_ref[:, h * LANE:(h + 1) * LANE]).astype(BF16)
            qr_ref[h * sq:(h + 1) * sq, :] = qh[:, LANE:]
        m_ref[...] = jnp.full(m_ref.shape, NEG, F32)
        l_ref[...] = jnp.zeros(l_ref.shape, F32)
        acc_ref[...] = jnp.zeros(acc_ref.shape, F32)

    def update(cb, krb):
        s = _dot_t(qa_ref[...], cb) + _dot_t(qr_ref[...], krb)
        m_new = jnp.maximum(m_ref[...], jnp.max(s, axis=-1, keepdims=True))
        alpha = jnp.exp(m_ref[...] - m_new)
        p = jnp.exp(s - m_new)
        l_ref[...] = l_ref[...] * alpha + jnp.sum(p, axis=-1, keepdims=True)
        acc_ref[...] = acc_ref[...] * alpha + _dot(p.astype(BF16), cb)
        m_ref[...] = m_new

    update(cp_ref[0].astype(BF16), krp_ref[0])

    @pl.when(kb == pl.num_programs(1) - 1)
    def _():
        update(cn_ref[0].astype(BF16), krn_ref[0, 0])
        o_lat = (acc_ref[...] / l_ref[...]).astype(BF16)
        for h in range(heads):
            o_ref[0, :, h * LANE:(h + 1) * LANE] = _dot(
                o_lat[h * sq:(h + 1) * sq], wuv_ref[:, h * LANE:(h + 1) * LANE]).astype(o_ref.dtype)


def _mla_sample_attn(q, c_past, kr_past, c_new, k_new, wuk, wuv, tk=1024):
    b, heads, sq, _ = q.shape
    past, kv_lora = c_past.shape[1:]
    tk = _tile(past, tk)
    kern = functools.partial(_mla_sample_kernel, heads=heads, sq=sq)
    return pl.pallas_call(
        kern,
        grid=(b, past // tk),
        in_specs=[pl.BlockSpec((1, heads, sq, 256), lambda i, j: (i, 0, 0, 0)),
                  pl.BlockSpec((1, tk, kv_lora), lambda i, j: (i, j, 0)),
                  pl.BlockSpec((1, tk, LANE), lambda i, j: (i, j, 0)),
                  pl.BlockSpec((1, sq, kv_lora), lambda i, j: (i, 0, 0)),
                  pl.BlockSpec((1, 1, sq, LANE), lambda i, j: (i, 0, 0, 1)),
                  _resident(wuk.shape), _resident(wuv.shape)],
        out_specs=pl.BlockSpec((1, sq, heads * LANE), lambda i, j: (i, 0, 0)),
        out_shape=jax.ShapeDtypeStruct((b, sq, heads * LANE), BF16),
        scratch_shapes=[pltpu.VMEM((heads * sq, kv_lora), BF16), pltpu.VMEM((heads * sq, LANE), BF16),
                        pltpu.VMEM((heads * sq, 1), F32), pltpu.VMEM((heads * sq, 1), F32),
                        pltpu.VMEM((heads * sq, kv_lora), F32)],
        compiler_params=_cp("parallel", "arbitrary"),
        name="mla_sample_attn",
    )(q, c_past, kr_past, c_new, k_new, wuk, wuv)


def _rwkv_pre_kernel(x_ref, gmix_ref, w_ref, sh0_ref, mu_ref, w0_ref, wb_ref, a0_ref, ab_ref, gb_ref,
                     kkp_ref, ka_ref, rk_ref, e_ref, et_ref,
                     r_ref, dec_ref, k_ref, v_ref, kk_ref, b_ref, g_ref, bonus_ref, sh_ref, last_ref, *, dim):
    si = pl.program_id(1)

    def put(ref, val):
        nb, ts, tiles, _ = ref.shape
        ref[...] = val.reshape(nb * ts, tiles, LANE).reshape(nb, ts, tiles, LANE)

    u = _rms(x_ref[0], gmix_ref[...]).astype(BF16)
    tm = u.shape[0]
    width = w_ref.shape[1]

    @pl.when(si == 0)
    def _():
        last_ref[...] = sh0_ref[0]

    def shifted(c0, c1):
        p = _dot(u, w_ref[:, c0:c1])
        row = lax.broadcasted_iota(jnp.int32, p.shape, 0)
        prev = jnp.where(row == 0, last_ref[:, c0:c1], pltpu.roll(p, 1, axis=0))
        last_ref[:, c0:c1] = p[tm - 1:tm, :]
        sh_ref[0, :, c0:c1] = p[tm - 1:tm, :]
        return p + mu_ref[:, c0:c1] * (prev - p)

    lora = shifted(3 * dim, width)
    k = shifted(dim, 2 * dim)
    dw = lora[:, :LANE]
    da = lora[:, LANE:2 * LANE]
    dg = lora[:, 2 * LANE:]
    z = -(w0_ref[...] + _dot(jnp.tanh(dw).astype(BF16), wb_ref[...]))
    a = _sigmoid(a0_ref[...] + _dot(da.astype(BF16), ab_ref[...]))
    g_ref[0] = _dot(_sigmoid(dg).astype(BF16), gb_ref[...])
    r = shifted(0, dim)
    softplus = jnp.maximum(z, 0.0) + jnp.log(1.0 + jnp.exp(-jnp.abs(z)))
    put(dec_ref, jnp.exp(-jnp.exp(-softplus - 0.5)))
    kk = k * kkp_ref[...]
    nrm = jnp.sqrt(_seg_sum(kk * kk, e_ref[...], et_ref[...]))
    v = shifted(2 * dim, 3 * dim)
    kk = kk / jnp.maximum(nrm, 1e-12)
    k2 = k * (1.0 + (a - 1.0) * ka_ref[...])
    put(k_ref, k2)
    put(kk_ref, kk)
    put(b_ref, kk * a)
    put(r_ref, r)
    put(v_ref, v)
    bonus_ref[0] = _seg_sum(r * k2 * rk_ref[...], e_ref[...], et_ref[...]) * v


def _rwkv_pre(x, gmix, w, sh0, mu, w0, wb, a0, ab, gb, kkp, ka, rk, e, et, tm=256):
    b, s, d = x.shape
    dim = w0.shape[1]
    width = w.shape[1]
    tm = _tile(s, tm)
    ts = min(SCAN_TS, s)
    assert tm % ts == 0 and dim % LANE == 0
    row = lambda i, j: (i, j, 0)
    act = pl.BlockSpec((1, tm, dim), row)
    act_shape = jax.ShapeDtypeStruct((b, s, dim), F32)
    tiles = dim // LANE
    assert tiles % 8 == 0
    seq = pl.BlockSpec((tm // ts, ts, tiles, LANE), lambda i, j: (j, 0, i, 0))
    seq_shape = jax.ShapeDtypeStruct((s // ts, ts, b * tiles, LANE), F32)
    consts = (gmix, w, mu, w0, wb, a0, ab, gb, kkp, ka, rk, e, et)
    return pl.pallas_call(
        functools.partial(_rwkv_pre_kernel, dim=dim),
        grid=(b, s // tm),
        in_specs=[pl.BlockSpec((1, tm, d), row), _resident(gmix.shape), _resident(w.shape),
                  pl.BlockSpec((1, 1, width), lambda i, j: (i, 0, 0))] + [_resident(c.shape) for c in consts[2:]],
        out_specs=[seq] * 6 + [act] * 2 + [pl.BlockSpec((1, 1, width), lambda i, j: (i, 0, 0))],
        out_shape=[seq_shape] * 6 + [act_shape] * 2 + [jax.ShapeDtypeStruct((b, 1, width), F32)],
        scratch_shapes=[pltpu.VMEM((1, width), F32)],
        compiler_params=_cp("parallel", "arbitrary"),
        name="rwkv_pre",
    )(x, gmix, w, sh0, *consts[2:])


def _rwkv_scan_kernel(r_ref, dec_ref, k_ref, v_ref, kk_ref, b_ref, s0_ref, o_ref, st_ref, xt_ref, ot_ref,
                      *, ts, n, unroll):
    tb = pl.program_id(1)
    hpl = LANE // n

    @pl.when(tb == 0)
    def _():
        st_ref[...] = s0_ref[...]

    srcs = (r_ref, dec_ref, k_ref, v_ref, kk_ref, b_ref)
    for i in range(ts):
        for a, ref in enumerate(srcs):
            xt_ref[a, i] = ref[i].T

    def step(i, c):
        for h2 in range(hpl):
            base = h2 * n
            vv = xt_ref[3, i, base:base + n, :]

            def p1(kx, sa):
                return sa + st_ref[h2, kx] * xt_ref[4, i, pl.ds(base + kx, 1), :]

            sa = lax.fori_loop(0, n, p1, jnp.zeros_like(vv), unroll=unroll)

            def p2(kx, o):
                s_new = (st_ref[h2, kx] * xt_ref[1, i, pl.ds(base + kx, 1), :]
                         - sa * xt_ref[5, i, pl.ds(base + kx, 1), :] + vv * xt_ref[2, i, pl.ds(base + kx, 1), :])
                st_ref[h2, kx] = s_new
                return o + s_new * xt_ref[0, i, pl.ds(base + kx, 1), :]

            ot_ref[i, base:base + n, :] = lax.fori_loop(0, n, p2, jnp.zeros_like(vv), unroll=unroll)
        return c

    lax.fori_loop(0, ts, step, 0)
    for i in range(ts):
        o_ref[i] = ot_ref[i].T


def _rwkv_scan(r, dec, k, v, kk, bb, s0):
    nt, ts, streams, _ = r.shape
    groups, hpl, n = s0.shape[:3]
    assert streams == groups * LANE
    seq = pl.BlockSpec((None, ts, LANE, LANE), lambda g, t: (t, 0, g, 0))
    state = pl.BlockSpec((None, hpl, n, n, LANE), lambda g, t: (g, 0, 0, 0, 0))
    return pl.pallas_call(
        functools.partial(_rwkv_scan_kernel, ts=ts, n=n, unroll=SCAN_UNROLL),
        grid=(groups, nt),
        in_specs=[seq] * 6 + [state],
        out_specs=[seq, state],
        out_shape=[jax.ShapeDtypeStruct(r.shape, F32), jax.ShapeDtypeStruct(s0.shape, F32)],
        scratch_shapes=[pltpu.VMEM((6, ts, LANE, LANE), F32), pltpu.VMEM((ts, LANE, LANE), F32)],
        compiler_params=_cp("parallel", "arbitrary"),
        name="rwkv_scan",
    )(r, dec, k, v, kk, bb, s0)


def _state_to_streams(wkv):
    b, h, n = wkv.shape[:3]
    hpl = LANE // n
    tiles = h // hpl
    bg = LANE // tiles
    g = b // bg
    t = wkv.reshape(g, bg, tiles, hpl, n, n)
    return t.transpose(0, 3, 5, 4, 1, 2).reshape(g, hpl, n, n, LANE)


def _state_from_streams(st, b, h):
    g, hpl, n = st.shape[:3]
    tiles = h // hpl
    bg = LANE // tiles
    t = st.reshape(g, hpl, n, n, bg, tiles)
    return t.transpose(0, 4, 5, 1, 3, 2).reshape(b, h, n, n)


def _mix_kernel(o_ref, bonus_ref, g_ref, ymla_ref, grw_ref, gmla_ref, lng_ref, lnb_ref, e_ref, et_ref,
                wr_ref, wm_ref, out_ref, *, inv_n):
    nb, ts, tiles, _ = o_ref.shape
    o = o_ref[...].reshape(nb * ts, tiles, LANE).reshape(nb * ts, tiles * LANE)
    e, et = e_ref[...], et_ref[...]
    d = o - _seg_sum(o, e, et) * inv_n
    var = _seg_sum(d * d, e, et) * inv_n
    on = d * lax.rsqrt(var + GN_EPS) * lng_ref[...] + lnb_ref[...]
    y = ((on + bonus_ref[0]) * g_ref[0]).astype(BF16)
    merged = grw_ref[0] * _dot(y, wr_ref[...]) + gmla_ref[0] * _dot(ymla_ref[0], wm_ref[...])
    out_ref[0] = merged.astype(out_ref.dtype)


def _mix(o, bonus, g, ymla, gates, lng, lnb, e, et, wr, wm, head_dim, tm=256):
    nt, ts = o.shape[:2]
    b, s, dim = bonus.shape
    tiles = dim // LANE
    d = wr.shape[1]
    md = ymla.shape[2]
    tm = _tile(s, tm)
    assert tm % ts == 0
    row = lambda i, j: (i, j, 0)
    return pl.pallas_call(
        functools.partial(_mix_kernel, inv_n=1.0 / head_dim),
        grid=(b, s // tm),
        in_specs=[pl.BlockSpec((tm // ts, ts, tiles, LANE), lambda i, j: (j, 0, i, 0)),
                  pl.BlockSpec((1, tm, dim), row), pl.BlockSpec((1, tm, dim), row), pl.BlockSpec((1, tm, md), row),
                  pl.BlockSpec((1, tm, d), lambda i, j: (i, j, 0)), pl.BlockSpec((1, tm, d), lambda i, j: (i, j, 1))]
                 + [_resident(c.shape) for c in (lng, lnb, e, et, wr, wm)],
        out_specs=pl.BlockSpec((1, tm, d), row),
        out_shape=jax.ShapeDtypeStruct((b, s, d), BF16),
        compiler_params=_cp("parallel", "parallel"),
        name="rwkv_post_mix",
    )(o, bonus, g, ymla, gates, gates, lng, lnb, e, et, wr, wm)


def _outproj_kernel(x_ref, m_ref, wo_ref, gc_ref, wq_ref, h_ref, q_ref):
    h = x_ref[...] + _dot(m_ref[...], wo_ref[...])
    h_ref[...] = h
    q_ref[...] = _dot(_rms(h, gc_ref[...]).astype(BF16), wq_ref[...]).astype(q_ref.dtype)


def _outproj(x, merged, wo, gc, wq, tm=256):
    t, d = x.shape
    tm = _tile(t, tm)
    row = lambda i: (i, 0)
    return pl.pallas_call(
        _outproj_kernel,
        grid=(t // tm,),
        in_specs=[pl.BlockSpec((tm, d), row), pl.BlockSpec((tm, d), row),
                  _resident(wo.shape), _resident(gc.shape), _resident(wq.shape)],
        out_specs=[pl.BlockSpec((tm, d), row), pl.BlockSpec((tm, d), row)],
        out_shape=[jax.ShapeDtypeStruct((t, d), F32), jax.ShapeDtypeStruct((t, d), BF16)],
        compiler_params=_cp("parallel"),
        name="outproj_crossq",
    )(x, merged, wo, gc, wq)


def _route_tile(lg, n_groups, per_group):
    n_exp = n_groups * per_group
    lane = lax.broadcasted_iota(jnp.int32, lg.shape, 1)
    first = lambda mask: jnp.min(jnp.where(mask, lane, 2 * LANE), axis=-1, keepdims=True)
    top = lambda mask: jnp.max(jnp.where(mask, lg, NEG), axis=-1, keepdims=True)
    is_g = lane < n_groups
    gmax = top(is_g)
    g_sel = first(is_g & (lg == gmax))
    p_grp = 1.0 / jnp.sum(jnp.where(is_g, jnp.exp(lg - gmax), 0.0), axis=-1, keepdims=True)
    eid = lane - n_groups
    in_grp = (eid >= g_sel * per_group) & (eid < (g_sel + 1) * per_group)
    v1 = top(in_grp)
    e1 = first(in_grp & (lg == v1))
    rest = in_grp & (lane != e1)
    v2 = top(rest)
    e2 = first(rest & (lg == v2))
    t = jnp.exp(v2 - v1)
    w1 = p_grp / (1.0 + t)
    w2 = w1 * t
    route = jnp.where(lane == 0, (e1 - n_groups).astype(F32),
                      jnp.where(lane == 1, (e2 - n_groups).astype(F32),
                                jnp.where(lane == 2, w1, jnp.where(lane == 3, w2, 0.0))))
    hits = jnp.where((lane == e1 - n_groups) | (lane == e2 - n_groups), 1.0, 0.0)
    assert n_exp <= LANE
    return route, jnp.sum(hits, axis=0, keepdims=True)


def _cross_kernel(q_ref, mk_ref, mv_ref, h_ref, wo_ref, gm_ref, wr_ref, br_ref, hn_all_ref, h2_ref, hn_ref, rt_ref,
                  cnt_ref, *, heads, scale, n_groups, per_group):
    del hn_all_ref
    q = q_ref[0]
    mk = mk_ref[0]
    mv = mv_ref[0]
    hd = q.shape[1] // heads
    outs = []
    for h in range(heads):
        sl = slice(h * hd, (h + 1) * hd)
        s = _dot_t(q[:, sl], mk[:, sl]) * scale
        p = jnp.exp(s - jnp.max(s, axis=-1, keepdims=True))
        p = p / jnp.sum(p, axis=-1, keepdims=True)
        outs.append(_dot(p.astype(BF16), mv[:, sl]).astype(BF16))
    o = jnp.concatenate(outs, axis=1)
    h2 = h_ref[0] + _dot(o, wo_ref[...])
    h2_ref[0] = h2
    hn = _rms(h2, gm_ref[...]).astype(BF16)
    hn_ref[...] = hn
    route, hist = _route_tile(_dot(hn, wr_ref[...]) + br_ref[...], n_groups, per_group)
    rt_ref[0] = route
    cnt_ref[0, 0] = jnp.broadcast_to(hist, cnt_ref.shape[2:])


def _cross(q, mk, mv, h, wo, gm, wr, br, heads, n_groups, per_group, total_rows, row_off, hn_all, tm=512):
    b, s, d = h.shape
    n_mem = mk.shape[1]
    tm = _tile(s, tm)
    assert row_off % tm == 0
    off, per_b = row_off // tm, s // tm
    row = lambda i, j: (i, j, 0)
    mem = pl.BlockSpec((1, n_mem, d), lambda i, j: (i, 0, 0))
    kern = functools.partial(_cross_kernel, heads=heads, scale=float(d // heads) ** -0.5,
                             n_groups=n_groups, per_group=per_group)
    in_specs = [pl.BlockSpec((1, tm, d), row), mem, mem, pl.BlockSpec((1, tm, d), row),
                _resident(wo.shape), _resident(gm.shape), _resident(wr.shape), _resident(br.shape),
                pl.BlockSpec(memory_space=pl.ANY)]
    args = (q, mk, mv, h, wo, gm, wr, br, hn_all)
    return pl.pallas_call(
        kern,
        grid=(b, s // tm),
        in_specs=in_specs,
        out_specs=[pl.BlockSpec((1, tm, d), row), pl.BlockSpec((tm, d), lambda i, j: (off + i * per_b + j, 0)),
                   pl.BlockSpec((1, tm, LANE), row), pl.BlockSpec((1, 1, 8, LANE), lambda i, j: (i, j, 0, 0))],
        out_shape=[jax.ShapeDtypeStruct((b, s, d), F32), jax.ShapeDtypeStruct((total_rows, d), BF16),
                   jax.ShapeDtypeStruct((b, s, LANE), F32), jax.ShapeDtypeStruct((b, s // tm, 8, LANE), F32)],
        input_output_aliases={len(args) - 1: 1},
        compiler_params=_cp("parallel", "parallel"),
        name="cross_attn_router",
    )(*args)


def _moe_kernel(be_ref, on_ref, first_ref, nxt_ref, par_ref, x_ref, wg_hbm, wu_hbm, wd_hbm, y_ref,
                wgf_ref, wuf_ref, wdf_ref, wgb_ref, wub_ref, wdb_ref, sem_ref):
    i = pl.program_id(0)

    def fetch(e, slot):
        return (pltpu.make_async_copy(wg_hbm.at[e], wgf_ref.at[slot], sem_ref.at[slot, 0]),
                pltpu.make_async_copy(wu_hbm.at[e], wuf_ref.at[slot], sem_ref.at[slot, 1]),
                pltpu.make_async_copy(wd_hbm.at[e], wdf_ref.at[slot], sem_ref.at[slot, 2]))

    @pl.when(first_ref[i] != 0)
    def _():
        slot = par_ref[i]

        @pl.when(i == 0)
        def _():
            for c in fetch(be_ref[0], 0):
                c.start()

        for c in fetch(be_ref[i], slot):
            c.wait()
        wgb_ref[...] = wgf_ref[slot].astype(BF16)
        wub_ref[...] = wuf_ref[slot].astype(BF16)
        wdb_ref[...] = wdf_ref[slot].astype(BF16)

        @pl.when(nxt_ref[i] >= 0)
        def _():
            for c in fetch(nxt_ref[i], 1 - slot):
                c.start()

    @pl.when(on_ref[i] != 0)
    def _():
        xb = x_ref[...]
        g = _dot(xb, wgb_ref[...])
        u = _dot(xb, wub_ref[...])
        hb = (g * _sigmoid(g) * u).astype(BF16)
        y_ref[...] = _dot(hb, wdb_ref[...])

    @pl.when(on_ref[i] == 0)
    def _():
        y_ref[...] = jnp.zeros(y_ref.shape, y_ref.dtype)


def _moe_experts(blk_e, blk_on, counts, x_slot, wg, wu, wd):
    ns, d = x_slot.shape
    n_exp, _, f = wg.shape
    n_blk = ns // MOE_ROWS
    prev_e = jnp.concatenate([jnp.full((1,), -1, jnp.int32), blk_e[:-1]])
    first = ((blk_on != 0) & (blk_e != prev_e)).astype(jnp.int32)
    par = ((jnp.cumsum(first) - 1) % 2).astype(jnp.int32)
    ids = jnp.where(counts > 0, jnp.arange(n_exp, dtype=jnp.int32), n_exp)
    nxt_tab = jnp.concatenate([lax.cummin(ids[::-1])[::-1][1:], jnp.full((1,), n_exp, jnp.int32)])
    nxt = nxt_tab[blk_e]
    nxt = jnp.where(nxt < n_exp, nxt, -1).astype(jnp.int32)
    hbm = pl.BlockSpec(memory_space=pl.ANY)
    grid_spec = pltpu.PrefetchScalarGridSpec(
        num_scalar_prefetch=5,
        grid=(n_blk,),
        in_specs=[pl.BlockSpec((MOE_ROWS, d), lambda i, *_: (i, 0)), hbm, hbm, hbm],
        out_specs=pl.BlockSpec((MOE_ROWS, d), lambda i, *_: (i, 0)),
        scratch_shapes=[pltpu.VMEM((2, d, f), F32), pltpu.VMEM((2, d, f), F32), pltpu.VMEM((2, f, d), F32),
                        pltpu.VMEM((d, f), BF16), pltpu.VMEM((d, f), BF16), pltpu.VMEM((f, d), BF16),
                        pltpu.SemaphoreType.DMA((2, 3))],
    )
    return pl.pallas_call(
        _moe_kernel,
        grid_spec=grid_spec,
        out_shape=jax.ShapeDtypeStruct((ns, d), F32),
        compiler_params=_cp("arbitrary"),
        name="moe_experts",
    )(blk_e, blk_on, first, nxt, par, x_slot, wg, wu, wd)


def _combine_kernel(idx_ref, nidx_ref, h_ref, rt_ref, g_ref, y_hbm, o_ref, buf_ref, sem_ref, *, tm, norm):
    i = pl.program_id(0)
    n = pl.num_programs(0)

    def issue(ids_ref, slot):
        def one(r, c):
            for ch in range(TOP_K):
                src = ids_ref[0, TOP_K * r + ch]
                pltpu.make_async_copy(y_hbm.at[pl.ds(src, 1)], buf_ref.at[slot, ch, pl.ds(r, 1)],
                                      sem_ref.at[slot, ch]).start()
            return c

        lax.fori_loop(0, tm, one, 0, unroll=8)

    cur = i % 2

    @pl.when(i == 0)
    def _():
        issue(idx_ref, 0)

    @pl.when(i + 1 < n)
    def _():
        issue(nidx_ref, 1 - cur)

    acc = h_ref[...]
    for ch in range(TOP_K):
        pltpu.make_async_copy(y_hbm.at[pl.ds(0, tm)], buf_ref.at[cur, ch], sem_ref.at[cur, ch]).wait()
        acc = acc + buf_ref[cur, ch] * rt_ref[:, TOP_K + ch:TOP_K + ch + 1]
    o_ref[...] = _rms(acc, g_ref[...]) if norm else acc


def _combine(h, y_slot, slot_of, route, row_off, g, tm=512):
    t, d = h.shape
    tm = _tile(t, tm)
    assert row_off % tm == 0
    off = row_off // tm
    nb = t // tm
    idx = slot_of[row_off * TOP_K:(row_off + t) * TOP_K].reshape(nb, 1, TOP_K * tm)
    norm = g is not None
    g = g if norm else jnp.ones((1, d), F32)
    ids = lambda f: pl.BlockSpec((None, 1, TOP_K * tm), f, memory_space=pltpu.SMEM)
    return pl.pallas_call(
        functools.partial(_combine_kernel, tm=tm, norm=norm),
        grid=(nb,),
        in_specs=[ids(lambda i: (i, 0, 0)), ids(lambda i: (jnp.minimum(i + 1, nb - 1), 0, 0)),
                  pl.BlockSpec((tm, d), lambda i: (i, 0)),
                  pl.BlockSpec((tm, LANE), lambda i: (i + off, 0)),
                  pl.BlockSpec((1, d), lambda i: (0, 0)),
                  pl.BlockSpec(memory_space=pl.ANY)],
        out_specs=pl.BlockSpec((tm, d), lambda i: (i, 0)),
        out_shape=jax.ShapeDtypeStruct((t, d), F32),
        scratch_shapes=[pltpu.VMEM((2, TOP_K, tm, d), F32), pltpu.SemaphoreType.DMA((2, TOP_K))],
        compiler_params=_cp("arbitrary"),
        name="moe_combine_norm",
    )(idx, idx, h, route, g, y_slot)


def _pad_cols(w, n):
    return jnp.pad(w, ((0, 0),) * (w.ndim - 1) + ((0, n - w.shape[-1]),))


def _pad_rows(w, n):
    return jnp.pad(w, ((0, n - w.shape[0]), (0, 0)))


def _rot_half_cols(w):
    half = w.shape[-1] // 2
    return jnp.concatenate([-w[..., half:], w[..., :half]], axis=-1)


def _rope_table(pos):
    half = ROPE // 2
    inv = ROPE_THETA ** (-jnp.arange(half, dtype=F32) / half)
    ang = pos.astype(F32)[:, None] * inv[None, :]
    cos, sin = jnp.cos(ang), jnp.sin(ang)
    return jnp.concatenate([cos, cos, sin, sin], axis=1)


def _prep_layer(lp, dims):
    ql, kl, rope, dim, dl, il, gl, d = (dims[k] for k in ("q_lora", "kv_lora", "rope", "rw_dim", "decay_lora",
                                                          "iclr_lora", "gate_lora", "d_model"))
    w_in = lp["w_in"]
    i0, i1, i2 = ql, ql + kl, ql + kl + rope
    i3 = i2 + 3 * dim + dl + il + gl
    w_kr = w_in[:, i1:i2]
    out = {}
    out["w_mla"] = jnp.concatenate(
        [w_in[:, :i0], w_kr, _rot_half_cols(w_kr), w_in[:, i0:i1]], axis=1).astype(BF16)
    rw = w_in[:, i2:i3]
    c3 = 3 * dim
    glp = -(-gl // LANE) * LANE

    def regroup(t):
        return jnp.concatenate([t[..., :c3], _pad_cols(t[..., c3:c3 + dl], LANE),
                                _pad_cols(t[..., c3 + dl:c3 + dl + il], LANE),
                                _pad_cols(t[..., c3 + dl + il:], glp)], axis=-1)

    out["regroup"] = regroup
    out["w_rw"] = regroup(rw).astype(BF16)
    out["mu"] = regroup(lp["rw_mu"][None, :])
    out["w_gates"] = w_in[:, i3:].astype(BF16)
    heads = lp["w_uq"].shape[1]
    nope = lp["w_uq"].shape[2] - rope
    wq = lp["w_uq"]
    out["w_q"] = jnp.concatenate([wq[..., :nope], wq[..., nope:], _rot_half_cols(wq[..., nope:])],
                                 axis=-1).reshape(ql, heads * 256).astype(BF16)
    out["w_uk"] = lp["w_uk"].reshape(kl, -1).astype(BF16)
    out["w_uv"] = lp["w_uv"].reshape(kl, -1).astype(BF16)
    out["w_uv_t"] = out["w_uv"].T
    out["wb"] = _pad_rows(lp["rw_wb"], LANE).astype(BF16)
    out["ab"] = _pad_rows(lp["rw_ab"], LANE).astype(BF16)
    out["gb"] = _pad_rows(lp["rw_gb"], glp).astype(BF16)
    n_heads = dims["rw_heads"]
    hd = dim // n_heads
    e = (jnp.arange(dim)[:, None] // hd == jnp.arange(LANE)[None, :]).astype(BF16)
    out["e"], out["et"] = e, e.T
    out["w_br_rwkv"] = lp["w_br_rwkv"].astype(BF16)
    out["w_br_mla"] = lp["w_br_mla"].astype(BF16)
    out["w_out"] = lp["w_out"].astype(BF16)
    out["w_mq"] = lp["w_mq"].reshape(d, d).astype(BF16)
    out["w_mk"] = lp["w_mk"].reshape(d, d).astype(BF16)
    out["w_mv"] = lp["w_mv"].reshape(d, d).astype(BF16)
    out["w_mo"] = lp["w_mo"].reshape(d, d).astype(BF16)
    n_g, n_e = lp["w_rg"].shape[1], lp["w_re"].shape[1]
    out["w_r"] = _pad_cols(jnp.concatenate([lp["w_rg"], lp["w_re"]], axis=1), LANE).astype(BF16)
    out["b_r"] = _pad_cols(jnp.concatenate([lp["b_rg"], lp["b_re"]])[None, :], LANE)
    out["w_eg"], out["w_eu"], out["w_ed"] = lp["w_eg"], lp["w_eu"], lp["w_ed"]
    return out


def _row(v):
    return v.reshape(1, -1)


def _mixer(x, lp, wp, dims, pos, shift0, wkv0, attend, base2=False):
    b, s, d = x.shape
    heads, rw_heads = dims["mla_heads"], dims["rw_heads"]
    gmix = _row(lp["g_mix"])
    cs = _rope_table(pos)
    scale = float(dims["nope"] + dims["rope"]) ** -0.5 * (LOG2E if base2 else 1.0)
    q, k, v, c, kr = _mla_proj(x, gmix, wp["w_mla"], _row(lp["g_q"]), wp["w_q"], _row(lp["g_kv"]),
                               wp["w_uk"], wp["w_uv_t"], cs, heads, scale)
    y_mla = attend(q, k, v, c)
    sh0 = wp["regroup"](shift0)
    r, dec, k2, vr, kk, bb, g, bonus, sh = _rwkv_pre(
        x, gmix, wp["w_rw"], sh0, wp["mu"], _row(lp["rw_w0"]), wp["wb"], _row(lp["rw_a0"]), wp["ab"], wp["gb"],
        _row(lp["rw_kk"]), _row(lp["rw_ka"]), _row(lp["rw_rk"]), wp["e"], wp["et"])
    n = dims["rw_dim"] // rw_heads
    o, s_t = _rwkv_scan(r, dec, k2, vr, kk, bb, _state_to_streams(wkv0))
    wkv = _state_from_streams(s_t, b, rw_heads)
    gates, = _proj(x.reshape(b * s, d), gmix, [wp["w_gates"]], 512, out_dtype=BF16, gate=True)
    gates = gates.reshape(b, s, 2 * d)
    dim = dims["rw_dim"]
    merged = _mix(o, bonus, g, y_mla, gates, _row(lp["rw_ln_g"]), _row(lp["rw_ln_b"]), wp["e"], wp["et"],
                  wp["w_br_rwkv"], wp["w_br_mla"], n).reshape(b * s, d)
    h, qc = _outproj(x.reshape(b * s, d), merged, wp["w_out"], _row(lp["g_cross"]), wp["w_mq"])
    c3, dl, il = 3 * dim, dims["decay_lora"], dims["iclr_lora"]
    shift = jnp.concatenate([sh[..., :c3], sh[..., c3:c3 + dl], sh[..., c3 + LANE:c3 + LANE + il],
                             sh[..., c3 + 2 * LANE:c3 + 2 * LANE + dims["gate_lora"]]], axis=-1)
    return h.reshape(b, s, d), qc.reshape(b, s, d), c, kr, k, wkv, shift


def _moe(hn, route, counts, wp):
    t, d = hn.shape
    n_exp = counts.shape[0]
    e_idx = route[:, :TOP_K]
    m = t * TOP_K
    n_blk = -(-(m + n_exp * (MOE_ROWS - 1)) // MOE_ROWS)
    ns = n_blk * MOE_ROWS
    flat_e = e_idx.reshape(m).astype(jnp.int32)
    order = jnp.argsort(flat_e).astype(jnp.int32)
    padded = (counts + MOE_ROWS - 1) // MOE_ROWS * MOE_ROWS
    pad_end = jnp.cumsum(padded)
    pad_start = pad_end - padded
    start = jnp.cumsum(counts) - counts
    blk_start = jnp.arange(n_blk, dtype=jnp.int32) * MOE_ROWS
    blk_e = jnp.minimum(jnp.sum(pad_end[None, :] <= blk_start[:, None], axis=1), n_exp - 1).astype(jnp.int32)
    blk_on = (blk_start < pad_end[-1]).astype(jnp.int32)
    off = (blk_start - pad_start[blk_e])[:, None] + jnp.arange(MOE_ROWS, dtype=jnp.int32)[None, :]
    valid = ((off < counts[blk_e][:, None]) & (blk_on[:, None] != 0)).reshape(ns)
    asg = order[jnp.clip(start[blk_e][:, None] + off, 0, m - 1).reshape(ns)]
    slot_tok = jnp.where(valid, asg // TOP_K, jnp.arange(ns, dtype=jnp.int32) % t)
    y_slot = _moe_experts(blk_e, blk_on, counts, hn[slot_tok], wp["w_eg"], wp["w_eu"], wp["w_ed"])
    key = jnp.where(valid, asg, m + jnp.arange(ns, dtype=jnp.int32))
    slot_of = jnp.argsort(key)[:m].astype(jnp.int32)
    return y_slot, slot_of


def kernel(x_prompt, x_sample, mem_prompt, cache_kv_latent, cache_k_rope, cache_mem_k, cache_mem_v, state_wkv, state_shift, g_mix, w_in, g_q, w_uq, g_kv, w_uk, w_uv, rw_mu, rw_w0, rw_wb, rw_a0, rw_ab, rw_gb, rw_kk, rw_ka, rw_rk, rw_ln_g, rw_ln_b, w_br_rwkv, w_br_mla, w_out, g_cross, g_mem, w_mq, w_mk, w_mv, w_mo, g_moe, w_rg, b_rg, w_re, b_re, w_eg, w_eu, w_ed, g_final):
    depth = g_mix.shape[0]
    b, s, d = x_prompt.shape
    bs, ss, _ = x_sample.shape
    past = cache_kv_latent.shape[2]
    n_mem = mem_prompt.shape[1]
    mem_heads = w_mq.shape[2]
    rw_heads, rw_hd = rw_rk.shape[1:]
    dims = dict(d_model=d, q_lora=g_q.shape[1], kv_lora=g_kv.shape[1], mla_heads=w_uq.shape[2],
                nope=w_uk.shape[3], rope=cache_k_rope.shape[3], rw_heads=rw_heads, rw_dim=rw_heads * rw_hd,
                decay_lora=rw_wb.shape[1], iclr_lora=rw_ab.shape[1], gate_lora=rw_gb.shape[1])
    assert dims["rope"] == ROPE and dims["nope"] == LANE and w_uv.shape[3] == LANE
    assert dims["decay_lora"] <= LANE and dims["iclr_lora"] <= LANE and rw_heads <= LANE
    n_groups, n_exp = w_rg.shape[2], w_re.shape[2]
    per_group = n_exp // n_groups
    stacked = dict(g_mix=g_mix, w_in=w_in, g_q=g_q, w_uq=w_uq, g_kv=g_kv, w_uk=w_uk, w_uv=w_uv, rw_mu=rw_mu,
                   rw_w0=rw_w0, rw_wb=rw_wb, rw_a0=rw_a0, rw_ab=rw_ab, rw_gb=rw_gb, rw_kk=rw_kk, rw_ka=rw_ka,
                   rw_rk=rw_rk, rw_ln_g=rw_ln_g, rw_ln_b=rw_ln_b, w_br_rwkv=w_br_rwkv, w_br_mla=w_br_mla,
                   w_out=w_out, g_cross=g_cross, g_mem=g_mem, w_mq=w_mq, w_mk=w_mk, w_mv=w_mv, w_mo=w_mo,
                   g_moe=g_moe, w_rg=w_rg, b_rg=b_rg, w_re=w_re, b_re=b_re, w_eg=w_eg, w_eu=w_eu, w_ed=w_ed)
    h_p, h_s = x_prompt, x_sample
    outs = [[] for _ in range(10)]
    for l in range(depth):
        lp = {name: val[l] for name, val in stacked.items()}
        wp = _prep_layer(lp, dims)

        h, qc, c, kr, _, wkv, shift = _mixer(
            h_p, lp, wp, dims, jnp.arange(s), jnp.zeros((b, 1, state_shift.shape[3]), F32),
            jnp.zeros((b, rw_heads, rw_hd, rw_hd), F32), lambda q, k, v, c_new: _mla_prompt_attn(q, k, v),
            base2=True)
        mk, mk_b, mv, mv_b = _proj(mem_prompt.reshape(b * n_mem, d), _row(lp["g_mem"]), [wp["w_mk"], wp["w_mv"]], 256,
                                   heads=mem_heads, twin=True)
        gm, wr, br = _row(lp["g_moe"]), wp["w_r"], wp["b_r"]
        tp, tsm = b * s, bs * ss
        h2_p, hn, rt_p, cnt_p = _cross(qc, mk_b.reshape(b, n_mem, d), mv_b.reshape(b, n_mem, d), h, wp["w_mo"], gm, wr,
                                       br, mem_heads, n_groups, per_group, tp + tsm, 0,
                                       hn_all=jnp.zeros((tp + tsm, d), BF16))
        for lst, val in zip(outs[:6], (c, kr, mk.reshape(b, n_mem, mem_heads, -1),
                                       mv.reshape(b, n_mem, mem_heads, -1), wkv, shift)):
            lst.append(val)

        kr_past = jnp.pad(cache_k_rope[l], ((0, 0), (0, 0), (0, LANE - ROPE))).astype(BF16)

        def attend_sample(q, k, v, c_new, l=l, kr_past=kr_past, wp=wp):
            return _mla_sample_attn(q, cache_kv_latent[l], kr_past, c_new, k, wp["w_uk"], wp["w_uv"])

        h, qc, c, kr, _, wkv, shift = _mixer(h_s, lp, wp, dims, past + jnp.arange(ss), state_shift[l],
                                             state_wkv[l], attend_sample)
        h2_s, hn, rt_s, cnt_s = _cross(qc, cache_mem_k[l].reshape(bs, n_mem, d).astype(BF16),
                                       cache_mem_v[l].reshape(bs, n_mem, d).astype(BF16), h, wp["w_mo"], gm, wr, br,
                                       mem_heads, n_groups, per_group, tp + tsm, tp, hn_all=hn)
        for lst, val in zip(outs[6:], (c, kr, wkv, shift)):
            lst.append(val)

        route = jnp.concatenate([rt_p.reshape(tp, LANE), rt_s.reshape(tsm, LANE)], axis=0)
        counts = (jnp.sum(cnt_p[:, :, 0, :n_exp], axis=(0, 1)) + jnp.sum(cnt_s[:, :, 0, :n_exp], axis=(0, 1)))
        y_slot, slot_of = _moe(hn, route, counts.astype(jnp.int32), wp)
        g_fin = _row(g_final) if l == depth - 1 else None
        h_p = _combine(h2_p.reshape(tp, d), y_slot, slot_of, route, 0, g_fin).reshape(b, s, d)
        h_s = _combine(h2_s.reshape(tsm, d), y_slot, slot_of, route, tp, g_fin).reshape(bs, ss, d)
    stacks = [jnp.stack(o) for o in outs]
    return (h_p, h_s, *stacks)
```

```python
import functools

import jax
import jax.numpy as jnp
from jax import lax
from jax.experimental import pallas as pl
from jax.experimental.pallas import tpu as pltpu

F32 = jnp.float32
BF16 = jnp.bfloat16

EPS = 1e-6
GN_EPS = 64e-5
ROPE_THETA = 10000.0
CHUNK = 64
TOP_K = 2
NEG = -1e30

LANE = 128
V7X_VMEM_LIMIT = 56 << 20
ROPE = 64
MOE_ROWS = 256
SCAN_TS = 16
SCAN_UNROLL = 32


def _cp(*sem):
    return pltpu.CompilerParams(dimension_semantics=sem, vmem_limit_bytes=V7X_VMEM_LIMIT)


def _tile(n, pref):
    if n <= pref:
        return n
    for t in range(pref, 7, -1):
        if n % t == 0 and t % 8 == 0:
            return t
    return n


def _resident(shape):
    nd = len(shape)
    return pl.BlockSpec(shape, lambda *_: (0,) * nd, pipeline_mode=pl.Buffered(1))


def _rms(x, g):
    return x * lax.rsqrt(jnp.mean(x * x, axis=-1, keepdims=True) + EPS) * g


def _sigmoid(x):
    return 1.0 / (1.0 + jnp.exp(-x))


def _dot(a, b):
    return jnp.dot(a, b, preferred_element_type=F32)


def _dot_t(a, b):
    return lax.dot_general(a, b, (((1,), (1,)), ((), ())), preferred_element_type=F32)


def _seg_sum(x, e, et):
    hi = x.astype(BF16)
    lo = (x - hi.astype(F32)).astype(BF16)
    s = _dot(hi, e) + _dot(lo, e)
    shi = s.astype(BF16)
    slo = (s - shi.astype(F32)).astype(BF16)
    return _dot(shi, et) + _dot(slo, et)


def _proj_kernel(x_ref, g_ref, *refs, n_w, gate, twin, cols):
    w_refs, o_refs = refs[:n_w], refs[n_w:]
    xn = _rms(x_ref[...], g_ref[...]).astype(BF16)
    per_w = 2 if twin else 1
    for i, w_ref in enumerate(w_refs):
        o_ref = o_refs[per_w * i]
        n = w_ref.shape[1]
        for c in range(0, n, cols):
            y = _dot(xn, w_ref[:, c:c + cols])
            if gate:
                y = _sigmoid(y)
            if len(o_ref.shape) == 3:
                hd = o_ref.shape[2]
                o_ref[:, c // hd:(c + cols) // hd, :] = y.reshape(y.shape[0], cols // hd, hd).astype(o_ref.dtype)
            else:
                o_ref[:, c:c + cols] = y.astype(o_ref.dtype)
            if twin:
                o_refs[per_w * i + 1][:, c:c + cols] = y.astype(BF16)


def _proj(x, g, ws, tm, out_dtype=F32, gate=False, heads=None, twin=False, cols=1024):
    m, k = x.shape
    tm = _tile(m, tm)
    out_specs, out_shape = [], []
    for w in ws:
        n = w.shape[1]
        if heads is None:
            out_specs.append(pl.BlockSpec((tm, n), lambda i: (i, 0)))
            out_shape.append(jax.ShapeDtypeStruct((m, n), out_dtype))
        else:
            out_specs.append(pl.BlockSpec((tm, heads, n // heads), lambda i: (i, 0, 0)))
            out_shape.append(jax.ShapeDtypeStruct((m, heads, n // heads), out_dtype))
        if twin:
            out_specs.append(pl.BlockSpec((tm, n), lambda i: (i, 0)))
            out_shape.append(jax.ShapeDtypeStruct((m, n), BF16))
    cols = min(cols, min(w.shape[1] for w in ws))
    return pl.pallas_call(
        functools.partial(_proj_kernel, n_w=len(ws), gate=gate, twin=twin, cols=cols),
        grid=(m // tm,),
        in_specs=[pl.BlockSpec((tm, k), lambda i: (i, 0)), _resident(g.shape)] + [_resident(w.shape) for w in ws],
        out_specs=out_specs,
        out_shape=out_shape,
        compiler_params=_cp("parallel"),
        name="norm_proj",
    )(x, g, *ws)


def _mla_proj_kernel(x_ref, gmix_ref, wm_ref, gq_ref, wq_ref, gkv_ref, wuk_ref, wuv_ref, cs_ref,
                     q_ref, k_ref, v_ref, c_ref, kr_ref, *, heads, q_lora, scale):
    u = _rms(x_ref[0], gmix_ref[...]).astype(BF16)
    p = _dot(u, wm_ref[...])
    cs = cs_ref[...]
    lane = lax.broadcasted_iota(jnp.int32, cs.shape, 1)

    def rope_tile(t2):
        t = t2 * cs
        return jnp.where(lane < ROPE, t + pltpu.roll(t, ROPE, axis=1), 0.0)

    krt = rope_tile(p[:, q_lora:q_lora + LANE])
    kr_ref[0] = krt[:, :ROPE]
    krb = krt.astype(BF16)
    c = _rms(p[:, q_lora + LANE:], gkv_ref[...])
    c_ref[0] = c
    cb = c.astype(BF16)
    kn = _dot(cb, wuk_ref[...])
    vt = _dot_t(wuv_ref[...], cb)
    qn = _rms(p[:, :q_lora], gq_ref[...]).astype(BF16)
    qf = _dot(qn, wq_ref[...])
    for h in range(heads):
        q_ref[0, h, :, :LANE] = (qf[:, h * 256:h * 256 + LANE] * scale).astype(BF16)
        q_ref[0, h, :, LANE:] = (rope_tile(qf[:, h * 256 + LANE:(h + 1) * 256]) * scale).astype(BF16)
        k_ref[0, h, :, :LANE] = kn[:, h * LANE:(h + 1) * LANE].astype(BF16)
        k_ref[0, h, :, LANE:] = krb
        v_ref[0, h] = vt[h * LANE:(h + 1) * LANE, :].astype(BF16)


def _mla_proj(x, gmix, wm, gq, wq, gkv, wuk, wuv, cs, heads, scale, tm=256):
    b, s, d = x.shape
    q_lora = gq.shape[1]
    kv_lora = gkv.shape[1]
    tm = _tile(s, tm)
    kern = functools.partial(_mla_proj_kernel, heads=heads, q_lora=q_lora, scale=scale)
    return pl.pallas_call(
        kern,
        grid=(b, s // tm),
        in_specs=[pl.BlockSpec((1, tm, d), lambda i, j: (i, j, 0)),
                  _resident(gmix.shape), _resident(wm.shape), _resident(gq.shape), _resident(wq.shape),
                  _resident(gkv.shape), _resident(wuk.shape), _resident(wuv.shape),
                  pl.BlockSpec((tm, LANE), lambda i, j: (j, 0))],
        out_specs=[pl.BlockSpec((1, heads, tm, 256), lambda i, j: (i, 0, j, 0)),
                   pl.BlockSpec((1, heads, tm, 256), lambda i, j: (i, 0, j, 0)),
                   pl.BlockSpec((1, heads, LANE, tm), lambda i, j: (i, 0, 0, j)),
                   pl.BlockSpec((1, tm, kv_lora), lambda i, j: (i, j, 0)),
                   pl.BlockSpec((1, tm, ROPE), lambda i, j: (i, j, 0))],
        out_shape=[jax.ShapeDtypeStruct((b, heads, s, 256), BF16),
                   jax.ShapeDtypeStruct((b, heads, s, 256), BF16),
                   jax.ShapeDtypeStruct((b, heads, LANE, s), BF16),
                   jax.ShapeDtypeStruct((b, s, kv_lora), F32),
                   jax.ShapeDtypeStruct((b, s, ROPE), F32)],
        compiler_params=_cp("parallel", "parallel"),
        name="mla_proj",
    )(x, gmix, wm, gq, wq, gkv, wuk, wuv, cs)


def _flash_kernel(q_ref, k_ref, vt_ref, o_ref, m_ref, l_ref, acc_ref, sc_ref, *, tq, hps):
    qi = pl.program_id(2)
    m_ref[...] = jnp.full(m_ref.shape, NEG, F32)
    l_ref[...] = jnp.zeros(l_ref.shape, F32)
    acc_ref[...] = jnp.zeros(acc_ref.shape, F32)

    def scores(j, slot):
        start = pl.multiple_of(j * tq, tq)
        for h in range(hps):
            sc_ref[slot, h] = _dot_t(k_ref[0, h, pl.ds(start, tq), :], q_ref[0, h])

    def consume(j, slot, diag):
        start = pl.multiple_of(j * tq, tq)
        for h in range(hps):
            s = sc_ref[slot, h]
            if diag:
                kc = lax.broadcasted_iota(jnp.int32, s.shape, 0) // CHUNK
                qc = lax.broadcasted_iota(jnp.int32, s.shape, 1) // CHUNK
                s = jnp.where(kc <= qc, s, NEG)
            m = m_ref[h]
            m_new = jnp.maximum(m, jnp.max(s, axis=0, keepdims=True))
            alpha = jnp.exp(m - m_new)
            p = jnp.exp(s - m_new)
            l_ref[h] = l_ref[h] * alpha + jnp.sum(p, axis=0, keepdims=True)
            acc_ref[h] = acc_ref[h] * alpha + _dot(vt_ref[0, h, :, pl.ds(start, tq)], p.astype(BF16))
            m_ref[h] = m_new

    scores(0, 0)

    def body(j, c):
        scores(2 * j + 1, 1)
        consume(2 * j, 0, False)
        scores(2 * j + 2, 0)
        consume(2 * j + 1, 1, False)
        return c

    lax.fori_loop(0, qi // 2, body, 0)

    @pl.when(qi % 2 == 1)
    def _():
        scores(qi, 1)
        consume(qi - 1, 0, False)
        consume(qi, 1, True)

    @pl.when(qi % 2 == 0)
    def _():
        consume(qi, 0, True)

    for h in range(hps):
        o_ref[0, :, h * LANE:(h + 1) * LANE] = (acc_ref[h] * (1.0 / l_ref[h])).T.astype(o_ref.dtype)


def _mla_prompt_attn(q, k, vt, tq=512, hps=4):
    b, heads, s, _ = q.shape
    tq = _tile(s, tq)
    hps = min(hps, heads)
    assert tq % CHUNK == 0 and heads % hps == 0
    return pl.pallas_call(
        functools.partial(_flash_kernel, tq=tq, hps=hps),
        grid=(b, heads // hps, s // tq),
        in_specs=[pl.BlockSpec((1, hps, tq, 256), lambda i, h, j: (i, h, j, 0)),
                  pl.BlockSpec((1, hps, s, 256), lambda i, h, j: (i, h, 0, 0)),
                  pl.BlockSpec((1, hps, LANE, s), lambda i, h, j: (i, h, 0, 0))],
        out_specs=pl.BlockSpec((1, tq, hps * LANE), lambda i, h, j: (i, j, h)),
        out_shape=jax.ShapeDtypeStruct((b, s, heads * LANE), BF16),
        scratch_shapes=[pltpu.VMEM((hps, 1, tq), F32), pltpu.VMEM((hps, 1, tq), F32),
                        pltpu.VMEM((hps, LANE, tq), F32), pltpu.VMEM((2, hps, tq, tq), F32)],
        compiler_params=_cp("parallel", "parallel", "arbitrary"),
        name="mla_prompt_attn",
    )(q, k, vt)


def _mla_sample_kernel(q_ref, cp_ref, krp_ref, cn_ref, krn_ref, wuk_ref, wuv_ref, o_ref,
                       qa_ref, qr_ref, m_ref, l_ref, acc_ref, *, heads, sq):
    kb = pl.program_id(1)

    @pl.when(kb == 0)
    def _():
        for h in range(heads):
            qh = q_ref[0, h]
            qa_ref[h * sq:(h + 1) * sq, :] = _dot_t(qh[:, :LANE], wuk_ref[:, h * LANE:(h + 1) * LANE]).astype(BF16)
            qr_ref[h * sq:(h + 1) * sq, :] = qh[:, LANE:]
        m_ref[...] = jnp.full(m_ref.shape, NEG, F32)
        l_ref[...] = jnp.zeros(l_ref.shape, F32)
        acc_ref[...] = jnp.zeros(acc_ref.shape, F32)

    def update(cb, krb):
        s = _dot_t(qa_ref[...], cb) + _dot_t(qr_ref[...], krb)
        m_new = jnp.maximum(m_ref[...], jnp.max(s, axis=-1, keepdims=True))
        alpha = jnp.exp(m_ref[...] - m_new)
        p = jnp.exp(s - m_new)
        l_ref[...] = l_ref[...] * alpha + jnp.sum(p, axis=-1, keepdims=True)
        acc_ref[...] = acc_ref[...] * alpha + _dot(p.astype(BF16), cb)
        m_ref[...] = m_new

    update(cp_ref[0].astype(BF16), krp_ref[0])

    @pl.when(kb == pl.num_programs(1) - 1)
    def _():
        update(cn_ref[0].astype(BF16), krn_ref[0, 0])
        o_lat = (acc_ref[...] / l_ref[...]).astype(BF16)
        for h in range(heads):
            o_ref[0, :, h * LANE:(h + 1) * LANE] = _dot(
                o_lat[h * sq:(h + 1) * sq], wuv_ref[:, h * LANE:(h + 1) * LANE]).astype(o_ref.dtype)


def _mla_sample_attn(q, c_past, kr_past, c_new, k_new, wuk, wuv, tk=1024):
    b, heads, sq, _ = q.shape
    past, kv_lora = c_past.shape[1:]
    tk = _tile(past, tk)
    kern = functools.partial(_mla_sample_kernel, heads=heads, sq=sq)
    return pl.pallas_call(
        kern,
        grid=(b, past // tk),
        in_specs=[pl.BlockSpec((1, heads, sq, 256), lambda i, j: (i, 0, 0, 0)),
                  pl.BlockSpec((1, tk, kv_lora), lambda i, j: (i, j, 0)),
                  pl.BlockSpec((1, tk, LANE), lambda i, j: (i, j, 0)),
                  pl.BlockSpec((1, sq, kv_lora), lambda i, j: (i, 0, 0)),
                  pl.BlockSpec((1, 1, sq, LANE), lambda i, j: (i, 0, 0, 1)),
                  _resident(wuk.shape), _resident(wuv.shape)],
        out_specs=pl.BlockSpec((1, sq, heads * LANE), lambda i, j: (i, 0, 0)),
        out_shape=jax.ShapeDtypeStruct((b, sq, heads * LANE), BF16),
        scratch_shapes=[pltpu.VMEM((heads * sq, kv_lora), BF16), pltpu.VMEM((heads * sq, LANE), BF16),
                        pltpu.VMEM((heads * sq, 1), F32), pltpu.VMEM((heads * sq, 1), F32),
                        pltpu.VMEM((heads * sq, kv_lora), F32)],
        compiler_params=_cp("parallel", "arbitrary"),
        name="mla_sample_attn",
    )(q, c_past, kr_past, c_new, k_new, wuk, wuv)


def _rwkv_pre_kernel(x_ref, gmix_ref, w_ref, sh0_ref, mu_ref, w0_ref, wb_ref, a0_ref, ab_ref, gb_ref,
                     kkp_ref, ka_ref, rk_ref, e_ref, et_ref,
                     r_ref, dec_ref, k_ref, v_ref, kk_ref, b_ref, g_ref, bonus_ref, sh_ref, last_ref, *, dim):
    si = pl.program_id(1)

    def put(ref, val):
        nb, ts, tiles, _ = ref.shape
        ref[...] = val.reshape(nb * ts, tiles, LANE).reshape(nb, ts, tiles, LANE)

    u = _rms(x_ref[0], gmix_ref[...]).astype(BF16)
    tm = u.shape[0]
    width = w_ref.shape[1]

    @pl.when(si == 0)
    def _():
        last_ref[...] = sh0_ref[0]

    def shifted(c0, c1):
        p = _dot(u, w_ref[:, c0:c1])
        row = lax.broadcasted_iota(jnp.int32, p.shape, 0)
        prev = jnp.where(row == 0, last_ref[:, c0:c1], pltpu.roll(p, 1, axis=0))
        last_ref[:, c0:c1] = p[tm - 1:tm, :]
        sh_ref[0, :, c0:c1] = p[tm - 1:tm, :]
        return p + mu_ref[:, c0:c1] * (prev - p)

    lora = shifted(3 * dim, width)
    k = shifted(dim, 2 * dim)
    dw = lora[:, :LANE]
    da = lora[:, LANE:2 * LANE]
    dg = lora[:, 2 * LANE:]
    z = -(w0_ref[...] + _dot(jnp.tanh(dw).astype(BF16), wb_ref[...]))
    a = _sigmoid(a0_ref[...] + _dot(da.astype(BF16), ab_ref[...]))
    g_ref[0] = _dot(_sigmoid(dg).astype(BF16), gb_ref[...])
    r = shifted(0, dim)
    softplus = jnp.maximum(z, 0.0) + jnp.log(1.0 + jnp.exp(-jnp.abs(z)))
    put(dec_ref, jnp.exp(-jnp.exp(-softplus - 0.5)))
    kk = k * kkp_ref[...]
    nrm = jnp.sqrt(_seg_sum(kk * kk, e_ref[...], et_ref[...]))
    v = shifted(2 * dim, 3 * dim)
    kk = kk / jnp.maximum(nrm, 1e-12)
    k2 = k * (1.0 + (a - 1.0) * ka_ref[...])
    put(k_ref, k2)
    put(kk_ref, kk)
    put(b_ref, kk * a)
    put(r_ref, r)
    put(v_ref, v)
    bonus_ref[0] = _seg_sum(r * k2 * rk_ref[...], e_ref[...], et_ref[...]) * v


def _rwkv_pre(x, gmix, w, sh0, mu, w0, wb, a0, ab, gb, kkp, ka, rk, e, et, tm=256):
    b, s, d = x.shape
    dim = w0.shape[1]
    width = w.shape[1]
    tm = _tile(s, tm)
    ts = min(SCAN_TS, s)
    assert tm % ts == 0 and dim % LANE == 0
    row = lambda i, j: (i, j, 0)
    act = pl.BlockSpec((1, tm, dim), row)
    act_shape = jax.ShapeDtypeStruct((b, s, dim), F32)
    tiles = dim // LANE
    assert tiles % 8 == 0
    seq = pl.BlockSpec((tm // ts, ts, tiles, LANE), lambda i, j: (j, 0, i, 0))
    seq_shape = jax.ShapeDtypeStruct((s // ts, ts, b * tiles, LANE), F32)
    consts = (gmix, w, mu, w0, wb, a0, ab, gb, kkp, ka, rk, e, et)
    return pl.pallas_call(
        functools.partial(_rwkv_pre_kernel, dim=dim),
        grid=(b, s // tm),
        in_specs=[pl.BlockSpec((1, tm, d), row), _resident(gmix.shape), _resident(w.shape),
                  pl.BlockSpec((1, 1, width), lambda i, j: (i, 0, 0))] + [_resident(c.shape) for c in consts[2:]],
        out_specs=[seq] * 6 + [act] * 2 + [pl.BlockSpec((1, 1, width), lambda i, j: (i, 0, 0))],
        out_shape=[seq_shape] * 6 + [act_shape] * 2 + [jax.ShapeDtypeStruct((b, 1, width), F32)],
        scratch_shapes=[pltpu.VMEM((1, width), F32)],
        compiler_params=_cp("parallel", "arbitrary"),
        name="rwkv_pre",
    )(x, gmix, w, sh0, *consts[2:])


def _rwkv_scan_kernel(r_ref, dec_ref, k_ref, v_ref, kk_ref, b_ref, s0_ref, o_ref, st_ref, xt_ref, ot_ref,
                      *, ts, n, unroll):
    tb = pl.program_id(1)
    hpl = LANE // n

    @pl.when(tb == 0)
    def _():
        st_ref[...] = s0_ref[...]

    srcs = (r_ref, dec_ref, k_ref, v_ref, kk_ref, b_ref)
    for i in range(ts):
        for a, ref in enumerate(srcs):
            xt_ref[a, i] = ref[i].T

    def step(i, c):
        for h2 in range(hpl):
            base = h2 * n
            vv = xt_ref[3, i, base:base + n, :]

            def p1(kx, sa):
                return sa + st_ref[h2, kx] * xt_ref[4, i, pl.ds(base + kx, 1), :]

            sa = lax.fori_loop(0, n, p1, jnp.zeros_like(vv), unroll=unroll)

            def p2(kx, o):
                s_new = (st_ref[h2, kx] * xt_ref[1, i, pl.ds(base + kx, 1), :]
                         - sa * xt_ref[5, i, pl.ds(base + kx, 1), :] + vv * xt_ref[2, i, pl.ds(base + kx, 1), :])
                st_ref[h2, kx] = s_new
                return o + s_new * xt_ref[0, i, pl.ds(base + kx, 1), :]

            ot_ref[i, base:base + n, :] = lax.fori_loop(0, n, p2, jnp.zeros_like(vv), unroll=unroll)
        return c

    lax.fori_loop(0, ts, step, 0)
    for i in range(ts):
        o_ref[i] = ot_ref[i].T


def _rwkv_scan(r, dec, k, v, kk, bb, s0):
    nt, ts, streams, _ = r.shape
    groups, hpl, n = s0.shape[:3]
    assert streams == groups * LANE
    seq = pl.BlockSpec((None, ts, LANE, LANE), lambda g, t: (t, 0, g, 0))
    state = pl.BlockSpec((None, hpl, n, n, LANE), lambda g, t: (g, 0, 0, 0, 0))
    return pl.pallas_call(
        functools.partial(_rwkv_scan_kernel, ts=ts, n=n, unroll=SCAN_UNROLL),
        grid=(groups, nt),
        in_specs=[seq] * 6 + [state],
        out_specs=[seq, state],
        out_shape=[jax.ShapeDtypeStruct(r.shape, F32), jax.ShapeDtypeStruct(s0.shape, F32)],
        scratch_shapes=[pltpu.VMEM((6, ts, LANE, LANE), F32), pltpu.VMEM((ts, LANE, LANE), F32)],
        compiler_params=_cp("parallel", "arbitrary"),
        name="rwkv_scan",
    )(r, dec, k, v, kk, bb, s0)


def _state_to_streams(wkv):
    b, h, n = wkv.shape[:3]
    hpl = LANE // n
    tiles = h // hpl
    bg = LANE // tiles
    g = b // bg
    t = wkv.reshape(g, bg, tiles, hpl, n, n)
    return t.transpose(0, 3, 5, 4, 1, 2).reshape(g, hpl, n, n, LANE)


def _state_from_streams(st, b, h):
    g, hpl, n = st.shape[:3]
    tiles = h // hpl
    bg = LANE // tiles
    t = st.reshape(g, hpl, n, n, bg, tiles)
    return t.transpose(0, 4, 5, 1, 3, 2).reshape(b, h, n, n)


def _mix_kernel(o_ref, bonus_ref, g_ref, ymla_ref, grw_ref, gmla_ref, lng_ref, lnb_ref, e_ref, et_ref,
                wr_ref, wm_ref, out_ref, *, inv_n):
    nb, ts, tiles, _ = o_ref.shape
    o = o_ref[...].reshape(nb * ts, tiles, LANE).reshape(nb * ts, tiles * LANE)
    e, et = e_ref[...], et_ref[...]
    d = o - _seg_sum(o, e, et) * inv_n
    var = _seg_sum(d * d, e, et) * inv_n
    on = d * lax.rsqrt(var + GN_EPS) * lng_ref[...] + lnb_ref[...]
    y = ((on + bonus_ref[0]) * g_ref[0]).astype(BF16)
    merged = grw_ref[0] * _dot(y, wr_ref[...]) + gmla_ref[0] * _dot(ymla_ref[0], wm_ref[...])
    out_ref[0] = merged.astype(out_ref.dtype)


def _mix(o, bonus, g, ymla, gates, lng, lnb, e, et, wr, wm, head_dim, tm=256):
    nt, ts = o.shape[:2]
    b, s, dim = bonus.shape
    tiles = dim // LANE
    d = wr.shape[1]
    md = ymla.shape[2]
    tm = _tile(s, tm)
    assert tm % ts == 0
    row = lambda i, j: (i, j, 0)
    return pl.pallas_call(
        functools.partial(_mix_kernel, inv_n=1.0 / head_dim),
        grid=(b, s // tm),
        in_specs=[pl.BlockSpec((tm // ts, ts, tiles, LANE), lambda i, j: (j, 0, i, 0)),
                  pl.BlockSpec((1, tm, dim), row), pl.BlockSpec((1, tm, dim), row), pl.BlockSpec((1, tm, md), row),
                  pl.BlockSpec((1, tm, d), lambda i, j: (i, j, 0)), pl.BlockSpec((1, tm, d), lambda i, j: (i, j, 1))]
                 + [_resident(c.shape) for c in (lng, lnb, e, et, wr, wm)],
        out_specs=pl.BlockSpec((1, tm, d), row),
        out_shape=jax.ShapeDtypeStruct((b, s, d), BF16),
        compiler_params=_cp("parallel", "parallel"),
        name="rwkv_post_mix",
    )(o, bonus, g, ymla, gates, gates, lng, lnb, e, et, wr, wm)


def _outproj_kernel(x_ref, m_ref, wo_ref, gc_ref, wq_ref, h_ref, q_ref):
    h = x_ref[...] + _dot(m_ref[...], wo_ref[...])
    h_ref[...] = h
    q_ref[...] = _dot(_rms(h, gc_ref[...]).astype(BF16), wq_ref[...]).astype(q_ref.dtype)


def _outproj(x, merged, wo, gc, wq, tm=256):
    t, d = x.shape
    tm = _tile(t, tm)
    row = lambda i: (i, 0)
    return pl.pallas_call(
        _outproj_kernel,
        grid=(t // tm,),
        in_specs=[pl.BlockSpec((tm, d), row), pl.BlockSpec((tm, d), row),
                  _resident(wo.shape), _resident(gc.shape), _resident(wq.shape)],
        out_specs=[pl.BlockSpec((tm, d), row), pl.BlockSpec((tm, d), row)],
        out_shape=[jax.ShapeDtypeStruct((t, d), F32), jax.ShapeDtypeStruct((t, d), BF16)],
        compiler_params=_cp("parallel"),
        name="outproj_crossq",
    )(x, merged, wo, gc, wq)


def _route_tile(lg, n_groups, per_group):
    n_exp = n_groups * per_group
    lane = lax.broadcasted_iota(jnp.int32, lg.shape, 1)
    first = lambda mask: jnp.min(jnp.where(mask, lane, 2 * LANE), axis=-1, keepdims=True)
    top = lambda mask: jnp.max(jnp.where(mask, lg, NEG), axis=-1, keepdims=True)
    is_g = lane < n_groups
    gmax = top(is_g)
    g_sel = first(is_g & (lg == gmax))
    p_grp = 1.0 / jnp.sum(jnp.where(is_g, jnp.exp(lg - gmax), 0.0), axis=-1, keepdims=True)
    eid = lane - n_groups
    in_grp = (eid >= g_sel * per_group) & (eid < (g_sel + 1) * per_group)
    v1 = top(in_grp)
    e1 = first(in_grp & (lg == v1))
    rest = in_grp & (lane != e1)
    v2 = top(rest)
    e2 = first(rest & (lg == v2))
    t = jnp.exp(v2 - v1)
    w1 = p_grp / (1.0 + t)
    w2 = w1 * t
    route = jnp.where(lane == 0, (e1 - n_groups).astype(F32),
                      jnp.where(lane == 1, (e2 - n_groups).astype(F32),
                                jnp.where(lane == 2, w1, jnp.where(lane == 3, w2, 0.0))))
    hits = jnp.where((lane == e1 - n_groups) | (lane == e2 - n_groups), 1.0, 0.0)
    assert n_exp <= LANE
    return route, jnp.sum(hits, axis=0, keepdims=True)


def _cross_kernel(q_ref, mk_ref, mv_ref, h_ref, wo_ref, gm_ref, wr_ref, br_ref, hn_all_ref, h2_ref, hn_ref, rt_ref,
                  cnt_ref, *, heads, scale, n_groups, per_group):
    del hn_all_ref
    q = q_ref[0]
    mk = mk_ref[0]
    mv = mv_ref[0]
    hd = q.shape[1] // heads
    outs = []
    for h in range(heads):
        sl = slice(h * hd, (h + 1) * hd)
        s = _dot_t(q[:, sl], mk[:, sl]) * scale
        p = jnp.exp(s - jnp.max(s, axis=-1, keepdims=True))
        p = p / jnp.sum(p, axis=-1, keepdims=True)
        outs.append(_dot(p.astype(BF16), mv[:, sl]).astype(BF16))
    o = jnp.concatenate(outs, axis=1)
    h2 = h_ref[0] + _dot(o, wo_ref[...])
    h2_ref[0] = h2
    hn = _rms(h2, gm_ref[...]).astype(BF16)
    hn_ref[...] = hn
    route, hist = _route_tile(_dot(hn, wr_ref[...]) + br_ref[...], n_groups, per_group)
    rt_ref[0] = route
    cnt_ref[0, 0] = jnp.broadcast_to(hist, cnt_ref.shape[2:])


def _cross(q, mk, mv, h, wo, gm, wr, br, heads, n_groups, per_group, total_rows, row_off, hn_all, tm=512):
    b, s, d = h.shape
    n_mem = mk.shape[1]
    tm = _tile(s, tm)
    assert row_off % tm == 0
    off, per_b = row_off // tm, s // tm
    row = lambda i, j: (i, j, 0)
    mem = pl.BlockSpec((1, n_mem, d), lambda i, j: (i, 0, 0))
    kern = functools.partial(_cross_kernel, heads=heads, scale=float(d // heads) ** -0.5,
                             n_groups=n_groups, per_group=per_group)
    in_specs = [pl.BlockSpec((1, tm, d), row), mem, mem, pl.BlockSpec((1, tm, d), row),
                _resident(wo.shape), _resident(gm.shape), _resident(wr.shape), _resident(br.shape),
                pl.BlockSpec(memory_space=pl.ANY)]
    args = (q, mk, mv, h, wo, gm, wr, br, hn_all)
    return pl.pallas_call(
        kern,
        grid=(b, s // tm),
        in_specs=in_specs,
        out_specs=[pl.BlockSpec((1, tm, d), row), pl.BlockSpec((tm, d), lambda i, j: (off + i * per_b + j, 0)),
                   pl.BlockSpec((1, tm, LANE), row), pl.BlockSpec((1, 1, 8, LANE), lambda i, j: (i, j, 0, 0))],
        out_shape=[jax.ShapeDtypeStruct((b, s, d), F32), jax.ShapeDtypeStruct((total_rows, d), BF16),
                   jax.ShapeDtypeStruct((b, s, LANE), F32), jax.ShapeDtypeStruct((b, s // tm, 8, LANE), F32)],
        input_output_aliases={len(args) - 1: 1},
        compiler_params=_cp("parallel", "parallel"),
        name="cross_attn_router",
    )(*args)


def _moe_kernel(be_ref, on_ref, first_ref, nxt_ref, par_ref, x_ref, wg_hbm, wu_hbm, wd_hbm, y_ref,
                wgf_ref, wuf_ref, wdf_ref, wgb_ref, wub_ref, wdb_ref, sem_ref):
    i = pl.program_id(0)

    def fetch(e, slot):
        return (pltpu.make_async_copy(wg_hbm.at[e], wgf_ref.at[slot], sem_ref.at[slot, 0]),
                pltpu.make_async_copy(wu_hbm.at[e], wuf_ref.at[slot], sem_ref.at[slot, 1]),
                pltpu.make_async_copy(wd_hbm.at[e], wdf_ref.at[slot], sem_ref.at[slot, 2]))

    @pl.when(first_ref[i] != 0)
    def _():
        slot = par_ref[i]

        @pl.when(i == 0)
        def _():
            for c in fetch(be_ref[0], 0):
                c.start()

        for c in fetch(be_ref[i], slot):
            c.wait()
        wgb_ref[...] = wgf_ref[slot].astype(BF16)
        wub_ref[...] = wuf_ref[slot].astype(BF16)
        wdb_ref[...] = wdf_ref[slot].astype(BF16)

        @pl.when(nxt_ref[i] >= 0)
        def _():
            for c in fetch(nxt_ref[i], 1 - slot):
                c.start()

    @pl.when(on_ref[i] != 0)
    def _():
        xb = x_ref[...]
        g = _dot(xb, wgb_ref[...])
        u = _dot(xb, wub_ref[...])
        hb = (g * _sigmoid(g) * u).astype(BF16)
        y_ref[...] = _dot(hb, wdb_ref[...])

    @pl.when(on_ref[i] == 0)
    def _():
        y_ref[...] = jnp.zeros(y_ref.shape, y_ref.dtype)


def _moe_experts(blk_e, blk_on, counts, x_slot, wg, wu, wd):
    ns, d = x_slot.shape
    n_exp, _, f = wg.shape
    n_blk = ns // MOE_ROWS
    prev_e = jnp.concatenate([jnp.full((1,), -1, jnp.int32), blk_e[:-1]])
    first = ((blk_on != 0) & (blk_e != prev_e)).astype(jnp.int32)
    par = ((jnp.cumsum(first) - 1) % 2).astype(jnp.int32)
    ids = jnp.where(counts > 0, jnp.arange(n_exp, dtype=jnp.int32), n_exp)
    nxt_tab = jnp.concatenate([lax.cummin(ids[::-1])[::-1][1:], jnp.full((1,), n_exp, jnp.int32)])
    nxt = nxt_tab[blk_e]
    nxt = jnp.where(nxt < n_exp, nxt, -1).astype(jnp.int32)
    hbm = pl.BlockSpec(memory_space=pl.ANY)
    grid_spec = pltpu.PrefetchScalarGridSpec(
        num_scalar_prefetch=5,
        grid=(n_blk,),
        in_specs=[pl.BlockSpec((MOE_ROWS, d), lambda i, *_: (i, 0)), hbm, hbm, hbm],
        out_specs=pl.BlockSpec((MOE_ROWS, d), lambda i, *_: (i, 0)),
        scratch_shapes=[pltpu.VMEM((2, d, f), F32), pltpu.VMEM((2, d, f), F32), pltpu.VMEM((2, f, d), F32),
                        pltpu.VMEM((d, f), BF16), pltpu.VMEM((d, f), BF16), pltpu.VMEM((f, d), BF16),
                        pltpu.SemaphoreType.DMA((2, 3))],
    )
    return pl.pallas_call(
        _moe_kernel,
        grid_spec=grid_spec,
        out_shape=jax.ShapeDtypeStruct((ns, d), F32),
        compiler_params=_cp("arbitrary"),
        name="moe_experts",
    )(blk_e, blk_on, first, nxt, par, x_slot, wg, wu, wd)


def _combine_kernel(idx_ref, nidx_ref, h_ref, rt_ref, g_ref, y_hbm, o_ref, buf_ref, sem_ref, *, tm, norm):
    i = pl.program_id(0)
    n = pl.num_programs(0)

    def issue(ids_ref, slot):
        def one(r, c):
            for ch in range(TOP_K):
                src = ids_ref[0, TOP_K * r + ch]
                pltpu.make_async_copy(y_hbm.at[pl.ds(src, 1)], buf_ref.at[slot, ch, pl.ds(r, 1)],
                                      sem_ref.at[slot, ch]).start()
            return c

        lax.fori_loop(0, tm, one, 0, unroll=8)

    cur = i % 2

    @pl.when(i == 0)
    def _():
        issue(idx_ref, 0)

    @pl.when(i + 1 < n)
    def _():
        issue(nidx_ref, 1 - cur)

    acc = h_ref[...]
    for ch in range(TOP_K):
        pltpu.make_async_copy(y_hbm.at[pl.ds(0, tm)], buf_ref.at[cur, ch], sem_ref.at[cur, ch]).wait()
        acc = acc + buf_ref[cur, ch] * rt_ref[:, TOP_K + ch:TOP_K + ch + 1]
    o_ref[...] = _rms(acc, g_ref[...]) if norm else acc


def _combine(h, y_slot, slot_of, route, row_off, g, tm=512):
    t, d = h.shape
    tm = _tile(t, tm)
    assert row_off % tm == 0
    off = row_off // tm
    nb = t // tm
    idx = slot_of[row_off * TOP_K:(row_off + t) * TOP_K].reshape(nb, 1, TOP_K * tm)
    norm = g is not None
    g = g if norm else jnp.ones((1, d), F32)
    ids = lambda f: pl.BlockSpec((None, 1, TOP_K * tm), f, memory_space=pltpu.SMEM)
    return pl.pallas_call(
        functools.partial(_combine_kernel, tm=tm, norm=norm),
        grid=(nb,),
        in_specs=[ids(lambda i: (i, 0, 0)), ids(lambda i: (jnp.minimum(i + 1, nb - 1), 0, 0)),
                  pl.BlockSpec((tm, d), lambda i: (i, 0)),
                  pl.BlockSpec((tm, LANE), lambda i: (i + off, 0)),
                  pl.BlockSpec((1, d), lambda i: (0, 0)),
                  pl.BlockSpec(memory_space=pl.ANY)],
        out_specs=pl.BlockSpec((tm, d), lambda i: (i, 0)),
        out_shape=jax.ShapeDtypeStruct((t, d), F32),
        scratch_shapes=[pltpu.VMEM((2, TOP_K, tm, d), F32), pltpu.SemaphoreType.DMA((2, TOP_K))],
        compiler_params=_cp("arbitrary"),
        name="moe_combine_norm",
    )(idx, idx, h, route, g, y_slot)


def _pad_cols(w, n):
    return jnp.pad(w, ((0, 0),) * (w.ndim - 1) + ((0, n - w.shape[-1]),))


def _pad_rows(w, n):
    return jnp.pad(w, ((0, n - w.shape[0]), (0, 0)))


def _rot_half_cols(w):
    half = w.shape[-1] // 2
    return jnp.concatenate([-w[..., half:], w[..., :half]], axis=-1)


def _rope_table(pos):
    half = ROPE // 2
    inv = ROPE_THETA ** (-jnp.arange(half, dtype=F32) / half)
    ang = pos.astype(F32)[:, None] * inv[None, :]
    cos, sin = jnp.cos(ang), jnp.sin(ang)
    return jnp.concatenate([cos, cos, sin, sin], axis=1)


def _prep_layer(lp, dims):
    ql, kl, rope, dim, dl, il, gl, d = (dims[k] for k in ("q_lora", "kv_lora", "rope", "rw_dim", "decay_lora",
                                                          "iclr_lora", "gate_lora", "d_model"))
    w_in = lp["w_in"]
    i0, i1, i2 = ql, ql + kl, ql + kl + rope
    i3 = i2 + 3 * dim + dl + il + gl
    w_kr = w_in[:, i1:i2]
    out = {}
    out["w_mla"] = jnp.concatenate(
        [w_in[:, :i0], w_kr, _rot_half_cols(w_kr), w_in[:, i0:i1]], axis=1).astype(BF16)
    rw = w_in[:, i2:i3]
    c3 = 3 * dim
    glp = -(-gl // LANE) * LANE

    def regroup(t):
        return jnp.concatenate([t[..., :c3], _pad_cols(t[..., c3:c3 + dl], LANE),
                                _pad_cols(t[..., c3 + dl:c3 + dl + il], LANE),
                                _pad_cols(t[..., c3 + dl + il:], glp)], axis=-1)

    out["regroup"] = regroup
    out["w_rw"] = regroup(rw).astype(BF16)
    out["mu"] = regroup(lp["rw_mu"][None, :])
    out["w_gates"] = w_in[:, i3:].astype(BF16)
    heads = lp["w_uq"].shape[1]
    nope = lp["w_uq"].shape[2] - rope
    wq = lp["w_uq"]
    out["w_q"] = jnp.concatenate([wq[..., :nope], wq[..., nope:], _rot_half_cols(wq[..., nope:])],
                                 axis=-1).reshape(ql, heads * 256).astype(BF16)
    out["w_uk"] = lp["w_uk"].reshape(kl, -1).astype(BF16)
    out["w_uv"] = lp["w_uv"].reshape(kl, -1).astype(BF16)
    out["w_uv_t"] = out["w_uv"].T
    out["wb"] = _pad_rows(lp["rw_wb"], LANE).astype(BF16)
    out["ab"] = _pad_rows(lp["rw_ab"], LANE).astype(BF16)
    out["gb"] = _pad_rows(lp["rw_gb"], glp).astype(BF16)
    n_heads = dims["rw_heads"]
    hd = dim // n_heads
    e = (jnp.arange(dim)[:, None] // hd == jnp.arange(LANE)[None, :]).astype(BF16)
    out["e"], out["et"] = e, e.T
    out["w_br_rwkv"] = lp["w_br_rwkv"].astype(BF16)
    out["w_br_mla"] = lp["w_br_mla"].astype(BF16)
    out["w_out"] = lp["w_out"].astype(BF16)
    out["w_mq"] = lp["w_mq"].reshape(d, d).astype(BF16)
    out["w_mk"] = lp["w_mk"].reshape(d, d).astype(BF16)
    out["w_mv"] = lp["w_mv"].reshape(d, d).astype(BF16)
    out["w_mo"] = lp["w_mo"].reshape(d, d).astype(BF16)
    n_g, n_e = lp["w_rg"].shape[1], lp["w_re"].shape[1]
    out["w_r"] = _pad_cols(jnp.concatenate([lp["w_rg"], lp["w_re"]], axis=1), LANE).astype(BF16)
    out["b_r"] = _pad_cols(jnp.concatenate([lp["b_rg"], lp["b_re"]])[None, :], LANE)
    out["w_eg"], out["w_eu"], out["w_ed"] = lp["w_eg"], lp["w_eu"], lp["w_ed"]
    return out


def _row(v):
    return v.reshape(1, -1)


def _mixer(x, lp, wp, dims, pos, shift0, wkv0, attend):
    b, s, d = x.shape
    heads, rw_heads = dims["mla_heads"], dims["rw_heads"]
    gmix = _row(lp["g_mix"])
    cs = _rope_table(pos)
    scale = float(dims["nope"] + dims["rope"]) ** -0.5
    q, k, v, c, kr = _mla_proj(x, gmix, wp["w_mla"], _row(lp["g_q"]), wp["w_q"], _row(lp["g_kv"]),
                               wp["w_uk"], wp["w_uv_t"], cs, heads, scale)
    y_mla = attend(q, k, v, c)
    sh0 = wp["regroup"](shift0)
    r, dec, k2, vr, kk, bb, g, bonus, sh = _rwkv_pre(
        x, gmix, wp["w_rw"], sh0, wp["mu"], _row(lp["rw_w0"]), wp["wb"], _row(lp["rw_a0"]), wp["ab"], wp["gb"],
        _row(lp["rw_kk"]), _row(lp["rw_ka"]), _row(lp["rw_rk"]), wp["e"], wp["et"])
    n = dims["rw_dim"] // rw_heads
    o, s_t = _rwkv_scan(r, dec, k2, vr, kk, bb, _state_to_streams(wkv0))
    wkv = _state_from_streams(s_t, b, rw_heads)
    gates, = _proj(x.reshape(b * s, d), gmix, [wp["w_gates"]], 512, out_dtype=BF16, gate=True)
    gates = gates.reshape(b, s, 2 * d)
    dim = dims["rw_dim"]
    merged = _mix(o, bonus, g, y_mla, gates, _row(lp["rw_ln_g"]), _row(lp["rw_ln_b"]), wp["e"], wp["et"],
                  wp["w_br_rwkv"], wp["w_br_mla"], n).reshape(b * s, d)
    h, qc = _outproj(x.reshape(b * s, d), merged, wp["w_out"], _row(lp["g_cross"]), wp["w_mq"])
    c3, dl, il = 3 * dim, dims["decay_lora"], dims["iclr_lora"]
    shift = jnp.concatenate([sh[..., :c3], sh[..., c3:c3 + dl], sh[..., c3 + LANE:c3 + LANE + il],
                             sh[..., c3 + 2 * LANE:c3 + 2 * LANE + dims["gate_lora"]]], axis=-1)
    return h.reshape(b, s, d), qc.reshape(b, s, d), c, kr, k, wkv, shift


def _moe(hn, route, counts, wp):
    t, d = hn.shape
    n_exp = counts.shape[0]
    e_idx = route[:, :TOP_K]
    m = t * TOP_K
    n_blk = -(-(m + n_exp * (MOE_ROWS - 1)) // MOE_ROWS)
    ns = n_blk * MOE_ROWS
    flat_e = e_idx.reshape(m).astype(jnp.int32)
    order = jnp.argsort(flat_e).astype(jnp.int32)
    padded = (counts + MOE_ROWS - 1) // MOE_ROWS * MOE_ROWS
    pad_end = jnp.cumsum(padded)
    pad_start = pad_end - padded
    start = jnp.cumsum(counts) - counts
    blk_start = jnp.arange(n_blk, dtype=jnp.int32) * MOE_ROWS
    blk_e = jnp.minimum(jnp.sum(pad_end[None, :] <= blk_start[:, None], axis=1), n_exp - 1).astype(jnp.int32)
    blk_on = (blk_start < pad_end[-1]).astype(jnp.int32)
    off = (blk_start - pad_start[blk_e])[:, None] + jnp.arange(MOE_ROWS, dtype=jnp.int32)[None, :]
    valid = ((off < counts[blk_e][:, None]) & (blk_on[:, None] != 0)).reshape(ns)
    asg = order[jnp.clip(start[blk_e][:, None] + off, 0, m - 1).reshape(ns)]
    slot_tok = jnp.where(valid, asg // TOP_K, jnp.arange(ns, dtype=jnp.int32) % t)
    y_slot = _moe_experts(blk_e, blk_on, counts, hn[slot_tok], wp["w_eg"], wp["w_eu"], wp["w_ed"])
    key = jnp.where(valid, asg, m + jnp.arange(ns, dtype=jnp.int32))
    slot_of = jnp.argsort(key)[:m].astype(jnp.int32)
    return y_slot, slot_of


def kernel(x_prompt, x_sample, mem_prompt, cache_kv_latent, cache_k_rope, cache_mem_k, cache_mem_v, state_wkv, state_shift, g_mix, w_in, g_q, w_uq, g_kv, w_uk, w_uv, rw_mu, rw_w0, rw_wb, rw_a0, rw_ab, rw_gb, rw_kk, rw_ka, rw_rk, rw_ln_g, rw_ln_b, w_br_rwkv, w_br_mla, w_out, g_cross, g_mem, w_mq, w_mk, w_mv, w_mo, g_moe, w_rg, b_rg, w_re, b_re, w_eg, w_eu, w_ed, g_final):
    depth = g_mix.shape[0]
    b, s, d = x_prompt.shape
    bs, ss, _ = x_sample.shape
    past = cache_kv_latent.shape[2]
    n_mem = mem_prompt.shape[1]
    mem_heads = w_mq.shape[2]
    rw_heads, rw_hd = rw_rk.shape[1:]
    dims = dict(d_model=d, q_lora=g_q.shape[1], kv_lora=g_kv.shape[1], mla_heads=w_uq.shape[2],
                nope=w_uk.shape[3], rope=cache_k_rope.shape[3], rw_heads=rw_heads, rw_dim=rw_heads * rw_hd,
                decay_lora=rw_wb.shape[1], iclr_lora=rw_ab.shape[1], gate_lora=rw_gb.shape[1])
    assert dims["rope"] == ROPE and dims["nope"] == LANE and w_uv.shape[3] == LANE
    assert dims["decay_lora"] <= LANE and dims["iclr_lora"] <= LANE and rw_heads <= LANE
    n_groups, n_exp = w_rg.shape[2], w_re.shape[2]
    per_group = n_exp // n_groups
    stacked = dict(g_mix=g_mix, w_in=w_in, g_q=g_q, w_uq=w_uq, g_kv=g_kv, w_uk=w_uk, w_uv=w_uv, rw_mu=rw_mu,
                   rw_w0=rw_w0, rw_wb=rw_wb, rw_a0=rw_a0, rw_ab=rw_ab, rw_gb=rw_gb, rw_kk=rw_kk, rw_ka=rw_ka,
                   rw_rk=rw_rk, rw_ln_g=rw_ln_g, rw_ln_b=rw_ln_b, w_br_rwkv=w_br_rwkv, w_br_mla=w_br_mla,
                   w_out=w_out, g_cross=g_cross, g_mem=g_mem, w_mq=w_mq, w_mk=w_mk, w_mv=w_mv, w_mo=w_mo,
                   g_moe=g_moe, w_rg=w_rg, b_rg=b_rg, w_re=w_re, b_re=b_re, w_eg=w_eg, w_eu=w_eu, w_ed=w_ed)
    h_p, h_s = x_prompt, x_sample
    outs = [[] for _ in range(10)]
    for l in range(depth):
        lp = {name: val[l] for name, val in stacked.items()}
        wp = _prep_layer(lp, dims)

        h, qc, c, kr, _, wkv, shift = _mixer(
            h_p, lp, wp, dims, jnp.arange(s), jnp.zeros((b, 1, state_shift.shape[3]), F32),
            jnp.zeros((b, rw_heads, rw_hd, rw_hd), F32), lambda q, k, v, c_new: _mla_prompt_attn(q, k, v))
        mk, mk_b, mv, mv_b = _proj(mem_prompt.reshape(b * n_mem, d), _row(lp["g_mem"]), [wp["w_mk"], wp["w_mv"]], 256,
                                   heads=mem_heads, twin=True)
        gm, wr, br = _row(lp["g_moe"]), wp["w_r"], wp["b_r"]
        tp, tsm = b * s, bs * ss
        h2_p, hn, rt_p, cnt_p = _cross(qc, mk_b.reshape(b, n_mem, d), mv_b.reshape(b, n_mem, d), h, wp["w_mo"], gm, wr,
                                       br, mem_heads, n_groups, per_group, tp + tsm, 0,
                                       hn_all=jnp.zeros((tp + tsm, d), BF16))
        for lst, val in zip(outs[:6], (c, kr, mk.reshape(b, n_mem, mem_heads, -1),
                                       mv.reshape(b, n_mem, mem_heads, -1), wkv, shift)):
            lst.append(val)

        kr_past = jnp.pad(cache_k_rope[l], ((0, 0), (0, 0), (0, LANE - ROPE))).astype(BF16)

        def attend_sample(q, k, v, c_new, l=l, kr_past=kr_past, wp=wp):
            return _mla_sample_attn(q, cache_kv_latent[l], kr_past, c_new, k, wp["w_uk"], wp["w_uv"])

        h, qc, c, kr, _, wkv, shift = _mixer(h_s, lp, wp, dims, past + jnp.arange(ss), state_shift[l],
                                             state_wkv[l], attend_sample)
        h2_s, hn, rt_s, cnt_s = _cross(qc, cache_mem_k[l].reshape(bs, n_mem, d).astype(BF16),
                                       cache_mem_v[l].reshape(bs, n_mem, d).astype(BF16), h, wp["w_mo"], gm, wr, br,
                                       mem_heads, n_groups, per_group, tp + tsm, tp, hn_all=hn)
        for lst, val in zip(outs[6:], (c, kr, wkv, shift)):
            lst.append(val)

        route = jnp.concatenate([rt_p.reshape(tp, LANE), rt_s.reshape(tsm, LANE)], axis=0)
        counts = (jnp.sum(cnt_p[:, :, 0, :n_exp], axis=(0, 1)) + jnp.sum(cnt_s[:, :, 0, :n_exp], axis=(0, 1)))
        y_slot, slot_of = _moe(hn, route, counts.astype(jnp.int32), wp)
        g_fin = _row(g_final) if l == depth - 1 else None
        h_p = _combine(h2_p.reshape(tp, d), y_slot, slot_of, route, 0, g_fin).reshape(b, s, d)
        h_s = _combine(h2_s.reshape(tsm, d), y_slot, slot_of, route, tp, g_fin).reshape(bs, ss, d)
    stacks = [jnp.stack(o) for o in outs]
    return (h_p, h_s, *stacks)
```
